```python
import numpy as np
import jax
import jax.numpy as jnp
from jax import lax

D_MODEL = 1024
BATCH = 16
SEQ = 2048
DEPTH = 1

GRID_W = 64
CTX_LEN = 256
N_MOD = 6
EPS = 1e-6

RW_HEAD_DIM = 64
RW_HEADS = (D_MODEL // 2) // RW_HEAD_DIM
RW_WIDTH = RW_HEADS * RW_HEAD_DIM
RW_DECAY_LORA = 64
RW_AAA_LORA = 64
RW_GATE_LORA = 128
RW_GN_EPS = 1e-5 * RW_HEAD_DIM
CONV_K = 3

HG_KEY_DIM = 64
HG_VAL_DIM = 64
HG_HEADS = (D_MODEL // 2) // HG_VAL_DIM
HG_KWIDTH = HG_HEADS * HG_KEY_DIM
HG_VWIDTH = HG_HEADS * HG_VAL_DIM
HG_CHUNK = 32

D_MIX = RW_WIDTH + HG_VWIDTH
IN_SPLITS = (RW_WIDTH, RW_WIDTH, RW_WIDTH, 2 * RW_DECAY_LORA, 2 * RW_AAA_LORA, RW_GATE_LORA,
             HG_KWIDTH, HG_KWIDTH, HG_KWIDTH, HG_VWIDTH, HG_VWIDTH)
P_IN = sum(IN_SPLITS)

N_EXPERTS = 64
N_GROUPS = 8
TOPK_GROUPS = 4
TOP_K = 8
D_EXPERT = D_MODEL // 4
D_SHARED = D_MODEL // 4
ROUTE_SCALE = 2.5
MOE_BLOCK = 256

kernel_name = 'hybrid_rwkv7_hgrn2_moe_dit_layer'


def _rmsnorm(x, g):
    xf = x.astype(jnp.float32)
    y = xf * lax.rsqrt(jnp.mean(xf * xf, axis=-1, keepdims=True) + EPS)
    return (y * g.astype(jnp.float32)).astype(x.dtype)


def _split_cols(u):
    idx = [int(i) for i in np.cumsum(IN_SPLITS)[:-1]]
    return jnp.split(u, idx, axis=-1)


def _heads(t, h, d):
    return t.reshape(t.shape[0], t.shape[1], h, d).astype(jnp.float32)


def _conv_latent(u, kern, rows):
    b, l, ch = u.shape
    u2 = u.reshape(b, rows, GRID_W, ch)
    out = lax.conv_general_dilated(u2, kern[:, :, None, :].astype(u.dtype), window_strides=(1, 1),
                                   padding='SAME', dimension_numbers=('NHWC', 'HWIO', 'NHWC'),
                                   feature_group_count=ch)
    return out.reshape(b, l, ch)


def _conv_context(u, kern):
    ch = u.shape[-1]
    return lax.conv_general_dilated(u, kern[CONV_K // 2][:, None, :].astype(u.dtype), window_strides=(1,),
                                    padding='SAME', dimension_numbers=('NWC', 'WIO', 'NWC'),
                                    feature_group_count=ch)


def _rwkv7_scan(r, w, k, v, a, b, s0, reverse, emit):
    def step(s, inp):
        r_t, w_t, k_t, v_t, a_t, b_t = inp
        sa = jnp.einsum('bhvk,bhk->bhv', s, a_t)
        s = s * w_t[:, :, None, :] + sa[..., None] * b_t[:, :, None, :] + v_t[..., None] * k_t[:, :, None, :]
        y = jnp.einsum('bhvk,bhk->bhv', s, r_t) if emit else None
        return s, y
    xs = tuple(jnp.moveaxis(t, 1, 0) for t in (r, w, k, v, a, b))
    s_fin, ys = lax.scan(step, s0, xs, reverse=reverse)
    return (jnp.moveaxis(ys, 0, 1) if emit else None), s_fin


def _rwkv7_mix(cols, conv_fn, s0, w0, w2, a0, a2, g2, k_k, k_a, r_k, lnx_g, lnx_b, emit):
    r, k, v, wd, ad, gd = cols
    r, k, v = jnp.split(conv_fn(jnp.concatenate([r, k, v], axis=-1)), 3, axis=-1)
    bsz, l, _ = r.shape
    rh = _heads(r, RW_HEADS, RW_HEAD_DIM)
    vh = _heads(v, RW_HEADS, RW_HEAD_DIM)
    kk = _heads(k * k_k, RW_HEADS, RW_HEAD_DIM)
    kk = kk / jnp.maximum(jnp.sqrt(jnp.sum(kk * kk, axis=-1, keepdims=True)), 1e-12)
    ys, bonus, states = [], [], []
    for d in range(2):
        wl = wd[..., d * RW_DECAY_LORA:(d + 1) * RW_DECAY_LORA]
        al = ad[..., d * RW_AAA_LORA:(d + 1) * RW_AAA_LORA]
        w = -jax.nn.softplus(-(w0[d] + jnp.tanh(wl) @ w2[d])) - 0.5
        decay = jnp.exp(-jnp.exp(_heads(w, RW_HEADS, RW_HEAD_DIM)))
        a_flat = jax.nn.sigmoid(a0[d] + al @ a2[d])
        a = _heads(a_flat, RW_HEADS, RW_HEAD_DIM)
        kd = _heads(k * (1.0 + (a_flat - 1.0) * k_a), RW_HEADS, RW_HEAD_DIM)
        y, s = _rwkv7_scan(rh, decay, kd, vh, -kk, kk * a, s0[d], d == 1, emit)
        states.append(s)
        if emit:
            ys.append(y)
            bonus.append(jnp.sum(rh * kd * r_k.astype(jnp.float32), axis=-1, keepdims=True) * vh)
    if not emit:
        return None, (states[0], states[1])
    y = ys[0] + ys[1]
    mu = jnp.mean(y, axis=-1, keepdims=True)
    var = jnp.mean(jnp.square(y - mu), axis=-1, keepdims=True)
    yn = ((y - mu) * lax.rsqrt(var + RW_GN_EPS)).reshape(bsz, l, RW_WIDTH) * lnx_g + lnx_b
    yn = yn + (bonus[0] + bonus[1]).reshape(bsz, l, RW_WIDTH)
    g = jax.nn.sigmoid(gd) @ g2
    return (yn * g).astype(r.dtype), (states[0], states[1])


def _gla_chunked(q, k, v, logf, s0, emit):
    bsz, l, h, _ = q.shape
    dv = v.shape[-1]
    n = l // HG_CHUNK
    blk = lambda t: jnp.moveaxis(t.reshape(bsz, n, HG_CHUNK, h, t.shape[-1]), 3, 1)
    q, k, v, logf = blk(q), blk(k), blk(v), blk(logf)
    cum = jnp.cumsum(logf, axis=3)
    last = cum[:, :, :, -1:, :]
    kv = jnp.einsum('bhnjk,bhnjv->bhnkv', k * jnp.exp(last - cum), v)
    chunk_decay = jnp.exp(last[:, :, :, 0, :])
    def step(s, inp):
        dc, kvc = inp
        return dc[..., None] * s + kvc, s
    s_fin, s_prev = lax.scan(step, s0, (jnp.moveaxis(chunk_decay, 2, 0), jnp.moveaxis(kv, 2, 0)))
    if not emit:
        return None, s_fin
    s_prev = jnp.moveaxis(s_prev, 0, 2)
    ref = cum[:, :, :, HG_CHUNK // 2:HG_CHUNK // 2 + 1, :]
    scores = jnp.einsum('bhnik,bhnjk->bhnij', q * jnp.exp(cum - ref), k * jnp.exp(ref - cum))
    causal = jnp.tril(jnp.ones((HG_CHUNK, HG_CHUNK), dtype=bool))
    scores = jnp.where(causal, scores, 0.0)
    o = (jnp.einsum('bhnij,bhnjv->bhniv', scores, v)
         + jnp.einsum('bhnik,bhnkv->bhniv', q * jnp.exp(cum), s_prev))
    return jnp.moveaxis(o, 1, 3).reshape(bsz, l, h, dv), s_fin


def _hgrn2_mix(cols, s0, lb, norm_g, emit):
    q, f_fw, f_bw, i, g = cols
    bsz, l, _ = q.shape
    qh = _heads(jax.nn.silu(q), HG_HEADS, HG_KEY_DIM) * (HG_KEY_DIM ** -0.5)
    ih = _heads(i, HG_HEADS, HG_VAL_DIM)
    outs, states = [], []
    for d, f_raw in enumerate((f_fw, f_bw)):
        fr = _heads(f_raw, HG_HEADS, HG_KEY_DIM)
        lbd = lb[d].reshape(HG_HEADS, HG_KEY_DIM)
        logf = jnp.log(lbd + (1.0 - lbd) * jax.nn.sigmoid(fr))
        kf = (1.0 - lbd) * jax.nn.sigmoid(-fr)
        args = (qh, kf, ih, logf)
        if d == 1:
            args = tuple(jnp.flip(t, axis=1) for t in args)
        o, s = _gla_chunked(args[0], args[1], args[2], args[3], s0[d], emit)
        states.append(s)
        if emit:
            outs.append(jnp.flip(o, axis=1) if d == 1 else o)
    if not emit:
        return None, (states[0], states[1])
    o = outs[0] + outs[1]
    o = o * lax.rsqrt(jnp.mean(o * o, axis=-1, keepdims=True) + EPS) * norm_g.astype(jnp.float32)
    o = o.reshape(bsz, l, HG_VWIDTH) * jax.nn.silu(g)
    return o.astype(g.dtype), (states[0], states[1])


def _swiglu(x, wg, wu, wd):
    return (jax.nn.silu(x @ wg) * (x @ wu)) @ wd


def _moe(h, router_w, router_b, wg, wu, wd, sg, su, sd):
    t = h.shape[0]
    scores = jax.nn.sigmoid(h.astype(jnp.float32) @ router_w.astype(jnp.float32))
    sel = scores + router_b.astype(jnp.float32)
    grp = sel.reshape(t, N_GROUPS, N_EXPERTS // N_GROUPS)
    grp_score = jnp.sum(lax.top_k(grp, 2)[0], axis=-1)
    _, top_g = lax.top_k(grp_score, TOPK_GROUPS)
    gmask = jnp.any(top_g[:, :, None] == jnp.arange(N_GROUPS)[None, None, :], axis=1)
    sel = jnp.where(jnp.repeat(gmask, N_EXPERTS // N_GROUPS, axis=1), sel, -jnp.inf)
    _, top_e = lax.top_k(sel, TOP_K)
    gate = jnp.take_along_axis(scores, top_e, axis=1)
    gate = gate / jnp.sum(gate, axis=-1, keepdims=True) * ROUTE_SCALE
    n_assign = t * TOP_K
    flat_e = top_e.reshape(-1)
    flat_tok = jnp.arange(n_assign, dtype=jnp.int32) // TOP_K
    flat_gate = gate.reshape(-1)
    order = jnp.argsort(flat_e)
    e_sorted, tok_sorted, gate_sorted = flat_e[order], flat_tok[order], flat_gate[order]
    counts = jnp.bincount(flat_e, length=N_EXPERTS)
    starts = jnp.cumsum(counts) - counts
    padded = (counts + MOE_BLOCK - 1) // MOE_BLOCK * MOE_BLOCK
    pad_end = jnp.cumsum(padded)
    pad_start = pad_end - padded
    dest = pad_start[e_sorted] + (jnp.arange(n_assign, dtype=jnp.int32) - starts[e_sorted])
    n_blocks = -(-n_assign // MOE_BLOCK) + N_EXPERTS
    tok_buf = jnp.zeros((n_blocks * MOE_BLOCK,), jnp.int32).at[dest].set(tok_sorted).reshape(n_blocks, MOE_BLOCK)
    gate_buf = jnp.zeros((n_blocks * MOE_BLOCK,), jnp.float32).at[dest].set(gate_sorted).reshape(n_blocks, MOE_BLOCK)
    blk_e = jnp.minimum(jnp.searchsorted(pad_end, jnp.arange(n_blocks, dtype=jnp.int32) * MOE_BLOCK, side='right'),
                        N_EXPERTS - 1)
    def step(out, inp):
        tok, gw, e = inp
        y = _swiglu(h[tok], wg[e], wu[e], wd[e]) * gw[:, None].astype(h.dtype)
        return out.at[tok].add(y), None
    routed, _ = lax.scan(step, jnp.zeros_like(h), (tok_buf, gate_buf, blk_e))
    return _swiglu(h, sg, su, sd) + routed


def setup_inputs(seed: int = 0) -> dict:
    key = jax.random.key(seed)
    ks = jax.random.split(key, 32)
    nrm = lambda k, shape, scale: jax.random.normal(k, shape, jnp.float32) * scale
    d = D_MODEL
    return {
        'x': nrm(ks[0], (BATCH, SEQ, d), 1.0),
        'c': nrm(ks[1], (BATCH, d), 1.0),
        'ctx': nrm(ks[2], (BATCH, CTX_LEN, d), 1.0),
        'c_ctx': nrm(ks[3], (d,), 1.0),
        'w_mod': nrm(ks[4], (DEPTH, d, N_MOD * d), 0.5 * d ** -0.5),
        'b_mod': nrm(ks[5], (DEPTH, N_MOD * d), 0.02),
        'norm1_g': 1.0 + nrm(ks[6], (DEPTH, d), 0.05),
        'norm2_g': 1.0 + nrm(ks[7], (DEPTH, d), 0.05),
        'w_in': nrm(ks[8], (DEPTH, d, P_IN), d ** -0.5),
        'rw_conv': jnp.zeros((DEPTH, CONV_K, CONV_K, 3 * RW_WIDTH), jnp.float32).at[:, CONV_K // 2, CONV_K // 2, :].set(1.0)
                   + nrm(ks[9], (DEPTH, CONV_K, CONV_K, 3 * RW_WIDTH), 0.15),
        'rw_w0': jax.random.uniform(ks[10], (DEPTH, 2, RW_WIDTH), jnp.float32, -6.0, -1.0),
        'rw_w2': nrm(ks[11], (DEPTH, 2, RW_DECAY_LORA, RW_WIDTH), 0.1),
        'rw_a0': nrm(ks[12], (DEPTH, 2, RW_WIDTH), 0.3),
        'rw_a2': nrm(ks[13], (DEPTH, 2, RW_AAA_LORA, RW_WIDTH), 0.1),
        'rw_g2': nrm(ks[14], (DEPTH, RW_GATE_LORA, RW_WIDTH), RW_GATE_LORA ** -0.5),
        'rw_k_k': 0.85 + nrm(ks[15], (DEPTH, RW_WIDTH), 0.05),
        'rw_k_a': 1.0 + nrm(ks[16], (DEPTH, RW_WIDTH), 0.05),
        'rw_r_k': nrm(ks[17], (DEPTH, RW_HEADS, RW_HEAD_DIM), 0.1),
        'rw_lnx_g': 1.0 + nrm(ks[18], (DEPTH, RW_WIDTH), 0.05),
        'rw_lnx_b': nrm(ks[19], (DEPTH, RW_WIDTH), 0.02),
        'hg_lb_logits': nrm(ks[20], (2, DEPTH + 1, HG_KWIDTH), 0.5),
        'hg_norm_g': 1.0 + nrm(ks[21], (DEPTH, HG_VAL_DIM), 0.05),
        'w_out': nrm(ks[22], (DEPTH, D_MIX, d), D_MIX ** -0.5),
        'router_w': nrm(ks[23], (DEPTH, d, N_EXPERTS), d ** -0.5),
        'router_b': nrm(ks[24], (DEPTH, N_EXPERTS), 0.01),
        'exp_w_gate': nrm(ks[25], (DEPTH, N_EXPERTS, d, D_EXPERT), d ** -0.5),
        'exp_w_up': nrm(ks[26], (DEPTH, N_EXPERTS, d, D_EXPERT), d ** -0.5),
        'exp_w_down': nrm(ks[27], (DEPTH, N_EXPERTS, D_EXPERT, d), D_EXPERT ** -0.5),
        'sh_w_gate': nrm(ks[28], (DEPTH, d, D_SHARED), d ** -0.5),
        'sh_w_up': nrm(ks[29], (DEPTH, d, D_SHARED), d ** -0.5),
        'sh_w_down': nrm(ks[30], (DEPTH, D_SHARED, d), D_SHARED ** -0.5),
        'final_norm_g': 1.0 + nrm(ks[31], (d,), 0.05),
    }


def reference(x, c, ctx, c_ctx, w_mod, b_mod, norm1_g, norm2_g, w_in, rw_conv, rw_w0, rw_w2, rw_a0, rw_a2,
              rw_g2, rw_k_k, rw_k_a, rw_r_k, rw_lnx_g, rw_lnx_b, hg_lb_logits, hg_norm_g, w_out, router_w,
              router_b, exp_w_gate, exp_w_up, exp_w_down, sh_w_gate, sh_w_up, sh_w_down, final_norm_g):
    bsz, seq, d = x.shape
    rows = seq // GRID_W
    ctx_len = ctx.shape[1]
    lb_all = jnp.cumsum(jax.nn.softmax(hg_lb_logits.astype(jnp.float32), axis=1), axis=1)
    zero_rw = jnp.zeros((bsz, RW_HEADS, RW_HEAD_DIM, RW_HEAD_DIM), jnp.float32)
    zero_hg = jnp.zeros((bsz, HG_HEADS, HG_KEY_DIM, HG_VAL_DIM), jnp.float32)
    for l in range(DEPTH):
        last = l == DEPTH - 1
        mod_lat = (jax.nn.silu(c) @ w_mod[l] + b_mod[l])[:, None, :]
        mod_ctx = (jax.nn.silu(c_ctx) @ w_mod[l] + b_mod[l])[None, None, :]
        sh1, sc1, g1, sh2, sc2, g2 = jnp.split(mod_lat, N_MOD, axis=-1)
        csh1, csc1, cg1, csh2, csc2, cg2 = jnp.split(mod_ctx, N_MOD, axis=-1)
        h_lat = _rmsnorm(x, norm1_g[l]) * (1.0 + sc1) + sh1
        h_ctx = _rmsnorm(ctx, norm1_g[l]) * (1.0 + csc1) + csh1
        cols_lat = _split_cols(h_lat @ w_in[l])
        cols_ctx = _split_cols(h_ctx @ w_in[l])
        kern = rw_conv[l]
        rw_p = (rw_w0[l], rw_w2[l], rw_a0[l], rw_a2[l], rw_g2[l], rw_k_k[l], rw_k_a[l], rw_r_k[l],
                rw_lnx_g[l], rw_lnx_b[l])
        lb_l = lb_all[:, l]
        y_rw_ctx, s_rw = _rwkv7_mix(cols_ctx[:6], lambda u: _conv_context(u, kern), (zero_rw, zero_rw), *rw_p,
                                    emit=not last)
        y_rw_lat, _ = _rwkv7_mix(cols_lat[:6], lambda u: _conv_latent(u, kern, rows), s_rw, *rw_p, emit=True)
        y_hg_ctx, s_hg = _hgrn2_mix(cols_ctx[6:], (zero_hg, zero_hg), lb_l, hg_norm_g[l], emit=not last)
        y_hg_lat, _ = _hgrn2_mix(cols_lat[6:], s_hg, lb_l, hg_norm_g[l], emit=True)
        x = x + g1 * (jnp.concatenate([y_rw_lat, y_hg_lat], axis=-1) @ w_out[l])
        moe_p = (router_w[l], router_b[l], exp_w_gate[l], exp_w_up[l], exp_w_down[l],
                 sh_w_gate[l], sh_w_up[l], sh_w_down[l])
        h2_lat = _rmsnorm(x, norm2_g[l]) * (1.0 + sc2) + sh2
        if last:
            x = x + g2 * _moe(h2_lat.reshape(-1, d), *moe_p).reshape(bsz, seq, d)
        else:
            ctx = ctx + cg1 * (jnp.concatenate([y_rw_ctx, y_hg_ctx], axis=-1) @ w_out[l])
            h2_ctx = _rmsnorm(ctx, norm2_g[l]) * (1.0 + csc2) + csh2
            tokens = jnp.concatenate([h2_ctx, h2_lat], axis=1).reshape(-1, d)
            f = _moe(tokens, *moe_p).reshape(bsz, ctx_len + seq, d)
            ctx = ctx + cg2 * f[:, :ctx_len]
            x = x + g2 * f[:, ctx_len:]
    return _rmsnorm(x, final_norm_g)
```

```python
import functools

import numpy as np
import jax
import jax.numpy as jnp
from jax import lax
from jax.experimental import pallas as pl
from jax.experimental.pallas import tpu as pltpu

D_MODEL = 1024
GRID_W = 64
N_MOD = 6
EPS = 1e-6

RW_HEAD_DIM = 64
RW_HEADS = 8
RW_WIDTH = 512
RW_LORA = 64
RW_GATE_LORA = 128
RW_GN_EPS = 1e-5 * RW_HEAD_DIM
CONV_K = 3

HG_HEADS = 8
HG_DIM = 64
HG_WIDTH = 512
HG_CHUNK = 32

RKV_W = 3 * RW_WIDTH
LORA_W = 4 * RW_LORA + RW_GATE_LORA
HG_W = 5 * HG_WIDTH
P_IN = RKV_W + LORA_W + HG_W

N_EXPERTS = 64
N_GROUPS = 8
TOPK_GROUPS = 4
TOP_K = 8
D_EXPERT = 256
ROUTE_SCALE = 2.5

LANES = 128
VMEM_LIMIT = 48 * 1024 * 1024

_HI = lax.Precision.HIGHEST
_F32 = jnp.float32
_BF16 = jnp.bfloat16


def _cparams(sem):
    return pltpu.CompilerParams(dimension_semantics=sem, vmem_limit_bytes=VMEM_LIMIT)


def _mod_kernel(c_ref, w_ref, b_ref, o_ref):
    c = c_ref[...]
    s = c * jax.nn.sigmoid(c)
    o_ref[...] = jnp.dot(s, w_ref[...], preferred_element_type=_F32, precision=_HI) + b_ref[...]


def _modulation(c_all, w_mod, b_mod):
    rows, d = c_all.shape
    n = w_mod.shape[1]
    tn = 1024
    return pl.pallas_call(
        _mod_kernel,
        grid=(n // tn,),
        in_specs=[pl.BlockSpec((rows, d), lambda j: (0, 0)),
                  pl.BlockSpec((d, tn), lambda j: (0, j)),
                  pl.BlockSpec((1, tn), lambda j: (0, j))],
        out_specs=pl.BlockSpec((rows, tn), lambda j: (0, j)),
        out_shape=jax.ShapeDtypeStruct((rows, n), _F32),
        compiler_params=_cparams(("parallel",)),
        name="modulation",
    )(c_all, w_mod, b_mod.reshape(1, n))


def _inproj_kernel(x_ref, g_ref, sc_ref, sh_ref, w_ref, rkv_ref, lora_ref, hg_ref):
    x = x_ref[0]
    y = x * lax.rsqrt(jnp.mean(x * x, axis=-1, keepdims=True) + EPS) * g_ref[...]
    h = y * (1.0 + sc_ref[0]) + sh_ref[0]
    u = jnp.dot(h.astype(_BF16), w_ref[...], preferred_element_type=_F32)
    rkv_ref[0] = u[:, :RKV_W]
    lora_ref[0] = u[:, RKV_W:RKV_W + LORA_W]
    hg_ref[0] = u[:, RKV_W + LORA_W:]


def _inproj(x, norm_g, sc, sh, w_in_bf16, tm):
    bsz, l, d = x.shape
    tm = min(tm, l)
    row = lambda b, i: (b, i, 0)
    per_b = lambda b, i: (b, 0, 0)
    return pl.pallas_call(
        _inproj_kernel,
        grid=(bsz, l // tm),
        in_specs=[pl.BlockSpec((1, tm, d), row),
                  pl.BlockSpec((1, d), lambda b, i: (0, 0)),
                  pl.BlockSpec((1, 1, d), per_b),
                  pl.BlockSpec((1, 1, d), per_b),
                  pl.BlockSpec((d, P_IN), lambda b, i: (0, 0))],
        out_specs=[pl.BlockSpec((1, tm, RKV_W), row),
                   pl.BlockSpec((1, tm, LORA_W), row),
                   pl.BlockSpec((1, tm, HG_W), row)],
        out_shape=[jax.ShapeDtypeStruct((bsz, l, RKV_W), _F32),
                   jax.ShapeDtypeStruct((bsz, l, LORA_W), _F32),
                   jax.ShapeDtypeStruct((bsz, l, HG_W), _F32)],
        compiler_params=_cparams(("parallel", "parallel")),
        name="inproj",
    )(x, norm_g.reshape(1, d), sc, sh, w_in_bf16)


def _rwkv_scan_kernel(r_ref, w_ref, k_ref, v_ref, a_ref, b_ref, y_ref, s_ref, *, tt):
    @pl.when(pl.program_id(1) == 0)
    def _():
        s_ref[...] = jnp.zeros_like(s_ref)

    n = RW_HEAD_DIM

    def step(t, carry):
        sa0 = jnp.zeros((n, LANES), _F32)
        sa1 = jnp.zeros((n, LANES), _F32)
        for k in range(0, n, 2):
            sa0 = sa0 + s_ref[k] * a_ref[t, pl.ds(k, 1), :]
            sa1 = sa1 + s_ref[k + 1] * a_ref[t, pl.ds(k + 1, 1), :]
        sa = sa0 + sa1
        vv = v_ref[t]
        y0 = jnp.zeros((n, LANES), _F32)
        y1 = jnp.zeros((n, LANES), _F32)
        for k in range(0, n, 2):
            s0 = (s_ref[k] * w_ref[t, pl.ds(k, 1), :] + sa * b_ref[t, pl.ds(k, 1), :]
                  + vv * k_ref[t, pl.ds(k, 1), :])
            s_ref[k] = s0
            y0 = y0 + s0 * r_ref[t, pl.ds(k, 1), :]
            s1 = (s_ref[k + 1] * w_ref[t, pl.ds(k + 1, 1), :] + sa * b_ref[t, pl.ds(k + 1, 1), :]
                  + vv * k_ref[t, pl.ds(k + 1, 1), :])
            s_ref[k + 1] = s1
            y1 = y1 + s1 * r_ref[t, pl.ds(k + 1, 1), :]
        y_ref[t] = y0 + y1
        return carry

    lax.fori_loop(0, tt, step, 0)


def _rwkv_scan(r, w, k, v, a, b, tt):
    nd, l, n, c = r.shape
    spec = pl.BlockSpec((None, tt, n, c), lambda d, i: (d, i, 0, 0))
    return pl.pallas_call(
        functools.partial(_rwkv_scan_kernel, tt=tt),
        grid=(nd, l // tt),
        in_specs=[spec] * 6,
        out_specs=spec,
        out_shape=jax.ShapeDtypeStruct((nd, l, n, c), _F32),
        scratch_shapes=[pltpu.VMEM((n, n, c), _F32)],
        compiler_params=_cparams(("parallel", "arbitrary")),
        name="rwkv_scan",
    )(r, w, k, v, a, b)


def _hgrn_kernel(q_ref, k_ref, v_ref, lf_ref, o_ref, st_ref, *, n_chunks):
    c = HG_CHUNK
    st_ref[...] = jnp.zeros_like(st_ref)
    row = lax.broadcasted_iota(jnp.int32, (c, c), 0)
    col = lax.broadcasted_iota(jnp.int32, (c, c), 1)
    causal = row >= col
    tri = causal.astype(_F32)
    nt = (((1,), (1,)), ((), ()))
    tn = (((0,), (0,)), ((), ()))

    def body(j, carry):
        sl = pl.ds(pl.multiple_of(j * c, c), c)
        lf = lf_ref[sl, :]
        cum = jnp.dot(tri, lf, preferred_element_type=_F32, precision=_HI)
        last = cum[c - 1:c, :]
        mid = cum[c // 2:c // 2 + 1, :]
        q = q_ref[sl, :]
        k = k_ref[sl, :]
        v = v_ref[sl, :]
        scores = lax.dot_general(q * jnp.exp(cum - mid), k * jnp.exp(mid - cum), nt,
                                 preferred_element_type=_F32, precision=_HI)
        scores = jnp.where(causal, scores, 0.0)
        st = st_ref[...]
        o = (jnp.dot(scores, v, preferred_element_type=_F32, precision=_HI)
             + lax.dot_general(q * jnp.exp(cum), st, nt, preferred_element_type=_F32, precision=_HI))
        o_ref[sl, :] = o
        kdec = k * jnp.exp(last - cum)
        st_ref[...] = st * jnp.exp(last) + lax.dot_general(v, kdec, tn, preferred_element_type=_F32,
                                                           precision=_HI)
        return carry

    lax.fori_loop(0, n_chunks, body, 0)


def _hgrn_scan(q, k, v, lf):
    nd, bsz, h, l, n = q.shape
    spec = pl.BlockSpec((None, None, None, l, n), lambda d, b, hh: (d, b, hh, 0, 0))
    return pl.pallas_call(
        functools.partial(_hgrn_kernel, n_chunks=l // HG_CHUNK),
        grid=(nd, bsz, h),
        in_specs=[spec] * 4,
        out_specs=spec,
        out_shape=jax.ShapeDtypeStruct((nd, bsz, h, l, n), _F32),
        scratch_shapes=[pltpu.VMEM((n, n), _F32)],
        compiler_params=_cparams(("parallel", "parallel", "parallel")),
        name="hgrn_scan",
    )(q, k, v, lf)


def _post_kernel(mix_ref, x_ref, w_ref, g1_ref, n2_ref, sc_ref, sh_ref, rw_ref, x1_ref, h2_ref, lg_ref):
    y = jnp.dot(mix_ref[0].astype(_BF16), w_ref[...], preferred_element_type=_F32)
    x1 = x_ref[0] + g1_ref[0] * y
    x1_ref[0] = x1
    h2 = x1 * lax.rsqrt(jnp.mean(x1 * x1, axis=-1, keepdims=True) + EPS) * n2_ref[...]
    h2 = h2 * (1.0 + sc_ref[0]) + sh_ref[0]
    h2_ref[0] = h2.astype(_BF16)
    lg_ref[0] = jnp.dot(h2, rw_ref[...], preferred_element_type=_F32, precision=_HI)


def _post(mix, x, w_out_bf16, g1, norm2_g, sc2, sh2, router_w_pad, tm):
    bsz, l, d = x.shape
    tm = min(tm, l)
    row = lambda b, i: (b, i, 0)
    per_b = lambda b, i: (b, 0, 0)
    const = lambda b, i: (0, 0)
    return pl.pallas_call(
        _post_kernel,
        grid=(bsz, l // tm),
        in_specs=[pl.BlockSpec((1, tm, d), row),
                  pl.BlockSpec((1, tm, d), row),
                  pl.BlockSpec((d, d), const),
                  pl.BlockSpec((1, 1, d), per_b),
                  pl.BlockSpec((1, d), const),
                  pl.BlockSpec((1, 1, d), per_b),
                  pl.BlockSpec((1, 1, d), per_b),
                  pl.BlockSpec((d, LANES), const)],
        out_specs=[pl.BlockSpec((1, tm, d), row),
                   pl.BlockSpec((1, tm, d), row),
                   pl.BlockSpec((1, tm, LANES), row)],
        out_shape=[jax.ShapeDtypeStruct((bsz, l, d), _F32),
                   jax.ShapeDtypeStruct((bsz, l, d), _BF16),
                   jax.ShapeDtypeStruct((bsz, l, LANES), _F32)],
        compiler_params=_cparams(("parallel", "parallel")),
        name="outproj_norm2_router",
    )(mix, x, w_out_bf16, g1, norm2_g.reshape(1, d), sc2, sh2, router_w_pad)


def _moe_kernel(h_ref, gsp_ref, x1_ref, g2_ref, fg_ref, wg_ref, wu_ref, wd_ref, o_ref, acc_ref):
    e = pl.program_id(2)
    h = h_ref[0]
    a = jnp.dot(h, wg_ref[0], preferred_element_type=_F32)
    u = jnp.dot(h, wu_ref[0], preferred_element_type=_F32)
    act = a * jax.nn.sigmoid(a) * u
    rid = lax.broadcasted_iota(jnp.int32, (LANES, D_EXPERT), 0)
    onehot = jnp.where((rid == e) | (rid == e + N_EXPERTS), 1.0, 0.0).astype(_BF16)
    gate = jnp.dot(gsp_ref[0], onehot, preferred_element_type=_F32)
    gate = jnp.where(e == N_EXPERTS, 1.0, gate)
    y = jnp.dot((act * gate).astype(_BF16), wd_ref[0], preferred_element_type=_F32)

    @pl.when(e == 0)
    def _():
        acc_ref[...] = y

    @pl.when(e > 0)
    def _():
        acc_ref[...] += y

    @pl.when(e == N_EXPERTS)
    def _():
        x2 = x1_ref[0] + g2_ref[0] * acc_ref[...]
        o_ref[0] = x2 * lax.rsqrt(jnp.mean(x2 * x2, axis=-1, keepdims=True) + EPS) * fg_ref[...]


def _moe(h2, gate_split, x1, g2, final_g, wg, wu, wd, tm):
    bsz, l, d = x1.shape
    tm = min(tm, l)
    row = lambda b, i, e: (b, i, 0)
    per_b = lambda b, i, e: (b, 0, 0)
    return pl.pallas_call(
        _moe_kernel,
        grid=(bsz, l // tm, N_EXPERTS + 1),
        in_specs=[pl.BlockSpec((1, tm, d), row),
                  pl.BlockSpec((1, tm, LANES), row),
                  pl.BlockSpec((1, tm, d), row),
                  pl.BlockSpec((1, 1, d), per_b),
                  pl.BlockSpec((1, d), lambda b, i, e: (0, 0)),
                  pl.BlockSpec((1, d, D_EXPERT), lambda b, i, e: (e, 0, 0)),
                  pl.BlockSpec((1, d, D_EXPERT), lambda b, i, e: (e, 0, 0)),
                  pl.BlockSpec((1, D_EXPERT, d), lambda b, i, e: (e, 0, 0))],
        out_specs=pl.BlockSpec((1, tm, d), row),
        out_shape=jax.ShapeDtypeStruct((bsz, l, d), _F32),
        scratch_shapes=[pltpu.VMEM((tm, d), _F32)],
        compiler_params=_cparams(("parallel", "parallel", "arbitrary")),
        name="moe_experts_final_norm",
    )(h2, gate_split, x1, g2, final_g.reshape(1, d), wg, wu, wd)


def _heads(t):
    return t.reshape(t.shape[0], t.shape[1], RW_HEADS, RW_HEAD_DIM)


def _conv_latent(u, kern, rows):
    b, l, ch = u.shape
    u2 = u.reshape(b, rows, GRID_W, ch)
    out = lax.conv_general_dilated(u2, kern[:, :, None, :], window_strides=(1, 1), padding='SAME',
                                   dimension_numbers=('NHWC', 'HWIO', 'NHWC'), feature_group_count=ch,
                                   precision=_HI)
    return out.reshape(b, l, ch)


def _conv_context(u, kern):
    ch = u.shape[-1]
    return lax.conv_general_dilated(u, kern[CONV_K // 2][:, None, :], window_strides=(1,), padding='SAME',
                                    dimension_numbers=('NWC', 'WIO', 'NWC'), feature_group_count=ch,
                                    precision=_HI)


def _rw_prep(rkv, lora, p):
    w0, w2, a0, a2, k_k, k_a = p
    r, k, v = jnp.split(rkv, 3, axis=-1)
    kk = _heads(k * k_k)
    kk = (kk / jnp.maximum(jnp.sqrt(jnp.sum(kk * kk, axis=-1, keepdims=True)), 1e-12)).reshape(k.shape)
    out = []
    for d in range(2):
        wl = lora[..., d * RW_LORA:(d + 1) * RW_LORA]
        al = lora[..., 2 * RW_LORA + d * RW_LORA:2 * RW_LORA + (d + 1) * RW_LORA]
        w = -jax.nn.softplus(-(w0[d] + jnp.dot(jnp.tanh(wl), w2[d], precision=_HI))) - 0.5
        decay = jnp.exp(-jnp.exp(w))
        a = jax.nn.sigmoid(a0[d] + jnp.dot(al, a2[d], precision=_HI))
        kd = k * (1.0 + (a - 1.0) * k_a)
        out.append(dict(r=r, w=decay, k=kd, v=v, a=-kk, b=kk * a))
    return out


def _to_chain_lanes(t):
    b, l, _ = t.shape
    return t.reshape(b, l, RW_HEADS, RW_HEAD_DIM).transpose(1, 3, 0, 2).reshape(l, RW_HEAD_DIM, b * RW_HEADS)


def _from_chain_lanes(t, b):
    l = t.shape[0]
    return t.reshape(l, RW_HEAD_DIM, b, RW_HEADS).transpose(2, 0, 3, 1).reshape(b, l, RW_WIDTH)


def _scan_order(ctx_t, lat_t, d):
    if d == 1:
        ctx_t, lat_t = ctx_t[:, ::-1], lat_t[:, ::-1]
    return jnp.concatenate([ctx_t, lat_t], axis=1)


def _routing(logits, router_b):
    t = logits.shape[0]
    scores = jax.nn.sigmoid(logits)
    sel = scores + router_b
    grp = sel.reshape(t, N_GROUPS, N_EXPERTS // N_GROUPS)
    grp_score = jnp.sum(lax.top_k(grp, 2)[0], axis=-1)
    _, top_g = lax.top_k(grp_score, TOPK_GROUPS)
    gmask = jnp.any(top_g[:, :, None] == jnp.arange(N_GROUPS)[None, None, :], axis=1)
    sel = jnp.where(jnp.repeat(gmask, N_EXPERTS // N_GROUPS, axis=1), sel, -jnp.inf)
    _, top_e = lax.top_k(sel, TOP_K)
    gate = jnp.take_along_axis(scores, top_e, axis=1)
    gate = gate / jnp.sum(gate, axis=-1, keepdims=True) * ROUTE_SCALE
    onehot = top_e[:, :, None] == jnp.arange(N_EXPERTS)[None, None, :]
    return jnp.sum(jnp.where(onehot, gate[:, :, None], 0.0), axis=1)


def kernel(x, c, ctx, c_ctx, w_mod, b_mod, norm1_g, norm2_g, w_in, rw_conv, rw_w0, rw_w2, rw_a0, rw_a2,
           rw_g2, rw_k_k, rw_k_a, rw_r_k, rw_lnx_g, rw_lnx_b, hg_lb_logits, hg_norm_g, w_out, router_w,
           router_b, exp_w_gate, exp_w_up, exp_w_down, sh_w_gate, sh_w_up, sh_w_down, final_norm_g):
    bsz, seq, d = x.shape
    ctx_len = ctx.shape[1]
    rows = seq // GRID_W
    assert w_mod.shape[0] == 1 and bsz * RW_HEADS == LANES
    lyr = 0

    c_all = jnp.concatenate([c, c_ctx[None, :], jnp.zeros((7, d), _F32)], axis=0)
    mod = _modulation(c_all, w_mod[lyr], b_mod[lyr])
    sh1, sc1, g1, sh2, sc2, g2 = [m[:, None, :] for m in jnp.split(mod[:bsz], N_MOD, axis=-1)]
    mod_ctx = jnp.broadcast_to(mod[bsz][None, None, :], (bsz, 1, N_MOD * d))
    csh1, csc1 = mod_ctx[..., :d], mod_ctx[..., d:2 * d]

    w_in_bf16 = w_in[lyr].astype(_BF16)
    rkv_lat, lora_lat, hg_lat = _inproj(x, norm1_g[lyr], sc1, sh1, w_in_bf16, tm=256)
    rkv_ctx, lora_ctx, hg_ctx = _inproj(ctx, norm1_g[lyr], csc1, csh1, w_in_bf16, tm=256)

    kern = rw_conv[lyr]
    rw_p = (rw_w0[lyr], rw_w2[lyr], rw_a0[lyr], rw_a2[lyr], rw_k_k[lyr], rw_k_a[lyr])
    prep_lat = _rw_prep(_conv_latent(rkv_lat, kern, rows), lora_lat, rw_p)
    prep_ctx = _rw_prep(_conv_context(rkv_ctx, kern), lora_ctx, rw_p)
    scan_in = [jnp.stack([_to_chain_lanes(_scan_order(prep_ctx[dd][nm], prep_lat[dd][nm], dd))
                          for dd in range(2)])
               for nm in ('r', 'w', 'k', 'v', 'a', 'b')]
    y_scan = _rwkv_scan(*scan_in, tt=16)[:, ctx_len:]
    y = _from_chain_lanes(y_scan[0] + y_scan[1, ::-1], bsz)
    yh = _heads(y)
    mu = jnp.mean(yh, axis=-1, keepdims=True)
    var = jnp.mean(jnp.square(yh - mu), axis=-1, keepdims=True)
    yn = ((yh - mu) * lax.rsqrt(var + RW_GN_EPS)).reshape(bsz, seq, RW_WIDTH) * rw_lnx_g[lyr] + rw_lnx_b[lyr]
    rh, vh = _heads(prep_lat[0]['r']), _heads(prep_lat[0]['v'])
    bonus = sum(jnp.sum(rh * _heads(prep_lat[dd]['k']) * rw_r_k[lyr], axis=-1, keepdims=True) * vh
                for dd in range(2))
    yn = yn + bonus.reshape(bsz, seq, RW_WIDTH)
    gd = lora_lat[..., 4 * RW_LORA:]
    y_rw = yn * jnp.dot(jax.nn.sigmoid(gd), rw_g2[lyr], precision=_HI)

    lb = jnp.cumsum(jax.nn.softmax(hg_lb_logits.astype(_F32), axis=1), axis=1)[:, lyr]

    def hg_parts(cols):
        q, f_fw, f_bw, i, g = jnp.split(cols, 5, axis=-1)
        qs = jax.nn.silu(q) * (HG_DIM ** -0.5)
        per_dir = []
        for dd, fr in enumerate((f_fw, f_bw)):
            logf = jnp.log(lb[dd] + (1.0 - lb[dd]) * jax.nn.sigmoid(fr))
            kf = (1.0 - lb[dd]) * jax.nn.sigmoid(-fr)
            per_dir.append((qs, kf, i, logf))
        return per_dir, g

    hg_l, g_lat = hg_parts(hg_lat)
    hg_c, _ = hg_parts(hg_ctx)

    def chain_major(t):
        return t.reshape(bsz, t.shape[1], HG_HEADS, HG_DIM).transpose(0, 2, 1, 3)

    hg_in = [jnp.stack([chain_major(_scan_order(hg_c[dd][j], hg_l[dd][j], dd)) for dd in range(2)])
             for j in range(4)]
    o_scan = _hgrn_scan(*hg_in)[:, :, :, ctx_len:]
    o = (o_scan[0] + o_scan[1, :, :, ::-1]).transpose(0, 2, 1, 3)
    o = o * lax.rsqrt(jnp.mean(o * o, axis=-1, keepdims=True) + EPS) * hg_norm_g[lyr]
    y_hg = o.reshape(bsz, seq, HG_WIDTH) * jax.nn.silu(g_lat)

    mix = jnp.concatenate([y_rw, y_hg], axis=-1)
    router_w_pad = jnp.pad(router_w[lyr], ((0, 0), (0, LANES - N_EXPERTS)))
    x1, h2, logits = _post(mix, x, w_out[lyr].astype(_BF16), g1, norm2_g[lyr], sc2, sh2, router_w_pad,
                           tm=512)
    gate = _routing(logits.reshape(bsz * seq, LANES)[:, :N_EXPERTS], router_b[lyr])
    gate_hi = gate.astype(_BF16)
    gate_lo = (gate - gate_hi.astype(_F32)).astype(_BF16)
    gate_split = jnp.concatenate([gate_hi, gate_lo], axis=-1).reshape(bsz, seq, LANES)

    wg = jnp.concatenate([exp_w_gate[lyr], sh_w_gate[lyr][None]], axis=0).astype(_BF16)
    wu = jnp.concatenate([exp_w_up[lyr], sh_w_up[lyr][None]], axis=0).astype(_BF16)
    wd = jnp.concatenate([exp_w_down[lyr], sh_w_down[lyr][None]], axis=0).astype(_BF16)
    return _moe(h2, gate_split, x1, g2, final_norm_g, wg, wu, wd, tm=1024)
```

```python
import functools

import jax
import jax.numpy as jnp
from jax import lax
from jax.experimental import pallas as pl
from jax.experimental.pallas import tpu as pltpu

D_MODEL = 1024
GRID_W = 64
N_MOD = 6
EPS = 1e-6

RW_HEAD_DIM = 64
RW_HEADS = 8
RW_WIDTH = 512
RW_LORA = 64
RW_GATE_LORA = 128
RW_GN_EPS = 1e-5 * RW_HEAD_DIM
CONV_K = 3

HG_HEADS = 8
HG_DIM = 64
HG_WIDTH = 512
HG_CHUNK = 32
HG_PAIRS = HG_HEADS // 2

RKV_W = 3 * RW_WIDTH
LORA_W = 4 * RW_LORA + RW_GATE_LORA
HG_W = 5 * HG_WIDTH
P_IN = RKV_W + LORA_W + HG_W

N_EXPERTS = 64
N_GROUPS = 8
GROUP_SIZE = N_EXPERTS // N_GROUPS
TOPK_GROUPS = 4
TOP_K = 8
D_EXPERT = 256
ROUTE_SCALE = 2.5

LANES = 128
SUBLANES = 8
VMEM_LIMIT = 48 * 1024 * 1024

_HI = lax.Precision.HIGHEST
_F32 = jnp.float32
_BF16 = jnp.bfloat16
_NT = (((1,), (1,)), ((), ()))
_TN = (((0,), (0,)), ((), ()))


def _cparams(sem):
    return pltpu.CompilerParams(dimension_semantics=sem, vmem_limit_bytes=VMEM_LIMIT)


def _dir_block(d, i, nb):
    return i + d * (nb - 1 - 2 * i)


def _mod_kernel(c_ref, w_ref, b_ref, o_ref):
    c = c_ref[...]
    s = c * jax.nn.sigmoid(c)
    o_ref[...] = jnp.dot(s, w_ref[...], preferred_element_type=_F32, precision=_HI) + b_ref[...]


def _modulation(c_all, w_mod, b_mod):
    rows, d = c_all.shape
    n = w_mod.shape[1]
    tn = 1024
    return pl.pallas_call(
        _mod_kernel,
        grid=(n // tn,),
        in_specs=[pl.BlockSpec((rows, d), lambda j: (0, 0)),
                  pl.BlockSpec((d, tn), lambda j: (0, j)),
                  pl.BlockSpec((1, tn), lambda j: (0, j))],
        out_specs=pl.BlockSpec((rows, tn), lambda j: (0, j)),
        out_shape=jax.ShapeDtypeStruct((rows, n), _F32),
        compiler_params=_cparams(("parallel",)),
        name="modulation",
    )(c_all, w_mod, b_mod.reshape(1, n))


def _inproj_kernel(x_ref, g_ref, sc_ref, sh_ref, w_ref, rkv_ref, lora_ref, hg_ref):
    x = x_ref[0]
    y = x * lax.rsqrt(jnp.mean(x * x, axis=-1, keepdims=True) + EPS) * g_ref[...]
    h = y * (1.0 + sc_ref[0]) + sh_ref[0]
    u = jnp.dot(h.astype(_BF16), w_ref[...], preferred_element_type=_F32)
    rkv_ref[0] = u[:, :RKV_W]
    lora_ref[0] = u[:, RKV_W:RKV_W + LORA_W]
    hg_ref[0] = u[:, RKV_W + LORA_W:]


def _inproj(x, norm_g, sc, sh, w_in_bf16, tm):
    bsz, l, d = x.shape
    tm = min(tm, l)
    row = lambda b, i: (b, i, 0)
    per_b = lambda b, i: (b, 0, 0)
    return pl.pallas_call(
        _inproj_kernel,
        grid=(bsz, l // tm),
        in_specs=[pl.BlockSpec((1, tm, d), row),
                  pl.BlockSpec((1, d), lambda b, i: (0, 0)),
                  pl.BlockSpec((1, 1, d), per_b),
                  pl.BlockSpec((1, 1, d), per_b),
                  pl.BlockSpec((d, P_IN), lambda b, i: (0, 0))],
        out_specs=[pl.BlockSpec((1, tm, RKV_W), row),
                   pl.BlockSpec((1, tm, LORA_W), row),
                   pl.BlockSpec((1, tm, HG_W), row)],
        out_shape=[jax.ShapeDtypeStruct((bsz, l, RKV_W), _F32),
                   jax.ShapeDtypeStruct((bsz, l, LORA_W), _F32),
                   jax.ShapeDtypeStruct((bsz, l, HG_W), _F32)],
        compiler_params=_cparams(("parallel", "parallel")),
        name="inproj",
    )(x, norm_g.reshape(1, d), sc, sh, w_in_bf16)


def _rwkv_scan_kernel(*refs, tt, emit, has_init):
    r_ref, kk_ref, v_ref, w_ref, k_ref, b_ref = refs[:6]
    rest = list(refs[6:])
    s0_ref = rest.pop(0) if has_init else None
    out_ref = rest.pop(0)
    s_ref = rest.pop(0)
    d = pl.program_id(0)
    i = pl.program_id(1)
    n = RW_HEAD_DIM

    @pl.when(i == 0)
    def _():
        s_ref[...] = s0_ref[...] if has_init else jnp.zeros_like(s_ref)

    def step(j, carry):
        t = j + d * (tt - 1 - 2 * j)
        sa0 = jnp.zeros((n, LANES), _F32)
        sa1 = jnp.zeros((n, LANES), _F32)
        for k in range(0, n, 2):
            sa0 = sa0 + s_ref[k] * kk_ref[t, pl.ds(k, 1), :]
            sa1 = sa1 + s_ref[k + 1] * kk_ref[t, pl.ds(k + 1, 1), :]
        sa = -(sa0 + sa1)
        vv = v_ref[t]
        y0 = jnp.zeros((n, LANES), _F32)
        y1 = jnp.zeros((n, LANES), _F32)
        for k in range(0, n, 2):
            s0 = (s_ref[k] * w_ref[t, pl.ds(k, 1), :] + sa * b_ref[t, pl.ds(k, 1), :]
                  + vv * k_ref[t, pl.ds(k, 1), :])
            s_ref[k] = s0
            s1 = (s_ref[k + 1] * w_ref[t, pl.ds(k + 1, 1), :] + sa * b_ref[t, pl.ds(k + 1, 1), :]
                  + vv * k_ref[t, pl.ds(k + 1, 1), :])
            s_ref[k + 1] = s1
            if emit:
                y0 = y0 + s0 * r_ref[t, pl.ds(k, 1), :]
                y1 = y1 + s1 * r_ref[t, pl.ds(k + 1, 1), :]
        if emit:
            out_ref[t] = y0 + y1
        return carry

    lax.fori_loop(0, tt, step, 0)

    if not emit:
        @pl.when(i == pl.num_programs(1) - 1)
        def _():
            out_ref[...] = s_ref[...]


def _rwkv_scan(shared, per_dir, s0, tt, emit):
    l, n, c = shared[0].shape
    nb = l // tt
    sh_spec = pl.BlockSpec((tt, n, c), lambda d, i: (_dir_block(d, i, nb), 0, 0))
    pd_spec = pl.BlockSpec((None, tt, n, c), lambda d, i: (d, _dir_block(d, i, nb), 0, 0))
    st_spec = pl.BlockSpec((None, n, n, c), lambda d, i: (d, 0, 0, 0))
    in_specs = [sh_spec] * 3 + [pd_spec] * 3
    args = list(shared) + list(per_dir)
    if s0 is not None:
        in_specs.append(st_spec)
        args.append(s0)
    if emit:
        out_spec, out_shape = pd_spec, jax.ShapeDtypeStruct((2, l, n, c), _F32)
    else:
        out_spec, out_shape = st_spec, jax.ShapeDtypeStruct((2, n, n, c), _F32)
    return pl.pallas_call(
        functools.partial(_rwkv_scan_kernel, tt=tt, emit=emit, has_init=s0 is not None),
        grid=(2, nb),
        in_specs=in_specs,
        out_specs=out_spec,
        out_shape=out_shape,
        scratch_shapes=[pltpu.VMEM((n, n, c), _F32)],
        compiler_params=_cparams(("parallel", "arbitrary")),
        name="rwkv_scan_emit" if emit else "rwkv_scan_state",
    )(*args)


def _hgrn_kernel(*refs, n_chunks, emit, has_init):
    q_ref, f_ref, i_ref, lb_ref = refs[:4]
    rest = list(refs[4:])
    s0_ref = rest.pop(0) if has_init else None
    out_ref = rest.pop(0)
    st_ref = rest.pop(0)
    c = HG_CHUNK
    d = pl.program_id(0)
    tb = pl.program_id(2)
    fwd = d == 0

    @pl.when(tb == 0)
    def _():
        st_ref[...] = s0_ref[...] if has_init else jnp.zeros_like(st_ref)

    row = lax.broadcasted_iota(jnp.int32, (c, c), 0)
    col = lax.broadcasted_iota(jnp.int32, (c, c), 1)
    causal = jnp.where(fwd, (row >= col).astype(_F32), (row <= col).astype(_F32))
    tri = causal.astype(_BF16)
    causal2 = jnp.concatenate([causal, causal], axis=0) > 0.5
    head0 = lax.broadcasted_iota(jnp.int32, (c, LANES), 1) < HG_DIM
    same_head = ((lax.broadcasted_iota(jnp.int32, (LANES, LANES), 0) < HG_DIM)
                 == (lax.broadcasted_iota(jnp.int32, (LANES, LANES), 1) < HG_DIM))
    lb = lb_ref[...]
    omlb = 1.0 - lb

    def body(jj, carry):
        cj = jnp.where(fwd, jj, n_chunks - 1 - jj)
        sl = pl.ds(pl.multiple_of(cj * c, c), c)
        fr = f_ref[sl, :]
        logf = jnp.log(lb + omlb * jax.nn.sigmoid(fr))
        kf = omlb * jax.nn.sigmoid(-fr)
        q = q_ref[sl, :]
        qs = q * jax.nn.sigmoid(q) * (HG_DIM ** -0.5)
        v = i_ref[sl, :]
        hi = logf.astype(_BF16)
        lo = (logf - hi.astype(_F32)).astype(_BF16)
        cum = (jnp.dot(tri, hi, preferred_element_type=_F32)
               + jnp.dot(tri, lo, preferred_element_type=_F32))
        tot = jnp.where(fwd, cum[c - 1:c, :], cum[0:1, :])
        dec = jnp.exp(tot)
        kdec = (kf * jnp.exp(tot - cum)).astype(_BF16)
        vb = v.astype(_BF16)
        if emit:
            mid = cum[c // 2:c // 2 + 1, :]
            qd = qs * jnp.exp(cum - mid)
            kd = (kf * jnp.exp(mid - cum)).astype(_BF16)
            qe = (qs * jnp.exp(cum)).astype(_BF16)
        for p in range(HG_PAIRS):
            ls = slice(p * LANES, (p + 1) * LANES)
            st = st_ref[p]
            if emit:
                qd_p = qd[:, ls]
                q2 = jnp.concatenate([jnp.where(head0, qd_p, 0.0), jnp.where(head0, 0.0, qd_p)], axis=0)
                sc = lax.dot_general(q2.astype(_BF16), kd[:, ls], _NT, preferred_element_type=_F32)
                sc = jnp.where(causal2, sc, 0.0)
                oi = jnp.dot(sc.astype(_BF16), vb[:, ls], preferred_element_type=_F32)
                o = jnp.where(head0, oi[:c], oi[c:])
                o = o + lax.dot_general(qe[:, ls], st.astype(_BF16), _NT, preferred_element_type=_F32)
                out_ref[sl, ls] = o
            upd = lax.dot_general(vb[:, ls], kdec[:, ls], _TN, preferred_element_type=_F32)
            st_ref[p] = st * dec[:, ls] + jnp.where(same_head, upd, 0.0)
        return carry

    lax.fori_loop(0, n_chunks, body, 0)

    if not emit:
        @pl.when(tb == pl.num_programs(2) - 1)
        def _():
            out_ref[...] = st_ref[...]


def _hgrn_scan(hg, lb, s0, tb, emit):
    bsz, l, _ = hg.shape
    tb = min(tb, l)
    nb = l // tb
    w = HG_WIDTH
    col = lambda j: pl.BlockSpec((None, tb, w), lambda d, b, i: (b, _dir_block(d, i, nb), j))
    f_spec = pl.BlockSpec((None, tb, w), lambda d, b, i: (b, _dir_block(d, i, nb), 1 + d))
    lb_spec = pl.BlockSpec((None, 1, w), lambda d, b, i: (d, 0, 0))
    st_spec = pl.BlockSpec((None, None, HG_PAIRS, LANES, LANES), lambda d, b, i: (d, b, 0, 0, 0))
    in_specs = [col(0), f_spec, col(3), lb_spec]
    args = [hg, hg, hg, lb]
    if s0 is not None:
        in_specs.append(st_spec)
        args.append(s0)
    if emit:
        out_spec = pl.BlockSpec((None, None, tb, w), lambda d, b, i: (d, b, _dir_block(d, i, nb), 0))
        out_shape = jax.ShapeDtypeStruct((2, bsz, l, w), _F32)
    else:
        out_spec = st_spec
        out_shape = jax.ShapeDtypeStruct((2, bsz, HG_PAIRS, LANES, LANES), _F32)
    return pl.pallas_call(
        functools.partial(_hgrn_kernel, n_chunks=tb // HG_CHUNK, emit=emit, has_init=s0 is not None),
        grid=(2, bsz, nb),
        in_specs=in_specs,
        out_specs=out_spec,
        out_shape=out_shape,
        scratch_shapes=[pltpu.VMEM((HG_PAIRS, LANES, LANES), _F32)],
        compiler_params=_cparams(("parallel", "parallel", "arbitrary")),
        name="hgrn_scan_emit" if emit else "hgrn_scan_state",
    )(*args)


def _post_kernel(mix_ref, x_ref, w_ref, g1_ref, n2_ref, sc_ref, sh_ref, rw_ref, x1_ref, h2_ref, lg_ref):
    y = jnp.dot(mix_ref[0].astype(_BF16), w_ref[...], preferred_element_type=_F32)
    x1 = x_ref[0] + g1_ref[0] * y
    x1_ref[0] = x1
    h2 = x1 * lax.rsqrt(jnp.mean(x1 * x1, axis=-1, keepdims=True) + EPS) * n2_ref[...]
    h2 = h2 * (1.0 + sc_ref[0]) + sh_ref[0]
    h2_ref[0] = h2.astype(_BF16)
    lg_ref[...] = lax.dot_general(rw_ref[...], h2, _NT, preferred_element_type=_F32, precision=_HI)


def _post(mix, x, w_out_bf16, g1, norm2_g, sc2, sh2, router_wt, tm):
    bsz, l, d = x.shape
    tm = min(tm, l)
    nt = l // tm
    row = lambda b, i: (b, i, 0)
    per_b = lambda b, i: (b, 0, 0)
    const = lambda b, i: (0, 0)
    return pl.pallas_call(
        _post_kernel,
        grid=(bsz, nt),
        in_specs=[pl.BlockSpec((1, tm, d), row),
                  pl.BlockSpec((1, tm, d), row),
                  pl.BlockSpec((d, d), const),
                  pl.BlockSpec((1, 1, d), per_b),
                  pl.BlockSpec((1, d), const),
                  pl.BlockSpec((1, 1, d), per_b),
                  pl.BlockSpec((1, 1, d), per_b),
                  pl.BlockSpec((N_EXPERTS, d), const)],
        out_specs=[pl.BlockSpec((1, tm, d), row),
                   pl.BlockSpec((1, tm, d), row),
                   pl.BlockSpec((N_EXPERTS, tm), lambda b, i: (0, b * nt + i))],
        out_shape=[jax.ShapeDtypeStruct((bsz, l, d), _F32),
                   jax.ShapeDtypeStruct((bsz, l, d), _BF16),
                   jax.ShapeDtypeStruct((N_EXPERTS, bsz * l), _F32)],
        compiler_params=_cparams(("parallel", "parallel")),
        name="outproj_norm2_router",
    )(mix, x, w_out_bf16, g1, norm2_g.reshape(1, d), sc2, sh2, router_wt)


def _sublane_all(x, op):
    for s in (4, 2, 1):
        x = op(x, pltpu.roll(x, s, 0))
    return x


def _router_kernel(lg_ref, bias_ref, out_ref):
    ng, gs = N_GROUPS, GROUP_SIZE
    tn = lg_ref.shape[1]
    neg = -jnp.inf
    gidx = lax.broadcasted_iota(jnp.int32, (ng, tn), 0)
    bias = jnp.concatenate([bias_ref[...]] * (tn // LANES), axis=1)
    scores = [jax.nn.sigmoid(lg_ref[j * ng:(j + 1) * ng, :]) for j in range(gs)]
    sel = [scores[j] + bias[j * ng:(j + 1) * ng, :] for j in range(gs)]

    m1 = functools.reduce(jnp.maximum, sel)
    cnt = functools.reduce(jnp.add, [(s == m1).astype(_F32) for s in sel])
    m2 = functools.reduce(jnp.maximum, [jnp.where(s < m1, s, neg) for s in sel])
    gscore = m1 + jnp.where(cnt >= 2.0, m1, m2)

    rank = jnp.zeros((ng, tn), jnp.int32)
    for s in range(1, ng):
        other = pltpu.roll(gscore, s, 0)
        beats = jnp.where(other > gscore, 1, jnp.where((other == gscore) & (gidx >= s), 1, 0))
        rank = rank + beats
    gsel = rank < TOPK_GROUPS

    key = [jnp.where(gsel, s, neg) for s in sel]
    eidx = [gidx * gs + j for j in range(gs)]
    avail = [jnp.ones((ng, tn), jnp.int32) for _ in range(gs)]
    for _ in range(TOP_K):
        cur = functools.reduce(jnp.maximum, [jnp.where(avail[j] > 0, key[j], neg) for j in range(gs)])
        mx = _sublane_all(cur, jnp.maximum)
        cand = [jnp.where((avail[j] > 0) & (key[j] == mx), eidx[j], N_EXPERTS) for j in range(gs)]
        mn = _sublane_all(functools.reduce(jnp.minimum, cand), jnp.minimum)
        avail = [jnp.where(eidx[j] == mn, 0, avail[j]) for j in range(gs)]

    graw = [jnp.where(avail[j] == 0, scores[j], 0.0) for j in range(gs)]
    den = _sublane_all(functools.reduce(jnp.add, graw), jnp.add)
    gate = jnp.concatenate([g / den * ROUTE_SCALE for g in graw], axis=0)
    hi = gate.astype(_BF16).astype(_F32)
    both = jnp.concatenate([hi, gate - hi], axis=0)
    out_ref[...] = both.T.astype(_BF16)


def _router(logits_t, bias_rows, tn):
    t = logits_t.shape[1]
    tn = min(tn, t)
    return pl.pallas_call(
        _router_kernel,
        grid=(t // tn,),
        in_specs=[pl.BlockSpec((N_EXPERTS, tn), lambda i: (0, i)),
                  pl.BlockSpec((N_EXPERTS, LANES), lambda i: (0, 0))],
        out_specs=pl.BlockSpec((tn, LANES), lambda i: (i, 0)),
        out_shape=jax.ShapeDtypeStruct((t, LANES), _BF16),
        compiler_params=_cparams(("parallel",)),
        name="router_topk",
    )(logits_t, bias_rows)


def _moe_kernel(h_ref, gsp_ref, x1_ref, g2_ref, fg_ref, wg_ref, wu_ref, wd_ref, o_ref, acc_ref):
    e = pl.program_id(2)
    h = h_ref[0]
    a = jnp.dot(h, wg_ref[0], preferred_element_type=_F32)
    u = jnp.dot(h, wu_ref[0], preferred_element_type=_F32)
    act = a * jax.nn.sigmoid(a) * u
    col = lax.broadcasted_iota(jnp.int32, (LANES, D_EXPERT), 0) % N_EXPERTS
    expert_of_col = (col % N_GROUPS) * GROUP_SIZE + col // N_GROUPS
    onehot = jnp.where(expert_of_col == e, 1.0, 0.0).astype(_BF16)
    gate = jnp.dot(gsp_ref[0], onehot, preferred_element_type=_F32)
    gate = jnp.where(e == N_EXPERTS, 1.0, gate)
    y = jnp.dot((act * gate).astype(_BF16), wd_ref[0], preferred_element_type=_F32)

    @pl.when(e == 0)
    def _():
        acc_ref[...] = y

    @pl.when(e > 0)
    def _():
        acc_ref[...] += y

    @pl.when(e == N_EXPERTS)
    def _():
        x2 = x1_ref[0] + g2_ref[0] * acc_ref[...]
        o_ref[0] = x2 * lax.rsqrt(jnp.mean(x2 * x2, axis=-1, keepdims=True) + EPS) * fg_ref[...]


def _moe(h2, gate_split, x1, g2, final_g, wg, wu, wd, tm):
    bsz, l, d = x1.shape
    tm = min(tm, l)
    row = lambda b, i, e: (b, i, 0)
    per_b = lambda b, i, e: (b, 0, 0)
    return pl.pallas_call(
        _moe_kernel,
        grid=(bsz, l // tm, N_EXPERTS + 1),
        in_specs=[pl.BlockSpec((1, tm, d), row),
                  pl.BlockSpec((1, tm, LANES), row),
                  pl.BlockSpec((1, tm, d), row),
                  pl.BlockSpec((1, 1, d), per_b),
                  pl.BlockSpec((1, d), lambda b, i, e: (0, 0)),
                  pl.BlockSpec((1, d, D_EXPERT), lambda b, i, e: (e, 0, 0)),
                  pl.BlockSpec((1, d, D_EXPERT), lambda b, i, e: (e, 0, 0)),
                  pl.BlockSpec((1, D_EXPERT, d), lambda b, i, e: (e, 0, 0))],
        out_specs=pl.BlockSpec((1, tm, d), row),
        out_shape=jax.ShapeDtypeStruct((bsz, l, d), _F32),
        scratch_shapes=[pltpu.VMEM((tm, d), _F32)],
        compiler_params=_cparams(("parallel", "parallel", "arbitrary")),
        name="moe_experts_final_norm",
    )(h2, gate_split, x1, g2, final_g.reshape(1, d), wg, wu, wd)


def _heads(t):
    return t.reshape(t.shape[0], t.shape[1], RW_HEADS, RW_HEAD_DIM)


def _conv_latent(u, kern, rows):
    b, l, ch = u.shape
    u2 = u.reshape(b, rows, GRID_W, ch)
    out = lax.conv_general_dilated(u2, kern[:, :, None, :], window_strides=(1, 1), padding='SAME',
                                   dimension_numbers=('NHWC', 'HWIO', 'NHWC'), feature_group_count=ch,
                                   precision=_HI)
    return out.reshape(b, l, ch)


def _conv_context(u, kern):
    ch = u.shape[-1]
    return lax.conv_general_dilated(u, kern[CONV_K // 2][:, None, :], window_strides=(1,), padding='SAME',
                                    dimension_numbers=('NWC', 'WIO', 'NWC'), feature_group_count=ch,
                                    precision=_HI)


def _rw_prep(rkv, lora, p):
    w0, w2, a0, a2, k_k, k_a = p
    r, k, v = jnp.split(rkv, 3, axis=-1)
    kk = _heads(k * k_k)
    kk = (kk / jnp.maximum(jnp.sqrt(jnp.sum(kk * kk, axis=-1, keepdims=True)), 1e-12)).reshape(k.shape)
    per_dir = []
    for d in range(2):
        wl = lora[..., d * RW_LORA:(d + 1) * RW_LORA]
        al = lora[..., 2 * RW_LORA + d * RW_LORA:2 * RW_LORA + (d + 1) * RW_LORA]
        w = -jax.nn.softplus(-(w0[d] + jnp.dot(jnp.tanh(wl), w2[d], precision=_HI))) - 0.5
        decay = jnp.exp(-jnp.exp(w))
        a = jax.nn.sigmoid(a0[d] + jnp.dot(al, a2[d], precision=_HI))
        kd = k * (1.0 + (a - 1.0) * k_a)
        per_dir.append((decay, kd, kk * a))
    return (r, kk, v), per_dir


def _to_chain_lanes(t):
    b, l, _ = t.shape
    return t.reshape(b, l, RW_HEADS, RW_HEAD_DIM).transpose(1, 3, 0, 2).reshape(l, RW_HEAD_DIM, b * RW_HEADS)


def _from_chain_lanes(t, b):
    l = t.shape[0]
    return t.reshape(l, RW_HEAD_DIM, b, RW_HEADS).transpose(2, 0, 3, 1).reshape(b, l, RW_WIDTH)


def _scan_operands(prep):
    shared, per_dir = prep
    return ([_to_chain_lanes(t) for t in shared],
            [jnp.stack([_to_chain_lanes(per_dir[d][j]) for d in range(2)]) for j in range(3)])


def kernel(x, c, ctx, c_ctx, w_mod, b_mod, norm1_g, norm2_g, w_in, rw_conv, rw_w0, rw_w2, rw_a0, rw_a2,
           rw_g2, rw_k_k, rw_k_a, rw_r_k, rw_lnx_g, rw_lnx_b, hg_lb_logits, hg_norm_g, w_out, router_w,
           router_b, exp_w_gate, exp_w_up, exp_w_down, sh_w_gate, sh_w_up, sh_w_down, final_norm_g):
    bsz, seq, d = x.shape
    rows = seq // GRID_W
    assert w_mod.shape[0] == 1 and bsz * RW_HEADS == LANES
    lyr = 0

    c_all = jnp.concatenate([c, c_ctx[None, :], jnp.zeros((SUBLANES - 1, d), _F32)], axis=0)
    mod = _modulation(c_all, w_mod[lyr], b_mod[lyr])
    sh1, sc1, g1, sh2, sc2, g2 = [m[:, None, :] for m in jnp.split(mod[:bsz], N_MOD, axis=-1)]
    mod_ctx = jnp.broadcast_to(mod[bsz][None, None, :], (bsz, 1, N_MOD * d))
    csh1, csc1 = mod_ctx[..., :d], mod_ctx[..., d:2 * d]

    w_in_bf16 = w_in[lyr].astype(_BF16)
    rkv_lat, lora_lat, hg_lat = _inproj(x, norm1_g[lyr], sc1, sh1, w_in_bf16, tm=256)
    rkv_ctx, lora_ctx, hg_ctx = _inproj(ctx, norm1_g[lyr], csc1, csh1, w_in_bf16, tm=256)

    kern = rw_conv[lyr]
    rw_p = (rw_w0[lyr], rw_w2[lyr], rw_a0[lyr], rw_a2[lyr], rw_k_k[lyr], rw_k_a[lyr])
    prep_lat = _rw_prep(_conv_latent(rkv_lat, kern, rows), lora_lat, rw_p)
    prep_ctx = _rw_prep(_conv_context(rkv_ctx, kern), lora_ctx, rw_p)
    s_rw = _rwkv_scan(*_scan_operands(prep_ctx), None, tt=16, emit=False)
    y_scan = _rwkv_scan(*_scan_operands(prep_lat), s_rw, tt=16, emit=True)
    y = _from_chain_lanes(y_scan[0] + y_scan[1], bsz)
    yh = _heads(y)
    mu = jnp.mean(yh, axis=-1, keepdims=True)
    var = jnp.mean(jnp.square(yh - mu), axis=-1, keepdims=True)
    yn = ((yh - mu) * lax.rsqrt(var + RW_GN_EPS)).reshape(bsz, seq, RW_WIDTH) * rw_lnx_g[lyr] + rw_lnx_b[lyr]
    (r_lat, _, v_lat), dir_lat = prep_lat
    rh, vh = _heads(r_lat), _heads(v_lat)
    bonus = sum(jnp.sum(rh * _heads(dir_lat[dd][1]) * rw_r_k[lyr], axis=-1, keepdims=True) * vh
                for dd in range(2))
    yn = yn + bonus.reshape(bsz, seq, RW_WIDTH)
    gd = lora_lat[..., 4 * RW_LORA:]
    y_rw = yn * jnp.dot(jax.nn.sigmoid(gd), rw_g2[lyr], precision=_HI)

    lb = jnp.cumsum(jax.nn.softmax(hg_lb_logits.astype(_F32), axis=1), axis=1)[:, lyr][:, None, :]
    s_hg = _hgrn_scan(hg_ctx, lb, None, tb=256, emit=False)
    o_scan = _hgrn_scan(hg_lat, lb, s_hg, tb=256, emit=True)
    o = (o_scan[0] + o_scan[1]).reshape(bsz, seq, HG_HEADS, HG_DIM)
    o = o * lax.rsqrt(jnp.mean(o * o, axis=-1, keepdims=True) + EPS) * hg_norm_g[lyr]
    y_hg = o.reshape(bsz, seq, HG_WIDTH) * jax.nn.silu(hg_lat[..., 4 * HG_WIDTH:])

    mix = jnp.concatenate([y_rw, y_hg], axis=-1)
    perm = lambda t: t.reshape(N_GROUPS, GROUP_SIZE, -1).transpose(1, 0, 2).reshape(N_EXPERTS, -1)
    router_wt = perm(router_w[lyr].T)
    bias_rows = jnp.broadcast_to(perm(router_b[lyr][:, None]), (N_EXPERTS, LANES))
    x1, h2, logits_t = _post(mix, x, w_out[lyr].astype(_BF16), g1, norm2_g[lyr], sc2, sh2, router_wt, tm=512)
    gate_split = _router(logits_t, bias_rows, tn=512).reshape(bsz, seq, LANES)

    wg = jnp.concatenate([exp_w_gate[lyr], sh_w_gate[lyr][None]], axis=0).astype(_BF16)
    wu = jnp.concatenate([exp_w_up[lyr], sh_w_up[lyr][None]], axis=0).astype(_BF16)
    wd = jnp.concatenate([exp_w_down[lyr], sh_w_down[lyr][None]], axis=0).astype(_BF16)
    return _moe(h2, gate_split, x1, g2, final_norm_g, wg, wu, wd, tm=1024)
```

```python
import functools

import jax
import jax.numpy as jnp
from jax import lax
from jax.experimental import pallas as pl
from jax.experimental.pallas import tpu as pltpu

D_MODEL = 1024
GRID_W = 64
N_MOD = 6
EPS = 1e-6

RW_HEAD_DIM = 64
RW_HEADS = 8
RW_WIDTH = 512
RW_LORA = 64
RW_GATE_LORA = 128
RW_GN_EPS = 1e-5 * RW_HEAD_DIM
CONV_K = 3

HG_HEADS = 8
HG_DIM = 64
HG_WIDTH = 512
HG_CHUNK = 32
HG_PAIRS = HG_HEADS // 2

RKV_W = 3 * RW_WIDTH
LORA_W = 4 * RW_LORA + RW_GATE_LORA
HG_W = 5 * HG_WIDTH
P_IN = RKV_W + LORA_W + HG_W

N_EXPERTS = 64
N_GROUPS = 8
GROUP_SIZE = N_EXPERTS // N_GROUPS
TOPK_GROUPS = 4
TOP_K = 8
D_EXPERT = 256
ROUTE_SCALE = 2.5

LANES = 128
SUBLANES = 8
VMEM_LIMIT = 48 * 1024 * 1024

_HI = lax.Precision.HIGHEST
_F32 = jnp.float32
_BF16 = jnp.bfloat16
_NT = (((1,), (1,)), ((), ()))
_TN = (((0,), (0,)), ((), ()))


def _cparams(sem):
    return pltpu.CompilerParams(dimension_semantics=sem, vmem_limit_bytes=VMEM_LIMIT)


def _dir_block(d, i, nb):
    return i + d * (nb - 1 - 2 * i)


def _mod_kernel(c_ref, w_ref, b_ref, o_ref):
    c = c_ref[...]
    s = c * jax.nn.sigmoid(c)
    o_ref[...] = jnp.dot(s, w_ref[...], preferred_element_type=_F32, precision=_HI) + b_ref[...]


def _modulation(c_all, w_mod, b_mod):
    rows, d = c_all.shape
    n = w_mod.shape[1]
    tn = 1024
    return pl.pallas_call(
        _mod_kernel,
        grid=(n // tn,),
        in_specs=[pl.BlockSpec((rows, d), lambda j: (0, 0)),
                  pl.BlockSpec((d, tn), lambda j: (0, j)),
                  pl.BlockSpec((1, tn), lambda j: (0, j))],
        out_specs=pl.BlockSpec((rows, tn), lambda j: (0, j)),
        out_shape=jax.ShapeDtypeStruct((rows, n), _F32),
        compiler_params=_cparams(("parallel",)),
        name="modulation",
    )(c_all, w_mod, b_mod.reshape(1, n))


def _inproj_kernel(x_ref, g_ref, sc_ref, sh_ref, w_ref, rkv_ref, lora_ref, hg_ref):
    x = x_ref[0]
    y = x * lax.rsqrt(jnp.mean(x * x, axis=-1, keepdims=True) + EPS) * g_ref[...]
    h = y * (1.0 + sc_ref[0]) + sh_ref[0]
    u = jnp.dot(h.astype(_BF16), w_ref[...], preferred_element_type=_F32)
    rkv_ref[0] = u[:, :RKV_W]
    lora_ref[0] = u[:, RKV_W:RKV_W + LORA_W]
    hg_ref[0] = u[:, RKV_W + LORA_W:]


def _inproj(x, norm_g, sc, sh, w_in_bf16, tm):
    bsz, l, d = x.shape
    tm = min(tm, l)
    row = lambda b, i: (b, i, 0)
    per_b = lambda b, i: (b, 0, 0)
    return pl.pallas_call(
        _inproj_kernel,
        grid=(bsz, l // tm),
        in_specs=[pl.BlockSpec((1, tm, d), row),
                  pl.BlockSpec((1, d), lambda b, i: (0, 0)),
                  pl.BlockSpec((1, 1, d), per_b),
                  pl.BlockSpec((1, 1, d), per_b),
                  pl.BlockSpec((d, P_IN), lambda b, i: (0, 0))],
        out_specs=[pl.BlockSpec((1, tm, RKV_W), row),
                   pl.BlockSpec((1, tm, LORA_W), row),
                   pl.BlockSpec((1, tm, HG_W), row)],
        out_shape=[jax.ShapeDtypeStruct((bsz, l, RKV_W), _F32),
                   jax.ShapeDtypeStruct((bsz, l, LORA_W), _F32),
                   jax.ShapeDtypeStruct((bsz, l, HG_W), _F32)],
        compiler_params=_cparams(("parallel", "parallel")),
        name="inproj",
    )(x, norm_g.reshape(1, d), sc, sh, w_in_bf16)


def _seg_sum(x, seg_ref):
    hi = x.astype(_BF16)
    lo = (x - hi.astype(_F32)).astype(_BF16)
    return (jnp.dot(hi, seg_ref[...], preferred_element_type=_F32)
            + jnp.dot(lo, seg_ref[...], preferred_element_type=_F32))


def _rw_prep_kernel(cur_ref, prev_ref, next_ref, lora_ref, kern_ref, w0_ref, w2_ref, a0_ref, a2_ref, g2_ref,
                    kk_ref, ka_ref, rk_ref, seg_ref,
                    r_out, kkn_out, v_out, w_out, kd_out, b_out, bonus_out, g_out, *, grid_conv):
    i = pl.program_id(1)
    tq = cur_ref.shape[0]
    hw = prev_ref.shape[0]
    w = RW_WIDTH
    pmask = jnp.where(i > 0, 1.0, 0.0)
    nmask = jnp.where(i < pl.num_programs(1) - 1, 1.0, 0.0)
    xpos = lax.broadcasted_iota(jnp.int32, (tq + 2 * hw, w), 0) % GRID_W
    conv = []
    for part in range(3):
        cs = slice(part * w, (part + 1) * w)
        xe = jnp.concatenate([prev_ref[:, cs] * pmask, cur_ref[:, cs], next_ref[:, cs] * nmask], axis=0)
        xl = pltpu.roll(xe, 1, 0)
        xr = pltpu.roll(xe, tq + 2 * hw - 1, 0)
        if grid_conv:
            xl = jnp.where(xpos == 0, 0.0, xl)
            xr = jnp.where(xpos == GRID_W - 1, 0.0, xr)
        acc = None
        for dy in range(CONV_K) if grid_conv else (CONV_K // 2,):
            off = hw + (dy - 1) * GRID_W
            for dx, src in enumerate((xl, xe, xr)):
                term = src[off:off + tq] * kern_ref[pl.ds(dy * CONV_K + dx, 1), cs]
                acc = term if acc is None else acc + term
        conv.append(acc)
    r, k, v = conv
    r_out[...] = r
    v_out[...] = v

    kk = k * kk_ref[...]
    kk = kk / jnp.maximum(jnp.sqrt(_seg_sum(kk * kk, seg_ref)), 1e-12)
    kkn_out[...] = kk

    lora = lora_ref[...]
    ksum = jnp.zeros_like(k)
    for d in range(2):
        wl = lora[:, d * RW_LORA:(d + 1) * RW_LORA]
        al = lora[:, (2 + d) * RW_LORA:(3 + d) * RW_LORA]
        z = w0_ref[pl.ds(d, 1), :] + jnp.dot(jnp.tanh(wl), w2_ref[d], preferred_element_type=_F32,
                                             precision=_HI)
        wlog = -jax.nn.softplus(-z) - 0.5
        w_out[d] = jnp.exp(-jnp.exp(wlog))
        a = jax.nn.sigmoid(a0_ref[pl.ds(d, 1), :] + jnp.dot(al, a2_ref[d], preferred_element_type=_F32,
                                                             precision=_HI))
        kd = k * (1.0 + (a - 1.0) * ka_ref[...])
        kd_out[d] = kd
        b_out[d] = kk * a
        ksum = ksum + kd
    bonus_out[...] = _seg_sum(r * ksum * rk_ref[...], seg_ref) * v
    gd = lora[:, 4 * RW_LORA:]
    g_out[...] = jnp.dot(jax.nn.sigmoid(gd), g2_ref[...], preferred_element_type=_F32, precision=_HI)


def _rw_prep(rkv, lora, kern9, w0, w2, a0, a2, g2, k_k, k_a, r_k, seg, tq, grid_conv):
    bsz, l, _ = rkv.shape
    tq = min(tq, l)
    nb = l // tq
    hw = GRID_W
    per = tq // hw
    w = RW_WIDTH
    row = lambda b, i: (b, i, 0)
    const2 = lambda b, i: (0, 0)
    const3 = lambda b, i: (0, 0, 0)
    tok = pl.BlockSpec((None, tq, w), row)
    tok2 = pl.BlockSpec((2, None, tq, w), lambda b, i: (0, b, i, 0))
    one = jax.ShapeDtypeStruct((bsz, l, w), _F32)
    two = jax.ShapeDtypeStruct((2, bsz, l, w), _F32)
    vec = lambda t: t.reshape(1, w)
    return pl.pallas_call(
        functools.partial(_rw_prep_kernel, grid_conv=grid_conv),
        grid=(bsz, nb),
        in_specs=[pl.BlockSpec((None, tq, RKV_W), row),
                  pl.BlockSpec((None, hw, RKV_W), lambda b, i: (b, jnp.maximum(i * per - 1, 0), 0)),
                  pl.BlockSpec((None, hw, RKV_W), lambda b, i: (b, jnp.minimum((i + 1) * per, l // hw - 1), 0)),
                  pl.BlockSpec((None, tq, LORA_W), row),
                  pl.BlockSpec((CONV_K * CONV_K, RKV_W), const2),
                  pl.BlockSpec((2, w), const2),
                  pl.BlockSpec((2, RW_LORA, w), const3),
                  pl.BlockSpec((2, w), const2),
                  pl.BlockSpec((2, RW_LORA, w), const3),
                  pl.BlockSpec((RW_GATE_LORA, w), const2),
                  pl.BlockSpec((1, w), const2),
                  pl.BlockSpec((1, w), const2),
                  pl.BlockSpec((1, w), const2),
                  pl.BlockSpec((w, w), const2)],
        out_specs=[tok, tok, tok, tok2, tok2, tok2, tok, tok],
        out_shape=[one, one, one, two, two, two, one, one],
        compiler_params=_cparams(("parallel", "parallel")),
        name="rwkv_prep_grid" if grid_conv else "rwkv_prep_seq",
    )(rkv, rkv, rkv, lora, kern9, w0, w2, a0, a2, g2, vec(k_k), vec(k_a), vec(r_k), seg)


def _rwkv_scan_kernel(*refs, tt, emit, has_init):
    r_ref, kk_ref, v_ref, w_ref, k_ref, b_ref = refs[:6]
    rest = list(refs[6:])
    s0_ref = rest.pop(0) if has_init else None
    out_ref = rest.pop(0)
    s_ref = rest.pop(0)
    d = pl.program_id(0)
    i = pl.program_id(1)
    n = RW_HEAD_DIM

    @pl.when(i == 0)
    def _():
        s_ref[...] = s0_ref[...] if has_init else jnp.zeros_like(s_ref)

    def step(j, carry):
        t = j + d * (tt - 1 - 2 * j)
        sa0 = jnp.zeros((n, LANES), _F32)
        sa1 = jnp.zeros((n, LANES), _F32)
        for k in range(0, n, 2):
            sa0 = sa0 + s_ref[k] * kk_ref[t, pl.ds(k, 1), :]
            sa1 = sa1 + s_ref[k + 1] * kk_ref[t, pl.ds(k + 1, 1), :]
        sa = -(sa0 + sa1)
        vv = v_ref[t]
        y0 = jnp.zeros((n, LANES), _F32)
        y1 = jnp.zeros((n, LANES), _F32)
        for k in range(0, n, 2):
            s0 = (s_ref[k] * w_ref[t, pl.ds(k, 1), :] + sa * b_ref[t, pl.ds(k, 1), :]
                  + vv * k_ref[t, pl.ds(k, 1), :])
            s_ref[k] = s0
            s1 = (s_ref[k + 1] * w_ref[t, pl.ds(k + 1, 1), :] + sa * b_ref[t, pl.ds(k + 1, 1), :]
                  + vv * k_ref[t, pl.ds(k + 1, 1), :])
            s_ref[k + 1] = s1
            if emit:
                y0 = y0 + s0 * r_ref[t, pl.ds(k, 1), :]
                y1 = y1 + s1 * r_ref[t, pl.ds(k + 1, 1), :]
        if emit:
            out_ref[t] = y0 + y1
        return carry

    lax.fori_loop(0, tt, step, 0)

    if not emit:
        @pl.when(i == pl.num_programs(1) - 1)
        def _():
            out_ref[...] = s_ref[...]


def _rwkv_scan(shared, per_dir, s0, tt, emit):
    l, n, c = shared[0].shape
    nb = l // tt
    sh_spec = pl.BlockSpec((tt, n, c), lambda d, i: (_dir_block(d, i, nb), 0, 0))
    pd_spec = pl.BlockSpec((None, tt, n, c), lambda d, i: (d, _dir_block(d, i, nb), 0, 0))
    st_spec = pl.BlockSpec((None, n, n, c), lambda d, i: (d, 0, 0, 0))
    in_specs = [sh_spec] * 3 + [pd_spec] * 3
    args = list(shared) + list(per_dir)
    if s0 is not None:
        in_specs.append(st_spec)
        args.append(s0)
    if emit:
        out_spec, out_shape = pd_spec, jax.ShapeDtypeStruct((2, l, n, c), _F32)
    else:
        out_spec, out_shape = st_spec, jax.ShapeDtypeStruct((2, n, n, c), _F32)
    return pl.pallas_call(
        functools.partial(_rwkv_scan_kernel, tt=tt, emit=emit, has_init=s0 is not None),
        grid=(2, nb),
        in_specs=in_specs,
        out_specs=out_spec,
        out_shape=out_shape,
        scratch_shapes=[pltpu.VMEM((n, n, c), _F32)],
        compiler_params=_cparams(("parallel", "arbitrary")),
        name="rwkv_scan_emit" if emit else "rwkv_scan_state",
    )(*args)


def _hgrn_kernel(*refs, n_chunks, emit, has_init):
    q_ref, f_ref, i_ref, lb_ref = refs[:4]
    rest = list(refs[4:])
    s0_ref = rest.pop(0) if has_init else None
    out_ref = rest.pop(0)
    st_ref = rest.pop(0)
    c = HG_CHUNK
    d = pl.program_id(0)
    tb = pl.program_id(2)
    fwd = d == 0

    @pl.when(tb == 0)
    def _():
        st_ref[...] = s0_ref[...] if has_init else jnp.zeros_like(st_ref)

    row = lax.broadcasted_iota(jnp.int32, (c, c), 0)
    col = lax.broadcasted_iota(jnp.int32, (c, c), 1)
    causal = jnp.where(fwd, (row >= col).astype(_F32), (row <= col).astype(_F32))
    tri = causal.astype(_BF16)
    causal2 = jnp.concatenate([causal, causal], axis=0) > 0.5
    head0 = lax.broadcasted_iota(jnp.int32, (c, LANES), 1) < HG_DIM
    same_head = ((lax.broadcasted_iota(jnp.int32, (LANES, LANES), 0) < HG_DIM)
                 == (lax.broadcasted_iota(jnp.int32, (LANES, LANES), 1) < HG_DIM))
    lb = lb_ref[...]
    omlb = 1.0 - lb

    def body(jj, carry):
        cj = jnp.where(fwd, jj, n_chunks - 1 - jj)
        sl = pl.ds(pl.multiple_of(cj * c, c), c)
        fr = f_ref[sl, :]
        logf = jnp.log(lb + omlb * jax.nn.sigmoid(fr))
        kf = omlb * jax.nn.sigmoid(-fr)
        q = q_ref[sl, :]
        qs = q * jax.nn.sigmoid(q) * (HG_DIM ** -0.5)
        v = i_ref[sl, :]
        hi = logf.astype(_BF16)
        lo = (logf - hi.astype(_F32)).astype(_BF16)
        cum = (jnp.dot(tri, hi, preferred_element_type=_F32)
               + jnp.dot(tri, lo, preferred_element_type=_F32))
        tot = jnp.where(fwd, cum[c - 1:c, :], cum[0:1, :])
        dec = jnp.exp(tot)
        kdec = (kf * jnp.exp(tot - cum)).astype(_BF16)
        vb = v.astype(_BF16)
        if emit:
            mid = cum[c // 2:c // 2 + 1, :]
            qd = qs * jnp.exp(cum - mid)
            kd = (kf * jnp.exp(mid - cum)).astype(_BF16)
            qe = (qs * jnp.exp(cum)).astype(_BF16)
        for p in range(HG_PAIRS):
            ls = slice(p * LANES, (p + 1) * LANES)
            st = st_ref[p]
            if emit:
                qd_p = qd[:, ls]
                q2 = jnp.concatenate([jnp.where(head0, qd_p, 0.0), jnp.where(head0, 0.0, qd_p)], axis=0)
                sc = lax.dot_general(q2.astype(_BF16), kd[:, ls], _NT, preferred_element_type=_F32)
                sc = jnp.where(causal2, sc, 0.0)
                oi = jnp.dot(sc.astype(_BF16), vb[:, ls], preferred_element_type=_F32)
                o = jnp.where(head0, oi[:c], oi[c:])
                o = o + lax.dot_general(qe[:, ls], st.astype(_BF16), _NT, preferred_element_type=_F32)
                out_ref[sl, ls] = o
            upd = lax.dot_general(vb[:, ls], kdec[:, ls], _TN, preferred_element_type=_F32)
            st_ref[p] = st * dec[:, ls] + jnp.where(same_head, upd, 0.0)
        return carry

    lax.fori_loop(0, n_chunks, body, 0)

    if not emit:
        @pl.when(tb == pl.num_programs(2) - 1)
        def _():
            out_ref[...] = st_ref[...]


def _hgrn_scan(hg, lb, s0, tb, emit):
    bsz, l, _ = hg.shape
    tb = min(tb, l)
    nb = l // tb
    w = HG_WIDTH
    col = lambda j: pl.BlockSpec((None, tb, w), lambda d, b, i: (b, _dir_block(d, i, nb), j))
    f_spec = pl.BlockSpec((None, tb, w), lambda d, b, i: (b, _dir_block(d, i, nb), 1 + d))
    lb_spec = pl.BlockSpec((None, 1, w), lambda d, b, i: (d, 0, 0))
    st_spec = pl.BlockSpec((None, None, HG_PAIRS, LANES, LANES), lambda d, b, i: (d, b, 0, 0, 0))
    in_specs = [col(0), f_spec, col(3), lb_spec]
    args = [hg, hg, hg, lb]
    if s0 is not None:
        in_specs.append(st_spec)
        args.append(s0)
    if emit:
        out_spec = pl.BlockSpec((None, None, tb, w), lambda d, b, i: (d, b, _dir_block(d, i, nb), 0))
        out_shape = jax.ShapeDtypeStruct((2, bsz, l, w), _F32)
    else:
        out_spec = st_spec
        out_shape = jax.ShapeDtypeStruct((2, bsz, HG_PAIRS, LANES, LANES), _F32)
    return pl.pallas_call(
        functools.partial(_hgrn_kernel, n_chunks=tb // HG_CHUNK, emit=emit, has_init=s0 is not None),
        grid=(2, bsz, nb),
        in_specs=in_specs,
        out_specs=out_spec,
        out_shape=out_shape,
        scratch_shapes=[pltpu.VMEM((HG_PAIRS, LANES, LANES), _F32)],
        compiler_params=_cparams(("parallel", "parallel", "arbitrary")),
        name="hgrn_scan_emit" if emit else "hgrn_scan_state",
    )(*args)


def _post_kernel(y_ref, bonus_ref, grw_ref, lng_ref, lnb_ref, o_ref, ghg_ref, hgn_ref, seg_ref,
                 x_ref, w_ref, g1_ref, n2_ref, sc_ref, sh_ref, rw_ref, x1_ref, h2_ref, lg_ref):
    inv_n = 1.0 / RW_HEAD_DIM
    ysum = y_ref[0] + y_ref[1]
    mu = _seg_sum(ysum, seg_ref) * inv_n
    dlt = ysum - mu
    var = _seg_sum(dlt * dlt, seg_ref) * inv_n
    yn = dlt * lax.rsqrt(var + RW_GN_EPS) * lng_ref[...] + lnb_ref[...] + bonus_ref[...]
    y_rw = yn * grw_ref[...]
    o = o_ref[0] + o_ref[1]
    o = o * lax.rsqrt(_seg_sum(o * o, seg_ref) * inv_n + EPS) * hgn_ref[...]
    g = ghg_ref[...]
    y_hg = o * (g * jax.nn.sigmoid(g))
    y = (jnp.dot(y_rw.astype(_BF16), w_ref[:RW_WIDTH, :], preferred_element_type=_F32)
         + jnp.dot(y_hg.astype(_BF16), w_ref[RW_WIDTH:, :], preferred_element_type=_F32))
    x1 = x_ref[0] + g1_ref[0] * y
    x1_ref[0] = x1
    h2 = x1 * lax.rsqrt(jnp.mean(x1 * x1, axis=-1, keepdims=True) + EPS) * n2_ref[...]
    h2 = h2 * (1.0 + sc_ref[0]) + sh_ref[0]
    h2_ref[0] = h2.astype(_BF16)
    lg_ref[...] = lax.dot_general(rw_ref[...], h2, _NT, preferred_element_type=_F32, precision=_HI)


def _post(y_dirs, bonus, g_rw, lnx_g, lnx_b, o_dirs, hg, hg_norm_row, seg,
          x, w_out_bf16, g1, norm2_g, sc2, sh2, router_wt, tm):
    bsz, l, d = x.shape
    tm = min(tm, l)
    nt = l // tm
    w = RW_WIDTH
    row = lambda b, i: (b, i, 0)
    per_b = lambda b, i: (b, 0, 0)
    const = lambda b, i: (0, 0)
    tok = pl.BlockSpec((None, tm, w), row)
    tok2 = pl.BlockSpec((2, None, tm, w), lambda b, i: (0, b, i, 0))
    vec = pl.BlockSpec((1, w), const)
    return pl.pallas_call(
        _post_kernel,
        grid=(bsz, nt),
        in_specs=[tok2, tok, tok, vec, vec, tok2,
                  pl.BlockSpec((None, tm, w), lambda b, i: (b, i, 4)),
                  vec,
                  pl.BlockSpec((w, w), const),
                  pl.BlockSpec((1, tm, d), row),
                  pl.BlockSpec((d, d), const),
                  pl.BlockSpec((1, 1, d), per_b),
                  pl.BlockSpec((1, d), const),
                  pl.BlockSpec((1, 1, d), per_b),
                  pl.BlockSpec((1, 1, d), per_b),
                  pl.BlockSpec((N_EXPERTS, d), const)],
        out_specs=[pl.BlockSpec((1, tm, d), row),
                   pl.BlockSpec((1, tm, d), row),
                   pl.BlockSpec((N_EXPERTS, tm), lambda b, i: (0, b * nt + i))],
        out_shape=[jax.ShapeDtypeStruct((bsz, l, d), _F32),
                   jax.ShapeDtypeStruct((bsz, l, d), _BF16),
                   jax.ShapeDtypeStruct((N_EXPERTS, bsz * l), _F32)],
        compiler_params=_cparams(("parallel", "parallel")),
        name="outproj_norm2_router",
    )(y_dirs, bonus, g_rw, lnx_g.reshape(1, w), lnx_b.reshape(1, w), o_dirs, hg, hg_norm_row, seg,
      x, w_out_bf16, g1, norm2_g.reshape(1, d), sc2, sh2, router_wt)


def _sublane_all(x, op):
    for s in (4, 2, 1):
        x = op(x, pltpu.roll(x, s, 0))
    return x


def _router_kernel(lg_ref, bias_ref, out_ref):
    ng, gs = N_GROUPS, GROUP_SIZE
    tn = lg_ref.shape[1]
    neg = -jnp.inf
    gidx = lax.broadcasted_iota(jnp.int32, (ng, tn), 0)
    bias = jnp.concatenate([bias_ref[...]] * (tn // LANES), axis=1)
    scores = [jax.nn.sigmoid(lg_ref[j * ng:(j + 1) * ng, :]) for j in range(gs)]
    sel = [scores[j] + bias[j * ng:(j + 1) * ng, :] for j in range(gs)]

    m1 = functools.reduce(jnp.maximum, sel)
    cnt = functools.reduce(jnp.add, [(s == m1).astype(_F32) for s in sel])
    m2 = functools.reduce(jnp.maximum, [jnp.where(s < m1, s, neg) for s in sel])
    gscore = m1 + jnp.where(cnt >= 2.0, m1, m2)

    rank = jnp.zeros((ng, tn), jnp.int32)
    for s in range(1, ng):
        other = pltpu.roll(gscore, s, 0)
        beats = jnp.where(other > gscore, 1, jnp.where((other == gscore) & (gidx >= s), 1, 0))
        rank = rank + beats
    gsel = rank < TOPK_GROUPS

    key = [jnp.where(gsel, s, neg) for s in sel]
    eidx = [gidx * gs + j for j in range(gs)]
    avail = [jnp.ones((ng, tn), jnp.int32) for _ in range(gs)]
    for _ in range(TOP_K):
        cur = functools.reduce(jnp.maximum, [jnp.where(avail[j] > 0, key[j], neg) for j in range(gs)])
        mx = _sublane_all(cur, jnp.maximum)
        cand = [jnp.where((avail[j] > 0) & (key[j] == mx), eidx[j], N_EXPERTS) for j in range(gs)]
        mn = _sublane_all(functools.reduce(jnp.minimum, cand), jnp.minimum)
        avail = [jnp.where(eidx[j] == mn, 0, avail[j]) for j in range(gs)]

    graw = [jnp.where(avail[j] == 0, scores[j], 0.0) for j in range(gs)]
    den = _sublane_all(functools.reduce(jnp.add, graw), jnp.add)
    gate = jnp.concatenate([g / den * ROUTE_SCALE for g in graw], axis=0)
    hi = gate.astype(_BF16).astype(_F32)
    both = jnp.concatenate([hi, gate - hi], axis=0)
    out_ref[...] = both.T.astype(_BF16)


def _router(logits_t, bias_rows, tn):
    t = logits_t.shape[1]
    tn = min(tn, t)
    return pl.pallas_call(
        _router_kernel,
        grid=(t // tn,),
        in_specs=[pl.BlockSpec((N_EXPERTS, tn), lambda i: (0, i)),
                  pl.BlockSpec((N_EXPERTS, LANES), lambda i: (0, 0))],
        out_specs=pl.BlockSpec((tn, LANES), lambda i: (i, 0)),
        out_shape=jax.ShapeDtypeStruct((t, LANES), _BF16),
        compiler_params=_cparams(("parallel",)),
        name="router_topk",
    )(logits_t, bias_rows)


def _moe_kernel(h_ref, gsp_ref, x1_ref, g2_ref, fg_ref, wg_ref, wu_ref, wd_ref, o_ref, acc_ref):
    e = pl.program_id(2)
    h = h_ref[0]
    a = jnp.dot(h, wg_ref[0], preferred_element_type=_F32)
    u = jnp.dot(h, wu_ref[0], preferred_element_type=_F32)
    act = a * jax.nn.sigmoid(a) * u
    col = lax.broadcasted_iota(jnp.int32, (LANES, D_EXPERT), 0) % N_EXPERTS
    expert_of_col = (col % N_GROUPS) * GROUP_SIZE + col // N_GROUPS
    onehot = jnp.where(expert_of_col == e, 1.0, 0.0).astype(_BF16)
    gate = jnp.dot(gsp_ref[0], onehot, preferred_element_type=_F32)
    gate = jnp.where(e == N_EXPERTS, 1.0, gate)
    y = jnp.dot((act * gate).astype(_BF16), wd_ref[0], preferred_element_type=_F32)

    @pl.when(e == 0)
    def _():
        acc_ref[...] = y

    @pl.when(e > 0)
    def _():
        acc_ref[...] += y

    @pl.when(e == N_EXPERTS)
    def _():
        x2 = x1_ref[0] + g2_ref[0] * acc_ref[...]
        o_ref[0] = x2 * lax.rsqrt(jnp.mean(x2 * x2, axis=-1, keepdims=True) + EPS) * fg_ref[...]


def _moe(h2, gate_split, x1, g2, final_g, wg, wu, wd, tm):
    bsz, l, d = x1.shape
    tm = min(tm, l)
    row = lambda b, i, e: (b, i, 0)
    per_b = lambda b, i, e: (b, 0, 0)
    return pl.pallas_call(
        _moe_kernel,
        grid=(bsz, l // tm, N_EXPERTS + 1),
        in_specs=[pl.BlockSpec((1, tm, d), row),
                  pl.BlockSpec((1, tm, LANES), row),
                  pl.BlockSpec((1, tm, d), row),
                  pl.BlockSpec((1, 1, d), per_b),
                  pl.BlockSpec((1, d), lambda b, i, e: (0, 0)),
                  pl.BlockSpec((1, d, D_EXPERT), lambda b, i, e: (e, 0, 0)),
                  pl.BlockSpec((1, d, D_EXPERT), lambda b, i, e: (e, 0, 0)),
                  pl.BlockSpec((1, D_EXPERT, d), lambda b, i, e: (e, 0, 0))],
        out_specs=pl.BlockSpec((1, tm, d), row),
        out_shape=jax.ShapeDtypeStruct((bsz, l, d), _F32),
        scratch_shapes=[pltpu.VMEM((tm, d), _F32)],
        compiler_params=_cparams(("parallel", "parallel", "arbitrary")),
        name="moe_experts_final_norm",
    )(h2, gate_split, x1, g2, final_g.reshape(1, d), wg, wu, wd)


def _to_chain_lanes(t):
    lead, (b, l, _) = t.shape[:-3], t.shape[-3:]
    nl = len(lead)
    t = t.reshape(*lead, b, l, RW_HEADS, RW_HEAD_DIM)
    t = t.transpose(*range(nl), nl + 1, nl + 3, nl, nl + 2)
    return t.reshape(*lead, l, RW_HEAD_DIM, b * RW_HEADS)


def _from_chain_lanes(t, b):
    l = t.shape[1]
    return t.reshape(2, l, RW_HEAD_DIM, b, RW_HEADS).transpose(0, 3, 1, 4, 2).reshape(2, b, l, RW_WIDTH)


def _scan_operands(prep):
    r, kk, v, w, kd, bb = prep[:6]
    return [_to_chain_lanes(t) for t in (r, kk, v)], [_to_chain_lanes(t) for t in (w, kd, bb)]


def kernel(x, c, ctx, c_ctx, w_mod, b_mod, norm1_g, norm2_g, w_in, rw_conv, rw_w0, rw_w2, rw_a0, rw_a2,
           rw_g2, rw_k_k, rw_k_a, rw_r_k, rw_lnx_g, rw_lnx_b, hg_lb_logits, hg_norm_g, w_out, router_w,
           router_b, exp_w_gate, exp_w_up, exp_w_down, sh_w_gate, sh_w_up, sh_w_down, final_norm_g):
    bsz, seq, d = x.shape
    rows = seq // GRID_W
    assert w_mod.shape[0] == 1 and bsz * RW_HEADS == LANES
    lyr = 0

    c_all = jnp.concatenate([c, c_ctx[None, :], jnp.zeros((SUBLANES - 1, d), _F32)], axis=0)
    mod = _modulation(c_all, w_mod[lyr], b_mod[lyr])
    sh1, sc1, g1, sh2, sc2, g2 = [m[:, None, :] for m in jnp.split(mod[:bsz], N_MOD, axis=-1)]
    mod_ctx = jnp.broadcast_to(mod[bsz][None, None, :], (bsz, 1, N_MOD * d))
    csh1, csc1 = mod_ctx[..., :d], mod_ctx[..., d:2 * d]

    w_in_bf16 = w_in[lyr].astype(_BF16)
    rkv_lat, lora_lat, hg_lat = _inproj(x, norm1_g[lyr], sc1, sh1, w_in_bf16, tm=256)
    rkv_ctx, lora_ctx, hg_ctx = _inproj(ctx, norm1_g[lyr], csc1, csh1, w_in_bf16, tm=256)

    kern9 = rw_conv[lyr].reshape(CONV_K * CONV_K, RKV_W)
    lane_head = jnp.arange(RW_WIDTH) // RW_HEAD_DIM
    seg = (lane_head[:, None] == lane_head[None, :]).astype(_BF16)
    rw_p = (kern9, rw_w0[lyr], rw_w2[lyr], rw_a0[lyr], rw_a2[lyr], rw_g2[lyr], rw_k_k[lyr], rw_k_a[lyr],
            rw_r_k[lyr], seg)
    prep_lat = _rw_prep(rkv_lat, lora_lat, *rw_p, tq=256, grid_conv=True)
    prep_ctx = _rw_prep(rkv_ctx, lora_ctx, *rw_p, tq=256, grid_conv=False)
    s_rw = _rwkv_scan(*_scan_operands(prep_ctx), None, tt=16, emit=False)
    y_dirs = _from_chain_lanes(_rwkv_scan(*_scan_operands(prep_lat), s_rw, tt=16, emit=True), bsz)

    lb = jnp.cumsum(jax.nn.softmax(hg_lb_logits.astype(_F32), axis=1), axis=1)[:, lyr][:, None, :]
    s_hg = _hgrn_scan(hg_ctx, lb, None, tb=256, emit=False)
    o_dirs = _hgrn_scan(hg_lat, lb, s_hg, tb=256, emit=True)

    perm = lambda t: t.reshape(N_GROUPS, GROUP_SIZE, -1).transpose(1, 0, 2).reshape(N_EXPERTS, -1)
    router_wt = perm(router_w[lyr].T)
    bias_rows = jnp.broadcast_to(perm(router_b[lyr][:, None]), (N_EXPERTS, LANES))
    hg_norm_row = jnp.tile(hg_norm_g[lyr], HG_HEADS).reshape(1, HG_WIDTH)
    x1, h2, logits_t = _post(y_dirs, prep_lat[6], prep_lat[7], rw_lnx_g[lyr], rw_lnx_b[lyr], o_dirs, hg_lat,
                             hg_norm_row, seg, x, w_out[lyr].astype(_BF16), g1, norm2_g[lyr], sc2, sh2,
                             router_wt, tm=512)
    gate_split = _router(logits_t, bias_rows, tn=512).reshape(bsz, seq, LANES)

    wg = jnp.concatenate([exp_w_gate[lyr], sh_w_gate[lyr][None]], axis=0).astype(_BF16)
    wu = jnp.concatenate([exp_w_up[lyr], sh_w_up[lyr][None]], axis=0).astype(_BF16)
    wd = jnp.concatenate([exp_w_down[lyr], sh_w_down[lyr][None]], axis=0).astype(_BF16)
    return _moe(h2, gate_split, x1, g2, final_norm_g, wg, wu, wd, tm=1024)
```

```python
import functools

import jax
import jax.numpy as jnp
from jax import lax
from jax.experimental import pallas as pl
from jax.experimental.pallas import tpu as pltpu
from jax.experimental.pallas import tpu_sc as plsc

D_MODEL = 1024
GRID_W = 64
N_MOD = 6
EPS = 1e-6

RW_HEAD_DIM = 64
RW_HEADS = 8
RW_WIDTH = 512
RW_LORA = 64
RW_GATE_LORA = 128
RW_GN_EPS = 1e-5 * RW_HEAD_DIM
CONV_K = 3

HG_HEADS = 8
HG_DIM = 64
HG_WIDTH = 512
HG_CHUNK = 32
HG_PAIRS = HG_HEADS // 2

RKV_W = 3 * RW_WIDTH
LORA_W = 4 * RW_LORA + RW_GATE_LORA
HG_W = 5 * HG_WIDTH
P_IN = RKV_W + LORA_W + HG_W

N_EXPERTS = 64
N_GROUPS = 8
GROUP_SIZE = N_EXPERTS // N_GROUPS
TOPK_GROUPS = 4
TOP_K = 8
D_EXPERT = 256
ROUTE_SCALE = 2.5
MOE_BLOCK = 512
SC_ROW = 256
SC_WINDOW = 128

LANES = 128
SUBLANES = 8
VMEM_LIMIT = 48 * 1024 * 1024

_HI = lax.Precision.HIGHEST
_F32 = jnp.float32
_BF16 = jnp.bfloat16
_NT = (((1,), (1,)), ((), ()))
_TN = (((0,), (0,)), ((), ()))


def _cparams(sem):
    return pltpu.CompilerParams(dimension_semantics=sem, vmem_limit_bytes=VMEM_LIMIT)


def _pack_bf16_pair(lo, hi):
    lo_bits = lax.bitcast_convert_type(lo.astype(_BF16).astype(_F32), jnp.uint32)
    hi_bits = lax.bitcast_convert_type(hi.astype(_BF16).astype(_F32), jnp.uint32)
    return (lo_bits >> 16) | (hi_bits & jnp.uint32(0xFFFF0000))


def _unpack_bf16_pair(u):
    lo = lax.bitcast_convert_type(u << 16, _F32)
    hi = lax.bitcast_convert_type(u & jnp.uint32(0xFFFF0000), _F32)
    return lo, hi


def _dir_block(d, i, nb):
    return i + d * (nb - 1 - 2 * i)


def _mod_kernel(c_ref, w_ref, b_ref, o_ref):
    c = c_ref[...]
    s = c * jax.nn.sigmoid(c)
    o_ref[...] = jnp.dot(s, w_ref[...], preferred_element_type=_F32, precision=_HI) + b_ref[...]


def _modulation(c_all, w_mod, b_mod):
    rows, d = c_all.shape
    n = w_mod.shape[1]
    tn = 1024
    return pl.pallas_call(
        _mod_kernel,
        grid=(n // tn,),
        in_specs=[pl.BlockSpec((rows, d), lambda j: (0, 0)),
                  pl.BlockSpec((d, tn), lambda j: (0, j)),
                  pl.BlockSpec((1, tn), lambda j: (0, j))],
        out_specs=pl.BlockSpec((rows, tn), lambda j: (0, j)),
        out_shape=jax.ShapeDtypeStruct((rows, n), _F32),
        compiler_params=_cparams(("parallel",)),
        name="modulation",
    )(c_all, w_mod, b_mod.reshape(1, n))


def _inproj_kernel(x_ref, g_ref, sc_ref, sh_ref, w_ref, rkv_ref, lora_ref, hg_ref):
    x = x_ref[0]
    y = x * lax.rsqrt(jnp.mean(x * x, axis=-1, keepdims=True) + EPS) * g_ref[...]
    h = y * (1.0 + sc_ref[0]) + sh_ref[0]
    u = jnp.dot(h.astype(_BF16), w_ref[...], preferred_element_type=_F32)
    rkv_ref[0] = u[:, :RKV_W]
    lora_ref[0] = u[:, RKV_W:RKV_W + LORA_W]
    hg_ref[0] = u[:, RKV_W + LORA_W:]


def _inproj(x, norm_g, sc, sh, w_in_bf16, tm):
    bsz, l, d = x.shape
    tm = min(tm, l)
    row = lambda b, i: (b, i, 0)
    per_b = lambda b, i: (b, 0, 0)
    return pl.pallas_call(
        _inproj_kernel,
        grid=(bsz, l // tm),
        in_specs=[pl.BlockSpec((1, tm, d), row),
                  pl.BlockSpec((1, d), lambda b, i: (0, 0)),
                  pl.BlockSpec((1, 1, d), per_b),
                  pl.BlockSpec((1, 1, d), per_b),
                  pl.BlockSpec((d, P_IN), lambda b, i: (0, 0))],
        out_specs=[pl.BlockSpec((1, tm, RKV_W), row),
                   pl.BlockSpec((1, tm, LORA_W), row),
                   pl.BlockSpec((1, tm, HG_W), row)],
        out_shape=[jax.ShapeDtypeStruct((bsz, l, RKV_W), _F32),
                   jax.ShapeDtypeStruct((bsz, l, LORA_W), _F32),
                   jax.ShapeDtypeStruct((bsz, l, HG_W), _F32)],
        compiler_params=_cparams(("parallel", "parallel")),
        name="inproj",
    )(x, norm_g.reshape(1, d), sc, sh, w_in_bf16)


def _seg_sum(x, seg_ref):
    hi = x.astype(_BF16)
    lo = (x - hi.astype(_F32)).astype(_BF16)
    return (jnp.dot(hi, seg_ref[...], preferred_element_type=_F32)
            + jnp.dot(lo, seg_ref[...], preferred_element_type=_F32))


def _rw_prep_kernel(cur_ref, prev_ref, next_ref, lora_ref, kern_ref, w0_ref, w2_ref, a0_ref, a2_ref, g2_ref,
                    kk_ref, ka_ref, rk_ref, seg_ref,
                    r_out, kkn_out, v_out, w_out, kd_out, b_out, bonus_out, g_out, *, grid_conv):
    i = pl.program_id(1)
    tq = cur_ref.shape[0]
    hw = prev_ref.shape[0]
    w = RW_WIDTH
    pmask = jnp.where(i > 0, 1.0, 0.0)
    nmask = jnp.where(i < pl.num_programs(1) - 1, 1.0, 0.0)
    xpos = lax.broadcasted_iota(jnp.int32, (tq + 2 * hw, w), 0) % GRID_W
    conv = []
    for part in range(3):
        cs = slice(part * w, (part + 1) * w)
        xe = jnp.concatenate([prev_ref[:, cs] * pmask, cur_ref[:, cs], next_ref[:, cs] * nmask], axis=0)
        xl = pltpu.roll(xe, 1, 0)
        xr = pltpu.roll(xe, tq + 2 * hw - 1, 0)
        if grid_conv:
            xl = jnp.where(xpos == 0, 0.0, xl)
            xr = jnp.where(xpos == GRID_W - 1, 0.0, xr)
        acc = None
        for dy in range(CONV_K) if grid_conv else (CONV_K // 2,):
            off = hw + (dy - 1) * GRID_W
            for dx, src in enumerate((xl, xe, xr)):
                term = src[off:off + tq] * kern_ref[pl.ds(dy * CONV_K + dx, 1), cs]
                acc = term if acc is None else acc + term
        conv.append(acc)
    r, k, v = conv
    r_out[...] = r
    v_out[...] = v

    kk = k * kk_ref[...]
    kk = kk / jnp.maximum(jnp.sqrt(_seg_sum(kk * kk, seg_ref)), 1e-12)
    kkn_out[...] = kk

    lora = lora_ref[...]
    ksum = jnp.zeros_like(k)
    for d in range(2):
        wl = lora[:, d * RW_LORA:(d + 1) * RW_LORA]
        al = lora[:, (2 + d) * RW_LORA:(3 + d) * RW_LORA]
        z = w0_ref[pl.ds(d, 1), :] + jnp.dot(jnp.tanh(wl), w2_ref[d], preferred_element_type=_F32,
                                             precision=_HI)
        wlog = -jax.nn.softplus(-z) - 0.5
        w_out[d] = jnp.exp(-jnp.exp(wlog))
        a = jax.nn.sigmoid(a0_ref[pl.ds(d, 1), :] + jnp.dot(al, a2_ref[d], preferred_element_type=_F32,
                                                             precision=_HI))
        kd = k * (1.0 + (a - 1.0) * ka_ref[...])
        kd_out[d] = kd
        b_out[d] = kk * a
        ksum = ksum + kd
    bonus_out[...] = _seg_sum(r * ksum * rk_ref[...], seg_ref) * v
    gd = lora[:, 4 * RW_LORA:]
    g_out[...] = jnp.dot(jax.nn.sigmoid(gd), g2_ref[...], preferred_element_type=_F32, precision=_HI)


def _rw_prep(rkv, lora, kern9, w0, w2, a0, a2, g2, k_k, k_a, r_k, seg, tq, grid_conv):
    bsz, l, _ = rkv.shape
    tq = min(tq, l)
    nb = l // tq
    hw = GRID_W
    per = tq // hw
    w = RW_WIDTH
    row = lambda b, i: (b, i, 0)
    const2 = lambda b, i: (0, 0)
    const3 = lambda b, i: (0, 0, 0)
    tok = pl.BlockSpec((None, tq, w), row)
    tok2 = pl.BlockSpec((2, None, tq, w), lambda b, i: (0, b, i, 0))
    one = jax.ShapeDtypeStruct((bsz, l, w), _F32)
    two = jax.ShapeDtypeStruct((2, bsz, l, w), _F32)
    vec = lambda t: t.reshape(1, w)
    return pl.pallas_call(
        functools.partial(_rw_prep_kernel, grid_conv=grid_conv),
        grid=(bsz, nb),
        in_specs=[pl.BlockSpec((None, tq, RKV_W), row),
                  pl.BlockSpec((None, hw, RKV_W), lambda b, i: (b, jnp.maximum(i * per - 1, 0), 0)),
                  pl.BlockSpec((None, hw, RKV_W), lambda b, i: (b, jnp.minimum((i + 1) * per, l // hw - 1), 0)),
                  pl.BlockSpec((None, tq, LORA_W), row),
                  pl.BlockSpec((CONV_K * CONV_K, RKV_W), const2),
                  pl.BlockSpec((2, w), const2),
                  pl.BlockSpec((2, RW_LORA, w), const3),
                  pl.BlockSpec((2, w), const2),
                  pl.BlockSpec((2, RW_LORA, w), const3),
                  pl.BlockSpec((RW_GATE_LORA, w), const2),
                  pl.BlockSpec((1, w), const2),
                  pl.BlockSpec((1, w), const2),
                  pl.BlockSpec((1, w), const2),
                  pl.BlockSpec((w, w), const2)],
        out_specs=[tok, tok, tok, tok2, tok2, tok2, tok, tok],
        out_shape=[one, one, one, two, two, two, one, one],
        compiler_params=_cparams(("parallel", "parallel")),
        name="rwkv_prep_grid" if grid_conv else "rwkv_prep_seq",
    )(rkv, rkv, rkv, lora, kern9, w0, w2, a0, a2, g2, vec(k_k), vec(k_a), vec(r_k), seg)


def _rwkv_scan_kernel(*refs, tt, emit, has_init):
    r_ref, kk_ref, v_ref, w_ref, k_ref, b_ref = refs[:6]
    rest = list(refs[6:])
    s0_ref = rest.pop(0) if has_init else None
    out_ref = rest.pop(0)
    s_ref = rest.pop(0)
    d = pl.program_id(0)
    i = pl.program_id(1)
    n = RW_HEAD_DIM

    @pl.when(i == 0)
    def _():
        s_ref[...] = s0_ref[...] if has_init else jnp.zeros_like(s_ref)

    hv = n // 2

    def step(j, carry):
        t = j + d * (tt - 1 - 2 * j)
        for half in range(2):
            vs = pl.ds(half * hv, hv)
            sa0 = jnp.zeros((hv, LANES), _F32)
            sa1 = jnp.zeros((hv, LANES), _F32)
            for k in range(0, n, 2):
                sa0 = sa0 + s_ref[k, vs, :] * kk_ref[t, pl.ds(k, 1), :]
                sa1 = sa1 + s_ref[k + 1, vs, :] * kk_ref[t, pl.ds(k + 1, 1), :]
            sa = -(sa0 + sa1)
            vv = v_ref[t, vs, :]
            y0 = jnp.zeros((hv, LANES), _F32)
            y1 = jnp.zeros((hv, LANES), _F32)
            for k in range(0, n, 2):
                s0 = (s_ref[k, vs, :] * w_ref[t, pl.ds(k, 1), :] + sa * b_ref[t, pl.ds(k, 1), :]
                      + vv * k_ref[t, pl.ds(k, 1), :])
                s_ref[k, vs, :] = s0
                s1 = (s_ref[k + 1, vs, :] * w_ref[t, pl.ds(k + 1, 1), :] + sa * b_ref[t, pl.ds(k + 1, 1), :]
                      + vv * k_ref[t, pl.ds(k + 1, 1), :])
                s_ref[k + 1, vs, :] = s1
                if emit:
                    y0 = y0 + s0 * r_ref[t, pl.ds(k, 1), :]
                    y1 = y1 + s1 * r_ref[t, pl.ds(k + 1, 1), :]
            if emit:
                out_ref[t, vs, :] = y0 + y1
        return carry

    lax.fori_loop(0, tt, step, 0)

    if not emit:
        @pl.when(i == pl.num_programs(1) - 1)
        def _():
            out_ref[...] = s_ref[...]


def _rwkv_scan(shared, per_dir, s0, tt, emit):
    l, n, c = shared[0].shape
    nb = l // tt
    sh_spec = pl.BlockSpec((tt, n, c), lambda d, i: (_dir_block(d, i, nb), 0, 0))
    pd_spec = pl.BlockSpec((None, tt, n, c), lambda d, i: (d, _dir_block(d, i, nb), 0, 0))
    st_spec = pl.BlockSpec((None, n, n, c), lambda d, i: (d, 0, 0, 0))
    in_specs = [sh_spec] * 3 + [pd_spec] * 3
    args = list(shared) + list(per_dir)
    if s0 is not None:
        in_specs.append(st_spec)
        args.append(s0)
    if emit:
        out_spec, out_shape = pd_spec, jax.ShapeDtypeStruct((2, l, n, c), _F32)
    else:
        out_spec, out_shape = st_spec, jax.ShapeDtypeStruct((2, n, n, c), _F32)
    return pl.pallas_call(
        functools.partial(_rwkv_scan_kernel, tt=tt, emit=emit, has_init=s0 is not None),
        grid=(2, nb),
        in_specs=in_specs,
        out_specs=out_spec,
        out_shape=out_shape,
        scratch_shapes=[pltpu.VMEM((n, n, c), _F32)],
        compiler_params=_cparams(("parallel", "arbitrary")),
        name="rwkv_scan_emit" if emit else "rwkv_scan_state",
    )(*args)


def _hgrn_kernel(*refs, n_chunks, emit, has_init):
    q_ref, f_ref, i_ref, lb_ref = refs[:4]
    rest = list(refs[4:])
    s0_ref = rest.pop(0) if has_init else None
    out_ref = rest.pop(0)
    st_ref = rest.pop(0)
    c = HG_CHUNK
    d = pl.program_id(0)
    tb = pl.program_id(2)
    fwd = d == 0

    @pl.when(tb == 0)
    def _():
        st_ref[...] = s0_ref[...] if has_init else jnp.zeros_like(st_ref)

    row = lax.broadcasted_iota(jnp.int32, (c, c), 0)
    col = lax.broadcasted_iota(jnp.int32, (c, c), 1)
    causal = jnp.where(fwd, (row >= col).astype(_F32), (row <= col).astype(_F32))
    tri = causal.astype(_BF16)
    causal2 = jnp.concatenate([causal, causal], axis=0) > 0.5
    head0 = lax.broadcasted_iota(jnp.int32, (c, LANES), 1) < HG_DIM
    same_head = ((lax.broadcasted_iota(jnp.int32, (LANES, LANES), 0) < HG_DIM)
                 == (lax.broadcasted_iota(jnp.int32, (LANES, LANES), 1) < HG_DIM))
    lb = lb_ref[...]
    omlb = 1.0 - lb

    def body(jj, carry):
        cj = jnp.where(fwd, jj, n_chunks - 1 - jj)
        sl = pl.ds(pl.multiple_of(cj * c, c), c)
        fr = f_ref[sl, :]
        logf = jnp.log(lb + omlb * jax.nn.sigmoid(fr))
        kf = omlb * jax.nn.sigmoid(-fr)
        q = q_ref[sl, :]
        qs = q * jax.nn.sigmoid(q) * (HG_DIM ** -0.5)
        v = i_ref[sl, :]
        hi = logf.astype(_BF16)
        lo = (logf - hi.astype(_F32)).astype(_BF16)
        cum = (jnp.dot(tri, hi, preferred_element_type=_F32)
               + jnp.dot(tri, lo, preferred_element_type=_F32))
        tot = jnp.where(fwd, cum[c - 1:c, :], cum[0:1, :])
        dec = jnp.exp(tot)
        kdec = (kf * jnp.exp(tot - cum)).astype(_BF16)
        vb = v.astype(_BF16)
        if emit:
            mid = cum[c // 2:c // 2 + 1, :]
            qd = qs * jnp.exp(cum - mid)
            kd = (kf * jnp.exp(mid - cum)).astype(_BF16)
            qe = (qs * jnp.exp(cum)).astype(_BF16)
        for p in range(HG_PAIRS):
            ls = slice(p * LANES, (p + 1) * LANES)
            st = st_ref[p]
            if emit:
                qd_p = qd[:, ls]
                q2 = jnp.concatenate([jnp.where(head0, qd_p, 0.0), jnp.where(head0, 0.0, qd_p)], axis=0)
                sc = lax.dot_general(q2.astype(_BF16), kd[:, ls], _NT, preferred_element_type=_F32)
                sc = jnp.where(causal2, sc, 0.0)
                oi = jnp.dot(sc.astype(_BF16), vb[:, ls], preferred_element_type=_F32)
                o = jnp.where(head0, oi[:c], oi[c:])
                o = o + lax.dot_general(qe[:, ls], st.astype(_BF16), _NT, preferred_element_type=_F32)
                out_ref[sl, ls] = o
            upd = lax.dot_general(vb[:, ls], kdec[:, ls], _TN, preferred_element_type=_F32)
            st_ref[p] = st * dec[:, ls] + jnp.where(same_head, upd, 0.0)
        return carry

    lax.fori_loop(0, n_chunks, body, 0, unroll=2)

    if not emit:
        @pl.when(tb == pl.num_programs(2) - 1)
        def _():
            out_ref[...] = st_ref[...]


def _hgrn_scan(hg, lb, s0, tb, emit):
    bsz, l, _ = hg.shape
    tb = min(tb, l)
    nb = l // tb
    w = HG_WIDTH
    col = lambda j: pl.BlockSpec((None, tb, w), lambda d, b, i: (b, _dir_block(d, i, nb), j))
    f_spec = pl.BlockSpec((None, tb, w), lambda d, b, i: (b, _dir_block(d, i, nb), 1 + d))
    lb_spec = pl.BlockSpec((None, 1, w), lambda d, b, i: (d, 0, 0))
    st_spec = pl.BlockSpec((None, None, HG_PAIRS, LANES, LANES), lambda d, b, i: (d, b, 0, 0, 0))
    in_specs = [col(0), f_spec, col(3), lb_spec]
    args = [hg, hg, hg, lb]
    if s0 is not None:
        in_specs.append(st_spec)
        args.append(s0)
    if emit:
        out_spec = pl.BlockSpec((None, None, tb, w), lambda d, b, i: (d, b, _dir_block(d, i, nb), 0))
        out_shape = jax.ShapeDtypeStruct((2, bsz, l, w), _F32)
    else:
        out_spec = st_spec
        out_shape = jax.ShapeDtypeStruct((2, bsz, HG_PAIRS, LANES, LANES), _F32)
    return pl.pallas_call(
        functools.partial(_hgrn_kernel, n_chunks=tb // HG_CHUNK, emit=emit, has_init=s0 is not None),
        grid=(2, bsz, nb),
        in_specs=in_specs,
        out_specs=out_spec,
        out_shape=out_shape,
        scratch_shapes=[pltpu.VMEM((HG_PAIRS, LANES, LANES), _F32)],
        compiler_params=_cparams(("parallel", "parallel", "arbitrary")),
        name="hgrn_scan_emit" if emit else "hgrn_scan_state",
    )(*args)


def _post_kernel(y_ref, bonus_ref, grw_ref, lng_ref, lnb_ref, o_ref, ghg_ref, hgn_ref, seg_ref,
                 x_ref, w_ref, g1_ref, n2_ref, sc_ref, sh_ref, rw_ref, x1_ref, h2_ref, lg_ref):
    inv_n = 1.0 / RW_HEAD_DIM
    ysum = y_ref[0] + y_ref[1]
    mu = _seg_sum(ysum, seg_ref) * inv_n
    dlt = ysum - mu
    var = _seg_sum(dlt * dlt, seg_ref) * inv_n
    yn = dlt * lax.rsqrt(var + RW_GN_EPS) * lng_ref[...] + lnb_ref[...] + bonus_ref[...]
    y_rw = yn * grw_ref[...]
    o = o_ref[0] + o_ref[1]
    o = o * lax.rsqrt(_seg_sum(o * o, seg_ref) * inv_n + EPS) * hgn_ref[...]
    g = ghg_ref[...]
    y_hg = o * (g * jax.nn.sigmoid(g))
    y = (jnp.dot(y_rw.astype(_BF16), w_ref[:RW_WIDTH, :], preferred_element_type=_F32)
         + jnp.dot(y_hg.astype(_BF16), w_ref[RW_WIDTH:, :], preferred_element_type=_F32))
    x1 = x_ref[0] + g1_ref[0] * y
    x1_ref[0] = x1
    h2 = x1 * lax.rsqrt(jnp.mean(x1 * x1, axis=-1, keepdims=True) + EPS) * n2_ref[...]
    h2 = h2 * (1.0 + sc_ref[0]) + sh_ref[0]
    h2_ref[0] = _pack_bf16_pair(h2[:, :D_MODEL // 2], h2[:, D_MODEL // 2:])
    lg_ref[...] = lax.dot_general(rw_ref[...], h2, _NT, preferred_element_type=_F32, precision=_HI)


def _post(y_dirs, bonus, g_rw, lnx_g, lnx_b, o_dirs, hg, hg_norm_row, seg,
          x, w_out_bf16, g1, norm2_g, sc2, sh2, router_wt, tm):
    bsz, l, d = x.shape
    tm = min(tm, l)
    nt = l // tm
    w = RW_WIDTH
    row = lambda b, i: (b, i, 0)
    per_b = lambda b, i: (b, 0, 0)
    const = lambda b, i: (0, 0)
    tok = pl.BlockSpec((None, tm, w), row)
    tok2 = pl.BlockSpec((2, None, tm, w), lambda b, i: (0, b, i, 0))
    vec = pl.BlockSpec((1, w), const)
    return pl.pallas_call(
        _post_kernel,
        grid=(bsz, nt),
        in_specs=[tok2, tok, tok, vec, vec, tok2,
                  pl.BlockSpec((None, tm, w), lambda b, i: (b, i, 4)),
                  vec,
                  pl.BlockSpec((w, w), const),
                  pl.BlockSpec((1, tm, d), row),
                  pl.BlockSpec((d, d), const),
                  pl.BlockSpec((1, 1, d), per_b),
                  pl.BlockSpec((1, d), const),
                  pl.BlockSpec((1, 1, d), per_b),
                  pl.BlockSpec((1, 1, d), per_b),
                  pl.BlockSpec((N_EXPERTS, d), const)],
        out_specs=[pl.BlockSpec((1, tm, d), row),
                   pl.BlockSpec((1, tm, d // 2), row),
                   pl.BlockSpec((N_EXPERTS, tm), lambda b, i: (0, b * nt + i))],
        out_shape=[jax.ShapeDtypeStruct((bsz, l, d), _F32),
                   jax.ShapeDtypeStruct((bsz, l, d // 2), jnp.uint32),
                   jax.ShapeDtypeStruct((N_EXPERTS, bsz * l), _F32)],
        compiler_params=_cparams(("parallel", "parallel")),
        name="outproj_norm2_router",
    )(y_dirs, bonus, g_rw, lnx_g.reshape(1, w), lnx_b.reshape(1, w), o_dirs, hg, hg_norm_row, seg,
      x, w_out_bf16, g1, norm2_g.reshape(1, d), sc2, sh2, router_wt)


def _sublane_all(x, op):
    for s in (4, 2, 1):
        x = op(x, pltpu.roll(x, s, 0))
    return x


def _router_kernel(lg_ref, bias_ref, e_ref, g_ref):
    ng, gs = N_GROUPS, GROUP_SIZE
    tn = lg_ref.shape[1]
    neg = -jnp.inf
    gidx = lax.broadcasted_iota(jnp.int32, (ng, tn), 0)
    bias = jnp.concatenate([bias_ref[...]] * (tn // LANES), axis=1)
    scores = [jax.nn.sigmoid(lg_ref[j * ng:(j + 1) * ng, :]) for j in range(gs)]
    sel = [scores[j] + bias[j * ng:(j + 1) * ng, :] for j in range(gs)]

    m1 = functools.reduce(jnp.maximum, sel)
    cnt = functools.reduce(jnp.add, [(s == m1).astype(_F32) for s in sel])
    m2 = functools.reduce(jnp.maximum, [jnp.where(s < m1, s, neg) for s in sel])
    gscore = m1 + jnp.where(cnt >= 2.0, m1, m2)

    rank = jnp.zeros((ng, tn), jnp.int32)
    for s in range(1, ng):
        other = pltpu.roll(gscore, s, 0)
        beats = jnp.where(other > gscore, 1, jnp.where((other == gscore) & (gidx >= s), 1, 0))
        rank = rank + beats
    gsel = rank < TOPK_GROUPS

    key = [jnp.where(gsel, s, neg) for s in sel]
    eidx = [gidx * gs + j for j in range(gs)]
    avail = [jnp.ones((ng, tn), jnp.int32) for _ in range(gs)]
    top_e = jnp.zeros((TOP_K, tn), jnp.int32)
    top_s = jnp.zeros((TOP_K, tn), _F32)
    for r in range(TOP_K):
        cur = functools.reduce(jnp.maximum, [jnp.where(avail[j] > 0, key[j], neg) for j in range(gs)])
        mx = _sublane_all(cur, jnp.maximum)
        cand = [jnp.where((avail[j] > 0) & (key[j] == mx), eidx[j], N_EXPERTS) for j in range(gs)]
        mn = _sublane_all(functools.reduce(jnp.minimum, cand), jnp.minimum)
        picked = functools.reduce(jnp.add, [jnp.where(eidx[j] == mn, scores[j], 0.0) for j in range(gs)])
        avail = [jnp.where(eidx[j] == mn, 0, avail[j]) for j in range(gs)]
        top_e = jnp.where(gidx == r, mn, top_e)
        top_s = jnp.where(gidx == r, _sublane_all(picked, jnp.add), top_s)
    den = _sublane_all(top_s, jnp.add)
    e_ref[...] = top_e
    g_ref[...] = top_s / den * ROUTE_SCALE


def _router(logits_t, bias_rows, tn):
    assert TOP_K == N_GROUPS == SUBLANES
    t = logits_t.shape[1]
    tn = min(tn, t)
    out = pl.BlockSpec((TOP_K, tn), lambda i: (0, i))
    return pl.pallas_call(
        _router_kernel,
        grid=(t // tn,),
        in_specs=[pl.BlockSpec((N_EXPERTS, tn), lambda i: (0, i)),
                  pl.BlockSpec((N_EXPERTS, LANES), lambda i: (0, 0))],
        out_specs=[out, out],
        out_shape=[jax.ShapeDtypeStruct((TOP_K, t), jnp.int32), jax.ShapeDtypeStruct((TOP_K, t), _F32)],
        compiler_params=_cparams(("parallel",)),
        name="router_topk",
    )(logits_t, bias_rows)


def _sc_mesh():
    return plsc.VectorSubcoreMesh(core_axis_name="c", subcore_axis_name="s")


def _sc_dispatch(x_half_rows, dest_half_rows, n_out):
    n_rows = x_half_rows.shape[0]

    @pl.kernel(out_type=jax.ShapeDtypeStruct((n_out, SC_ROW), x_half_rows.dtype), mesh=_sc_mesh(),
               scratch_types=[])
    def scatter_rows(x_hbm, d_hbm, o_hbm):
        def body(x_vmem, i_vmem):
            for j in range(TOP_K):
                pltpu.sync_copy(x_vmem, o_hbm.at[i_vmem.at[j]])

        pltpu.emit_pipeline(
            body,
            grid=(n_rows // SC_WINDOW,),
            in_specs=[pl.BlockSpec((SC_WINDOW, SC_ROW), lambda i: (i, 0)),
                      pl.BlockSpec((TOP_K, SC_WINDOW), lambda i: (0, i))],
            out_specs=[],
            core_axis_name=("c", "s"),
            dimension_semantics=(pltpu.PARALLEL,),
        )(x_hbm, d_hbm)

    return scatter_rows(x_half_rows, dest_half_rows)


def _sc_gather(y_half_rows, idx):
    n = idx.shape[1]

    @pl.kernel(out_type=jax.ShapeDtypeStruct((n, SC_ROW), y_half_rows.dtype), mesh=_sc_mesh(),
               scratch_types=[])
    def gather_rows(y_hbm, i_hbm, o_hbm):
        def body(i_vmem, o_vmem):
            pltpu.sync_copy(y_hbm.at[i_vmem.at[0]], o_vmem)

        pltpu.emit_pipeline(
            body,
            grid=(n // SC_WINDOW,),
            in_specs=[pl.BlockSpec((1, SC_WINDOW), lambda i: (0, i))],
            out_specs=[pl.BlockSpec((SC_WINDOW, SC_ROW), lambda i: (i, 0))],
            core_axis_name=("c", "s"),
            dimension_semantics=(pltpu.PARALLEL,),
        )(i_hbm, o_hbm)

    return gather_rows(y_half_rows, idx)


def _dispatch_plan(top_e, n_blocks):
    onehot = top_e[:, :, None] == jnp.arange(N_EXPERTS, dtype=jnp.int32)[None, None, :]
    sel = jnp.any(onehot, axis=0).astype(jnp.int32)
    cum_incl = jnp.cumsum(sel, axis=0)
    counts = cum_incl[-1]
    padded = (counts + MOE_BLOCK - 1) // MOE_BLOCK * MOE_BLOCK
    pad_end = jnp.cumsum(padded)
    slot = cum_incl - sel + (pad_end - padded)[None, :]
    dest = jnp.sum(jnp.where(onehot, slot[None], 0), axis=-1)
    blk_start = jnp.arange(n_blocks, dtype=jnp.int32) * MOE_BLOCK
    blk_e = jnp.minimum(jnp.searchsorted(pad_end, blk_start, side='right'), N_EXPERTS - 1)
    n_used = pad_end[-1] // MOE_BLOCK
    return dest, jnp.concatenate([blk_e.astype(jnp.int32), n_used[None].astype(jnp.int32)])


def _swiglu_packed(xp, wg_ref, wu_ref, wd_ref):
    half = D_MODEL // 2
    lo, hi = _unpack_bf16_pair(xp)
    lo, hi = lo.astype(_BF16), hi.astype(_BF16)
    a = (jnp.dot(lo, wg_ref[:half, :], preferred_element_type=_F32)
         + jnp.dot(hi, wg_ref[half:, :], preferred_element_type=_F32))
    u = (jnp.dot(lo, wu_ref[:half, :], preferred_element_type=_F32)
         + jnp.dot(hi, wu_ref[half:, :], preferred_element_type=_F32))
    act = (a * jax.nn.sigmoid(a) * u).astype(_BF16)
    return jnp.dot(act, wd_ref[...], preferred_element_type=_F32)


def _expert_kernel(plan_ref, x_ref, wg_ref, wu_ref, wd_ref, y_ref):
    @pl.when(pl.program_id(0) < plan_ref[pl.num_programs(0)])
    def _():
        y = _swiglu_packed(x_ref[...], wg_ref, wu_ref, wd_ref)
        y_ref[...] = _pack_bf16_pair(y[:, :D_MODEL // 2], y[:, D_MODEL // 2:])


def _experts(plan, x_sorted, wg, wu, wd):
    n_rows, half = x_sorted.shape
    n_blocks = n_rows // MOE_BLOCK
    d = 2 * half
    grid_spec = pltpu.PrefetchScalarGridSpec(
        num_scalar_prefetch=1,
        grid=(n_blocks,),
        in_specs=[pl.BlockSpec((MOE_BLOCK, half), lambda i, plan: (i, 0)),
                  pl.BlockSpec((None, d, D_EXPERT), lambda i, plan: (plan[i], 0, 0)),
                  pl.BlockSpec((None, d, D_EXPERT), lambda i, plan: (plan[i], 0, 0)),
                  pl.BlockSpec((None, D_EXPERT, d), lambda i, plan: (plan[i], 0, 0))],
        out_specs=pl.BlockSpec((MOE_BLOCK, half), lambda i, plan: (i, 0)),
    )
    return pl.pallas_call(
        _expert_kernel,
        grid_spec=grid_spec,
        out_shape=jax.ShapeDtypeStruct((n_rows, half), jnp.uint32),
        compiler_params=_cparams(("arbitrary",)),
        name="moe_routed_experts",
    )(plan, x_sorted, wg, wu, wd)


def _combine_kernel(y_ref, gate_ref, h_ref, x1_ref, g2_ref, fg_ref, sg_ref, su_ref, sd_ref, o_ref):
    half = D_MODEL // 2
    gate = gate_ref[...]
    lo = jnp.zeros((y_ref.shape[1], half), _F32)
    hi = jnp.zeros((y_ref.shape[1], half), _F32)
    for j in range(TOP_K):
        ylo, yhi = _unpack_bf16_pair(y_ref[j])
        gj = gate[:, j:j + 1]
        lo = lo + gj * ylo
        hi = hi + gj * yhi
    moe = jnp.concatenate([lo, hi], axis=1) + _swiglu_packed(h_ref[...], sg_ref, su_ref, sd_ref)
    x2 = x1_ref[0] + g2_ref[0] * moe
    o_ref[0] = x2 * lax.rsqrt(jnp.mean(x2 * x2, axis=-1, keepdims=True) + EPS) * fg_ref[...]


def _combine(y_rows, gate_tk, h2_packed, x1, g2, final_g, sg, su, sd, tm):
    bsz, l, d = x1.shape
    tm = min(tm, l)
    nt = l // tm
    half = d // 2
    flat = lambda b, i: (b * nt + i, 0)
    const = lambda b, i: (0, 0)
    return pl.pallas_call(
        _combine_kernel,
        grid=(bsz, nt),
        in_specs=[pl.BlockSpec((TOP_K, tm, half), lambda b, i: (0, b * nt + i, 0)),
                  pl.BlockSpec((tm, TOP_K), flat),
                  pl.BlockSpec((tm, half), flat),
                  pl.BlockSpec((1, tm, d), lambda b, i: (b, i, 0)),
                  pl.BlockSpec((1, 1, d), lambda b, i: (b, 0, 0)),
                  pl.BlockSpec((1, d), const),
                  pl.BlockSpec((d, D_EXPERT), const),
                  pl.BlockSpec((d, D_EXPERT), const),
                  pl.BlockSpec((D_EXPERT, d), const)],
        out_specs=pl.BlockSpec((1, tm, d), lambda b, i: (b, i, 0)),
        out_shape=jax.ShapeDtypeStruct((bsz, l, d), _F32),
        compiler_params=_cparams(("parallel", "parallel")),
        name="moe_combine_shared_final_norm",
    )(y_rows, gate_tk, h2_packed, x1, g2, final_g.reshape(1, d), sg, su, sd)


def _to_chain_lanes(t):
    lead, (b, l, _) = t.shape[:-3], t.shape[-3:]
    nl = len(lead)
    t = t.reshape(*lead, b, l, RW_HEADS, RW_HEAD_DIM)
    t = t.transpose(*range(nl), nl + 1, nl + 3, nl, nl + 2)
    return t.reshape(*lead, l, RW_HEAD_DIM, b * RW_HEADS)


def _from_chain_lanes(t, b):
    l = t.shape[1]
    return t.reshape(2, l, RW_HEAD_DIM, b, RW_HEADS).transpose(0, 3, 1, 4, 2).reshape(2, b, l, RW_WIDTH)


def _scan_operands(prep):
    r, kk, v, w, kd, bb = prep[:6]
    return [_to_chain_lanes(t) for t in (r, kk, v)], [_to_chain_lanes(t) for t in (w, kd, bb)]


def kernel(x, c, ctx, c_ctx, w_mod, b_mod, norm1_g, norm2_g, w_in, rw_conv, rw_w0, rw_w2, rw_a0, rw_a2,
           rw_g2, rw_k_k, rw_k_a, rw_r_k, rw_lnx_g, rw_lnx_b, hg_lb_logits, hg_norm_g, w_out, router_w,
           router_b, exp_w_gate, exp_w_up, exp_w_down, sh_w_gate, sh_w_up, sh_w_down, final_norm_g):
    bsz, seq, d = x.shape
    assert w_mod.shape[0] == 1 and bsz * RW_HEADS == LANES
    lyr = 0

    c_all = jnp.concatenate([c, c_ctx[None, :], jnp.zeros((SUBLANES - 1, d), _F32)], axis=0)
    mod = _modulation(c_all, w_mod[lyr], b_mod[lyr])
    sh1, sc1, g1, sh2, sc2, g2 = [m[:, None, :] for m in jnp.split(mod[:bsz], N_MOD, axis=-1)]
    mod_ctx = jnp.broadcast_to(mod[bsz][None, None, :], (bsz, 1, N_MOD * d))
    csh1, csc1 = mod_ctx[..., :d], mod_ctx[..., d:2 * d]

    w_in_bf16 = w_in[lyr].astype(_BF16)
    rkv_lat, lora_lat, hg_lat = _inproj(x, norm1_g[lyr], sc1, sh1, w_in_bf16, tm=256)
    rkv_ctx, lora_ctx, hg_ctx = _inproj(ctx, norm1_g[lyr], csc1, csh1, w_in_bf16, tm=256)

    kern9 = rw_conv[lyr].reshape(CONV_K * CONV_K, RKV_W)
    lane_head = jnp.arange(RW_WIDTH) // RW_HEAD_DIM
    seg = (lane_head[:, None] == lane_head[None, :]).astype(_BF16)
    rw_p = (kern9, rw_w0[lyr], rw_w2[lyr], rw_a0[lyr], rw_a2[lyr], rw_g2[lyr], rw_k_k[lyr], rw_k_a[lyr],
            rw_r_k[lyr], seg)
    prep_lat = _rw_prep(rkv_lat, lora_lat, *rw_p, tq=256, grid_conv=True)
    prep_ctx = _rw_prep(rkv_ctx, lora_ctx, *rw_p, tq=256, grid_conv=False)
    s_rw = _rwkv_scan(*_scan_operands(prep_ctx), None, tt=16, emit=False)
    y_dirs = _from_chain_lanes(_rwkv_scan(*_scan_operands(prep_lat), s_rw, tt=16, emit=True), bsz)

    lb = jnp.cumsum(jax.nn.softmax(hg_lb_logits.astype(_F32), axis=1), axis=1)[:, lyr][:, None, :]
    s_hg = _hgrn_scan(hg_ctx, lb, None, tb=256, emit=False)
    o_dirs = _hgrn_scan(hg_lat, lb, s_hg, tb=256, emit=True)

    perm = lambda t: t.reshape(N_GROUPS, GROUP_SIZE, -1).transpose(1, 0, 2).reshape(N_EXPERTS, -1)
    router_wt = perm(router_w[lyr].T)
    bias_rows = jnp.broadcast_to(perm(router_b[lyr][:, None]), (N_EXPERTS, LANES))
    hg_norm_row = jnp.tile(hg_norm_g[lyr], HG_HEADS).reshape(1, HG_WIDTH)
    x1, h2, logits_t = _post(y_dirs, prep_lat[6], prep_lat[7], rw_lnx_g[lyr], rw_lnx_b[lyr], o_dirs, hg_lat,
                             hg_norm_row, seg, x, w_out[lyr].astype(_BF16), g1, norm2_g[lyr], sc2, sh2,
                             router_wt, tm=512)
    top_e, gate = _router(logits_t, bias_rows, tn=512)

    n_tok = bsz * seq
    n_blocks = n_tok * TOP_K // MOE_BLOCK + N_EXPERTS
    dest, plan = _dispatch_plan(top_e, n_blocks)
    dest_half = (2 * dest[:, :, None] + jnp.arange(2, dtype=jnp.int32)).reshape(TOP_K, 2 * n_tok)
    x_sorted = _sc_dispatch(h2.reshape(2 * n_tok, SC_ROW), dest_half, 2 * n_blocks * MOE_BLOCK)
    bf = lambda t: t.astype(_BF16)
    y_sorted = _experts(plan, x_sorted.reshape(n_blocks * MOE_BLOCK, d // 2),
                        bf(exp_w_gate[lyr]), bf(exp_w_up[lyr]), bf(exp_w_down[lyr]))
    y_rows = _sc_gather(y_sorted.reshape(2 * n_blocks * MOE_BLOCK, SC_ROW),
                        dest_half.reshape(1, TOP_K * 2 * n_tok))
    return _combine(y_rows.reshape(TOP_K, n_tok, d // 2), gate.T, h2.reshape(n_tok, d // 2), x1, g2,
                    final_norm_g, bf(sh_w_gate[lyr]), bf(sh_w_up[lyr]), bf(sh_w_down[lyr]), tm=512)
```

```python
import functools

import jax
import jax.numpy as jnp
from jax import lax
from jax.experimental import pallas as pl
from jax.experimental.pallas import tpu as pltpu
from jax.experimental.pallas import tpu_sc as plsc

D_MODEL = 1024
GRID_W = 64
N_MOD = 6
EPS = 1e-6

RW_HEAD_DIM = 64
RW_HEADS = 8
RW_WIDTH = 512
RW_LORA = 64
RW_GATE_LORA = 128
RW_GN_EPS = 1e-5 * RW_HEAD_DIM
CONV_K = 3

HG_HEADS = 8
HG_DIM = 64
HG_WIDTH = 512
HG_CHUNK = 32
HG_PAIRS = HG_HEADS // 2

RKV_W = 3 * RW_WIDTH
LORA_W = 4 * RW_LORA + RW_GATE_LORA
HG_W = 5 * HG_WIDTH
P_IN = RKV_W + LORA_W + HG_W

N_EXPERTS = 64
N_GROUPS = 8
GROUP_SIZE = N_EXPERTS // N_GROUPS
TOPK_GROUPS = 4
TOP_K = 8
D_EXPERT = 256
ROUTE_SCALE = 2.5
MOE_BLOCK = 512
SC_ROW = 256
SC_WINDOW = 128

LANES = 128
SUBLANES = 8
VMEM_LIMIT = 48 * 1024 * 1024

_HI = lax.Precision.HIGHEST
_F32 = jnp.float32
_BF16 = jnp.bfloat16
_NT = (((1,), (1,)), ((), ()))
_TN = (((0,), (0,)), ((), ()))


def _cparams(sem):
    return pltpu.CompilerParams(dimension_semantics=sem, vmem_limit_bytes=VMEM_LIMIT)


def _pack_bf16_pair(lo, hi):
    lo_bits = lax.bitcast_convert_type(lo.astype(_BF16).astype(_F32), jnp.uint32)
    hi_bits = lax.bitcast_convert_type(hi.astype(_BF16).astype(_F32), jnp.uint32)
    return (lo_bits >> 16) | (hi_bits & jnp.uint32(0xFFFF0000))


def _unpack_bf16_pair(u):
    lo = lax.bitcast_convert_type(u << 16, _F32)
    hi = lax.bitcast_convert_type(u & jnp.uint32(0xFFFF0000), _F32)
    return lo, hi


def _pack_quarters(x):
    q = x.shape[1] // 4
    return jnp.stack([_pack_bf16_pair(x[:, 2 * h * q:(2 * h + 1) * q], x[:, (2 * h + 1) * q:(2 * h + 2) * q])
                      for h in range(2)])


def _unpack_quarters(u0, u1):
    return _unpack_bf16_pair(u0) + _unpack_bf16_pair(u1)


def _dir_block(d, i, nb):
    return i + d * (nb - 1 - 2 * i)


def _mod_kernel(c_ref, w_ref, b_ref, o_ref):
    c = c_ref[...]
    s = c * jax.nn.sigmoid(c)
    o_ref[...] = jnp.dot(s, w_ref[...], preferred_element_type=_F32, precision=_HI) + b_ref[...]


def _modulation(c_all, w_mod, b_mod):
    rows, d = c_all.shape
    n = w_mod.shape[1]
    tn = 1024
    return pl.pallas_call(
        _mod_kernel,
        grid=(n // tn,),
        in_specs=[pl.BlockSpec((rows, d), lambda j: (0, 0)),
                  pl.BlockSpec((d, tn), lambda j: (0, j)),
                  pl.BlockSpec((1, tn), lambda j: (0, j))],
        out_specs=pl.BlockSpec((rows, tn), lambda j: (0, j)),
        out_shape=jax.ShapeDtypeStruct((rows, n), _F32),
        compiler_params=_cparams(("parallel",)),
        name="modulation",
    )(c_all, w_mod, b_mod.reshape(1, n))


def _inproj_kernel(x_ref, g_ref, sc_ref, sh_ref, w_ref, rkv_ref, lora_ref, hg_ref):
    x = x_ref[0]
    y = x * lax.rsqrt(jnp.mean(x * x, axis=-1, keepdims=True) + EPS) * g_ref[...]
    h = y * (1.0 + sc_ref[0]) + sh_ref[0]
    u = jnp.dot(h.astype(_BF16), w_ref[...], preferred_element_type=_F32)
    rkv_ref[0] = u[:, :RKV_W]
    lora_ref[0] = u[:, RKV_W:RKV_W + LORA_W]
    hg_ref[0] = u[:, RKV_W + LORA_W:]


def _inproj(x, norm_g, sc, sh, w_in_bf16, tm):
    bsz, l, d = x.shape
    tm = min(tm, l)
    row = lambda b, i: (b, i, 0)
    per_b = lambda b, i: (b, 0, 0)
    return pl.pallas_call(
        _inproj_kernel,
        grid=(bsz, l // tm),
        in_specs=[pl.BlockSpec((1, tm, d), row),
                  pl.BlockSpec((1, d), lambda b, i: (0, 0)),
                  pl.BlockSpec((1, 1, d), per_b),
                  pl.BlockSpec((1, 1, d), per_b),
                  pl.BlockSpec((d, P_IN), lambda b, i: (0, 0))],
        out_specs=[pl.BlockSpec((1, tm, RKV_W), row),
                   pl.BlockSpec((1, tm, LORA_W), row),
                   pl.BlockSpec((1, tm, HG_W), row)],
        out_shape=[jax.ShapeDtypeStruct((bsz, l, RKV_W), _F32),
                   jax.ShapeDtypeStruct((bsz, l, LORA_W), _F32),
                   jax.ShapeDtypeStruct((bsz, l, HG_W), _F32)],
        compiler_params=_cparams(("parallel", "parallel")),
        name="inproj",
    )(x, norm_g.reshape(1, d), sc, sh, w_in_bf16)


def _seg_sum(x, seg_ref):
    hi = x.astype(_BF16)
    lo = (x - hi.astype(_F32)).astype(_BF16)
    return (jnp.dot(hi, seg_ref[...], preferred_element_type=_F32)
            + jnp.dot(lo, seg_ref[...], preferred_element_type=_F32))


def _rw_prep_kernel(cur_ref, prev_ref, next_ref, lora_ref, kern_ref, w0_ref, w2_ref, a0_ref, a2_ref, g2_ref,
                    kk_ref, ka_ref, rk_ref, seg_ref,
                    r_out, kkn_out, v_out, w_out, kd_out, b_out, bonus_out, g_out, *, grid_conv):
    i = pl.program_id(1)
    tq = cur_ref.shape[0]
    hw = prev_ref.shape[0]
    w = RW_WIDTH
    pmask = jnp.where(i > 0, 1.0, 0.0)
    nmask = jnp.where(i < pl.num_programs(1) - 1, 1.0, 0.0)
    xpos = lax.broadcasted_iota(jnp.int32, (tq + 2 * hw, w), 0) % GRID_W
    conv = []
    for part in range(3):
        cs = slice(part * w, (part + 1) * w)
        xe = jnp.concatenate([prev_ref[:, cs] * pmask, cur_ref[:, cs], next_ref[:, cs] * nmask], axis=0)
        xl = pltpu.roll(xe, 1, 0)
        xr = pltpu.roll(xe, tq + 2 * hw - 1, 0)
        if grid_conv:
            xl = jnp.where(xpos == 0, 0.0, xl)
            xr = jnp.where(xpos == GRID_W - 1, 0.0, xr)
        acc = None
        for dy in range(CONV_K) if grid_conv else (CONV_K // 2,):
            off = hw + (dy - 1) * GRID_W
            for dx, src in enumerate((xl, xe, xr)):
                term = src[off:off + tq] * kern_ref[pl.ds(dy * CONV_K + dx, 1), cs]
                acc = term if acc is None else acc + term
        conv.append(acc)
    r, k, v = conv
    r_out[...] = r
    v_out[...] = v

    kk = k * kk_ref[...]
    kk = kk / jnp.maximum(jnp.sqrt(_seg_sum(kk * kk, seg_ref)), 1e-12)
    kkn_out[...] = kk

    lora = lora_ref[...]
    ksum = jnp.zeros_like(k)
    for d in range(2):
        wl = lora[:, d * RW_LORA:(d + 1) * RW_LORA]
        al = lora[:, (2 + d) * RW_LORA:(3 + d) * RW_LORA]
        z = w0_ref[pl.ds(d, 1), :] + jnp.dot(jnp.tanh(wl), w2_ref[d], preferred_element_type=_F32,
                                             precision=_HI)
        wlog = -jax.nn.softplus(-z) - 0.5
        w_out[d] = jnp.exp(-jnp.exp(wlog))
        a = jax.nn.sigmoid(a0_ref[pl.ds(d, 1), :] + jnp.dot(al, a2_ref[d], preferred_element_type=_F32,
                                                             precision=_HI))
        kd = k * (1.0 + (a - 1.0) * ka_ref[...])
        kd_out[d] = kd
        b_out[d] = kk * a
        ksum = ksum + kd
    bonus_out[...] = _seg_sum(r * ksum * rk_ref[...], seg_ref) * v
    gd = lora[:, 4 * RW_LORA:]
    g_out[...] = jnp.dot(jax.nn.sigmoid(gd), g2_ref[...], preferred_element_type=_F32, precision=_HI)


def _rw_prep(rkv, lora, kern9, w0, w2, a0, a2, g2, k_k, k_a, r_k, seg, tq, grid_conv):
    bsz, l, _ = rkv.shape
    tq = min(tq, l)
    nb = l // tq
    hw = GRID_W
    per = tq // hw
    w = RW_WIDTH
    row = lambda b, i: (b, i, 0)
    const2 = lambda b, i: (0, 0)
    const3 = lambda b, i: (0, 0, 0)
    tok = pl.BlockSpec((None, tq, w), row)
    tok2 = pl.BlockSpec((2, None, tq, w), lambda b, i: (0, b, i, 0))
    one = jax.ShapeDtypeStruct((bsz, l, w), _F32)
    two = jax.ShapeDtypeStruct((2, bsz, l, w), _F32)
    vec = lambda t: t.reshape(1, w)
    return pl.pallas_call(
        functools.partial(_rw_prep_kernel, grid_conv=grid_conv),
        grid=(bsz, nb),
        in_specs=[pl.BlockSpec((None, tq, RKV_W), row),
                  pl.BlockSpec((None, hw, RKV_W), lambda b, i: (b, jnp.maximum(i * per - 1, 0), 0)),
                  pl.BlockSpec((None, hw, RKV_W), lambda b, i: (b, jnp.minimum((i + 1) * per, l // hw - 1), 0)),
                  pl.BlockSpec((None, tq, LORA_W), row),
                  pl.BlockSpec((CONV_K * CONV_K, RKV_W), const2),
                  pl.BlockSpec((2, w), const2),
                  pl.BlockSpec((2, RW_LORA, w), const3),
                  pl.BlockSpec((2, w), const2),
                  pl.BlockSpec((2, RW_LORA, w), const3),
                  pl.BlockSpec((RW_GATE_LORA, w), const2),
                  pl.BlockSpec((1, w), const2),
                  pl.BlockSpec((1, w), const2),
                  pl.BlockSpec((1, w), const2),
                  pl.BlockSpec((w, w), const2)],
        out_specs=[tok, tok, tok, tok2, tok2, tok2, tok, tok],
        out_shape=[one, one, one, two, two, two, one, one],
        compiler_params=_cparams(("parallel", "parallel")),
        name="rwkv_prep_grid" if grid_conv else "rwkv_prep_seq",
    )(rkv, rkv, rkv, lora, kern9, w0, w2, a0, a2, g2, vec(k_k), vec(k_a), vec(r_k), seg)


def _rwkv_scan_kernel(*refs, tt, emit, has_init):
    r_ref, kk_ref, v_ref, w_ref, k_ref, b_ref = refs[:6]
    rest = list(refs[6:])
    s0_ref = rest.pop(0) if has_init else None
    out_ref = rest.pop(0)
    s_ref, sa_ref = rest
    d = pl.program_id(0)
    i = pl.program_id(1)
    n = RW_HEAD_DIM

    @pl.when(i == 0)
    def _():
        s_ref[...] = s0_ref[...] if has_init else jnp.zeros_like(s_ref)

    hv = n // 2
    halves = [pl.ds(h * hv, hv) for h in range(2)]

    def project(t, vs):
        acc = [jnp.zeros((hv, LANES), _F32), jnp.zeros((hv, LANES), _F32)]
        for k in range(n):
            acc[k % 2] = acc[k % 2] + s_ref[k, vs, :] * kk_ref[t, pl.ds(k, 1), :]
        return -(acc[0] + acc[1])

    first = d * (tt - 1)
    for vs in halves:
        sa_ref[vs, :] = project(first, vs)

    def step(j, carry):
        t = j + d * (tt - 1 - 2 * j)
        tn = jnp.clip(t + 1 - 2 * d, 0, tt - 1)
        for vs in halves:
            sa = sa_ref[vs, :]
            vv = v_ref[t, vs, :]
            y = [jnp.zeros((hv, LANES), _F32), jnp.zeros((hv, LANES), _F32)]
            nsa = [jnp.zeros((hv, LANES), _F32), jnp.zeros((hv, LANES), _F32)]
            for k in range(n):
                row = pl.ds(k, 1)
                s_new = s_ref[k, vs, :] * w_ref[t, row, :] + sa * b_ref[t, row, :] + vv * k_ref[t, row, :]
                s_ref[k, vs, :] = s_new
                nsa[k % 2] = nsa[k % 2] + s_new * kk_ref[tn, row, :]
                if emit:
                    y[k % 2] = y[k % 2] + s_new * r_ref[t, row, :]
            sa_ref[vs, :] = -(nsa[0] + nsa[1])
            if emit:
                out_ref[t, vs, :] = y[0] + y[1]
        return carry

    lax.fori_loop(0, tt, step, 0)

    if not emit:
        @pl.when(i == pl.num_programs(1) - 1)
        def _():
            out_ref[...] = s_ref[...]


def _rwkv_scan(shared, per_dir, s0, tt, emit):
    l, n, c = shared[0].shape
    nb = l // tt
    sh_spec = pl.BlockSpec((tt, n, c), lambda d, i: (_dir_block(d, i, nb), 0, 0))
    pd_spec = pl.BlockSpec((None, tt, n, c), lambda d, i: (d, _dir_block(d, i, nb), 0, 0))
    st_spec = pl.BlockSpec((None, n, n, c), lambda d, i: (d, 0, 0, 0))
    in_specs = [sh_spec] * 3 + [pd_spec] * 3
    args = list(shared) + list(per_dir)
    if s0 is not None:
        in_specs.append(st_spec)
        args.append(s0)
    if emit:
        out_spec, out_shape = pd_spec, jax.ShapeDtypeStruct((2, l, n, c), _F32)
    else:
        out_spec, out_shape = st_spec, jax.ShapeDtypeStruct((2, n, n, c), _F32)
    return pl.pallas_call(
        functools.partial(_rwkv_scan_kernel, tt=tt, emit=emit, has_init=s0 is not None),
        grid=(2, nb),
        in_specs=in_specs,
        out_specs=out_spec,
        out_shape=out_shape,
        scratch_shapes=[pltpu.VMEM((n, n, c), _F32), pltpu.VMEM((n, c), _F32)],
        compiler_params=_cparams(("parallel", "arbitrary")),
        name="rwkv_scan_emit" if emit else "rwkv_scan_state",
    )(*args)


def _hgrn_kernel(*refs, n_chunks, emit, has_init):
    q_ref, f_ref, i_ref, lb_ref = refs[:4]
    rest = list(refs[4:])
    s0_ref = rest.pop(0) if has_init else None
    out_ref = rest.pop(0)
    st_ref = rest.pop(0)
    c = HG_CHUNK
    d = pl.program_id(0)
    tb = pl.program_id(2)
    fwd = d == 0

    @pl.when(tb == 0)
    def _():
        st_ref[...] = s0_ref[...] if has_init else jnp.zeros_like(st_ref)

    row = lax.broadcasted_iota(jnp.int32, (c, c), 0)
    col = lax.broadcasted_iota(jnp.int32, (c, c), 1)
    causal = jnp.where(fwd, (row >= col).astype(_F32), (row <= col).astype(_F32))
    tri = causal.astype(_BF16)
    causal2 = jnp.concatenate([causal, causal], axis=0) > 0.5
    head0 = lax.broadcasted_iota(jnp.int32, (c, LANES), 1) < HG_DIM
    same_head = ((lax.broadcasted_iota(jnp.int32, (LANES, LANES), 0) < HG_DIM)
                 == (lax.broadcasted_iota(jnp.int32, (LANES, LANES), 1) < HG_DIM))
    lb = lb_ref[...]
    omlb = 1.0 - lb

    def body(jj, carry):
        cj = jnp.where(fwd, jj, n_chunks - 1 - jj)
        sl = pl.ds(pl.multiple_of(cj * c, c), c)
        fr = f_ref[sl, :]
        logf = jnp.log(lb + omlb * jax.nn.sigmoid(fr))
        kf = omlb * jax.nn.sigmoid(-fr)
        q = q_ref[sl, :]
        qs = q * jax.nn.sigmoid(q) * (HG_DIM ** -0.5)
        v = i_ref[sl, :]
        hi = logf.astype(_BF16)
        lo = (logf - hi.astype(_F32)).astype(_BF16)
        cum = (jnp.dot(tri, hi, preferred_element_type=_F32)
               + jnp.dot(tri, lo, preferred_element_type=_F32))
        tot = jnp.where(fwd, cum[c - 1:c, :], cum[0:1, :])
        dec = jnp.exp(tot)
        kdec = (kf * jnp.exp(tot - cum)).astype(_BF16)
        vb = v.astype(_BF16)
        if emit:
            mid = cum[c // 2:c // 2 + 1, :]
            qd = qs * jnp.exp(cum - mid)
            kd = (kf * jnp.exp(mid - cum)).astype(_BF16)
            qe = (qs * jnp.exp(cum)).astype(_BF16)
        for p in range(HG_PAIRS):
            ls = slice(p * LANES, (p + 1) * LANES)
            st = st_ref[p]
            if emit:
                qd_p = qd[:, ls]
                q2 = jnp.concatenate([jnp.where(head0, qd_p, 0.0), jnp.where(head0, 0.0, qd_p)], axis=0)
                sc = lax.dot_general(q2.astype(_BF16), kd[:, ls], _NT, preferred_element_type=_F32)
                sc = jnp.where(causal2, sc, 0.0)
                oi = jnp.dot(sc.astype(_BF16), vb[:, ls], preferred_element_type=_F32)
                o = jnp.where(head0, oi[:c], oi[c:])
                o = o + lax.dot_general(qe[:, ls], st.astype(_BF16), _NT, preferred_element_type=_F32)
                out_ref[sl, ls] = o
            upd = lax.dot_general(vb[:, ls], kdec[:, ls], _TN, preferred_element_type=_F32)
            st_ref[p] = st * dec[:, ls] + jnp.where(same_head, upd, 0.0)
        return carry

    lax.fori_loop(0, n_chunks, body, 0, unroll=2)

    if not emit:
        @pl.when(tb == pl.num_programs(2) - 1)
        def _():
            out_ref[...] = st_ref[...]


def _hgrn_scan(hg, lb, s0, tb, emit):
    bsz, l, _ = hg.shape
    tb = min(tb, l)
    nb = l // tb
    w = HG_WIDTH
    col = lambda j: pl.BlockSpec((None, tb, w), lambda d, b, i: (b, _dir_block(d, i, nb), j))
    f_spec = pl.BlockSpec((None, tb, w), lambda d, b, i: (b, _dir_block(d, i, nb), 1 + d))
    lb_spec = pl.BlockSpec((None, 1, w), lambda d, b, i: (d, 0, 0))
    st_spec = pl.BlockSpec((None, None, HG_PAIRS, LANES, LANES), lambda d, b, i: (d, b, 0, 0, 0))
    in_specs = [col(0), f_spec, col(3), lb_spec]
    args = [hg, hg, hg, lb]
    if s0 is not None:
        in_specs.append(st_spec)
        args.append(s0)
    if emit:
        out_spec = pl.BlockSpec((None, None, tb, w), lambda d, b, i: (d, b, _dir_block(d, i, nb), 0))
        out_shape = jax.ShapeDtypeStruct((2, bsz, l, w), _F32)
    else:
        out_spec = st_spec
        out_shape = jax.ShapeDtypeStruct((2, bsz, HG_PAIRS, LANES, LANES), _F32)
    return pl.pallas_call(
        functools.partial(_hgrn_kernel, n_chunks=tb // HG_CHUNK, emit=emit, has_init=s0 is not None),
        grid=(2, bsz, nb),
        in_specs=in_specs,
        out_specs=out_spec,
        out_shape=out_shape,
        scratch_shapes=[pltpu.VMEM((HG_PAIRS, LANES, LANES), _F32)],
        compiler_params=_cparams(("parallel", "parallel", "arbitrary")),
        name="hgrn_scan_emit" if emit else "hgrn_scan_state",
    )(*args)


def _post_kernel(y_ref, bonus_ref, grw_ref, lng_ref, lnb_ref, o_ref, ghg_ref, hgn_ref, seg_ref,
                 x_ref, w_ref, g1_ref, n2_ref, sc_ref, sh_ref, rw_ref, x1_ref, h2_ref, lg_ref):
    inv_n = 1.0 / RW_HEAD_DIM
    ysum = y_ref[0] + y_ref[1]
    mu = _seg_sum(ysum, seg_ref) * inv_n
    dlt = ysum - mu
    var = _seg_sum(dlt * dlt, seg_ref) * inv_n
    yn = dlt * lax.rsqrt(var + RW_GN_EPS) * lng_ref[...] + lnb_ref[...] + bonus_ref[...]
    y_rw = yn * grw_ref[...]
    o = o_ref[0] + o_ref[1]
    o = o * lax.rsqrt(_seg_sum(o * o, seg_ref) * inv_n + EPS) * hgn_ref[...]
    g = ghg_ref[...]
    y_hg = o * (g * jax.nn.sigmoid(g))
    y = (jnp.dot(y_rw.astype(_BF16), w_ref[:RW_WIDTH, :], preferred_element_type=_F32)
         + jnp.dot(y_hg.astype(_BF16), w_ref[RW_WIDTH:, :], preferred_element_type=_F32))
    x1 = x_ref[0] + g1_ref[0] * y
    x1_ref[0] = x1
    h2 = x1 * lax.rsqrt(jnp.mean(x1 * x1, axis=-1, keepdims=True) + EPS) * n2_ref[...]
    h2 = h2 * (1.0 + sc_ref[0]) + sh_ref[0]
    h2_ref[...] = _pack_quarters(h2)
    lg_ref[...] = lax.dot_general(rw_ref[...], h2, _NT, preferred_element_type=_F32, precision=_HI)


def _post(y_dirs, bonus, g_rw, lnx_g, lnx_b, o_dirs, hg, hg_norm_row, seg,
          x, w_out_bf16, g1, norm2_g, sc2, sh2, router_wt, tm):
    bsz, l, d = x.shape
    tm = min(tm, l)
    nt = l // tm
    w = RW_WIDTH
    row = lambda b, i: (b, i, 0)
    per_b = lambda b, i: (b, 0, 0)
    const = lambda b, i: (0, 0)
    tok = pl.BlockSpec((None, tm, w), row)
    tok2 = pl.BlockSpec((2, None, tm, w), lambda b, i: (0, b, i, 0))
    vec = pl.BlockSpec((1, w), const)
    return pl.pallas_call(
        _post_kernel,
        grid=(bsz, nt),
        in_specs=[tok2, tok, tok, vec, vec, tok2,
                  pl.BlockSpec((None, tm, w), lambda b, i: (b, i, 4)),
                  vec,
                  pl.BlockSpec((w, w), const),
                  pl.BlockSpec((1, tm, d), row),
                  pl.BlockSpec((d, d), const),
                  pl.BlockSpec((1, 1, d), per_b),
                  pl.BlockSpec((1, d), const),
                  pl.BlockSpec((1, 1, d), per_b),
                  pl.BlockSpec((1, 1, d), per_b),
                  pl.BlockSpec((N_EXPERTS, d), const)],
        out_specs=[pl.BlockSpec((1, tm, d), row),
                   pl.BlockSpec((2, tm, d // 4), lambda b, i: (0, b * nt + i, 0)),
                   pl.BlockSpec((N_EXPERTS, tm), lambda b, i: (0, b * nt + i))],
        out_shape=[jax.ShapeDtypeStruct((bsz, l, d), _F32),
                   jax.ShapeDtypeStruct((2, bsz * l, d // 4), jnp.uint32),
                   jax.ShapeDtypeStruct((N_EXPERTS, bsz * l), _F32)],
        compiler_params=_cparams(("parallel", "parallel")),
        name="outproj_norm2_router",
    )(y_dirs, bonus, g_rw, lnx_g.reshape(1, w), lnx_b.reshape(1, w), o_dirs, hg, hg_norm_row, seg,
      x, w_out_bf16, g1, norm2_g.reshape(1, d), sc2, sh2, router_wt)


def _sublane_all(x, op):
    for s in (4, 2, 1):
        x = op(x, pltpu.roll(x, s, 0))
    return x


def _router_kernel(lg_ref, bias_ref, tri_ref, e_ref, g_ref, rank_ref, cnt_ref):
    ng, gs = N_GROUPS, GROUP_SIZE
    tn = lg_ref.shape[1]
    neg = -jnp.inf
    gidx = lax.broadcasted_iota(jnp.int32, (ng, tn), 0)
    bias = jnp.concatenate([bias_ref[...]] * (tn // LANES), axis=1)
    scores = [jax.nn.sigmoid(lg_ref[j * ng:(j + 1) * ng, :]) for j in range(gs)]
    sel = [scores[j] + bias[j * ng:(j + 1) * ng, :] for j in range(gs)]

    m1 = functools.reduce(jnp.maximum, sel)
    cnt = functools.reduce(jnp.add, [(s == m1).astype(_F32) for s in sel])
    m2 = functools.reduce(jnp.maximum, [jnp.where(s < m1, s, neg) for s in sel])
    gscore = m1 + jnp.where(cnt >= 2.0, m1, m2)

    rank = jnp.zeros((ng, tn), jnp.int32)
    for s in range(1, ng):
        other = pltpu.roll(gscore, s, 0)
        beats = jnp.where(other > gscore, 1, jnp.where((other == gscore) & (gidx >= s), 1, 0))
        rank = rank + beats
    gsel = rank < TOPK_GROUPS

    key = [jnp.where(gsel, s, neg) for s in sel]
    eidx = [gidx * gs + j for j in range(gs)]
    avail = [jnp.ones((ng, tn), jnp.int32) for _ in range(gs)]
    top_e = jnp.zeros((TOP_K, tn), jnp.int32)
    top_s = jnp.zeros((TOP_K, tn), _F32)
    for r in range(TOP_K):
        cur = functools.reduce(jnp.maximum, [jnp.where(avail[j] > 0, key[j], neg) for j in range(gs)])
        mx = _sublane_all(cur, jnp.maximum)
        cand = [jnp.where((avail[j] > 0) & (key[j] == mx), eidx[j], N_EXPERTS) for j in range(gs)]
        mn = _sublane_all(functools.reduce(jnp.minimum, cand), jnp.minimum)
        picked = functools.reduce(jnp.add, [jnp.where(eidx[j] == mn, scores[j], 0.0) for j in range(gs)])
        avail = [jnp.where(eidx[j] == mn, 0, avail[j]) for j in range(gs)]
        top_e = jnp.where(gidx == r, mn, top_e)
        top_s = jnp.where(gidx == r, _sublane_all(picked, jnp.add), top_s)
    den = _sublane_all(top_s, jnp.add)
    e_ref[...] = top_e
    g_ref[...] = top_s / den * ROUTE_SCALE

    @pl.when(pl.program_id(0) == 0)
    def _():
        cnt_ref[...] = jnp.zeros_like(cnt_ref)

    taken = jnp.concatenate([jnp.where(a == 0, 1.0, 0.0) for a in avail], axis=0)
    before = jnp.dot(taken.astype(_BF16), tri_ref[...], preferred_element_type=_F32) - taken
    before = before + jnp.concatenate([cnt_ref[...]] * (tn // LANES), axis=1)
    rank = jnp.zeros((TOP_K, tn), _F32)
    for r in range(TOP_K):
        e_r = jnp.broadcast_to(top_e[r:r + 1, :], (ng, tn))
        hit = functools.reduce(jnp.add, [jnp.where(eidx[j] == e_r, before[j * ng:(j + 1) * ng, :], 0.0)
                                         for j in range(gs)])
        rank = jnp.where(gidx == r, _sublane_all(hit, jnp.add), rank)
    rank_ref[...] = rank.astype(jnp.int32)
    ones = jnp.ones((tn, LANES), _BF16)
    cnt_ref[...] += jnp.dot(taken.astype(_BF16), ones, preferred_element_type=_F32)


def _router(logits_t, bias_rows, tn):
    assert TOP_K == N_GROUPS == SUBLANES
    t = logits_t.shape[1]
    tn = min(tn, t)
    tri = (jnp.arange(tn)[:, None] <= jnp.arange(tn)[None, :]).astype(_BF16)
    out = pl.BlockSpec((TOP_K, tn), lambda i: (0, i))
    cnt = pl.BlockSpec((N_EXPERTS, LANES), lambda i: (0, 0))
    return pl.pallas_call(
        _router_kernel,
        grid=(t // tn,),
        in_specs=[pl.BlockSpec((N_EXPERTS, tn), lambda i: (0, i)), cnt,
                  pl.BlockSpec((tn, tn), lambda i: (0, 0))],
        out_specs=[out, out, out, cnt],
        out_shape=[jax.ShapeDtypeStruct((TOP_K, t), jnp.int32), jax.ShapeDtypeStruct((TOP_K, t), _F32),
                   jax.ShapeDtypeStruct((TOP_K, t), jnp.int32),
                   jax.ShapeDtypeStruct((N_EXPERTS, LANES), _F32)],
        compiler_params=_cparams(("arbitrary",)),
        name="router_topk",
    )(logits_t, bias_rows, tri)


def _sc_mesh():
    return plsc.VectorSubcoreMesh(core_axis_name="c", subcore_axis_name="s")


def _sc_dispatch(x_half_rows, dest_half_rows, n_out):
    n_rows = x_half_rows.shape[0]

    @pl.kernel(out_type=jax.ShapeDtypeStruct((n_out, SC_ROW), x_half_rows.dtype), mesh=_sc_mesh(),
               scratch_types=[])
    def scatter_rows(x_hbm, d_hbm, o_hbm):
        def body(x_vmem, i_vmem):
            for j in range(TOP_K):
                pltpu.sync_copy(x_vmem, o_hbm.at[i_vmem.at[j]])

        pltpu.emit_pipeline(
            body,
            grid=(n_rows // SC_WINDOW,),
            in_specs=[pl.BlockSpec((SC_WINDOW, SC_ROW), lambda i: (i, 0)),
                      pl.BlockSpec((TOP_K, SC_WINDOW), lambda i: (0, i))],
            out_specs=[],
            core_axis_name=("c", "s"),
            dimension_semantics=(pltpu.PARALLEL,),
        )(x_hbm, d_hbm)

    return scatter_rows(x_half_rows, dest_half_rows)


def _sc_gather(y_half_rows, idx):
    n = idx.shape[1]

    @pl.kernel(out_type=jax.ShapeDtypeStruct((n, SC_ROW), y_half_rows.dtype), mesh=_sc_mesh(),
               scratch_types=[])
    def gather_rows(y_hbm, i_hbm, o_hbm):
        def body(i_vmem, o_vmem):
            pltpu.sync_copy(y_hbm.at[i_vmem.at[0]], o_vmem)

        pltpu.emit_pipeline(
            body,
            grid=(n // SC_WINDOW,),
            in_specs=[pl.BlockSpec((1, SC_WINDOW), lambda i: (0, i))],
            out_specs=[pl.BlockSpec((SC_WINDOW, SC_ROW), lambda i: (i, 0))],
            core_axis_name=("c", "s"),
            dimension_semantics=(pltpu.PARALLEL,),
        )(i_hbm, o_hbm)

    return gather_rows(y_half_rows, idx)


def _dispatch_plan(top_e, rank, counts, n_blocks):
    padded = (counts + MOE_BLOCK - 1) // MOE_BLOCK * MOE_BLOCK
    pad_end = jnp.cumsum(padded)
    pad_start = pad_end - padded
    onehot = top_e[:, :, None] == jnp.arange(N_EXPERTS, dtype=jnp.int32)[None, None, :]
    dest = rank + jnp.sum(jnp.where(onehot, pad_start[None, None, :], 0), axis=-1)
    blk_start = jnp.arange(n_blocks, dtype=jnp.int32) * MOE_BLOCK
    blk_e = jnp.minimum(jnp.searchsorted(pad_end, blk_start, side='right'), N_EXPERTS - 1)
    n_used = pad_end[-1] // MOE_BLOCK
    return dest, jnp.concatenate([blk_e.astype(jnp.int32), n_used[None].astype(jnp.int32)])


def _swiglu_packed(u0, u1, wg_ref, wu_ref, wd_ref):
    q = D_MODEL // 4
    parts = [p.astype(_BF16) for p in _unpack_quarters(u0, u1)]
    a = sum(jnp.dot(p, wg_ref[i * q:(i + 1) * q, :], preferred_element_type=_F32) for i, p in enumerate(parts))
    u = sum(jnp.dot(p, wu_ref[i * q:(i + 1) * q, :], preferred_element_type=_F32) for i, p in enumerate(parts))
    act = (a * jax.nn.sigmoid(a) * u).astype(_BF16)
    return jnp.dot(act, wd_ref[...], preferred_element_type=_F32)


def _expert_kernel(plan_ref, x_ref, wg_ref, wu_ref, wd_ref, y_ref):
    @pl.when(pl.program_id(0) < plan_ref[pl.num_programs(0)])
    def _():
        y_ref[...] = _pack_quarters(_swiglu_packed(x_ref[0], x_ref[1], wg_ref, wu_ref, wd_ref))


def _experts(plan, x_sorted, wg, wu, wd):
    _, n_rows, q = x_sorted.shape
    n_blocks = n_rows // MOE_BLOCK
    d = 4 * q
    rows = pl.BlockSpec((2, MOE_BLOCK, q), lambda i, plan: (0, i, 0))
    grid_spec = pltpu.PrefetchScalarGridSpec(
        num_scalar_prefetch=1,
        grid=(n_blocks,),
        in_specs=[rows,
                  pl.BlockSpec((None, d, D_EXPERT), lambda i, plan: (plan[i], 0, 0)),
                  pl.BlockSpec((None, d, D_EXPERT), lambda i, plan: (plan[i], 0, 0)),
                  pl.BlockSpec((None, D_EXPERT, d), lambda i, plan: (plan[i], 0, 0))],
        out_specs=rows,
    )
    return pl.pallas_call(
        _expert_kernel,
        grid_spec=grid_spec,
        out_shape=jax.ShapeDtypeStruct((2, n_rows, q), jnp.uint32),
        compiler_params=_cparams(("arbitrary",)),
        name="moe_routed_experts",
    )(plan, x_sorted, wg, wu, wd)


def _combine_kernel(y_ref, gate_ref, h_ref, x1_ref, g2_ref, fg_ref, sg_ref, su_ref, sd_ref, o_ref):
    gate = gate_ref[...]
    acc = None
    for j in range(TOP_K):
        gj = gate[:, j:j + 1]
        parts = [gj * p for p in _unpack_quarters(y_ref[j, 0], y_ref[j, 1])]
        acc = parts if acc is None else [a + p for a, p in zip(acc, parts)]
    moe = jnp.concatenate(acc, axis=1) + _swiglu_packed(h_ref[0], h_ref[1], sg_ref, su_ref, sd_ref)
    x2 = x1_ref[0] + g2_ref[0] * moe
    o_ref[0] = x2 * lax.rsqrt(jnp.mean(x2 * x2, axis=-1, keepdims=True) + EPS) * fg_ref[...]


def _combine(y_rows, gate_tk, h2_packed, x1, g2, final_g, sg, su, sd, tm):
    bsz, l, d = x1.shape
    tm = min(tm, l)
    nt = l // tm
    q = d // 4
    const = lambda b, i: (0, 0)
    return pl.pallas_call(
        _combine_kernel,
        grid=(bsz, nt),
        in_specs=[pl.BlockSpec((TOP_K, 2, tm, q), lambda b, i: (0, 0, b * nt + i, 0)),
                  pl.BlockSpec((tm, TOP_K), lambda b, i: (b * nt + i, 0)),
                  pl.BlockSpec((2, tm, q), lambda b, i: (0, b * nt + i, 0)),
                  pl.BlockSpec((1, tm, d), lambda b, i: (b, i, 0)),
                  pl.BlockSpec((1, 1, d), lambda b, i: (b, 0, 0)),
                  pl.BlockSpec((1, d), const),
                  pl.BlockSpec((d, D_EXPERT), const),
                  pl.BlockSpec((d, D_EXPERT), const),
                  pl.BlockSpec((D_EXPERT, d), const)],
        out_specs=pl.BlockSpec((1, tm, d), lambda b, i: (b, i, 0)),
        out_shape=jax.ShapeDtypeStruct((bsz, l, d), _F32),
        compiler_params=_cparams(("parallel", "parallel")),
        name="moe_combine_shared_final_norm",
    )(y_rows, gate_tk, h2_packed, x1, g2, final_g.reshape(1, d), sg, su, sd)


def _to_chain_lanes(t):
    lead, (b, l, _) = t.shape[:-3], t.shape[-3:]
    nl = len(lead)
    t = t.reshape(*lead, b, l, RW_HEADS, RW_HEAD_DIM)
    t = t.transpose(*range(nl), nl + 1, nl + 3, nl, nl + 2)
    return t.reshape(*lead, l, RW_HEAD_DIM, b * RW_HEADS)


def _from_chain_lanes(t, b):
    l = t.shape[1]
    return t.reshape(2, l, RW_HEAD_DIM, b, RW_HEADS).transpose(0, 3, 1, 4, 2).reshape(2, b, l, RW_WIDTH)


def _scan_operands(prep):
    r, kk, v, w, kd, bb = prep[:6]
    return [_to_chain_lanes(t) for t in (r, kk, v)], [_to_chain_lanes(t) for t in (w, kd, bb)]


def kernel(x, c, ctx, c_ctx, w_mod, b_mod, norm1_g, norm2_g, w_in, rw_conv, rw_w0, rw_w2, rw_a0, rw_a2,
           rw_g2, rw_k_k, rw_k_a, rw_r_k, rw_lnx_g, rw_lnx_b, hg_lb_logits, hg_norm_g, w_out, router_w,
           router_b, exp_w_gate, exp_w_up, exp_w_down, sh_w_gate, sh_w_up, sh_w_down, final_norm_g):
    bsz, seq, d = x.shape
    assert w_mod.shape[0] == 1 and bsz * RW_HEADS == LANES
    lyr = 0

    c_all = jnp.concatenate([c, c_ctx[None, :], jnp.zeros((SUBLANES - 1, d), _F32)], axis=0)
    mod = _modulation(c_all, w_mod[lyr], b_mod[lyr])
    sh1, sc1, g1, sh2, sc2, g2 = [m[:, None, :] for m in jnp.split(mod[:bsz], N_MOD, axis=-1)]
    mod_ctx = jnp.broadcast_to(mod[bsz][None, None, :], (bsz, 1, N_MOD * d))
    csh1, csc1 = mod_ctx[..., :d], mod_ctx[..., d:2 * d]

    w_in_bf16 = w_in[lyr].astype(_BF16)
    rkv_lat, lora_lat, hg_lat = _inproj(x, norm1_g[lyr], sc1, sh1, w_in_bf16, tm=256)
    rkv_ctx, lora_ctx, hg_ctx = _inproj(ctx, norm1_g[lyr], csc1, csh1, w_in_bf16, tm=256)

    kern9 = rw_conv[lyr].reshape(CONV_K * CONV_K, RKV_W)
    lane_head = jnp.arange(RW_WIDTH) // RW_HEAD_DIM
    seg = (lane_head[:, None] == lane_head[None, :]).astype(_BF16)
    rw_p = (kern9, rw_w0[lyr], rw_w2[lyr], rw_a0[lyr], rw_a2[lyr], rw_g2[lyr], rw_k_k[lyr], rw_k_a[lyr],
            rw_r_k[lyr], seg)
    prep_lat = _rw_prep(rkv_lat, lora_lat, *rw_p, tq=256, grid_conv=True)
    prep_ctx = _rw_prep(rkv_ctx, lora_ctx, *rw_p, tq=256, grid_conv=False)
    s_rw = _rwkv_scan(*_scan_operands(prep_ctx), None, tt=16, emit=False)
    y_dirs = _from_chain_lanes(_rwkv_scan(*_scan_operands(prep_lat), s_rw, tt=16, emit=True), bsz)

    lb = jnp.cumsum(jax.nn.softmax(hg_lb_logits.astype(_F32), axis=1), axis=1)[:, lyr][:, None, :]
    s_hg = _hgrn_scan(hg_ctx, lb, None, tb=256, emit=False)
    o_dirs = _hgrn_scan(hg_lat, lb, s_hg, tb=256, emit=True)

    perm = lambda t: t.reshape(N_GROUPS, GROUP_SIZE, -1).transpose(1, 0, 2).reshape(N_EXPERTS, -1)
    router_wt = perm(router_w[lyr].T)
    bias_rows = jnp.broadcast_to(perm(router_b[lyr][:, None]), (N_EXPERTS, LANES))
    hg_norm_row = jnp.tile(hg_norm_g[lyr], HG_HEADS).reshape(1, HG_WIDTH)
    x1, h2, logits_t = _post(y_dirs, prep_lat[6], prep_lat[7], rw_lnx_g[lyr], rw_lnx_b[lyr], o_dirs, hg_lat,
                             hg_norm_row, seg, x, w_out[lyr].astype(_BF16), g1, norm2_g[lyr], sc2, sh2,
                             router_wt, tm=512)
    top_e, gate, rank, counts_rows = _router(logits_t, bias_rows, tn=512)
    counts = counts_rows[:, 0].astype(jnp.int32).reshape(GROUP_SIZE, N_GROUPS).T.reshape(N_EXPERTS)

    n_tok = bsz * seq
    n_blocks = n_tok * TOP_K // MOE_BLOCK + N_EXPERTS
    n_slots = n_blocks * MOE_BLOCK
    dest, plan = _dispatch_plan(top_e, rank, counts, n_blocks)
    dest_half = (dest[:, None, :] + jnp.array([0, n_slots], jnp.int32)[None, :, None]).reshape(TOP_K, 2 * n_tok)
    x_sorted = _sc_dispatch(h2.reshape(2 * n_tok, SC_ROW), dest_half, 2 * n_slots)
    bf = lambda t: t.astype(_BF16)
    y_sorted = _experts(plan, x_sorted.reshape(2, n_slots, SC_ROW),
                        bf(exp_w_gate[lyr]), bf(exp_w_up[lyr]), bf(exp_w_down[lyr]))
    y_rows = _sc_gather(y_sorted.reshape(2 * n_slots, SC_ROW), dest_half.reshape(1, TOP_K * 2 * n_tok))
    return _combine(y_rows.reshape(TOP_K, 2, n_tok, SC_ROW), gate.T, h2, x1, g2,
                    final_norm_g, bf(sh_w_gate[lyr]), bf(sh_w_up[lyr]), bf(sh_w_down[lyr]), tm=512)
```

```python
import functools

import jax
import jax.numpy as jnp
from jax import lax
from jax.experimental import pallas as pl
from jax.experimental.pallas import tpu as pltpu
from jax.experimental.pallas import tpu_sc as plsc

D_MODEL = 1024
GRID_W = 64
N_MOD = 6
EPS = 1e-6

RW_HEAD_DIM = 64
RW_HEADS = 8
RW_WIDTH = 512
RW_LORA = 64
RW_GATE_LORA = 128
RW_GN_EPS = 1e-5 * RW_HEAD_DIM
CONV_K = 3

HG_HEADS = 8
HG_DIM = 64
HG_WIDTH = 512
HG_CHUNK = 32
HG_PAIRS = HG_HEADS // 2

RKV_W = 3 * RW_WIDTH
LORA_W = 4 * RW_LORA + RW_GATE_LORA
HG_W = 5 * HG_WIDTH
P_IN = RKV_W + LORA_W + HG_W

N_EXPERTS = 64
N_GROUPS = 8
GROUP_SIZE = N_EXPERTS // N_GROUPS
TOPK_GROUPS = 4
TOP_K = 8
D_EXPERT = 256
ROUTE_SCALE = 2.5
MOE_BLOCK = 512
SC_ROW = 256
SC_WINDOW = 128

LANES = 128
SUBLANES = 8
VMEM_LIMIT = 48 * 1024 * 1024

_HI = lax.Precision.HIGHEST
_F32 = jnp.float32
_BF16 = jnp.bfloat16
_NT = (((1,), (1,)), ((), ()))
_TN = (((0,), (0,)), ((), ()))


def _cparams(sem):
    return pltpu.CompilerParams(dimension_semantics=sem, vmem_limit_bytes=VMEM_LIMIT)


def _pack_bf16_pair(lo, hi):
    lo_bits = lax.bitcast_convert_type(lo.astype(_BF16).astype(_F32), jnp.uint32)
    hi_bits = lax.bitcast_convert_type(hi.astype(_BF16).astype(_F32), jnp.uint32)
    return (lo_bits >> 16) | (hi_bits & jnp.uint32(0xFFFF0000))


def _unpack_bf16_pair(u):
    lo = lax.bitcast_convert_type(u << 16, _F32)
    hi = lax.bitcast_convert_type(u & jnp.uint32(0xFFFF0000), _F32)
    return lo, hi


def _pack_quarters(x):
    q = x.shape[1] // 4
    return jnp.stack([_pack_bf16_pair(x[:, 2 * h * q:(2 * h + 1) * q], x[:, (2 * h + 1) * q:(2 * h + 2) * q])
                      for h in range(2)])


def _unpack_quarters(u0, u1):
    return _unpack_bf16_pair(u0) + _unpack_bf16_pair(u1)


def _dir_block(d, i, nb):
    return i + d * (nb - 1 - 2 * i)


def _mod_kernel(c_ref, w_ref, b_ref, o_ref):
    c = c_ref[...]
    s = c * jax.nn.sigmoid(c)
    o_ref[...] = jnp.dot(s, w_ref[...], preferred_element_type=_F32, precision=_HI) + b_ref[...]


def _modulation(c_all, w_mod, b_mod):
    rows, d = c_all.shape
    n = w_mod.shape[1]
    tn = 1024
    return pl.pallas_call(
        _mod_kernel,
        grid=(n // tn,),
        in_specs=[pl.BlockSpec((rows, d), lambda j: (0, 0)),
                  pl.BlockSpec((d, tn), lambda j: (0, j)),
                  pl.BlockSpec((1, tn), lambda j: (0, j))],
        out_specs=pl.BlockSpec((rows, tn), lambda j: (0, j)),
        out_shape=jax.ShapeDtypeStruct((rows, n), _F32),
        compiler_params=_cparams(("parallel",)),
        name="modulation",
    )(c_all, w_mod, b_mod.reshape(1, n))


def _inproj_kernel(x_ref, g_ref, sc_ref, sh_ref, w_ref, rkv_ref, lora_ref, hg_ref):
    x = x_ref[0]
    y = x * lax.rsqrt(jnp.mean(x * x, axis=-1, keepdims=True) + EPS) * g_ref[...]
    h = y * (1.0 + sc_ref[0]) + sh_ref[0]
    u = jnp.dot(h.astype(_BF16), w_ref[...], preferred_element_type=_F32)
    rkv_ref[0] = u[:, :RKV_W]
    lora_ref[0] = u[:, RKV_W:RKV_W + LORA_W]
    hg_ref[0] = u[:, RKV_W + LORA_W:]


def _inproj(x, norm_g, sc, sh, w_in_bf16, tm):
    bsz, l, d = x.shape
    tm = min(tm, l)
    row = lambda b, i: (b, i, 0)
    per_b = lambda b, i: (b, 0, 0)
    return pl.pallas_call(
        _inproj_kernel,
        grid=(bsz, l // tm),
        in_specs=[pl.BlockSpec((1, tm, d), row),
                  pl.BlockSpec((1, d), lambda b, i: (0, 0)),
                  pl.BlockSpec((1, 1, d), per_b),
                  pl.BlockSpec((1, 1, d), per_b),
                  pl.BlockSpec((d, P_IN), lambda b, i: (0, 0))],
        out_specs=[pl.BlockSpec((1, tm, RKV_W), row),
                   pl.BlockSpec((1, tm, LORA_W), row),
                   pl.BlockSpec((1, tm, HG_W), row)],
        out_shape=[jax.ShapeDtypeStruct((bsz, l, RKV_W), _F32),
                   jax.ShapeDtypeStruct((bsz, l, LORA_W), _F32),
                   jax.ShapeDtypeStruct((bsz, l, HG_W), _F32)],
        compiler_params=_cparams(("parallel", "parallel")),
        name="inproj",
    )(x, norm_g.reshape(1, d), sc, sh, w_in_bf16)


def _seg_sum(x, seg_ref):
    hi = x.astype(_BF16)
    lo = (x - hi.astype(_F32)).astype(_BF16)
    return (jnp.dot(hi, seg_ref[...], preferred_element_type=_F32)
            + jnp.dot(lo, seg_ref[...], preferred_element_type=_F32))


def _rw_prep_kernel(cur_ref, prev_ref, next_ref, lora_ref, kern_ref, w0_ref, w2_ref, a0_ref, a2_ref, g2_ref,
                    kk_ref, ka_ref, rk_ref, seg_ref,
                    r_out, kkn_out, v_out, w_out, kd_out, b_out, bonus_out, g_out, *, grid_conv):
    i = pl.program_id(1)
    tq = cur_ref.shape[0]
    hw = prev_ref.shape[0]
    w = RW_WIDTH
    pmask = jnp.where(i > 0, 1.0, 0.0)
    nmask = jnp.where(i < pl.num_programs(1) - 1, 1.0, 0.0)
    xpos = lax.broadcasted_iota(jnp.int32, (tq + 2 * hw, w), 0) % GRID_W
    conv = []
    for part in range(3):
        cs = slice(part * w, (part + 1) * w)
        xe = jnp.concatenate([prev_ref[:, cs] * pmask, cur_ref[:, cs], next_ref[:, cs] * nmask], axis=0)
        xl = pltpu.roll(xe, 1, 0)
        xr = pltpu.roll(xe, tq + 2 * hw - 1, 0)
        if grid_conv:
            xl = jnp.where(xpos == 0, 0.0, xl)
            xr = jnp.where(xpos == GRID_W - 1, 0.0, xr)
        acc = None
        for dy in range(CONV_K) if grid_conv else (CONV_K // 2,):
            off = hw + (dy - 1) * GRID_W
            for dx, src in enumerate((xl, xe, xr)):
                term = src[off:off + tq] * kern_ref[pl.ds(dy * CONV_K + dx, 1), cs]
                acc = term if acc is None else acc + term
        conv.append(acc)
    r, k, v = conv
    r_out[...] = r
    v_out[...] = v

    kk = k * kk_ref[...]
    kk = kk / jnp.maximum(jnp.sqrt(_seg_sum(kk * kk, seg_ref)), 1e-12)
    kkn_out[...] = kk

    lora = lora_ref[...]
    ksum = jnp.zeros_like(k)
    for d in range(2):
        wl = lora[:, d * RW_LORA:(d + 1) * RW_LORA]
        al = lora[:, (2 + d) * RW_LORA:(3 + d) * RW_LORA]
        z = w0_ref[pl.ds(d, 1), :] + jnp.dot(jnp.tanh(wl), w2_ref[d], preferred_element_type=_F32,
                                             precision=_HI)
        wlog = -jax.nn.softplus(-z) - 0.5
        w_out[d] = jnp.exp(-jnp.exp(wlog))
        a = jax.nn.sigmoid(a0_ref[pl.ds(d, 1), :] + jnp.dot(al, a2_ref[d], preferred_element_type=_F32,
                                                             precision=_HI))
        kd = k * (1.0 + (a - 1.0) * ka_ref[...])
        kd_out[d] = kd
        b_out[d] = kk * a
        ksum = ksum + kd
    bonus_out[...] = _seg_sum(r * ksum * rk_ref[...], seg_ref) * v
    gd = lora[:, 4 * RW_LORA:]
    g_out[...] = jnp.dot(jax.nn.sigmoid(gd), g2_ref[...], preferred_element_type=_F32, precision=_HI)


def _rw_prep(rkv, lora, kern9, w0, w2, a0, a2, g2, k_k, k_a, r_k, seg, tq, grid_conv):
    bsz, l, _ = rkv.shape
    tq = min(tq, l)
    nb = l // tq
    hw = GRID_W
    per = tq // hw
    w = RW_WIDTH
    row = lambda b, i: (b, i, 0)
    const2 = lambda b, i: (0, 0)
    const3 = lambda b, i: (0, 0, 0)
    tok = pl.BlockSpec((None, tq, w), row)
    tok2 = pl.BlockSpec((2, None, tq, w), lambda b, i: (0, b, i, 0))
    one = jax.ShapeDtypeStruct((bsz, l, w), _F32)
    two = jax.ShapeDtypeStruct((2, bsz, l, w), _F32)
    vec = lambda t: t.reshape(1, w)
    return pl.pallas_call(
        functools.partial(_rw_prep_kernel, grid_conv=grid_conv),
        grid=(bsz, nb),
        in_specs=[pl.BlockSpec((None, tq, RKV_W), row),
                  pl.BlockSpec((None, hw, RKV_W), lambda b, i: (b, jnp.maximum(i * per - 1, 0), 0)),
                  pl.BlockSpec((None, hw, RKV_W), lambda b, i: (b, jnp.minimum((i + 1) * per, l // hw - 1), 0)),
                  pl.BlockSpec((None, tq, LORA_W), row),
                  pl.BlockSpec((CONV_K * CONV_K, RKV_W), const2),
                  pl.BlockSpec((2, w), const2),
                  pl.BlockSpec((2, RW_LORA, w), const3),
                  pl.BlockSpec((2, w), const2),
                  pl.BlockSpec((2, RW_LORA, w), const3),
                  pl.BlockSpec((RW_GATE_LORA, w), const2),
                  pl.BlockSpec((1, w), const2),
                  pl.BlockSpec((1, w), const2),
                  pl.BlockSpec((1, w), const2),
                  pl.BlockSpec((w, w), const2)],
        out_specs=[tok, tok, tok, tok2, tok2, tok2, tok, tok],
        out_shape=[one, one, one, two, two, two, one, one],
        compiler_params=_cparams(("parallel", "parallel")),
        name="rwkv_prep_grid" if grid_conv else "rwkv_prep_seq",
    )(rkv, rkv, rkv, lora, kern9, w0, w2, a0, a2, g2, vec(k_k), vec(k_a), vec(r_k), seg)


def _rwkv_scan_kernel(*refs, tt, emit, has_init):
    r_ref, kk_ref, v_ref, w_ref, k_ref, b_ref = refs[:6]
    rest = list(refs[6:])
    s0_ref = rest.pop(0) if has_init else None
    out_ref = rest.pop(0)
    s_ref, sa_ref = rest
    d = pl.program_id(0)
    i = pl.program_id(1)
    n = RW_HEAD_DIM

    @pl.when(i == 0)
    def _():
        s_ref[...] = s0_ref[...] if has_init else jnp.zeros_like(s_ref)

    hv = n // 2
    halves = [pl.ds(h * hv, hv) for h in range(2)]

    def project(t, vs):
        acc = [jnp.zeros((hv, LANES), _F32), jnp.zeros((hv, LANES), _F32)]
        for k in range(n):
            acc[k % 2] = acc[k % 2] + s_ref[k, vs, :] * kk_ref[t, pl.ds(k, 1), :]
        return -(acc[0] + acc[1])

    first = d * (tt - 1)
    for vs in halves:
        sa_ref[vs, :] = project(first, vs)

    def step(j, carry):
        t = j + d * (tt - 1 - 2 * j)
        tn = jnp.clip(t + 1 - 2 * d, 0, tt - 1)
        for vs in halves:
            sa = sa_ref[vs, :]
            vv = v_ref[t, vs, :]
            y = [jnp.zeros((hv, LANES), _F32), jnp.zeros((hv, LANES), _F32)]
            nsa = [jnp.zeros((hv, LANES), _F32), jnp.zeros((hv, LANES), _F32)]
            for k in range(n):
                row = pl.ds(k, 1)
                s_new = s_ref[k, vs, :] * w_ref[t, row, :] + sa * b_ref[t, row, :] + vv * k_ref[t, row, :]
                s_ref[k, vs, :] = s_new
                nsa[k % 2] = nsa[k % 2] + s_new * kk_ref[tn, row, :]
                if emit:
                    y[k % 2] = y[k % 2] + s_new * r_ref[t, row, :]
            sa_ref[vs, :] = -(nsa[0] + nsa[1])
            if emit:
                out_ref[t, vs, :] = y[0] + y[1]
        return carry

    lax.fori_loop(0, tt, step, 0)

    if not emit:
        @pl.when(i == pl.num_programs(1) - 1)
        def _():
            out_ref[...] = s_ref[...]


def _rwkv_scan(shared, per_dir, s0, tt, emit):
    l, n, c = shared[0].shape
    nb = l // tt
    sh_spec = pl.BlockSpec((tt, n, c), lambda d, i: (_dir_block(d, i, nb), 0, 0))
    pd_spec = pl.BlockSpec((None, tt, n, c), lambda d, i: (d, _dir_block(d, i, nb), 0, 0))
    st_spec = pl.BlockSpec((None, n, n, c), lambda d, i: (d, 0, 0, 0))
    in_specs = [sh_spec] * 3 + [pd_spec] * 3
    args = list(shared) + list(per_dir)
    if s0 is not None:
        in_specs.append(st_spec)
        args.append(s0)
    if emit:
        out_spec, out_shape = pd_spec, jax.ShapeDtypeStruct((2, l, n, c), _F32)
    else:
        out_spec, out_shape = st_spec, jax.ShapeDtypeStruct((2, n, n, c), _F32)
    return pl.pallas_call(
        functools.partial(_rwkv_scan_kernel, tt=tt, emit=emit, has_init=s0 is not None),
        grid=(2, nb),
        in_specs=in_specs,
        out_specs=out_spec,
        out_shape=out_shape,
        scratch_shapes=[pltpu.VMEM((n, n, c), _F32), pltpu.VMEM((n, c), _F32)],
        compiler_params=_cparams(("parallel", "arbitrary")),
        name="rwkv_scan_emit" if emit else "rwkv_scan_state",
    )(*args)


def _hgrn_kernel(*refs, n_chunks, emit, has_init, reverse):
    q_ref, f_ref, i_ref, lb_ref, mats_ref = refs[:5]
    rest = list(refs[5:])
    s0_ref = rest.pop(0) if has_init else None
    out_ref = rest.pop(0)
    st_ref, upd_ref, ent_ref = rest
    c = HG_CHUNK
    tb = n_chunks * c
    blk = pl.program_id(1)

    @pl.when(blk == 0)
    def _():
        st_ref[...] = s0_ref[...] if has_init else jnp.zeros_like(st_ref)

    lb = lb_ref[...]
    omlb = 1.0 - lb
    fr = f_ref[...]
    logf = jnp.log(lb + omlb * jax.nn.sigmoid(fr))
    kf = omlb * jax.nn.sigmoid(-fr)
    hi = logf.astype(_BF16)
    lo = (logf - hi.astype(_F32)).astype(_BF16)
    sums = (jnp.dot(mats_ref[...], hi, preferred_element_type=_F32)
            + jnp.dot(mats_ref[...], lo, preferred_element_type=_F32))
    cum, tot, mid = sums[:tb], sums[tb:2 * tb], sums[2 * tb:]
    dec = jnp.exp(tot)
    kdec = (kf * jnp.exp(tot - cum)).astype(_BF16)
    vb = i_ref[...].astype(_BF16)
    same_head = ((lax.broadcasted_iota(jnp.int32, (LANES, LANES), 0) < HG_DIM)
                 == (lax.broadcasted_iota(jnp.int32, (LANES, LANES), 1) < HG_DIM))
    pairs = [slice(p * LANES, (p + 1) * LANES) for p in range(HG_PAIRS)]
    chunks = [slice(j * c, (j + 1) * c) for j in range(n_chunks)]

    if emit:
        q = q_ref[...]
        qs = q * jax.nn.sigmoid(q) * (HG_DIM ** -0.5)
        qd = qs * jnp.exp(cum - mid)
        kd = (kf * jnp.exp(mid - cum)).astype(_BF16)
        qe = (qs * jnp.exp(cum)).astype(_BF16)
        head0 = lax.broadcasted_iota(jnp.int32, (tb, LANES), 1) < HG_DIM
        causal = mats_ref[:tb, :] > 0
        causal2 = jnp.concatenate([causal, causal], axis=0)
        for ls in pairs:
            qd_p = qd[:, ls]
            q2 = jnp.concatenate([jnp.where(head0, qd_p, 0.0), jnp.where(head0, 0.0, qd_p)], axis=0)
            sc = lax.dot_general(q2.astype(_BF16), kd[:, ls], _NT, preferred_element_type=_F32)
            sc = jnp.where(causal2, sc, 0.0)
            oi = jnp.dot(sc.astype(_BF16), vb[:, ls], preferred_element_type=_F32)
            out_ref[:, ls] = jnp.where(head0, oi[:tb], oi[tb:])

    for j, rows in enumerate(chunks):
        for p, ls in enumerate(pairs):
            upd = lax.dot_general(vb[rows, ls], kdec[rows, ls], _TN, preferred_element_type=_F32)
            upd_ref[j, p] = jnp.where(same_head, upd, 0.0)

    order = list(range(n_chunks))[::-1] if reverse else list(range(n_chunks))
    for p, ls in enumerate(pairs):
        st = st_ref[p]
        for j in order:
            if emit:
                ent_ref[j, p] = st.astype(_BF16)
            st = st * dec[j * c:j * c + 1, ls] + upd_ref[j, p]
        st_ref[p] = st

    if emit:
        for j, rows in enumerate(chunks):
            for p, ls in enumerate(pairs):
                out_ref[rows, ls] += lax.dot_general(qe[rows, ls], ent_ref[j, p], _NT,
                                                     preferred_element_type=_F32)
    else:
        @pl.when(blk == pl.num_programs(1) - 1)
        def _():
            out_ref[...] = st_ref[...]


def _hgrn_mats(tb, reverse):
    t = jnp.arange(tb)
    same = (t[:, None] // HG_CHUNK) == (t[None, :] // HG_CHUNK)
    ref_pos = (t // HG_CHUNK) * HG_CHUNK + HG_CHUNK // 2
    if reverse:
        cum = same & (t[None, :] >= t[:, None])
        mid = same & (t[None, :] >= ref_pos[:, None])
    else:
        cum = same & (t[None, :] <= t[:, None])
        mid = same & (t[None, :] <= ref_pos[:, None])
    return jnp.concatenate([cum, same, mid], axis=0).astype(_BF16)


def _hgrn_scan(hg, lb, s0, tb, emit, reverse):
    bsz, l, _ = hg.shape
    tb = min(tb, l)
    nb = l // tb
    w = HG_WIDTH
    n_chunks = tb // HG_CHUNK
    tblk = (lambda i: nb - 1 - i) if reverse else (lambda i: i)
    col = lambda j: pl.BlockSpec((None, tb, w), lambda b, i: (b, tblk(i), j))
    const = lambda b, i: (0, 0)
    st_spec = pl.BlockSpec((None, HG_PAIRS, LANES, LANES), lambda b, i: (b, 0, 0, 0))
    in_specs = [col(0), col(2 if reverse else 1), col(3), pl.BlockSpec((1, w), const),
                pl.BlockSpec((3 * tb, tb), const)]
    args = [hg, hg, hg, lb, _hgrn_mats(tb, reverse)]
    if s0 is not None:
        in_specs.append(st_spec)
        args.append(s0)
    if emit:
        out_spec = pl.BlockSpec((None, tb, w), lambda b, i: (b, tblk(i), 0))
        out_shape = jax.ShapeDtypeStruct((bsz, l, w), _F32)
    else:
        out_spec = st_spec
        out_shape = jax.ShapeDtypeStruct((bsz, HG_PAIRS, LANES, LANES), _F32)
    return pl.pallas_call(
        functools.partial(_hgrn_kernel, n_chunks=n_chunks, emit=emit, has_init=s0 is not None,
                          reverse=reverse),
        grid=(bsz, nb),
        in_specs=in_specs,
        out_specs=out_spec,
        out_shape=out_shape,
        scratch_shapes=[pltpu.VMEM((HG_PAIRS, LANES, LANES), _F32),
                        pltpu.VMEM((n_chunks, HG_PAIRS, LANES, LANES), _F32),
                        pltpu.VMEM((n_chunks, HG_PAIRS, LANES, LANES), _BF16)],
        compiler_params=_cparams(("parallel", "arbitrary")),
        name=("hgrn_emit" if emit else "hgrn_state") + ("_bwd" if reverse else "_fwd"),
    )(*args)


def _post_kernel(y_ref, bonus_ref, grw_ref, lng_ref, lnb_ref, of_ref, ob_ref, ghg_ref, hgn_ref, seg_ref,
                 x_ref, w_ref, g1_ref, n2_ref, sc_ref, sh_ref, rw_ref, x1_ref, h2_ref, lg_ref):
    inv_n = 1.0 / RW_HEAD_DIM
    ysum = y_ref[0] + y_ref[1]
    mu = _seg_sum(ysum, seg_ref) * inv_n
    dlt = ysum - mu
    var = _seg_sum(dlt * dlt, seg_ref) * inv_n
    yn = dlt * lax.rsqrt(var + RW_GN_EPS) * lng_ref[...] + lnb_ref[...] + bonus_ref[...]
    y_rw = yn * grw_ref[...]
    o = of_ref[...] + ob_ref[...]
    o = o * lax.rsqrt(_seg_sum(o * o, seg_ref) * inv_n + EPS) * hgn_ref[...]
    g = ghg_ref[...]
    y_hg = o * (g * jax.nn.sigmoid(g))
    y = (jnp.dot(y_rw.astype(_BF16), w_ref[:RW_WIDTH, :], preferred_element_type=_F32)
         + jnp.dot(y_hg.astype(_BF16), w_ref[RW_WIDTH:, :], preferred_element_type=_F32))
    x1 = x_ref[0] + g1_ref[0] * y
    x1_ref[0] = x1
    h2 = x1 * lax.rsqrt(jnp.mean(x1 * x1, axis=-1, keepdims=True) + EPS) * n2_ref[...]
    h2 = h2 * (1.0 + sc_ref[0]) + sh_ref[0]
    h2_ref[...] = _pack_quarters(h2)
    lg_ref[...] = lax.dot_general(rw_ref[...], h2, _NT, preferred_element_type=_F32, precision=_HI)


def _post(y_dirs, bonus, g_rw, lnx_g, lnx_b, o_dirs, hg, hg_norm_row, seg,
          x, w_out_bf16, g1, norm2_g, sc2, sh2, router_wt, tm):
    bsz, l, d = x.shape
    tm = min(tm, l)
    nt = l // tm
    w = RW_WIDTH
    row = lambda b, i: (b, i, 0)
    per_b = lambda b, i: (b, 0, 0)
    const = lambda b, i: (0, 0)
    tok = pl.BlockSpec((None, tm, w), row)
    tok2 = pl.BlockSpec((2, None, tm, w), lambda b, i: (0, b, i, 0))
    vec = pl.BlockSpec((1, w), const)
    return pl.pallas_call(
        _post_kernel,
        grid=(bsz, nt),
        in_specs=[tok2, tok, tok, vec, vec, tok, tok,
                  pl.BlockSpec((None, tm, w), lambda b, i: (b, i, 4)),
                  vec,
                  pl.BlockSpec((w, w), const),
                  pl.BlockSpec((1, tm, d), row),
                  pl.BlockSpec((d, d), const),
                  pl.BlockSpec((1, 1, d), per_b),
                  pl.BlockSpec((1, d), const),
                  pl.BlockSpec((1, 1, d), per_b),
                  pl.BlockSpec((1, 1, d), per_b),
                  pl.BlockSpec((N_EXPERTS, d), const)],
        out_specs=[pl.BlockSpec((1, tm, d), row),
                   pl.BlockSpec((2, tm, d // 4), lambda b, i: (0, b * nt + i, 0)),
                   pl.BlockSpec((N_EXPERTS, tm), lambda b, i: (0, b * nt + i))],
        out_shape=[jax.ShapeDtypeStruct((bsz, l, d), _F32),
                   jax.ShapeDtypeStruct((2, bsz * l, d // 4), jnp.uint32),
                   jax.ShapeDtypeStruct((N_EXPERTS, bsz * l), _F32)],
        compiler_params=_cparams(("parallel", "parallel")),
        name="outproj_norm2_router",
    )(y_dirs, bonus, g_rw, lnx_g.reshape(1, w), lnx_b.reshape(1, w), o_dirs[0], o_dirs[1], hg, hg_norm_row, seg,
      x, w_out_bf16, g1, norm2_g.reshape(1, d), sc2, sh2, router_wt)


def _sublane_all(x, op):
    for s in (4, 2, 1):
        x = op(x, pltpu.roll(x, s, 0))
    return x


def _router_kernel(lg_ref, bias_ref, tri_ref, e_ref, g_ref, rank_ref, cnt_ref):
    ng, gs = N_GROUPS, GROUP_SIZE
    tn = lg_ref.shape[1]
    neg = -jnp.inf
    gidx = lax.broadcasted_iota(jnp.int32, (ng, tn), 0)
    bias = jnp.concatenate([bias_ref[...]] * (tn // LANES), axis=1)
    scores = [jax.nn.sigmoid(lg_ref[j * ng:(j + 1) * ng, :]) for j in range(gs)]
    sel = [scores[j] + bias[j * ng:(j + 1) * ng, :] for j in range(gs)]

    m1 = functools.reduce(jnp.maximum, sel)
    cnt = functools.reduce(jnp.add, [(s == m1).astype(_F32) for s in sel])
    m2 = functools.reduce(jnp.maximum, [jnp.where(s < m1, s, neg) for s in sel])
    gscore = m1 + jnp.where(cnt >= 2.0, m1, m2)

    rank = jnp.zeros((ng, tn), jnp.int32)
    for s in range(1, ng):
        other = pltpu.roll(gscore, s, 0)
        beats = jnp.where(other > gscore, 1, jnp.where((other == gscore) & (gidx >= s), 1, 0))
        rank = rank + beats
    gsel = rank < TOPK_GROUPS

    key = [jnp.where(gsel, s, neg) for s in sel]
    eidx = [gidx * gs + j for j in range(gs)]
    avail = [jnp.ones((ng, tn), jnp.int32) for _ in range(gs)]
    top_e = jnp.zeros((TOP_K, tn), jnp.int32)
    top_s = jnp.zeros((TOP_K, tn), _F32)
    for r in range(TOP_K):
        cur = functools.reduce(jnp.maximum, [jnp.where(avail[j] > 0, key[j], neg) for j in range(gs)])
        mx = _sublane_all(cur, jnp.maximum)
        cand = [jnp.where((avail[j] > 0) & (key[j] == mx), eidx[j], N_EXPERTS) for j in range(gs)]
        mn = _sublane_all(functools.reduce(jnp.minimum, cand), jnp.minimum)
        picked = functools.reduce(jnp.add, [jnp.where(eidx[j] == mn, scores[j], 0.0) for j in range(gs)])
        avail = [jnp.where(eidx[j] == mn, 0, avail[j]) for j in range(gs)]
        top_e = jnp.where(gidx == r, mn, top_e)
        top_s = jnp.where(gidx == r, _sublane_all(picked, jnp.add), top_s)
    den = _sublane_all(top_s, jnp.add)
    e_ref[...] = top_e
    g_ref[...] = top_s / den * ROUTE_SCALE

    @pl.when(pl.program_id(0) == 0)
    def _():
        cnt_ref[...] = jnp.zeros_like(cnt_ref)

    taken = jnp.concatenate([jnp.where(a == 0, 1.0, 0.0) for a in avail], axis=0)
    before = jnp.dot(taken.astype(_BF16), tri_ref[...], preferred_element_type=_F32) - taken
    before = before + jnp.concatenate([cnt_ref[...]] * (tn // LANES), axis=1)
    rank = jnp.zeros((TOP_K, tn), _F32)
    for r in range(TOP_K):
        e_r = jnp.broadcast_to(top_e[r:r + 1, :], (ng, tn))
        hit = functools.reduce(jnp.add, [jnp.where(eidx[j] == e_r, before[j * ng:(j + 1) * ng, :], 0.0)
                                         for j in range(gs)])
        rank = jnp.where(gidx == r, _sublane_all(hit, jnp.add), rank)
    rank_ref[...] = rank.astype(jnp.int32)
    ones = jnp.ones((tn, LANES), _BF16)
    cnt_ref[...] += jnp.dot(taken.astype(_BF16), ones, preferred_element_type=_F32)


def _router(logits_t, bias_rows, tn):
    assert TOP_K == N_GROUPS == SUBLANES
    t = logits_t.shape[1]
    tn = min(tn, t)
    tri = (jnp.arange(tn)[:, None] <= jnp.arange(tn)[None, :]).astype(_BF16)
    out = pl.BlockSpec((TOP_K, tn), lambda i: (0, i))
    cnt = pl.BlockSpec((N_EXPERTS, LANES), lambda i: (0, 0))
    return pl.pallas_call(
        _router_kernel,
        grid=(t // tn,),
        in_specs=[pl.BlockSpec((N_EXPERTS, tn), lambda i: (0, i)), cnt,
                  pl.BlockSpec((tn, tn), lambda i: (0, 0))],
        out_specs=[out, out, out, cnt],
        out_shape=[jax.ShapeDtypeStruct((TOP_K, t), jnp.int32), jax.ShapeDtypeStruct((TOP_K, t), _F32),
                   jax.ShapeDtypeStruct((TOP_K, t), jnp.int32),
                   jax.ShapeDtypeStruct((N_EXPERTS, LANES), _F32)],
        compiler_params=_cparams(("arbitrary",)),
        name="router_topk",
    )(logits_t, bias_rows, tri)


def _sc_mesh():
    return plsc.VectorSubcoreMesh(core_axis_name="c", subcore_axis_name="s")


def _sc_dispatch(x_half_rows, dest_half_rows, n_out):
    n_rows = x_half_rows.shape[0]

    @pl.kernel(out_type=jax.ShapeDtypeStruct((n_out, SC_ROW), x_half_rows.dtype), mesh=_sc_mesh(),
               scratch_types=[])
    def scatter_rows(x_hbm, d_hbm, o_hbm):
        def body(x_vmem, i_vmem):
            for j in range(TOP_K):
                pltpu.sync_copy(x_vmem, o_hbm.at[i_vmem.at[j]])

        pltpu.emit_pipeline(
            body,
            grid=(n_rows // SC_WINDOW,),
            in_specs=[pl.BlockSpec((SC_WINDOW, SC_ROW), lambda i: (i, 0)),
                      pl.BlockSpec((TOP_K, SC_WINDOW), lambda i: (0, i))],
            out_specs=[],
            core_axis_name=("c", "s"),
            dimension_semantics=(pltpu.PARALLEL,),
        )(x_hbm, d_hbm)

    return scatter_rows(x_half_rows, dest_half_rows)


def _sc_gather(y_half_rows, idx):
    n = idx.shape[1]

    @pl.kernel(out_type=jax.ShapeDtypeStruct((n, SC_ROW), y_half_rows.dtype), mesh=_sc_mesh(),
               scratch_types=[])
    def gather_rows(y_hbm, i_hbm, o_hbm):
        def body(i_vmem, o_vmem):
            pltpu.sync_copy(y_hbm.at[i_vmem.at[0]], o_vmem)

        pltpu.emit_pipeline(
            body,
            grid=(n // SC_WINDOW,),
            in_specs=[pl.BlockSpec((1, SC_WINDOW), lambda i: (0, i))],
            out_specs=[pl.BlockSpec((SC_WINDOW, SC_ROW), lambda i: (i, 0))],
            core_axis_name=("c", "s"),
            dimension_semantics=(pltpu.PARALLEL,),
        )(i_hbm, o_hbm)

    return gather_rows(y_half_rows, idx)


def _dispatch_plan(top_e, rank, counts, n_blocks):
    padded = (counts + MOE_BLOCK - 1) // MOE_BLOCK * MOE_BLOCK
    pad_end = jnp.cumsum(padded)
    pad_start = pad_end - padded
    onehot = top_e[:, :, None] == jnp.arange(N_EXPERTS, dtype=jnp.int32)[None, None, :]
    dest = rank + jnp.sum(jnp.where(onehot, pad_start[None, None, :], 0), axis=-1)
    blk_start = jnp.arange(n_blocks, dtype=jnp.int32) * MOE_BLOCK
    blk_e = jnp.minimum(jnp.sum(pad_end[None, :] <= blk_start[:, None], axis=1), N_EXPERTS - 1)
    n_used = pad_end[-1] // MOE_BLOCK
    return dest, jnp.concatenate([blk_e.astype(jnp.int32), n_used[None].astype(jnp.int32)])


def _swiglu_packed(u0, u1, wg_ref, wu_ref, wd_ref):
    q = D_MODEL // 4
    parts = [p.astype(_BF16) for p in _unpack_quarters(u0, u1)]
    a = sum(jnp.dot(p, wg_ref[i * q:(i + 1) * q, :], preferred_element_type=_F32) for i, p in enumerate(parts))
    u = sum(jnp.dot(p, wu_ref[i * q:(i + 1) * q, :], preferred_element_type=_F32) for i, p in enumerate(parts))
    act = (a * jax.nn.sigmoid(a) * u).astype(_BF16)
    return jnp.dot(act, wd_ref[...], preferred_element_type=_F32)


def _expert_kernel(plan_ref, x_ref, wg_ref, wu_ref, wd_ref, y_ref):
    @pl.when(pl.program_id(0) < plan_ref[pl.num_programs(0)])
    def _():
        y_ref[...] = _pack_quarters(_swiglu_packed(x_ref[0], x_ref[1], wg_ref, wu_ref, wd_ref))


def _experts(plan, x_sorted, wg, wu, wd):
    _, n_rows, q = x_sorted.shape
    n_blocks = n_rows // MOE_BLOCK
    d = 4 * q
    rows = pl.BlockSpec((2, MOE_BLOCK, q), lambda i, plan: (0, i, 0))
    grid_spec = pltpu.PrefetchScalarGridSpec(
        num_scalar_prefetch=1,
        grid=(n_blocks,),
        in_specs=[rows,
                  pl.BlockSpec((None, d, D_EXPERT), lambda i, plan: (plan[i], 0, 0)),
                  pl.BlockSpec((None, d, D_EXPERT), lambda i, plan: (plan[i], 0, 0)),
                  pl.BlockSpec((None, D_EXPERT, d), lambda i, plan: (plan[i], 0, 0))],
        out_specs=rows,
    )
    return pl.pallas_call(
        _expert_kernel,
        grid_spec=grid_spec,
        out_shape=jax.ShapeDtypeStruct((2, n_rows, q), jnp.uint32),
        compiler_params=_cparams(("arbitrary",)),
        name="moe_routed_experts",
    )(plan, x_sorted, wg, wu, wd)


def _combine_kernel(y_ref, gate_ref, h_ref, x1_ref, g2_ref, fg_ref, sg_ref, su_ref, sd_ref, o_ref):
    gate = gate_ref[...]
    acc = None
    for j in range(TOP_K):
        gj = gate[:, j:j + 1]
        parts = [gj * p for p in _unpack_quarters(y_ref[j, 0], y_ref[j, 1])]
        acc = parts if acc is None else [a + p for a, p in zip(acc, parts)]
    moe = jnp.concatenate(acc, axis=1) + _swiglu_packed(h_ref[0], h_ref[1], sg_ref, su_ref, sd_ref)
    x2 = x1_ref[0] + g2_ref[0] * moe
    o_ref[0] = x2 * lax.rsqrt(jnp.mean(x2 * x2, axis=-1, keepdims=True) + EPS) * fg_ref[...]


def _combine(y_rows, gate_tk, h2_packed, x1, g2, final_g, sg, su, sd, tm):
    bsz, l, d = x1.shape
    tm = min(tm, l)
    nt = l // tm
    q = d // 4
    const = lambda b, i: (0, 0)
    return pl.pallas_call(
        _combine_kernel,
        grid=(bsz, nt),
        in_specs=[pl.BlockSpec((TOP_K, 2, tm, q), lambda b, i: (0, 0, b * nt + i, 0)),
                  pl.BlockSpec((tm, TOP_K), lambda b, i: (b * nt + i, 0)),
                  pl.BlockSpec((2, tm, q), lambda b, i: (0, b * nt + i, 0)),
                  pl.BlockSpec((1, tm, d), lambda b, i: (b, i, 0)),
                  pl.BlockSpec((1, 1, d), lambda b, i: (b, 0, 0)),
                  pl.BlockSpec((1, d), const),
                  pl.BlockSpec((d, D_EXPERT), const),
                  pl.BlockSpec((d, D_EXPERT), const),
                  pl.BlockSpec((D_EXPERT, d), const)],
        out_specs=pl.BlockSpec((1, tm, d), lambda b, i: (b, i, 0)),
        out_shape=jax.ShapeDtypeStruct((bsz, l, d), _F32),
        compiler_params=_cparams(("parallel", "parallel")),
        name="moe_combine_shared_final_norm",
    )(y_rows, gate_tk, h2_packed, x1, g2, final_g.reshape(1, d), sg, su, sd)


def _to_chain_lanes(t):
    lead, (b, l, _) = t.shape[:-3], t.shape[-3:]
    nl = len(lead)
    t = t.reshape(*lead, b, l, RW_HEADS, RW_HEAD_DIM)
    t = t.transpose(*range(nl), nl + 1, nl + 3, nl, nl + 2)
    return t.reshape(*lead, l, RW_HEAD_DIM, b * RW_HEADS)


def _from_chain_lanes(t, b):
    l = t.shape[1]
    return t.reshape(2, l, RW_HEAD_DIM, b, RW_HEADS).transpose(0, 3, 1, 4, 2).reshape(2, b, l, RW_WIDTH)


def _scan_operands(prep):
    r, kk, v, w, kd, bb = prep[:6]
    return [_to_chain_lanes(t) for t in (r, kk, v)], [_to_chain_lanes(t) for t in (w, kd, bb)]


def kernel(x, c, ctx, c_ctx, w_mod, b_mod, norm1_g, norm2_g, w_in, rw_conv, rw_w0, rw_w2, rw_a0, rw_a2,
           rw_g2, rw_k_k, rw_k_a, rw_r_k, rw_lnx_g, rw_lnx_b, hg_lb_logits, hg_norm_g, w_out, router_w,
           router_b, exp_w_gate, exp_w_up, exp_w_down, sh_w_gate, sh_w_up, sh_w_down, final_norm_g):
    bsz, seq, d = x.shape
    assert w_mod.shape[0] == 1 and bsz * RW_HEADS == LANES
    lyr = 0

    c_all = jnp.concatenate([c, c_ctx[None, :], jnp.zeros((SUBLANES - 1, d), _F32)], axis=0)
    mod = _modulation(c_all, w_mod[lyr], b_mod[lyr])
    sh1, sc1, g1, sh2, sc2, g2 = [m[:, None, :] for m in jnp.split(mod[:bsz], N_MOD, axis=-1)]
    mod_ctx = jnp.broadcast_to(mod[bsz][None, None, :], (bsz, 1, N_MOD * d))
    csh1, csc1 = mod_ctx[..., :d], mod_ctx[..., d:2 * d]

    w_in_bf16 = w_in[lyr].astype(_BF16)
    rkv_lat, lora_lat, hg_lat = _inproj(x, norm1_g[lyr], sc1, sh1, w_in_bf16, tm=256)
    rkv_ctx, lora_ctx, hg_ctx = _inproj(ctx, norm1_g[lyr], csc1, csh1, w_in_bf16, tm=256)

    kern9 = rw_conv[lyr].reshape(CONV_K * CONV_K, RKV_W)
    lane_head = jnp.arange(RW_WIDTH) // RW_HEAD_DIM
    seg = (lane_head[:, None] == lane_head[None, :]).astype(_BF16)
    rw_p = (kern9, rw_w0[lyr], rw_w2[lyr], rw_a0[lyr], rw_a2[lyr], rw_g2[lyr], rw_k_k[lyr], rw_k_a[lyr],
            rw_r_k[lyr], seg)
    prep_lat = _rw_prep(rkv_lat, lora_lat, *rw_p, tq=256, grid_conv=True)
    prep_ctx = _rw_prep(rkv_ctx, lora_ctx, *rw_p, tq=256, grid_conv=False)
    s_rw = _rwkv_scan(*_scan_operands(prep_ctx), None, tt=16, emit=False)
    y_dirs = _from_chain_lanes(_rwkv_scan(*_scan_operands(prep_lat), s_rw, tt=16, emit=True), bsz)

    lb = jnp.cumsum(jax.nn.softmax(hg_lb_logits.astype(_F32), axis=1), axis=1)[:, lyr]
    o_dirs = []
    for dd, rev in enumerate((False, True)):
        s_hg = _hgrn_scan(hg_ctx, lb[dd:dd + 1], None, tb=256, emit=False, reverse=rev)
        o_dirs.append(_hgrn_scan(hg_lat, lb[dd:dd + 1], s_hg, tb=256, emit=True, reverse=rev))

    perm = lambda t: t.reshape(N_GROUPS, GROUP_SIZE, -1).transpose(1, 0, 2).reshape(N_EXPERTS, -1)
    router_wt = perm(router_w[lyr].T)
    bias_rows = jnp.broadcast_to(perm(router_b[lyr][:, None]), (N_EXPERTS, LANES))
    hg_norm_row = jnp.tile(hg_norm_g[lyr], HG_HEADS).reshape(1, HG_WIDTH)
    x1, h2, logits_t = _post(y_dirs, prep_lat[6], prep_lat[7], rw_lnx_g[lyr], rw_lnx_b[lyr], o_dirs, hg_lat,
                             hg_norm_row, seg, x, w_out[lyr].astype(_BF16), g1, norm2_g[lyr], sc2, sh2,
                             router_wt, tm=512)
    top_e, gate, rank, counts_rows = _router(logits_t, bias_rows, tn=512)
    counts = counts_rows[:, 0].astype(jnp.int32).reshape(GROUP_SIZE, N_GROUPS).T.reshape(N_EXPERTS)

    n_tok = bsz * seq
    n_blocks = n_tok * TOP_K // MOE_BLOCK + N_EXPERTS
    n_slots = n_blocks * MOE_BLOCK
    dest, plan = _dispatch_plan(top_e, rank, counts, n_blocks)
    dest_half = (dest[:, None, :] + jnp.array([0, n_slots], jnp.int32)[None, :, None]).reshape(TOP_K, 2 * n_tok)
    x_sorted = _sc_dispatch(h2.reshape(2 * n_tok, SC_ROW), dest_half, 2 * n_slots)
    bf = lambda t: t.astype(_BF16)
    y_sorted = _experts(plan, x_sorted.reshape(2, n_slots, SC_ROW),
                        bf(exp_w_gate[lyr]), bf(exp_w_up[lyr]), bf(exp_w_down[lyr]))
    y_rows = _sc_gather(y_sorted.reshape(2 * n_slots, SC_ROW), dest_half.reshape(1, TOP_K * 2 * n_tok))
    return _combine(y_rows.reshape(TOP_K, 2, n_tok, SC_ROW), gate.T, h2, x1, g2,
                    final_norm_g, bf(sh_w_gate[lyr]), bf(sh_w_up[lyr]), bf(sh_w_down[lyr]), tm=512)
```

```python
import functools

import jax
import jax.numpy as jnp
from jax import lax
from jax.experimental import pallas as pl
from jax.experimental.pallas import tpu as pltpu
from jax.experimental.pallas import tpu_sc as plsc

D_MODEL = 1024
GRID_W = 64
N_MOD = 6
EPS = 1e-6

RW_HEAD_DIM = 64
RW_HEADS = 8
RW_WIDTH = 512
RW_LORA = 64
RW_GATE_LORA = 128
RW_GN_EPS = 1e-5 * RW_HEAD_DIM
CONV_K = 3

HG_HEADS = 8
HG_DIM = 64
HG_WIDTH = 512
HG_CHUNK = 32
HG_PAIRS = HG_HEADS // 2

RKV_W = 3 * RW_WIDTH
LORA_W = 4 * RW_LORA + RW_GATE_LORA
HG_W = 5 * HG_WIDTH
P_IN = RKV_W + LORA_W + HG_W

N_EXPERTS = 64
N_GROUPS = 8
GROUP_SIZE = N_EXPERTS // N_GROUPS
TOPK_GROUPS = 4
TOP_K = 8
D_EXPERT = 256
ROUTE_SCALE = 2.5
MOE_BLOCK = 512
SC_ROW = 256
SC_WINDOW = 128

LANES = 128
SUBLANES = 8
VMEM_LIMIT = 48 * 1024 * 1024

_HI = lax.Precision.HIGHEST
_F32 = jnp.float32
_BF16 = jnp.bfloat16
_NT = (((1,), (1,)), ((), ()))
_TN = (((0,), (0,)), ((), ()))


def _cparams(sem):
    return pltpu.CompilerParams(dimension_semantics=sem, vmem_limit_bytes=VMEM_LIMIT)


def _pack_bf16_pair(lo, hi):
    lo_bits = lax.bitcast_convert_type(lo.astype(_BF16).astype(_F32), jnp.uint32)
    hi_bits = lax.bitcast_convert_type(hi.astype(_BF16).astype(_F32), jnp.uint32)
    return (lo_bits >> 16) | (hi_bits & jnp.uint32(0xFFFF0000))


def _unpack_bf16_pair(u):
    lo = lax.bitcast_convert_type(u << 16, _F32)
    hi = lax.bitcast_convert_type(u & jnp.uint32(0xFFFF0000), _F32)
    return lo, hi


def _pack_quarters(x):
    q = x.shape[1] // 4
    return jnp.stack([_pack_bf16_pair(x[:, 2 * h * q:(2 * h + 1) * q], x[:, (2 * h + 1) * q:(2 * h + 2) * q])
                      for h in range(2)])


def _unpack_quarters(u0, u1):
    return _unpack_bf16_pair(u0) + _unpack_bf16_pair(u1)


def _dir_block(d, i, nb):
    return i + d * (nb - 1 - 2 * i)


def _mod_kernel(c_ref, w_ref, b_ref, o_ref):
    c = c_ref[...]
    s = c * jax.nn.sigmoid(c)
    o_ref[...] = jnp.dot(s, w_ref[...], preferred_element_type=_F32, precision=_HI) + b_ref[...]


def _modulation(c_all, w_mod, b_mod):
    rows, d = c_all.shape
    n = w_mod.shape[1]
    tn = 1024
    return pl.pallas_call(
        _mod_kernel,
        grid=(n // tn,),
        in_specs=[pl.BlockSpec((rows, d), lambda j: (0, 0)),
                  pl.BlockSpec((d, tn), lambda j: (0, j)),
                  pl.BlockSpec((1, tn), lambda j: (0, j))],
        out_specs=pl.BlockSpec((rows, tn), lambda j: (0, j)),
        out_shape=jax.ShapeDtypeStruct((rows, n), _F32),
        compiler_params=_cparams(("parallel",)),
        name="modulation",
    )(c_all, w_mod, b_mod.reshape(1, n))


def _inproj_kernel(x_ref, g_ref, sc_ref, sh_ref, w_ref, rkv_ref, lora_ref, hg_ref):
    x = x_ref[0]
    y = x * lax.rsqrt(jnp.mean(x * x, axis=-1, keepdims=True) + EPS) * g_ref[...]
    h = y * (1.0 + sc_ref[0]) + sh_ref[0]
    u = jnp.dot(h.astype(_BF16), w_ref[...], preferred_element_type=_F32)
    rkv_ref[0] = u[:, :RKV_W]
    lora_ref[0] = u[:, RKV_W:RKV_W + LORA_W]
    hg_ref[0] = u[:, RKV_W + LORA_W:]


def _inproj(x, norm_g, sc, sh, w_in_bf16, tm):
    bsz, l, d = x.shape
    tm = min(tm, l)
    row = lambda b, i: (b, i, 0)
    per_b = lambda b, i: (b, 0, 0)
    return pl.pallas_call(
        _inproj_kernel,
        grid=(bsz, l // tm),
        in_specs=[pl.BlockSpec((1, tm, d), row),
                  pl.BlockSpec((1, d), lambda b, i: (0, 0)),
                  pl.BlockSpec((1, 1, d), per_b),
                  pl.BlockSpec((1, 1, d), per_b),
                  pl.BlockSpec((d, P_IN), lambda b, i: (0, 0))],
        out_specs=[pl.BlockSpec((1, tm, RKV_W), row),
                   pl.BlockSpec((1, tm, LORA_W), row),
                   pl.BlockSpec((1, tm, HG_W), row)],
        out_shape=[jax.ShapeDtypeStruct((bsz, l, RKV_W), _F32),
                   jax.ShapeDtypeStruct((bsz, l, LORA_W), _F32),
                   jax.ShapeDtypeStruct((bsz, l, HG_W), _F32)],
        compiler_params=_cparams(("parallel", "parallel")),
        name="inproj",
    )(x, norm_g.reshape(1, d), sc, sh, w_in_bf16)


def _seg_sum(x, seg_ref):
    hi = x.astype(_BF16)
    lo = (x - hi.astype(_F32)).astype(_BF16)
    return (jnp.dot(hi, seg_ref[...], preferred_element_type=_F32)
            + jnp.dot(lo, seg_ref[...], preferred_element_type=_F32))


def _dot_split(x, w_ref):
    hi = x.astype(_BF16)
    lo = (x - hi.astype(_F32)).astype(_BF16)
    return (jnp.dot(hi, w_ref[0], preferred_element_type=_F32) + jnp.dot(lo, w_ref[0], preferred_element_type=_F32)
            + jnp.dot(hi, w_ref[1], preferred_element_type=_F32))


def _split_bf16(w):
    hi = w.astype(_BF16)
    return jnp.stack([hi, (w - hi.astype(_F32)).astype(_BF16)])


def _rw_prep_kernel(cur_ref, prev_ref, next_ref, lora_ref, kern_ref, w0_ref, w2_ref, a0_ref, a2_ref, g2_ref,
                    kk_ref, ka_ref, rk_ref, seg_ref,
                    r_out, kkn_out, v_out, w_out, kd_out, b_out, bonus_out, g_out, *, grid_conv):
    i = pl.program_id(1)
    tq = cur_ref.shape[0]
    hw = prev_ref.shape[0]
    w = RW_WIDTH
    pmask = jnp.where(i > 0, 1.0, 0.0)
    nmask = jnp.where(i < pl.num_programs(1) - 1, 1.0, 0.0)
    xpos = lax.broadcasted_iota(jnp.int32, (tq + 2 * hw, w), 0) % GRID_W
    conv = []
    for part in range(3):
        cs = slice(part * w, (part + 1) * w)
        xe = jnp.concatenate([prev_ref[:, cs] * pmask, cur_ref[:, cs], next_ref[:, cs] * nmask], axis=0)
        xl = pltpu.roll(xe, 1, 0)
        xr = pltpu.roll(xe, tq + 2 * hw - 1, 0)
        if grid_conv:
            xl = jnp.where(xpos == 0, 0.0, xl)
            xr = jnp.where(xpos == GRID_W - 1, 0.0, xr)
        acc = None
        for dy in range(CONV_K) if grid_conv else (CONV_K // 2,):
            off = hw + (dy - 1) * GRID_W
            for dx, src in enumerate((xl, xe, xr)):
                term = src[off:off + tq] * kern_ref[pl.ds(dy * CONV_K + dx, 1), cs]
                acc = term if acc is None else acc + term
        conv.append(acc)
    r, k, v = conv
    r_out[...] = r
    v_out[...] = v

    kk = k * kk_ref[...]
    kk = kk / jnp.maximum(jnp.sqrt(_seg_sum(kk * kk, seg_ref)), 1e-12)
    kkn_out[...] = kk

    lora = lora_ref[...]
    ksum = jnp.zeros_like(k)
    for d in range(2):
        wl = lora[:, d * RW_LORA:(d + 1) * RW_LORA]
        al = lora[:, (2 + d) * RW_LORA:(3 + d) * RW_LORA]
        z = w0_ref[pl.ds(d, 1), :] + _dot_split(jnp.tanh(wl), w2_ref.at[d])
        wlog = -jax.nn.softplus(-z) - 0.5
        w_out[d] = jnp.exp(-jnp.exp(wlog))
        a = jax.nn.sigmoid(a0_ref[pl.ds(d, 1), :] + _dot_split(al, a2_ref.at[d]))
        kd = k * (1.0 + (a - 1.0) * ka_ref[...])
        kd_out[d] = kd
        b_out[d] = kk * a
        ksum = ksum + kd
    bonus_out[...] = _seg_sum(r * ksum * rk_ref[...], seg_ref) * v
    gd = lora[:, 4 * RW_LORA:]
    g_out[...] = _dot_split(jax.nn.sigmoid(gd), g2_ref)


def _rw_prep(rkv, lora, kern9, w0, w2, a0, a2, g2, k_k, k_a, r_k, seg, tq, grid_conv):
    bsz, l, _ = rkv.shape
    tq = min(tq, l)
    nb = l // tq
    hw = GRID_W
    per = tq // hw
    w = RW_WIDTH
    row = lambda b, i: (b, i, 0)
    const2 = lambda b, i: (0, 0)
    const3 = lambda b, i: (0, 0, 0)
    const4 = lambda b, i: (0, 0, 0, 0)
    tok = pl.BlockSpec((None, tq, w), row)
    tok2 = pl.BlockSpec((2, None, tq, w), lambda b, i: (0, b, i, 0))
    one = jax.ShapeDtypeStruct((bsz, l, w), _F32)
    two = jax.ShapeDtypeStruct((2, bsz, l, w), _F32)
    vec = lambda t: t.reshape(1, w)
    return pl.pallas_call(
        functools.partial(_rw_prep_kernel, grid_conv=grid_conv),
        grid=(bsz, nb),
        in_specs=[pl.BlockSpec((None, tq, RKV_W), row),
                  pl.BlockSpec((None, hw, RKV_W), lambda b, i: (b, jnp.maximum(i * per - 1, 0), 0)),
                  pl.BlockSpec((None, hw, RKV_W), lambda b, i: (b, jnp.minimum((i + 1) * per, l // hw - 1), 0)),
                  pl.BlockSpec((None, tq, LORA_W), row),
                  pl.BlockSpec((CONV_K * CONV_K, RKV_W), const2),
                  pl.BlockSpec((2, w), const2),
                  pl.BlockSpec((2, 2, RW_LORA, w), const4),
                  pl.BlockSpec((2, w), const2),
                  pl.BlockSpec((2, 2, RW_LORA, w), const4),
                  pl.BlockSpec((2, RW_GATE_LORA, w), const3),
                  pl.BlockSpec((1, w), const2),
                  pl.BlockSpec((1, w), const2),
                  pl.BlockSpec((1, w), const2),
                  pl.BlockSpec((w, w), const2)],
        out_specs=[tok, tok, tok, tok2, tok2, tok2, tok, tok],
        out_shape=[one, one, one, two, two, two, one, one],
        compiler_params=_cparams(("parallel", "parallel")),
        name="rwkv_prep_grid" if grid_conv else "rwkv_prep_seq",
    )(rkv, rkv, rkv, lora, kern9, w0, w2, a0, a2, g2, vec(k_k), vec(k_a), vec(r_k), seg)


def _rwkv_scan_kernel(*refs, tt, emit, has_init):
    r_ref, kk_ref, v_ref, w_ref, k_ref, b_ref = refs[:6]
    rest = list(refs[6:])
    s0_ref = rest.pop(0) if has_init else None
    out_ref = rest.pop(0)
    s_ref, sa_ref = rest
    d = pl.program_id(0)
    i = pl.program_id(1)
    n = RW_HEAD_DIM

    @pl.when(i == 0)
    def _():
        s_ref[...] = s0_ref[...] if has_init else jnp.zeros_like(s_ref)

    hv = n // 2
    halves = [pl.ds(h * hv, hv) for h in range(2)]

    def project(t, vs):
        acc = [jnp.zeros((hv, LANES), _F32), jnp.zeros((hv, LANES), _F32)]
        for k in range(n):
            acc[k % 2] = acc[k % 2] + s_ref[k, vs, :] * kk_ref[t, pl.ds(k, 1), :]
        return -(acc[0] + acc[1])

    first = d * (tt - 1)
    for vs in halves:
        sa_ref[vs, :] = project(first, vs)

    def step(j, carry):
        t = j + d * (tt - 1 - 2 * j)
        tn = jnp.clip(t + 1 - 2 * d, 0, tt - 1)
        for vs in halves:
            sa = sa_ref[vs, :]
            vv = v_ref[t, vs, :]
            y = [jnp.zeros((hv, LANES), _F32), jnp.zeros((hv, LANES), _F32)]
            nsa = [jnp.zeros((hv, LANES), _F32), jnp.zeros((hv, LANES), _F32)]
            for k in range(n):
                row = pl.ds(k, 1)
                s_new = s_ref[k, vs, :] * w_ref[t, row, :] + sa * b_ref[t, row, :] + vv * k_ref[t, row, :]
                s_ref[k, vs, :] = s_new
                nsa[k % 2] = nsa[k % 2] + s_new * kk_ref[tn, row, :]
                if emit:
                    y[k % 2] = y[k % 2] + s_new * r_ref[t, row, :]
            sa_ref[vs, :] = -(nsa[0] + nsa[1])
            if emit:
                out_ref[t, vs, :] = y[0] + y[1]
        return carry

    lax.fori_loop(0, tt, step, 0)

    if not emit:
        @pl.when(i == pl.num_programs(1) - 1)
        def _():
            out_ref[...] = s_ref[...]


def _rwkv_scan(shared, per_dir, s0, tt, emit):
    l, n, c = shared[0].shape
    nb = l // tt
    sh_spec = pl.BlockSpec((tt, n, c), lambda d, i: (_dir_block(d, i, nb), 0, 0))
    pd_spec = pl.BlockSpec((None, tt, n, c), lambda d, i: (d, _dir_block(d, i, nb), 0, 0))
    st_spec = pl.BlockSpec((None, n, n, c), lambda d, i: (d, 0, 0, 0))
    in_specs = [sh_spec] * 3 + [pd_spec] * 3
    args = list(shared) + list(per_dir)
    if s0 is not None:
        in_specs.append(st_spec)
        args.append(s0)
    if emit:
        out_spec, out_shape = pd_spec, jax.ShapeDtypeStruct((2, l, n, c), _F32)
    else:
        out_spec, out_shape = st_spec, jax.ShapeDtypeStruct((2, n, n, c), _F32)
    return pl.pallas_call(
        functools.partial(_rwkv_scan_kernel, tt=tt, emit=emit, has_init=s0 is not None),
        grid=(2, nb),
        in_specs=in_specs,
        out_specs=out_spec,
        out_shape=out_shape,
        scratch_shapes=[pltpu.VMEM((n, n, c), _F32), pltpu.VMEM((n, c), _F32)],
        compiler_params=_cparams(("parallel", "arbitrary")),
        name="rwkv_scan_emit" if emit else "rwkv_scan_state",
    )(*args)


def _hgrn_kernel(*refs, n_chunks, emit, has_init, reverse):
    q_ref, f_ref, i_ref, lb_ref, mats_ref = refs[:5]
    rest = list(refs[5:])
    s0_ref = rest.pop(0) if has_init else None
    out_ref = rest.pop(0)
    st_ref, upd_ref, ent_ref = rest
    c = HG_CHUNK
    tb = n_chunks * c
    blk = pl.program_id(1)

    @pl.when(blk == 0)
    def _():
        st_ref[...] = s0_ref[...] if has_init else jnp.zeros_like(st_ref)

    lb = lb_ref[...]
    omlb = 1.0 - lb
    fr = f_ref[...]
    logf = jnp.log(lb + omlb * jax.nn.sigmoid(fr))
    kf = omlb * jax.nn.sigmoid(-fr)
    hi = logf.astype(_BF16)
    lo = (logf - hi.astype(_F32)).astype(_BF16)
    sums = (jnp.dot(mats_ref[...], hi, preferred_element_type=_F32)
            + jnp.dot(mats_ref[...], lo, preferred_element_type=_F32))
    cum, tot, mid = sums[:tb], sums[tb:2 * tb], sums[2 * tb:]
    dec = jnp.exp(tot)
    kdec = (kf * jnp.exp(tot - cum)).astype(_BF16)
    vb = i_ref[...].astype(_BF16)
    same_head = ((lax.broadcasted_iota(jnp.int32, (LANES, LANES), 0) < HG_DIM)
                 == (lax.broadcasted_iota(jnp.int32, (LANES, LANES), 1) < HG_DIM))
    pairs = [slice(p * LANES, (p + 1) * LANES) for p in range(HG_PAIRS)]
    chunks = [slice(j * c, (j + 1) * c) for j in range(n_chunks)]

    if emit:
        q = q_ref[...]
        qs = q * jax.nn.sigmoid(q) * (HG_DIM ** -0.5)
        qd = qs * jnp.exp(cum - mid)
        kd = (kf * jnp.exp(mid - cum)).astype(_BF16)
        qe = (qs * jnp.exp(cum)).astype(_BF16)
        head0 = lax.broadcasted_iota(jnp.int32, (tb, LANES), 1) < HG_DIM
        causal = mats_ref[:tb, :] > 0
        causal2 = jnp.concatenate([causal, causal], axis=0)
        for ls in pairs:
            qd_p = qd[:, ls]
            q2 = jnp.concatenate([jnp.where(head0, qd_p, 0.0), jnp.where(head0, 0.0, qd_p)], axis=0)
            sc = lax.dot_general(q2.astype(_BF16), kd[:, ls], _NT, preferred_element_type=_F32)
            sc = jnp.where(causal2, sc, 0.0)
            oi = jnp.dot(sc.astype(_BF16), vb[:, ls], preferred_element_type=_F32)
            out_ref[:, ls] = jnp.where(head0, oi[:tb], oi[tb:])

    for j, rows in enumerate(chunks):
        for p, ls in enumerate(pairs):
            upd = lax.dot_general(vb[rows, ls], kdec[rows, ls], _TN, preferred_element_type=_F32)
            upd_ref[j, p] = jnp.where(same_head, upd, 0.0)

    order = list(range(n_chunks))[::-1] if reverse else list(range(n_chunks))
    for p, ls in enumerate(pairs):
        st = st_ref[p]
        for j in order:
            if emit:
                ent_ref[j, p] = st.astype(_BF16)
            st = st * dec[j * c:j * c + 1, ls] + upd_ref[j, p]
        st_ref[p] = st

    if emit:
        for j, rows in enumerate(chunks):
            for p, ls in enumerate(pairs):
                out_ref[rows, ls] += lax.dot_general(qe[rows, ls], ent_ref[j, p], _NT,
                                                     preferred_element_type=_F32)
    else:
        @pl.when(blk == pl.num_programs(1) - 1)
        def _():
            out_ref[...] = st_ref[...]


def _hgrn_mats(tb, reverse):
    t = jnp.arange(tb)
    same = (t[:, None] // HG_CHUNK) == (t[None, :] // HG_CHUNK)
    ref_pos = (t // HG_CHUNK) * HG_CHUNK + HG_CHUNK // 2
    if reverse:
        cum = same & (t[None, :] >= t[:, None])
        mid = same & (t[None, :] >= ref_pos[:, None])
    else:
        cum = same & (t[None, :] <= t[:, None])
        mid = same & (t[None, :] <= ref_pos[:, None])
    return jnp.concatenate([cum, same, mid], axis=0).astype(_BF16)


def _hgrn_scan(hg, lb, s0, tb, emit, reverse):
    bsz, l, _ = hg.shape
    tb = min(tb, l)
    nb = l // tb
    w = HG_WIDTH
    n_chunks = tb // HG_CHUNK
    tblk = (lambda i: nb - 1 - i) if reverse else (lambda i: i)
    col = lambda j: pl.BlockSpec((None, tb, w), lambda b, i: (b, tblk(i), j))
    const = lambda b, i: (0, 0)
    st_spec = pl.BlockSpec((None, HG_PAIRS, LANES, LANES), lambda b, i: (b, 0, 0, 0))
    in_specs = [col(0), col(2 if reverse else 1), col(3), pl.BlockSpec((1, w), const),
                pl.BlockSpec((3 * tb, tb), const)]
    args = [hg, hg, hg, lb, _hgrn_mats(tb, reverse)]
    if s0 is not None:
        in_specs.append(st_spec)
        args.append(s0)
    if emit:
        out_spec = pl.BlockSpec((None, tb, w), lambda b, i: (b, tblk(i), 0))
        out_shape = jax.ShapeDtypeStruct((bsz, l, w), _F32)
    else:
        out_spec = st_spec
        out_shape = jax.ShapeDtypeStruct((bsz, HG_PAIRS, LANES, LANES), _F32)
    return pl.pallas_call(
        functools.partial(_hgrn_kernel, n_chunks=n_chunks, emit=emit, has_init=s0 is not None,
                          reverse=reverse),
        grid=(bsz, nb),
        in_specs=in_specs,
        out_specs=out_spec,
        out_shape=out_shape,
        scratch_shapes=[pltpu.VMEM((HG_PAIRS, LANES, LANES), _F32),
                        pltpu.VMEM((n_chunks, HG_PAIRS, LANES, LANES), _F32),
                        pltpu.VMEM((n_chunks, HG_PAIRS, LANES, LANES), _BF16)],
        compiler_params=_cparams(("parallel", "arbitrary")),
        name=("hgrn_emit" if emit else "hgrn_state") + ("_bwd" if reverse else "_fwd"),
    )(*args)


def _post_kernel(y_ref, bonus_ref, grw_ref, lng_ref, lnb_ref, of_ref, ob_ref, ghg_ref, hgn_ref, seg_ref,
                 x_ref, w_ref, g1_ref, n2_ref, sc_ref, sh_ref, rw_ref, x1_ref, h2_ref, lg_ref):
    inv_n = 1.0 / RW_HEAD_DIM
    ysum = y_ref[0] + y_ref[1]
    mu = _seg_sum(ysum, seg_ref) * inv_n
    dlt = ysum - mu
    var = _seg_sum(dlt * dlt, seg_ref) * inv_n
    yn = dlt * lax.rsqrt(var + RW_GN_EPS) * lng_ref[...] + lnb_ref[...] + bonus_ref[...]
    y_rw = yn * grw_ref[...]
    o = of_ref[...] + ob_ref[...]
    o = o * lax.rsqrt(_seg_sum(o * o, seg_ref) * inv_n + EPS) * hgn_ref[...]
    g = ghg_ref[...]
    y_hg = o * (g * jax.nn.sigmoid(g))
    y = (jnp.dot(y_rw.astype(_BF16), w_ref[:RW_WIDTH, :], preferred_element_type=_F32)
         + jnp.dot(y_hg.astype(_BF16), w_ref[RW_WIDTH:, :], preferred_element_type=_F32))
    x1 = x_ref[0] + g1_ref[0] * y
    x1_ref[0] = x1
    h2 = x1 * lax.rsqrt(jnp.mean(x1 * x1, axis=-1, keepdims=True) + EPS) * n2_ref[...]
    h2 = h2 * (1.0 + sc_ref[0]) + sh_ref[0]
    h2_ref[...] = _pack_quarters(h2)
    lg_ref[...] = lax.dot_general(rw_ref[...], h2, _NT, preferred_element_type=_F32, precision=_HI)


def _post(y_dirs, bonus, g_rw, lnx_g, lnx_b, o_dirs, hg, hg_norm_row, seg,
          x, w_out_bf16, g1, norm2_g, sc2, sh2, router_wt, tm):
    bsz, l, d = x.shape
    tm = min(tm, l)
    nt = l // tm
    w = RW_WIDTH
    row = lambda b, i: (b, i, 0)
    per_b = lambda b, i: (b, 0, 0)
    const = lambda b, i: (0, 0)
    tok = pl.BlockSpec((None, tm, w), row)
    tok2 = pl.BlockSpec((2, None, tm, w), lambda b, i: (0, b, i, 0))
    vec = pl.BlockSpec((1, w), const)
    return pl.pallas_call(
        _post_kernel,
        grid=(bsz, nt),
        in_specs=[tok2, tok, tok, vec, vec, tok, tok,
                  pl.BlockSpec((None, tm, w), lambda b, i: (b, i, 4)),
                  vec,
                  pl.BlockSpec((w, w), const),
                  pl.BlockSpec((1, tm, d), row),
                  pl.BlockSpec((d, d), const),
                  pl.BlockSpec((1, 1, d), per_b),
                  pl.BlockSpec((1, d), const),
                  pl.BlockSpec((1, 1, d), per_b),
                  pl.BlockSpec((1, 1, d), per_b),
                  pl.BlockSpec((N_EXPERTS, d), const)],
        out_specs=[pl.BlockSpec((1, tm, d), row),
                   pl.BlockSpec((2, tm, d // 4), lambda b, i: (0, b * nt + i, 0)),
                   pl.BlockSpec((N_EXPERTS, tm), lambda b, i: (0, b * nt + i))],
        out_shape=[jax.ShapeDtypeStruct((bsz, l, d), _F32),
                   jax.ShapeDtypeStruct((2, bsz * l, d // 4), jnp.uint32),
                   jax.ShapeDtypeStruct((N_EXPERTS, bsz * l), _F32)],
        compiler_params=_cparams(("parallel", "parallel")),
        name="outproj_norm2_router",
    )(y_dirs, bonus, g_rw, lnx_g.reshape(1, w), lnx_b.reshape(1, w), o_dirs[0], o_dirs[1], hg, hg_norm_row, seg,
      x, w_out_bf16, g1, norm2_g.reshape(1, d), sc2, sh2, router_wt)


def _sublane_all(x, op):
    for s in (4, 2, 1):
        x = op(x, pltpu.roll(x, s, 0))
    return x


def _router_kernel(lg_ref, bias_ref, tri_ref, e_ref, g_ref, rank_ref, cnt_ref):
    ng, gs = N_GROUPS, GROUP_SIZE
    tn = lg_ref.shape[1]
    neg = -jnp.inf
    gidx = lax.broadcasted_iota(jnp.int32, (ng, tn), 0)
    bias = jnp.concatenate([bias_ref[...]] * (tn // LANES), axis=1)
    scores = [jax.nn.sigmoid(lg_ref[j * ng:(j + 1) * ng, :]) for j in range(gs)]
    sel = [scores[j] + bias[j * ng:(j + 1) * ng, :] for j in range(gs)]

    m1 = functools.reduce(jnp.maximum, sel)
    cnt = functools.reduce(jnp.add, [(s == m1).astype(_F32) for s in sel])
    m2 = functools.reduce(jnp.maximum, [jnp.where(s < m1, s, neg) for s in sel])
    gscore = m1 + jnp.where(cnt >= 2.0, m1, m2)

    rank = jnp.zeros((ng, tn), jnp.int32)
    for s in range(1, ng):
        other = pltpu.roll(gscore, s, 0)
        beats = jnp.where(other > gscore, 1, jnp.where((other == gscore) & (gidx >= s), 1, 0))
        rank = rank + beats
    gsel = rank < TOPK_GROUPS

    key = [jnp.where(gsel, s, neg) for s in sel]
    eidx = [gidx * gs + j for j in range(gs)]
    avail = [jnp.ones((ng, tn), jnp.int32) for _ in range(gs)]
    top_e = jnp.zeros((TOP_K, tn), jnp.int32)
    top_s = jnp.zeros((TOP_K, tn), _F32)
    for r in range(TOP_K):
        cur = functools.reduce(jnp.maximum, [jnp.where(avail[j] > 0, key[j], neg) for j in range(gs)])
        mx = _sublane_all(cur, jnp.maximum)
        cand = [jnp.where((avail[j] > 0) & (key[j] == mx), eidx[j], N_EXPERTS) for j in range(gs)]
        mn = _sublane_all(functools.reduce(jnp.minimum, cand), jnp.minimum)
        picked = functools.reduce(jnp.add, [jnp.where(eidx[j] == mn, scores[j], 0.0) for j in range(gs)])
        avail = [jnp.where(eidx[j] == mn, 0, avail[j]) for j in range(gs)]
        top_e = jnp.where(gidx == r, mn, top_e)
        top_s = jnp.where(gidx == r, _sublane_all(picked, jnp.add), top_s)
    den = _sublane_all(top_s, jnp.add)
    e_ref[...] = top_e
    g_ref[...] = top_s / den * ROUTE_SCALE

    @pl.when(pl.program_id(0) == 0)
    def _():
        cnt_ref[...] = jnp.zeros_like(cnt_ref)

    taken = jnp.concatenate([jnp.where(a == 0, 1.0, 0.0) for a in avail], axis=0)
    before = jnp.dot(taken.astype(_BF16), tri_ref[...], preferred_element_type=_F32) - taken
    before = before + jnp.concatenate([cnt_ref[...]] * (tn // LANES), axis=1)
    rank = jnp.zeros((TOP_K, tn), _F32)
    for r in range(TOP_K):
        e_r = jnp.broadcast_to(top_e[r:r + 1, :], (ng, tn))
        hit = functools.reduce(jnp.add, [jnp.where(eidx[j] == e_r, before[j * ng:(j + 1) * ng, :], 0.0)
                                         for j in range(gs)])
        rank = jnp.where(gidx == r, _sublane_all(hit, jnp.add), rank)
    rank_ref[...] = rank.astype(jnp.int32)
    ones = jnp.ones((tn, LANES), _BF16)
    cnt_ref[...] += jnp.dot(taken.astype(_BF16), ones, preferred_element_type=_F32)


def _router(logits_t, bias_rows, tn):
    assert TOP_K == N_GROUPS == SUBLANES
    t = logits_t.shape[1]
    tn = min(tn, t)
    tri = (jnp.arange(tn)[:, None] <= jnp.arange(tn)[None, :]).astype(_BF16)
    out = pl.BlockSpec((TOP_K, tn), lambda i: (0, i))
    cnt = pl.BlockSpec((N_EXPERTS, LANES), lambda i: (0, 0))
    return pl.pallas_call(
        _router_kernel,
        grid=(t // tn,),
        in_specs=[pl.BlockSpec((N_EXPERTS, tn), lambda i: (0, i)), cnt,
                  pl.BlockSpec((tn, tn), lambda i: (0, 0))],
        out_specs=[out, out, out, cnt],
        out_shape=[jax.ShapeDtypeStruct((TOP_K, t), jnp.int32), jax.ShapeDtypeStruct((TOP_K, t), _F32),
                   jax.ShapeDtypeStruct((TOP_K, t), jnp.int32),
                   jax.ShapeDtypeStruct((N_EXPERTS, LANES), _F32)],
        compiler_params=_cparams(("arbitrary",)),
        name="router_topk",
    )(logits_t, bias_rows, tri)


def _sc_mesh():
    return plsc.VectorSubcoreMesh(core_axis_name="c", subcore_axis_name="s")


def _sc_dispatch(x_half_rows, dest_half_rows, n_out):
    n_rows = x_half_rows.shape[0]

    @pl.kernel(out_type=jax.ShapeDtypeStruct((n_out, SC_ROW), x_half_rows.dtype), mesh=_sc_mesh(),
               scratch_types=[])
    def scatter_rows(x_hbm, d_hbm, o_hbm):
        def body(x_vmem, i_vmem):
            for j in range(TOP_K):
                pltpu.sync_copy(x_vmem, o_hbm.at[i_vmem.at[j]])

        pltpu.emit_pipeline(
            body,
            grid=(n_rows // SC_WINDOW,),
            in_specs=[pl.BlockSpec((SC_WINDOW, SC_ROW), lambda i: (i, 0)),
                      pl.BlockSpec((TOP_K, SC_WINDOW), lambda i: (0, i))],
            out_specs=[],
            core_axis_name=("c", "s"),
            dimension_semantics=(pltpu.PARALLEL,),
        )(x_hbm, d_hbm)

    return scatter_rows(x_half_rows, dest_half_rows)


def _sc_gather(y_half_rows, idx):
    n = idx.shape[1]

    @pl.kernel(out_type=jax.ShapeDtypeStruct((n, SC_ROW), y_half_rows.dtype), mesh=_sc_mesh(),
               scratch_types=[])
    def gather_rows(y_hbm, i_hbm, o_hbm):
        def body(i_vmem, o_vmem):
            pltpu.sync_copy(y_hbm.at[i_vmem.at[0]], o_vmem)

        pltpu.emit_pipeline(
            body,
            grid=(n // SC_WINDOW,),
            in_specs=[pl.BlockSpec((1, SC_WINDOW), lambda i: (0, i))],
            out_specs=[pl.BlockSpec((SC_WINDOW, SC_ROW), lambda i: (i, 0))],
            core_axis_name=("c", "s"),
            dimension_semantics=(pltpu.PARALLEL,),
        )(i_hbm, o_hbm)

    return gather_rows(y_half_rows, idx)


def _dispatch_plan(top_e, rank, counts, n_blocks):
    padded = (counts + MOE_BLOCK - 1) // MOE_BLOCK * MOE_BLOCK
    pad_end = jnp.cumsum(padded)
    pad_start = pad_end - padded
    onehot = top_e[:, :, None] == jnp.arange(N_EXPERTS, dtype=jnp.int32)[None, None, :]
    dest = rank + jnp.sum(jnp.where(onehot, pad_start[None, None, :], 0), axis=-1)
    blk_start = jnp.arange(n_blocks, dtype=jnp.int32) * MOE_BLOCK
    blk_e = jnp.minimum(jnp.sum(pad_end[None, :] <= blk_start[:, None], axis=1), N_EXPERTS - 1)
    n_used = pad_end[-1] // MOE_BLOCK
    return dest, jnp.concatenate([blk_e.astype(jnp.int32), n_used[None].astype(jnp.int32)])


def _swiglu_packed(u0, u1, wg_ref, wu_ref, wd_ref):
    x = jnp.concatenate(_unpack_quarters(u0, u1), axis=1).astype(_BF16)
    a = jnp.dot(x, wg_ref[...].astype(_BF16), preferred_element_type=_F32)
    u = jnp.dot(x, wu_ref[...].astype(_BF16), preferred_element_type=_F32)
    act = (a * jax.nn.sigmoid(a) * u).astype(_BF16)
    return jnp.dot(act, wd_ref[...].astype(_BF16), preferred_element_type=_F32)


def _expert_kernel(plan_ref, x_ref, wg_ref, wu_ref, wd_ref, y_ref):
    @pl.when(pl.program_id(0) < plan_ref[pl.num_programs(0)])
    def _():
        y_ref[...] = _pack_quarters(_swiglu_packed(x_ref[0], x_ref[1], wg_ref, wu_ref, wd_ref))


def _experts(plan, x_sorted, wg, wu, wd):
    _, n_rows, q = x_sorted.shape
    n_blocks = n_rows // MOE_BLOCK
    d = 4 * q
    rows = pl.BlockSpec((2, MOE_BLOCK, q), lambda i, plan: (0, i, 0))
    grid_spec = pltpu.PrefetchScalarGridSpec(
        num_scalar_prefetch=1,
        grid=(n_blocks,),
        in_specs=[rows,
                  pl.BlockSpec((None, d, D_EXPERT), lambda i, plan: (plan[i], 0, 0)),
                  pl.BlockSpec((None, d, D_EXPERT), lambda i, plan: (plan[i], 0, 0)),
                  pl.BlockSpec((None, D_EXPERT, d), lambda i, plan: (plan[i], 0, 0))],
        out_specs=rows,
    )
    return pl.pallas_call(
        _expert_kernel,
        grid_spec=grid_spec,
        out_shape=jax.ShapeDtypeStruct((2, n_rows, q), jnp.uint32),
        compiler_params=_cparams(("arbitrary",)),
        name="moe_routed_experts",
    )(plan, x_sorted, wg, wu, wd)


def _combine_kernel(y_ref, gate_ref, h_ref, x1_ref, g2_ref, fg_ref, sg_ref, su_ref, sd_ref, o_ref):
    gate = gate_ref[...]
    acc = None
    for j in range(TOP_K):
        gj = gate[:, j:j + 1]
        parts = [gj * p for p in _unpack_quarters(y_ref[j, 0], y_ref[j, 1])]
        acc = parts if acc is None else [a + p for a, p in zip(acc, parts)]
    moe = jnp.concatenate(acc, axis=1) + _swiglu_packed(h_ref[0], h_ref[1], sg_ref, su_ref, sd_ref)
    x2 = x1_ref[0] + g2_ref[0] * moe
    o_ref[0] = x2 * lax.rsqrt(jnp.mean(x2 * x2, axis=-1, keepdims=True) + EPS) * fg_ref[...]


def _combine(y_rows, gate_tk, h2_packed, x1, g2, final_g, sg, su, sd, tm):
    bsz, l, d = x1.shape
    tm = min(tm, l)
    nt = l // tm
    q = d // 4
    const = lambda b, i: (0, 0)
    return pl.pallas_call(
        _combine_kernel,
        grid=(bsz, nt),
        in_specs=[pl.BlockSpec((TOP_K, 2, tm, q), lambda b, i: (0, 0, b * nt + i, 0)),
                  pl.BlockSpec((tm, TOP_K), lambda b, i: (b * nt + i, 0)),
                  pl.BlockSpec((2, tm, q), lambda b, i: (0, b * nt + i, 0)),
                  pl.BlockSpec((1, tm, d), lambda b, i: (b, i, 0)),
                  pl.BlockSpec((1, 1, d), lambda b, i: (b, 0, 0)),
                  pl.BlockSpec((1, d), const),
                  pl.BlockSpec((d, D_EXPERT), const),
                  pl.BlockSpec((d, D_EXPERT), const),
                  pl.BlockSpec((D_EXPERT, d), const)],
        out_specs=pl.BlockSpec((1, tm, d), lambda b, i: (b, i, 0)),
        out_shape=jax.ShapeDtypeStruct((bsz, l, d), _F32),
        compiler_params=_cparams(("parallel", "parallel")),
        name="moe_combine_shared_final_norm",
    )(y_rows, gate_tk, h2_packed, x1, g2, final_g.reshape(1, d), sg, su, sd)


def _to_chain_lanes(t):
    lead, (b, l, _) = t.shape[:-3], t.shape[-3:]
    nl = len(lead)
    t = t.reshape(*lead, b, l, RW_HEADS, RW_HEAD_DIM)
    t = t.transpose(*range(nl), nl + 1, nl + 3, nl, nl + 2)
    return t.reshape(*lead, l, RW_HEAD_DIM, b * RW_HEADS)


def _from_chain_lanes(t, b):
    l = t.shape[1]
    return t.reshape(2, l, RW_HEAD_DIM, b, RW_HEADS).transpose(0, 3, 1, 4, 2).reshape(2, b, l, RW_WIDTH)


def _scan_operands(prep):
    r, kk, v, w, kd, bb = prep[:6]
    return [_to_chain_lanes(t) for t in (r, kk, v)], [_to_chain_lanes(t) for t in (w, kd, bb)]


def kernel(x, c, ctx, c_ctx, w_mod, b_mod, norm1_g, norm2_g, w_in, rw_conv, rw_w0, rw_w2, rw_a0, rw_a2,
           rw_g2, rw_k_k, rw_k_a, rw_r_k, rw_lnx_g, rw_lnx_b, hg_lb_logits, hg_norm_g, w_out, router_w,
           router_b, exp_w_gate, exp_w_up, exp_w_down, sh_w_gate, sh_w_up, sh_w_down, final_norm_g):
    bsz, seq, d = x.shape
    assert w_mod.shape[0] == 1 and bsz * RW_HEADS == LANES
    lyr = 0

    c_all = jnp.concatenate([c, c_ctx[None, :], jnp.zeros((SUBLANES - 1, d), _F32)], axis=0)
    mod = _modulation(c_all, w_mod[lyr], b_mod[lyr])
    sh1, sc1, g1, sh2, sc2, g2 = [m[:, None, :] for m in jnp.split(mod[:bsz], N_MOD, axis=-1)]
    mod_ctx = jnp.broadcast_to(mod[bsz][None, None, :], (bsz, 1, N_MOD * d))
    csh1, csc1 = mod_ctx[..., :d], mod_ctx[..., d:2 * d]

    w_in_bf16 = w_in[lyr].astype(_BF16)
    rkv_lat, lora_lat, hg_lat = _inproj(x, norm1_g[lyr], sc1, sh1, w_in_bf16, tm=256)
    rkv_ctx, lora_ctx, hg_ctx = _inproj(ctx, norm1_g[lyr], csc1, csh1, w_in_bf16, tm=256)

    kern9 = rw_conv[lyr].reshape(CONV_K * CONV_K, RKV_W)
    lane_head = jnp.arange(RW_WIDTH) // RW_HEAD_DIM
    seg = (lane_head[:, None] == lane_head[None, :]).astype(_BF16)
    per_dir_split = lambda t: jnp.swapaxes(_split_bf16(t), 0, 1)
    rw_p = (kern9, rw_w0[lyr], per_dir_split(rw_w2[lyr]), rw_a0[lyr], per_dir_split(rw_a2[lyr]),
            _split_bf16(rw_g2[lyr]), rw_k_k[lyr], rw_k_a[lyr], rw_r_k[lyr], seg)
    prep_lat = _rw_prep(rkv_lat, lora_lat, *rw_p, tq=256, grid_conv=True)
    prep_ctx = _rw_prep(rkv_ctx, lora_ctx, *rw_p, tq=256, grid_conv=False)
    s_rw = _rwkv_scan(*_scan_operands(prep_ctx), None, tt=16, emit=False)
    y_dirs = _from_chain_lanes(_rwkv_scan(*_scan_operands(prep_lat), s_rw, tt=16, emit=True), bsz)

    lb = jnp.cumsum(jax.nn.softmax(hg_lb_logits.astype(_F32), axis=1), axis=1)[:, lyr]
    o_dirs = []
    for dd, rev in enumerate((False, True)):
        s_hg = _hgrn_scan(hg_ctx, lb[dd:dd + 1], None, tb=256, emit=False, reverse=rev)
        o_dirs.append(_hgrn_scan(hg_lat, lb[dd:dd + 1], s_hg, tb=256, emit=True, reverse=rev))

    perm = lambda t: t.reshape(N_GROUPS, GROUP_SIZE, -1).transpose(1, 0, 2).reshape(N_EXPERTS, -1)
    router_wt = perm(router_w[lyr].T)
    bias_rows = jnp.broadcast_to(perm(router_b[lyr][:, None]), (N_EXPERTS, LANES))
    hg_norm_row = jnp.tile(hg_norm_g[lyr], HG_HEADS).reshape(1, HG_WIDTH)
    x1, h2, logits_t = _post(y_dirs, prep_lat[6], prep_lat[7], rw_lnx_g[lyr], rw_lnx_b[lyr], o_dirs, hg_lat,
                             hg_norm_row, seg, x, w_out[lyr].astype(_BF16), g1, norm2_g[lyr], sc2, sh2,
                             router_wt, tm=512)
    top_e, gate, rank, counts_rows = _router(logits_t, bias_rows, tn=512)
    counts = counts_rows[:, 0].astype(jnp.int32).reshape(GROUP_SIZE, N_GROUPS).T.reshape(N_EXPERTS)

    n_tok = bsz * seq
    n_blocks = n_tok * TOP_K // MOE_BLOCK + N_EXPERTS
    n_slots = n_blocks * MOE_BLOCK
    dest, plan = _dispatch_plan(top_e, rank, counts, n_blocks)
    dest_half = (dest[:, None, :] + jnp.array([0, n_slots], jnp.int32)[None, :, None]).reshape(TOP_K, 2 * n_tok)
    x_sorted = _sc_dispatch(h2.reshape(2 * n_tok, SC_ROW), dest_half, 2 * n_slots)
    y_sorted = _experts(plan, x_sorted.reshape(2, n_slots, SC_ROW),
                        exp_w_gate[lyr], exp_w_up[lyr], exp_w_down[lyr])
    y_rows = _sc_gather(y_sorted.reshape(2 * n_slots, SC_ROW), dest_half.reshape(1, TOP_K * 2 * n_tok))
    return _combine(y_rows.reshape(TOP_K, 2, n_tok, SC_ROW), gate.T, h2, x1, g2,
                    final_norm_g, sh_w_gate[lyr], sh_w_up[lyr], sh_w_down[lyr], tm=512)
```

```python
import functools

import jax
import jax.numpy as jnp
from jax import lax
from jax.experimental import pallas as pl
from jax.experimental.pallas import tpu as pltpu
from jax.experimental.pallas import tpu_sc as plsc

D_MODEL = 1024
GRID_W = 64
N_MOD = 6
EPS = 1e-6

RW_HEAD_DIM = 64
RW_HEADS = 8
RW_WIDTH = 512
RW_LORA = 64
RW_GATE_LORA = 128
RW_GN_EPS = 1e-5 * RW_HEAD_DIM
CONV_K = 3

HG_HEADS = 8
HG_DIM = 64
HG_WIDTH = 512
HG_CHUNK = 32
HG_PAIRS = HG_HEADS // 2

RKV_W = 3 * RW_WIDTH
LORA_W = 4 * RW_LORA + RW_GATE_LORA
HG_W = 5 * HG_WIDTH
P_IN = RKV_W + LORA_W + HG_W

N_EXPERTS = 64
N_GROUPS = 8
GROUP_SIZE = N_EXPERTS // N_GROUPS
TOPK_GROUPS = 4
TOP_K = 8
D_EXPERT = 256
ROUTE_SCALE = 2.5
MOE_BLOCK = 512
SC_ROW = 256
SC_WINDOW = 128

LANES = 128
SUBLANES = 8
VMEM_LIMIT = 48 * 1024 * 1024

_HI = lax.Precision.HIGHEST
_F32 = jnp.float32
_BF16 = jnp.bfloat16
_NT = (((1,), (1,)), ((), ()))
_TN = (((0,), (0,)), ((), ()))


def _cparams(sem):
    return pltpu.CompilerParams(dimension_semantics=sem, vmem_limit_bytes=VMEM_LIMIT)


def _pack_bf16_pair(lo, hi):
    return pltpu.pack_elementwise([lo, hi], packed_dtype=_BF16)


def _unpack_bf16_pair(u):
    return tuple(pltpu.unpack_elementwise(u, index=i, packed_dtype=_BF16, unpacked_dtype=_F32) for i in range(2))


def _pack_quarters(x):
    q = x.shape[1] // 4
    return jnp.stack([_pack_bf16_pair(x[:, 2 * h * q:(2 * h + 1) * q], x[:, (2 * h + 1) * q:(2 * h + 2) * q])
                      for h in range(2)])


def _unpack_quarters(u0, u1):
    return _unpack_bf16_pair(u0) + _unpack_bf16_pair(u1)


def _dir_block(d, i, nb):
    return i + d * (nb - 1 - 2 * i)


def _mod_kernel(c_ref, w_ref, b_ref, o_ref):
    c = c_ref[...]
    s = c * jax.nn.sigmoid(c)
    o_ref[...] = jnp.dot(s, w_ref[...], preferred_element_type=_F32, precision=_HI) + b_ref[...]


def _modulation(c_all, w_mod, b_mod):
    rows, d = c_all.shape
    n = w_mod.shape[1]
    tn = 1024
    return pl.pallas_call(
        _mod_kernel,
        grid=(n // tn,),
        in_specs=[pl.BlockSpec((rows, d), lambda j: (0, 0)),
                  pl.BlockSpec((d, tn), lambda j: (0, j)),
                  pl.BlockSpec((1, tn), lambda j: (0, j))],
        out_specs=pl.BlockSpec((rows, tn), lambda j: (0, j)),
        out_shape=jax.ShapeDtypeStruct((rows, n), _F32),
        compiler_params=_cparams(("parallel",)),
        name="modulation",
    )(c_all, w_mod, b_mod.reshape(1, n))


def _inproj_kernel(x_ref, g_ref, sc_ref, sh_ref, w_ref, rkv_ref, lora_ref, hg_ref):
    x = x_ref[0]
    y = x * lax.rsqrt(jnp.mean(x * x, axis=-1, keepdims=True) + EPS) * g_ref[...]
    h = y * (1.0 + sc_ref[0]) + sh_ref[0]
    u = jnp.dot(h.astype(_BF16), w_ref[...], preferred_element_type=_F32)
    rkv_ref[0] = u[:, :RKV_W]
    lora_ref[0] = u[:, RKV_W:RKV_W + LORA_W]
    hg_ref[0] = u[:, RKV_W + LORA_W:]


def _inproj(x, norm_g, sc, sh, w_in_bf16, tm):
    bsz, l, d = x.shape
    tm = min(tm, l)
    row = lambda b, i: (b, i, 0)
    per_b = lambda b, i: (b, 0, 0)
    return pl.pallas_call(
        _inproj_kernel,
        grid=(bsz, l // tm),
        in_specs=[pl.BlockSpec((1, tm, d), row),
                  pl.BlockSpec((1, d), lambda b, i: (0, 0)),
                  pl.BlockSpec((1, 1, d), per_b),
                  pl.BlockSpec((1, 1, d), per_b),
                  pl.BlockSpec((d, P_IN), lambda b, i: (0, 0))],
        out_specs=[pl.BlockSpec((1, tm, RKV_W), row),
                   pl.BlockSpec((1, tm, LORA_W), row),
                   pl.BlockSpec((1, tm, HG_W), row)],
        out_shape=[jax.ShapeDtypeStruct((bsz, l, RKV_W), _F32),
                   jax.ShapeDtypeStruct((bsz, l, LORA_W), _F32),
                   jax.ShapeDtypeStruct((bsz, l, HG_W), _F32)],
        compiler_params=_cparams(("parallel", "parallel")),
        name="inproj",
    )(x, norm_g.reshape(1, d), sc, sh, w_in_bf16)


def _seg_sum(x, seg_ref):
    hi = x.astype(_BF16)
    lo = (x - hi.astype(_F32)).astype(_BF16)
    return (jnp.dot(hi, seg_ref[...], preferred_element_type=_F32)
            + jnp.dot(lo, seg_ref[...], preferred_element_type=_F32))


def _dot_split(x, w_ref):
    hi = x.astype(_BF16)
    lo = (x - hi.astype(_F32)).astype(_BF16)
    return (jnp.dot(hi, w_ref[0], preferred_element_type=_F32) + jnp.dot(lo, w_ref[0], preferred_element_type=_F32)
            + jnp.dot(hi, w_ref[1], preferred_element_type=_F32))


def _split_bf16(w):
    hi = w.astype(_BF16)
    return jnp.stack([hi, (w - hi.astype(_F32)).astype(_BF16)])


def _rw_prep_kernel(cur_ref, prev_ref, next_ref, lora_ref, kern_ref, w0_ref, w2_ref, a0_ref, a2_ref, g2_ref,
                    kk_ref, ka_ref, rk_ref, seg_ref,
                    r_out, kkn_out, v_out, w_out, kd_out, b_out, bonus_out, g_out, *, grid_conv):
    i = pl.program_id(1)
    tq = cur_ref.shape[0]
    hw = prev_ref.shape[0]
    w = RW_WIDTH
    pmask = jnp.where(i > 0, 1.0, 0.0)
    nmask = jnp.where(i < pl.num_programs(1) - 1, 1.0, 0.0)
    xpos = lax.broadcasted_iota(jnp.int32, (tq + 2 * hw, w), 0) % GRID_W
    conv = []
    for part in range(3):
        cs = slice(part * w, (part + 1) * w)
        xe = jnp.concatenate([prev_ref[:, cs] * pmask, cur_ref[:, cs], next_ref[:, cs] * nmask], axis=0)
        xl = pltpu.roll(xe, 1, 0)
        xr = pltpu.roll(xe, tq + 2 * hw - 1, 0)
        if grid_conv:
            xl = jnp.where(xpos == 0, 0.0, xl)
            xr = jnp.where(xpos == GRID_W - 1, 0.0, xr)
        acc = None
        for dy in range(CONV_K) if grid_conv else (CONV_K // 2,):
            off = hw + (dy - 1) * GRID_W
            for dx, src in enumerate((xl, xe, xr)):
                term = src[off:off + tq] * kern_ref[pl.ds(dy * CONV_K + dx, 1), cs]
                acc = term if acc is None else acc + term
        conv.append(acc)
    r, k, v = conv
    r_out[...] = r.astype(_BF16)
    v_out[...] = v.astype(_BF16)

    kk = k * kk_ref[...]
    kk = kk / jnp.maximum(jnp.sqrt(_seg_sum(kk * kk, seg_ref)), 1e-12)
    kkn_out[...] = kk.astype(_BF16)

    lora = lora_ref[...]
    ksum = jnp.zeros_like(k)
    for d in range(2):
        wl = lora[:, d * RW_LORA:(d + 1) * RW_LORA]
        al = lora[:, (2 + d) * RW_LORA:(3 + d) * RW_LORA]
        z = w0_ref[pl.ds(d, 1), :] + _dot_split(jnp.tanh(wl), w2_ref.at[d])
        wlog = -jax.nn.softplus(-z) - 0.5
        w_out[d] = jnp.exp(-jnp.exp(wlog))
        a = jax.nn.sigmoid(a0_ref[pl.ds(d, 1), :] + _dot_split(al, a2_ref.at[d]))
        kd = k * (1.0 + (a - 1.0) * ka_ref[...])
        kd_out[d] = kd.astype(_BF16)
        b_out[d] = (kk * a).astype(_BF16)
        ksum = ksum + kd
    bonus_out[...] = _seg_sum(r * ksum * rk_ref[...], seg_ref) * v
    gd = lora[:, 4 * RW_LORA:]
    g_out[...] = _dot_split(jax.nn.sigmoid(gd), g2_ref)


def _rw_prep(rkv, lora, kern9, w0, w2, a0, a2, g2, k_k, k_a, r_k, seg, tq, grid_conv):
    bsz, l, _ = rkv.shape
    tq = min(tq, l)
    nb = l // tq
    hw = GRID_W
    per = tq // hw
    w = RW_WIDTH
    row = lambda b, i: (b, i, 0)
    const2 = lambda b, i: (0, 0)
    const3 = lambda b, i: (0, 0, 0)
    const4 = lambda b, i: (0, 0, 0, 0)
    tok = pl.BlockSpec((None, tq, w), row)
    tok2 = pl.BlockSpec((2, None, tq, w), lambda b, i: (0, b, i, 0))
    one = jax.ShapeDtypeStruct((bsz, l, w), _F32)
    two = jax.ShapeDtypeStruct((2, bsz, l, w), _F32)
    one_h = jax.ShapeDtypeStruct((bsz, l, w), _BF16)
    two_h = jax.ShapeDtypeStruct((2, bsz, l, w), _BF16)
    vec = lambda t: t.reshape(1, w)
    return pl.pallas_call(
        functools.partial(_rw_prep_kernel, grid_conv=grid_conv),
        grid=(bsz, nb),
        in_specs=[pl.BlockSpec((None, tq, RKV_W), row),
                  pl.BlockSpec((None, hw, RKV_W), lambda b, i: (b, jnp.maximum(i * per - 1, 0), 0)),
                  pl.BlockSpec((None, hw, RKV_W), lambda b, i: (b, jnp.minimum((i + 1) * per, l // hw - 1), 0)),
                  pl.BlockSpec((None, tq, LORA_W), row),
                  pl.BlockSpec((CONV_K * CONV_K, RKV_W), const2),
                  pl.BlockSpec((2, w), const2),
                  pl.BlockSpec((2, 2, RW_LORA, w), const4),
                  pl.BlockSpec((2, w), const2),
                  pl.BlockSpec((2, 2, RW_LORA, w), const4),
                  pl.BlockSpec((2, RW_GATE_LORA, w), const3),
                  pl.BlockSpec((1, w), const2),
                  pl.BlockSpec((1, w), const2),
                  pl.BlockSpec((1, w), const2),
                  pl.BlockSpec((w, w), const2)],
        out_specs=[tok, tok, tok, tok2, tok2, tok2, tok, tok],
        out_shape=[one_h, one_h, one_h, two, two_h, two_h, one, one],
        compiler_params=_cparams(("parallel", "parallel")),
        name="rwkv_prep_grid" if grid_conv else "rwkv_prep_seq",
    )(rkv, rkv, rkv, lora, kern9, w0, w2, a0, a2, g2, vec(k_k), vec(k_a), vec(r_k), seg)


def _rwkv_scan_kernel(*refs, tt, emit, has_init):
    r_in, kk_in, v_in, w_ref, k_in, b_in = refs[:6]
    rest = list(refs[6:])
    s0_ref = rest.pop(0) if has_init else None
    out_ref = rest.pop(0)
    s_ref, sa_ref, r_ref, kk_ref, v_ref, k_ref, b_ref = rest
    for src, dst in ((r_in, r_ref), (kk_in, kk_ref), (v_in, v_ref), (k_in, k_ref), (b_in, b_ref)):
        dst[...] = src[...].astype(_F32)
    d = pl.program_id(0)
    i = pl.program_id(1)
    n = RW_HEAD_DIM

    @pl.when(i == 0)
    def _():
        s_ref[...] = s0_ref[...] if has_init else jnp.zeros_like(s_ref)

    hv = n // 2
    halves = [pl.ds(h * hv, hv) for h in range(2)]

    def project(t, vs):
        acc = [jnp.zeros((hv, LANES), _F32), jnp.zeros((hv, LANES), _F32)]
        for k in range(n):
            acc[k % 2] = acc[k % 2] + s_ref[k, vs, :] * kk_ref[t, pl.ds(k, 1), :]
        return -(acc[0] + acc[1])

    first = d * (tt - 1)
    for vs in halves:
        sa_ref[vs, :] = project(first, vs)

    def step(j, carry):
        t = j + d * (tt - 1 - 2 * j)
        tn = jnp.clip(t + 1 - 2 * d, 0, tt - 1)
        for vs in halves:
            sa = sa_ref[vs, :]
            vv = v_ref[t, vs, :]
            y = [jnp.zeros((hv, LANES), _F32), jnp.zeros((hv, LANES), _F32)]
            nsa = [jnp.zeros((hv, LANES), _F32), jnp.zeros((hv, LANES), _F32)]
            for k in range(n):
                row = pl.ds(k, 1)
                s_new = s_ref[k, vs, :] * w_ref[t, row, :] + sa * b_ref[t, row, :] + vv * k_ref[t, row, :]
                s_ref[k, vs, :] = s_new
                nsa[k % 2] = nsa[k % 2] + s_new * kk_ref[tn, row, :]
                if emit:
                    y[k % 2] = y[k % 2] + s_new * r_ref[t, row, :]
            sa_ref[vs, :] = -(nsa[0] + nsa[1])
            if emit:
                out_ref[t, vs, :] = y[0] + y[1]
        return carry

    lax.fori_loop(0, tt, step, 0)

    if not emit:
        @pl.when(i == pl.num_programs(1) - 1)
        def _():
            out_ref[...] = s_ref[...]


def _rwkv_scan(shared, per_dir, s0, tt, emit):
    l, n, c = shared[0].shape
    nb = l // tt
    sh_spec = pl.BlockSpec((tt, n, c), lambda d, i: (_dir_block(d, i, nb), 0, 0))
    pd_spec = pl.BlockSpec((None, tt, n, c), lambda d, i: (d, _dir_block(d, i, nb), 0, 0))
    st_spec = pl.BlockSpec((None, n, n, c), lambda d, i: (d, 0, 0, 0))
    in_specs = [sh_spec] * 3 + [pd_spec] * 3
    args = list(shared) + list(per_dir)
    if s0 is not None:
        in_specs.append(st_spec)
        args.append(s0)
    if emit:
        out_spec, out_shape = pd_spec, jax.ShapeDtypeStruct((2, l, n, c), _F32)
    else:
        out_spec, out_shape = st_spec, jax.ShapeDtypeStruct((2, n, n, c), _F32)
    return pl.pallas_call(
        functools.partial(_rwkv_scan_kernel, tt=tt, emit=emit, has_init=s0 is not None),
        grid=(2, nb),
        in_specs=in_specs,
        out_specs=out_spec,
        out_shape=out_shape,
        scratch_shapes=[pltpu.VMEM((n, n, c), _F32), pltpu.VMEM((n, c), _F32)] + [pltpu.VMEM((tt, n, c), _F32)] * 5,
        compiler_params=_cparams(("parallel", "arbitrary")),
        name="rwkv_scan_emit" if emit else "rwkv_scan_state",
    )(*args)


def _hgrn_kernel(*refs, n_chunks, emit, has_init, reverse):
    q_ref, f_ref, i_ref, lb_ref, mats_ref = refs[:5]
    rest = list(refs[5:])
    s0_ref = rest.pop(0) if has_init else None
    out_ref = rest.pop(0)
    st_ref, upd_ref, ent_ref = rest
    c = HG_CHUNK
    tb = n_chunks * c
    blk = pl.program_id(1)

    @pl.when(blk == 0)
    def _():
        st_ref[...] = s0_ref[...] if has_init else jnp.zeros_like(st_ref)

    lb = lb_ref[...]
    omlb = 1.0 - lb
    fr = f_ref[...]
    logf = jnp.log(lb + omlb * jax.nn.sigmoid(fr))
    kf = omlb * jax.nn.sigmoid(-fr)
    hi = logf.astype(_BF16)
    lo = (logf - hi.astype(_F32)).astype(_BF16)
    sums = (jnp.dot(mats_ref[...], hi, preferred_element_type=_F32)
            + jnp.dot(mats_ref[...], lo, preferred_element_type=_F32))
    cum, tot, mid = sums[:tb], sums[tb:2 * tb], sums[2 * tb:]
    dec = jnp.exp(tot)
    kdec = (kf * jnp.exp(tot - cum)).astype(_BF16)
    vb = i_ref[...].astype(_BF16)
    same_head = ((lax.broadcasted_iota(jnp.int32, (LANES, LANES), 0) < HG_DIM)
                 == (lax.broadcasted_iota(jnp.int32, (LANES, LANES), 1) < HG_DIM))
    pairs = [slice(p * LANES, (p + 1) * LANES) for p in range(HG_PAIRS)]
    chunks = [slice(j * c, (j + 1) * c) for j in range(n_chunks)]

    if emit:
        q = q_ref[...]
        qs = q * jax.nn.sigmoid(q) * (HG_DIM ** -0.5)
        qd = qs * jnp.exp(cum - mid)
        kd = (kf * jnp.exp(mid - cum)).astype(_BF16)
        qe = (qs * jnp.exp(cum)).astype(_BF16)
        head0 = lax.broadcasted_iota(jnp.int32, (tb, LANES), 1) < HG_DIM
        causal = mats_ref[:tb, :] > 0
        causal2 = jnp.concatenate([causal, causal], axis=0)
        for ls in pairs:
            qd_p = qd[:, ls]
            q2 = jnp.concatenate([jnp.where(head0, qd_p, 0.0), jnp.where(head0, 0.0, qd_p)], axis=0)
            sc = lax.dot_general(q2.astype(_BF16), kd[:, ls], _NT, preferred_element_type=_F32)
            sc = jnp.where(causal2, sc, 0.0)
            oi = jnp.dot(sc.astype(_BF16), vb[:, ls], preferred_element_type=_F32)
            out_ref[:, ls] = jnp.where(head0, oi[:tb], oi[tb:])

    for j, rows in enumerate(chunks):
        for p, ls in enumerate(pairs):
            upd = lax.dot_general(vb[rows, ls], kdec[rows, ls], _TN, preferred_element_type=_F32)
            upd_ref[j, p] = jnp.where(same_head, upd, 0.0)

    order = list(range(n_chunks))[::-1] if reverse else list(range(n_chunks))
    for p, ls in enumerate(pairs):
        st = st_ref[p]
        for j in order:
            if emit:
                ent_ref[j, p] = st.astype(_BF16)
            st = st * dec[j * c:j * c + 1, ls] + upd_ref[j, p]
        st_ref[p] = st

    if emit:
        for j, rows in enumerate(chunks):
            for p, ls in enumerate(pairs):
                out_ref[rows, ls] += lax.dot_general(qe[rows, ls], ent_ref[j, p], _NT,
                                                     preferred_element_type=_F32)
    else:
        @pl.when(blk == pl.num_programs(1) - 1)
        def _():
            out_ref[...] = st_ref[...]


def _hgrn_mats(tb, reverse):
    t = jnp.arange(tb)
    same = (t[:, None] // HG_CHUNK) == (t[None, :] // HG_CHUNK)
    ref_pos = (t // HG_CHUNK) * HG_CHUNK + HG_CHUNK // 2
    if reverse:
        cum = same & (t[None, :] >= t[:, None])
        mid = same & (t[None, :] >= ref_pos[:, None])
    else:
        cum = same & (t[None, :] <= t[:, None])
        mid = same & (t[None, :] <= ref_pos[:, None])
    return jnp.concatenate([cum, same, mid], axis=0).astype(_BF16)


def _hgrn_scan(hg, lb, s0, tb, emit, reverse):
    bsz, l, _ = hg.shape
    tb = min(tb, l)
    nb = l // tb
    w = HG_WIDTH
    n_chunks = tb // HG_CHUNK
    tblk = (lambda i: nb - 1 - i) if reverse else (lambda i: i)
    col = lambda j: pl.BlockSpec((None, tb, w), lambda b, i: (b, tblk(i), j))
    const = lambda b, i: (0, 0)
    st_spec = pl.BlockSpec((None, HG_PAIRS, LANES, LANES), lambda b, i: (b, 0, 0, 0))
    in_specs = [col(0), col(2 if reverse else 1), col(3), pl.BlockSpec((1, w), const),
                pl.BlockSpec((3 * tb, tb), const)]
    args = [hg, hg, hg, lb, _hgrn_mats(tb, reverse)]
    if s0 is not None:
        in_specs.append(st_spec)
        args.append(s0)
    if emit:
        out_spec = pl.BlockSpec((None, tb, w), lambda b, i: (b, tblk(i), 0))
        out_shape = jax.ShapeDtypeStruct((bsz, l, w), _F32)
    else:
        out_spec = st_spec
        out_shape = jax.ShapeDtypeStruct((bsz, HG_PAIRS, LANES, LANES), _F32)
    return pl.pallas_call(
        functools.partial(_hgrn_kernel, n_chunks=n_chunks, emit=emit, has_init=s0 is not None,
                          reverse=reverse),
        grid=(bsz, nb),
        in_specs=in_specs,
        out_specs=out_spec,
        out_shape=out_shape,
        scratch_shapes=[pltpu.VMEM((HG_PAIRS, LANES, LANES), _F32),
                        pltpu.VMEM((n_chunks, HG_PAIRS, LANES, LANES), _F32),
                        pltpu.VMEM((n_chunks, HG_PAIRS, LANES, LANES), _BF16)],
        compiler_params=_cparams(("parallel", "arbitrary")),
        name=("hgrn_emit" if emit else "hgrn_state") + ("_bwd" if reverse else "_fwd"),
    )(*args)


def _post_kernel(y_ref, bonus_ref, grw_ref, lng_ref, lnb_ref, of_ref, ob_ref, ghg_ref, hgn_ref, seg_ref,
                 x_ref, w_ref, g1_ref, n2_ref, sc_ref, sh_ref, rw_ref, x1_ref, h2_ref, lg_ref):
    inv_n = 1.0 / RW_HEAD_DIM
    ysum = y_ref[0] + y_ref[1]
    mu = _seg_sum(ysum, seg_ref) * inv_n
    dlt = ysum - mu
    var = _seg_sum(dlt * dlt, seg_ref) * inv_n
    yn = dlt * lax.rsqrt(var + RW_GN_EPS) * lng_ref[...] + lnb_ref[...] + bonus_ref[...]
    y_rw = yn * grw_ref[...]
    o = of_ref[...] + ob_ref[...]
    o = o * lax.rsqrt(_seg_sum(o * o, seg_ref) * inv_n + EPS) * hgn_ref[...]
    g = ghg_ref[...]
    y_hg = o * (g * jax.nn.sigmoid(g))
    y = (jnp.dot(y_rw.astype(_BF16), w_ref[:RW_WIDTH, :], preferred_element_type=_F32)
         + jnp.dot(y_hg.astype(_BF16), w_ref[RW_WIDTH:, :], preferred_element_type=_F32))
    x1 = x_ref[0] + g1_ref[0] * y
    x1_ref[0] = x1
    h2 = x1 * lax.rsqrt(jnp.mean(x1 * x1, axis=-1, keepdims=True) + EPS) * n2_ref[...]
    h2 = h2 * (1.0 + sc_ref[0]) + sh_ref[0]
    h2_ref[...] = _pack_quarters(h2)
    lg_ref[...] = lax.dot_general(rw_ref[...], h2, _NT, preferred_element_type=_F32, precision=_HI)


def _post(y_dirs, bonus, g_rw, lnx_g, lnx_b, o_dirs, hg, hg_norm_row, seg,
          x, w_out_bf16, g1, norm2_g, sc2, sh2, router_wt, tm):
    bsz, l, d = x.shape
    tm = min(tm, l)
    nt = l // tm
    w = RW_WIDTH
    row = lambda b, i: (b, i, 0)
    per_b = lambda b, i: (b, 0, 0)
    const = lambda b, i: (0, 0)
    tok = pl.BlockSpec((None, tm, w), row)
    tok2 = pl.BlockSpec((2, None, tm, w), lambda b, i: (0, b, i, 0))
    vec = pl.BlockSpec((1, w), const)
    return pl.pallas_call(
        _post_kernel,
        grid=(bsz, nt),
        in_specs=[tok2, tok, tok, vec, vec, tok, tok,
                  pl.BlockSpec((None, tm, w), lambda b, i: (b, i, 4)),
                  vec,
                  pl.BlockSpec((w, w), const),
                  pl.BlockSpec((1, tm, d), row),
                  pl.BlockSpec((d, d), const),
                  pl.BlockSpec((1, 1, d), per_b),
                  pl.BlockSpec((1, d), const),
                  pl.BlockSpec((1, 1, d), per_b),
                  pl.BlockSpec((1, 1, d), per_b),
                  pl.BlockSpec((N_EXPERTS, d), const)],
        out_specs=[pl.BlockSpec((1, tm, d), row),
                   pl.BlockSpec((2, tm, d // 4), lambda b, i: (0, b * nt + i, 0)),
                   pl.BlockSpec((N_EXPERTS, tm), lambda b, i: (0, b * nt + i))],
        out_shape=[jax.ShapeDtypeStruct((bsz, l, d), _F32),
                   jax.ShapeDtypeStruct((2, bsz * l, d // 4), jnp.uint32),
                   jax.ShapeDtypeStruct((N_EXPERTS, bsz * l), _F32)],
        compiler_params=_cparams(("parallel", "parallel")),
        name="outproj_norm2_router",
    )(y_dirs, bonus, g_rw, lnx_g.reshape(1, w), lnx_b.reshape(1, w), o_dirs[0], o_dirs[1], hg, hg_norm_row, seg,
      x, w_out_bf16, g1, norm2_g.reshape(1, d), sc2, sh2, router_wt)


def _sublane_all(x, op):
    for s in (4, 2, 1):
        x = op(x, pltpu.roll(x, s, 0))
    return x


def _router_kernel(lg_ref, bias_ref, tri_ref, e_ref, g_ref, rank_ref, cnt_ref):
    ng, gs = N_GROUPS, GROUP_SIZE
    tn = lg_ref.shape[1]
    neg = -jnp.inf
    gidx = lax.broadcasted_iota(jnp.int32, (ng, tn), 0)
    bias = jnp.concatenate([bias_ref[...]] * (tn // LANES), axis=1)
    scores = [jax.nn.sigmoid(lg_ref[j * ng:(j + 1) * ng, :]) for j in range(gs)]
    sel = [scores[j] + bias[j * ng:(j + 1) * ng, :] for j in range(gs)]

    m1 = functools.reduce(jnp.maximum, sel)
    cnt = functools.reduce(jnp.add, [(s == m1).astype(_F32) for s in sel])
    m2 = functools.reduce(jnp.maximum, [jnp.where(s < m1, s, neg) for s in sel])
    gscore = m1 + jnp.where(cnt >= 2.0, m1, m2)

    rank = jnp.zeros((ng, tn), jnp.int32)
    for s in range(1, ng):
        other = pltpu.roll(gscore, s, 0)
        beats = jnp.where(other > gscore, 1, jnp.where((other == gscore) & (gidx >= s), 1, 0))
        rank = rank + beats
    gsel = rank < TOPK_GROUPS

    key = [jnp.where(gsel, s, neg) for s in sel]
    eidx = [gidx * gs + j for j in range(gs)]
    avail = [jnp.ones((ng, tn), jnp.int32) for _ in range(gs)]
    top_e = jnp.zeros((TOP_K, tn), jnp.int32)
    top_s = jnp.zeros((TOP_K, tn), _F32)
    for r in range(TOP_K):
        cur = functools.reduce(jnp.maximum, [jnp.where(avail[j] > 0, key[j], neg) for j in range(gs)])
        mx = _sublane_all(cur, jnp.maximum)
        cand = [jnp.where((avail[j] > 0) & (key[j] == mx), eidx[j], N_EXPERTS) for j in range(gs)]
        mn = _sublane_all(functools.reduce(jnp.minimum, cand), jnp.minimum)
        picked = functools.reduce(jnp.add, [jnp.where(eidx[j] == mn, scores[j], 0.0) for j in range(gs)])
        avail = [jnp.where(eidx[j] == mn, 0, avail[j]) for j in range(gs)]
        top_e = jnp.where(gidx == r, mn, top_e)
        top_s = jnp.where(gidx == r, _sublane_all(picked, jnp.add), top_s)
    den = _sublane_all(top_s, jnp.add)
    e_ref[...] = top_e
    g_ref[...] = top_s / den * ROUTE_SCALE

    @pl.when(pl.program_id(0) == 0)
    def _():
        cnt_ref[...] = jnp.zeros_like(cnt_ref)

    taken = jnp.concatenate([jnp.where(a == 0, 1.0, 0.0) for a in avail], axis=0)
    before = jnp.dot(taken.astype(_BF16), tri_ref[...], preferred_element_type=_F32) - taken
    before = before + jnp.concatenate([cnt_ref[...]] * (tn // LANES), axis=1)
    rank = jnp.zeros((TOP_K, tn), _F32)
    for r in range(TOP_K):
        e_r = jnp.broadcast_to(top_e[r:r + 1, :], (ng, tn))
        hit = functools.reduce(jnp.add, [jnp.where(eidx[j] == e_r, before[j * ng:(j + 1) * ng, :], 0.0)
                                         for j in range(gs)])
        rank = jnp.where(gidx == r, _sublane_all(hit, jnp.add), rank)
    rank_ref[...] = rank.astype(jnp.int32)
    ones = jnp.ones((tn, LANES), _BF16)
    cnt_ref[...] += jnp.dot(taken.astype(_BF16), ones, preferred_element_type=_F32)


def _router(logits_t, bias_rows, tn):
    assert TOP_K == N_GROUPS == SUBLANES
    t = logits_t.shape[1]
    tn = min(tn, t)
    tri = (jnp.arange(tn)[:, None] <= jnp.arange(tn)[None, :]).astype(_BF16)
    out = pl.BlockSpec((TOP_K, tn), lambda i: (0, i))
    cnt = pl.BlockSpec((N_EXPERTS, LANES), lambda i: (0, 0))
    return pl.pallas_call(
        _router_kernel,
        grid=(t // tn,),
        in_specs=[pl.BlockSpec((N_EXPERTS, tn), lambda i: (0, i)), cnt,
                  pl.BlockSpec((tn, tn), lambda i: (0, 0))],
        out_specs=[out, out, out, cnt],
        out_shape=[jax.ShapeDtypeStruct((TOP_K, t), jnp.int32), jax.ShapeDtypeStruct((TOP_K, t), _F32),
                   jax.ShapeDtypeStruct((TOP_K, t), jnp.int32),
                   jax.ShapeDtypeStruct((N_EXPERTS, LANES), _F32)],
        compiler_params=_cparams(("arbitrary",)),
        name="router_topk",
    )(logits_t, bias_rows, tri)


def _sc_mesh():
    return plsc.VectorSubcoreMesh(core_axis_name="c", subcore_axis_name="s")


def _sc_dispatch(x_half_rows, dest_half_rows, n_out):
    n_rows = x_half_rows.shape[0]

    @pl.kernel(out_type=jax.ShapeDtypeStruct((n_out, SC_ROW), x_half_rows.dtype), mesh=_sc_mesh(),
               scratch_types=[])
    def scatter_rows(x_hbm, d_hbm, o_hbm):
        def body(x_vmem, i_vmem):
            for j in range(TOP_K):
                pltpu.sync_copy(x_vmem, o_hbm.at[i_vmem.at[j]])

        pltpu.emit_pipeline(
            body,
            grid=(n_rows // SC_WINDOW,),
            in_specs=[pl.BlockSpec((SC_WINDOW, SC_ROW), lambda i: (i, 0)),
                      pl.BlockSpec((TOP_K, SC_WINDOW), lambda i: (0, i))],
            out_specs=[],
            core_axis_name=("c", "s"),
            dimension_semantics=(pltpu.PARALLEL,),
        )(x_hbm, d_hbm)

    return scatter_rows(x_half_rows, dest_half_rows)


def _sc_gather(y_half_rows, idx):
    n = idx.shape[1]

    @pl.kernel(out_type=jax.ShapeDtypeStruct((n, SC_ROW), y_half_rows.dtype), mesh=_sc_mesh(),
               scratch_types=[])
    def gather_rows(y_hbm, i_hbm, o_hbm):
        def body(i_vmem, o_vmem):
            pltpu.sync_copy(y_hbm.at[i_vmem.at[0]], o_vmem)

        pltpu.emit_pipeline(
            body,
            grid=(n // SC_WINDOW,),
            in_specs=[pl.BlockSpec((1, SC_WINDOW), lambda i: (0, i))],
            out_specs=[pl.BlockSpec((SC_WINDOW, SC_ROW), lambda i: (i, 0))],
            core_axis_name=("c", "s"),
            dimension_semantics=(pltpu.PARALLEL,),
        )(i_hbm, o_hbm)

    return gather_rows(y_half_rows, idx)


def _dispatch_plan(top_e, rank, counts, n_blocks):
    padded = (counts + MOE_BLOCK - 1) // MOE_BLOCK * MOE_BLOCK
    pad_end = jnp.cumsum(padded)
    pad_start = pad_end - padded
    onehot = top_e[:, :, None] == jnp.arange(N_EXPERTS, dtype=jnp.int32)[None, None, :]
    dest = rank + jnp.sum(jnp.where(onehot, pad_start[None, None, :], 0), axis=-1)
    blk_start = jnp.arange(n_blocks, dtype=jnp.int32) * MOE_BLOCK
    blk_e = jnp.minimum(jnp.sum(pad_end[None, :] <= blk_start[:, None], axis=1), N_EXPERTS - 1)
    n_used = pad_end[-1] // MOE_BLOCK
    return dest, jnp.concatenate([blk_e.astype(jnp.int32), n_used[None].astype(jnp.int32)])


def _swiglu_packed(u0, u1, wg_ref, wu_ref, wd_ref):
    x = jnp.concatenate(_unpack_quarters(u0, u1), axis=1).astype(_BF16)
    a = jnp.dot(x, wg_ref[...].astype(_BF16), preferred_element_type=_F32)
    u = jnp.dot(x, wu_ref[...].astype(_BF16), preferred_element_type=_F32)
    act = (a * jax.nn.sigmoid(a) * u).astype(_BF16)
    return jnp.dot(act, wd_ref[...].astype(_BF16), preferred_element_type=_F32)


def _expert_kernel(plan_ref, x_ref, wg_ref, wu_ref, wd_ref, y_ref):
    @pl.when(pl.program_id(0) < plan_ref[pl.num_programs(0)])
    def _():
        y_ref[...] = _pack_quarters(_swiglu_packed(x_ref[0], x_ref[1], wg_ref, wu_ref, wd_ref))


def _experts(plan, x_sorted, wg, wu, wd):
    _, n_rows, q = x_sorted.shape
    n_blocks = n_rows // MOE_BLOCK
    d = 4 * q
    rows = pl.BlockSpec((2, MOE_BLOCK, q), lambda i, plan: (0, i, 0))
    grid_spec = pltpu.PrefetchScalarGridSpec(
        num_scalar_prefetch=1,
        grid=(n_blocks,),
        in_specs=[rows,
                  pl.BlockSpec((None, d, D_EXPERT), lambda i, plan: (plan[i], 0, 0)),
                  pl.BlockSpec((None, d, D_EXPERT), lambda i, plan: (plan[i], 0, 0)),
                  pl.BlockSpec((None, D_EXPERT, d), lambda i, plan: (plan[i], 0, 0))],
        out_specs=rows,
    )
    return pl.pallas_call(
        _expert_kernel,
        grid_spec=grid_spec,
        out_shape=jax.ShapeDtypeStruct((2, n_rows, q), jnp.uint32),
        compiler_params=_cparams(("arbitrary",)),
        name="moe_routed_experts",
    )(plan, x_sorted, wg, wu, wd)


def _combine_kernel(y_ref, gate_ref, h_ref, x1_ref, g2_ref, fg_ref, sg_ref, su_ref, sd_ref, o_ref):
    gate = gate_ref[...]
    acc = None
    for j in range(TOP_K):
        gj = gate[:, j:j + 1]
        parts = [gj * p for p in _unpack_quarters(y_ref[j, 0], y_ref[j, 1])]
        acc = parts if acc is None else [a + p for a, p in zip(acc, parts)]
    moe = jnp.concatenate(acc, axis=1) + _swiglu_packed(h_ref[0], h_ref[1], sg_ref, su_ref, sd_ref)
    x2 = x1_ref[0] + g2_ref[0] * moe
    o_ref[0] = x2 * lax.rsqrt(jnp.mean(x2 * x2, axis=-1, keepdims=True) + EPS) * fg_ref[...]


def _combine(y_rows, gate_tk, h2_packed, x1, g2, final_g, sg, su, sd, tm):
    bsz, l, d = x1.shape
    tm = min(tm, l)
    nt = l // tm
    q = d // 4
    const = lambda b, i: (0, 0)
    return pl.pallas_call(
        _combine_kernel,
        grid=(bsz, nt),
        in_specs=[pl.BlockSpec((TOP_K, 2, tm, q), lambda b, i: (0, 0, b * nt + i, 0)),
                  pl.BlockSpec((tm, TOP_K), lambda b, i: (b * nt + i, 0)),
                  pl.BlockSpec((2, tm, q), lambda b, i: (0, b * nt + i, 0)),
                  pl.BlockSpec((1, tm, d), lambda b, i: (b, i, 0)),
                  pl.BlockSpec((1, 1, d), lambda b, i: (b, 0, 0)),
                  pl.BlockSpec((1, d), const),
                  pl.BlockSpec((d, D_EXPERT), const),
                  pl.BlockSpec((d, D_EXPERT), const),
                  pl.BlockSpec((D_EXPERT, d), const)],
        out_specs=pl.BlockSpec((1, tm, d), lambda b, i: (b, i, 0)),
        out_shape=jax.ShapeDtypeStruct((bsz, l, d), _F32),
        compiler_params=_cparams(("parallel", "parallel")),
        name="moe_combine_shared_final_norm",
    )(y_rows, gate_tk, h2_packed, x1, g2, final_g.reshape(1, d), sg, su, sd)


def _to_chain_lanes(t):
    lead, (b, l, _) = t.shape[:-3], t.shape[-3:]
    nl = len(lead)
    t = t.reshape(*lead, b, l, RW_HEADS, RW_HEAD_DIM)
    t = t.transpose(*range(nl), nl + 1, nl + 3, nl, nl + 2)
    return t.reshape(*lead, l, RW_HEAD_DIM, b * RW_HEADS)


def _from_chain_lanes(t, b):
    l = t.shape[1]
    return t.reshape(2, l, RW_HEAD_DIM, b, RW_HEADS).transpose(0, 3, 1, 4, 2).reshape(2, b, l, RW_WIDTH)


def _scan_operands(prep):
    r, kk, v, w, kd, bb = prep[:6]
    return [_to_chain_lanes(t) for t in (r, kk, v)], [_to_chain_lanes(t) for t in (w, kd, bb)]


def kernel(x, c, ctx, c_ctx, w_mod, b_mod, norm1_g, norm2_g, w_in, rw_conv, rw_w0, rw_w2, rw_a0, rw_a2,
           rw_g2, rw_k_k, rw_k_a, rw_r_k, rw_lnx_g, rw_lnx_b, hg_lb_logits, hg_norm_g, w_out, router_w,
           router_b, exp_w_gate, exp_w_up, exp_w_down, sh_w_gate, sh_w_up, sh_w_down, final_norm_g):
    bsz, seq, d = x.shape
    assert w_mod.shape[0] == 1 and bsz * RW_HEADS == LANES
    lyr = 0

    c_all = jnp.concatenate([c, c_ctx[None, :], jnp.zeros((SUBLANES - 1, d), _F32)], axis=0)
    mod = _modulation(c_all, w_mod[lyr], b_mod[lyr])
    sh1, sc1, g1, sh2, sc2, g2 = [m[:, None, :] for m in jnp.split(mod[:bsz], N_MOD, axis=-1)]
    mod_ctx = jnp.broadcast_to(mod[bsz][None, None, :], (bsz, 1, N_MOD * d))
    csh1, csc1 = mod_ctx[..., :d], mod_ctx[..., d:2 * d]

    w_in_bf16 = w_in[lyr].astype(_BF16)
    rkv_lat, lora_lat, hg_lat = _inproj(x, norm1_g[lyr], sc1, sh1, w_in_bf16, tm=256)
    rkv_ctx, lora_ctx, hg_ctx = _inproj(ctx, norm1_g[lyr], csc1, csh1, w_in_bf16, tm=256)

    kern9 = rw_conv[lyr].reshape(CONV_K * CONV_K, RKV_W)
    lane_head = jnp.arange(RW_WIDTH) // RW_HEAD_DIM
    seg = (lane_head[:, None] == lane_head[None, :]).astype(_BF16)
    per_dir_split = lambda t: jnp.swapaxes(_split_bf16(t), 0, 1)
    rw_p = (kern9, rw_w0[lyr], per_dir_split(rw_w2[lyr]), rw_a0[lyr], per_dir_split(rw_a2[lyr]),
            _split_bf16(rw_g2[lyr]), rw_k_k[lyr], rw_k_a[lyr], rw_r_k[lyr], seg)
    prep_lat = _rw_prep(rkv_lat, lora_lat, *rw_p, tq=256, grid_conv=True)
    prep_ctx = _rw_prep(rkv_ctx, lora_ctx, *rw_p, tq=256, grid_conv=False)
    s_rw = _rwkv_scan(*_scan_operands(prep_ctx), None, tt=32, emit=False)
    y_dirs = _from_chain_lanes(_rwkv_scan(*_scan_operands(prep_lat), s_rw, tt=32, emit=True), bsz)

    lb = jnp.cumsum(jax.nn.softmax(hg_lb_logits.astype(_F32), axis=1), axis=1)[:, lyr]
    o_dirs = []
    for dd, rev in enumerate((False, True)):
        s_hg = _hgrn_scan(hg_ctx, lb[dd:dd + 1], None, tb=256, emit=False, reverse=rev)
        o_dirs.append(_hgrn_scan(hg_lat, lb[dd:dd + 1], s_hg, tb=256, emit=True, reverse=rev))

    perm = lambda t: t.reshape(N_GROUPS, GROUP_SIZE, -1).transpose(1, 0, 2).reshape(N_EXPERTS, -1)
    router_wt = perm(router_w[lyr].T)
    bias_rows = jnp.broadcast_to(perm(router_b[lyr][:, None]), (N_EXPERTS, LANES))
    hg_norm_row = jnp.tile(hg_norm_g[lyr], HG_HEADS).reshape(1, HG_WIDTH)
    x1, h2, logits_t = _post(y_dirs, prep_lat[6], prep_lat[7], rw_lnx_g[lyr], rw_lnx_b[lyr], o_dirs, hg_lat,
                             hg_norm_row, seg, x, w_out[lyr].astype(_BF16), g1, norm2_g[lyr], sc2, sh2,
                             router_wt, tm=512)
    top_e, gate, rank, counts_rows = _router(logits_t, bias_rows, tn=512)
    counts = counts_rows[:, 0].astype(jnp.int32).reshape(GROUP_SIZE, N_GROUPS).T.reshape(N_EXPERTS)

    n_tok = bsz * seq
    n_blocks = n_tok * TOP_K // MOE_BLOCK + N_EXPERTS
    n_slots = n_blocks * MOE_BLOCK
    dest, plan = _dispatch_plan(top_e, rank, counts, n_blocks)
    dest_half = (dest[:, None, :] + jnp.array([0, n_slots], jnp.int32)[None, :, None]).reshape(TOP_K, 2 * n_tok)
    x_sorted = _sc_dispatch(h2.reshape(2 * n_tok, SC_ROW), dest_half, 2 * n_slots)
    y_sorted = _experts(plan, x_sorted.reshape(2, n_slots, SC_ROW),
                        exp_w_gate[lyr], exp_w_up[lyr], exp_w_down[lyr])
    y_rows = _sc_gather(y_sorted.reshape(2 * n_slots, SC_ROW), dest_half.reshape(1, TOP_K * 2 * n_tok))
    return _combine(y_rows.reshape(TOP_K, 2, n_tok, SC_ROW), gate.T, h2, x1, g2,
                    final_norm_g, sh_w_gate[lyr], sh_w_up[lyr], sh_w_down[lyr], tm=512)
```

```python
import functools

import jax
import jax.numpy as jnp
from jax import lax
from jax.experimental import pallas as pl
from jax.experimental.pallas import tpu as pltpu
from jax.experimental.pallas import tpu_sc as plsc

D_MODEL = 1024
GRID_W = 64
N_MOD = 6
EPS = 1e-6

RW_HEAD_DIM = 64
RW_HEADS = 8
RW_WIDTH = 512
RW_LORA = 64
RW_GATE_LORA = 128
RW_GN_EPS = 1e-5 * RW_HEAD_DIM
CONV_K = 3

HG_HEADS = 8
HG_DIM = 64
HG_WIDTH = 512
HG_CHUNK = 32
HG_PAIRS = HG_HEADS // 2

RKV_W = 3 * RW_WIDTH
LORA_W = 4 * RW_LORA + RW_GATE_LORA
HG_W = 5 * HG_WIDTH
P_IN = RKV_W + LORA_W + HG_W

N_EXPERTS = 64
N_GROUPS = 8
GROUP_SIZE = N_EXPERTS // N_GROUPS
TOPK_GROUPS = 4
TOP_K = 8
D_EXPERT = 256
ROUTE_SCALE = 2.5
MOE_BLOCK = 1024
SC_ROW = 256
SC_WINDOW = 128

LANES = 128
SUBLANES = 8
VMEM_LIMIT = 48 * 1024 * 1024

_HI = lax.Precision.HIGHEST
_F32 = jnp.float32
_BF16 = jnp.bfloat16
_NT = (((1,), (1,)), ((), ()))
_TN = (((0,), (0,)), ((), ()))


def _cparams(sem):
    return pltpu.CompilerParams(dimension_semantics=sem, vmem_limit_bytes=VMEM_LIMIT)


def _pack_bf16_pair(lo, hi):
    return pltpu.pack_elementwise([lo, hi], packed_dtype=_BF16)


def _unpack_bf16_pair(u):
    return tuple(pltpu.unpack_elementwise(u, index=i, packed_dtype=_BF16, unpacked_dtype=_F32) for i in range(2))


def _pack_quarters(x):
    q = x.shape[1] // 4
    return jnp.stack([_pack_bf16_pair(x[:, 2 * h * q:(2 * h + 1) * q], x[:, (2 * h + 1) * q:(2 * h + 2) * q])
                      for h in range(2)])


def _unpack_quarters(u0, u1):
    return _unpack_bf16_pair(u0) + _unpack_bf16_pair(u1)


def _dir_block(d, i, nb):
    return i + d * (nb - 1 - 2 * i)


def _mod_kernel(c_ref, w_ref, b_ref, o_ref):
    c = c_ref[...]
    s = c * jax.nn.sigmoid(c)
    o_ref[...] = jnp.dot(s, w_ref[...], preferred_element_type=_F32, precision=_HI) + b_ref[...]


def _modulation(c_all, w_mod, b_mod):
    rows, d = c_all.shape
    n = w_mod.shape[1]
    tn = 1024
    return pl.pallas_call(
        _mod_kernel,
        grid=(n // tn,),
        in_specs=[pl.BlockSpec((rows, d), lambda j: (0, 0)),
                  pl.BlockSpec((d, tn), lambda j: (0, j)),
                  pl.BlockSpec((1, tn), lambda j: (0, j))],
        out_specs=pl.BlockSpec((rows, tn), lambda j: (0, j)),
        out_shape=jax.ShapeDtypeStruct((rows, n), _F32),
        compiler_params=_cparams(("parallel",)),
        name="modulation",
    )(c_all, w_mod, b_mod.reshape(1, n))


def _inproj_kernel(x_ref, g_ref, sc_ref, sh_ref, w_ref, rkv_ref, lora_ref, hg_ref):
    x = x_ref[0]
    y = x * lax.rsqrt(jnp.mean(x * x, axis=-1, keepdims=True) + EPS) * g_ref[...]
    h = y * (1.0 + sc_ref[0]) + sh_ref[0]
    u = jnp.dot(h.astype(_BF16), w_ref[...], preferred_element_type=_F32)
    rkv_ref[0] = u[:, :RKV_W]
    lora_ref[0] = u[:, RKV_W:RKV_W + LORA_W]
    hg_ref[0] = u[:, RKV_W + LORA_W:]


def _inproj(x, norm_g, sc, sh, w_in_bf16, tm):
    bsz, l, d = x.shape
    tm = min(tm, l)
    row = lambda b, i: (b, i, 0)
    per_b = lambda b, i: (b, 0, 0)
    return pl.pallas_call(
        _inproj_kernel,
        grid=(bsz, l // tm),
        in_specs=[pl.BlockSpec((1, tm, d), row),
                  pl.BlockSpec((1, d), lambda b, i: (0, 0)),
                  pl.BlockSpec((1, 1, d), per_b),
                  pl.BlockSpec((1, 1, d), per_b),
                  pl.BlockSpec((d, P_IN), lambda b, i: (0, 0))],
        out_specs=[pl.BlockSpec((1, tm, RKV_W), row),
                   pl.BlockSpec((1, tm, LORA_W), row),
                   pl.BlockSpec((1, tm, HG_W), row)],
        out_shape=[jax.ShapeDtypeStruct((bsz, l, RKV_W), _F32),
                   jax.ShapeDtypeStruct((bsz, l, LORA_W), _F32),
                   jax.ShapeDtypeStruct((bsz, l, HG_W), _F32)],
        compiler_params=_cparams(("parallel", "parallel")),
        name="inproj",
    )(x, norm_g.reshape(1, d), sc, sh, w_in_bf16)


def _seg_sum(x, seg_ref, nonneg=False):
    hi = x.astype(_BF16)
    out = jnp.dot(hi, seg_ref[...], preferred_element_type=_F32)
    if nonneg:
        return out
    lo = (x - hi.astype(_F32)).astype(_BF16)
    return out + jnp.dot(lo, seg_ref[...], preferred_element_type=_F32)


def _dot_split(x, w_ref):
    hi = x.astype(_BF16)
    lo = (x - hi.astype(_F32)).astype(_BF16)
    return (jnp.dot(hi, w_ref[0], preferred_element_type=_F32) + jnp.dot(lo, w_ref[0], preferred_element_type=_F32)
            + jnp.dot(hi, w_ref[1], preferred_element_type=_F32))


def _split_bf16(w):
    hi = w.astype(_BF16)
    return jnp.stack([hi, (w - hi.astype(_F32)).astype(_BF16)])


def _rw_prep_kernel(cur_ref, prev_ref, next_ref, lora_ref, kern_ref, w0_ref, w2_ref, a0_ref, a2_ref, g2_ref,
                    kk_ref, ka_ref, rk_ref, seg_ref,
                    r_out, kkn_out, v_out, w_out, kd_out, b_out, bonus_out, g_out, *, grid_conv):
    i = pl.program_id(1)
    tq = cur_ref.shape[0]
    hw = prev_ref.shape[0]
    w = RW_WIDTH
    pmask = jnp.where(i > 0, 1.0, 0.0)
    nmask = jnp.where(i < pl.num_programs(1) - 1, 1.0, 0.0)
    xpos = lax.broadcasted_iota(jnp.int32, (tq + 2 * hw, w), 0) % GRID_W
    conv = []
    for part in range(3):
        cs = slice(part * w, (part + 1) * w)
        xe = jnp.concatenate([prev_ref[:, cs] * pmask, cur_ref[:, cs], next_ref[:, cs] * nmask], axis=0)
        xl = pltpu.roll(xe, 1, 0)
        xr = pltpu.roll(xe, tq + 2 * hw - 1, 0)
        if grid_conv:
            xl = jnp.where(xpos == 0, 0.0, xl)
            xr = jnp.where(xpos == GRID_W - 1, 0.0, xr)
        acc = None
        for dy in range(CONV_K) if grid_conv else (CONV_K // 2,):
            off = hw + (dy - 1) * GRID_W
            for dx, src in enumerate((xl, xe, xr)):
                term = src[off:off + tq] * kern_ref[pl.ds(dy * CONV_K + dx, 1), cs]
                acc = term if acc is None else acc + term
        conv.append(acc)
    r, k, v = conv
    r_out[...] = r.astype(_BF16)
    v_out[...] = v.astype(_BF16)

    kk = k * kk_ref[...]
    kk = kk / jnp.maximum(jnp.sqrt(_seg_sum(kk * kk, seg_ref, nonneg=True)), 1e-12)
    kkn_out[...] = kk.astype(_BF16)

    lora = lora_ref[...]
    ksum = jnp.zeros_like(k)
    for d in range(2):
        wl = lora[:, d * RW_LORA:(d + 1) * RW_LORA]
        al = lora[:, (2 + d) * RW_LORA:(3 + d) * RW_LORA]
        z = w0_ref[pl.ds(d, 1), :] + _dot_split(jnp.tanh(wl), w2_ref.at[d])
        wlog = -jax.nn.softplus(-z) - 0.5
        w_out[d] = jnp.exp(-jnp.exp(wlog))
        a = jax.nn.sigmoid(a0_ref[pl.ds(d, 1), :] + _dot_split(al, a2_ref.at[d]))
        kd = k * (1.0 + (a - 1.0) * ka_ref[...])
        kd_out[d] = kd.astype(_BF16)
        b_out[d] = (kk * a).astype(_BF16)
        ksum = ksum + kd
    bonus_out[...] = _seg_sum(r * ksum * rk_ref[...], seg_ref) * v
    gd = lora[:, 4 * RW_LORA:]
    g_out[...] = _dot_split(jax.nn.sigmoid(gd), g2_ref)


def _rw_prep(rkv, lora, kern9, w0, w2, a0, a2, g2, k_k, k_a, r_k, seg, tq, grid_conv):
    bsz, l, _ = rkv.shape
    tq = min(tq, l)
    nb = l // tq
    hw = GRID_W
    per = tq // hw
    w = RW_WIDTH
    row = lambda b, i: (b, i, 0)
    const2 = lambda b, i: (0, 0)
    const3 = lambda b, i: (0, 0, 0)
    const4 = lambda b, i: (0, 0, 0, 0)
    tok = pl.BlockSpec((None, tq, w), row)
    tok2 = pl.BlockSpec((2, None, tq, w), lambda b, i: (0, b, i, 0))
    one = jax.ShapeDtypeStruct((bsz, l, w), _F32)
    two = jax.ShapeDtypeStruct((2, bsz, l, w), _F32)
    one_h = jax.ShapeDtypeStruct((bsz, l, w), _BF16)
    two_h = jax.ShapeDtypeStruct((2, bsz, l, w), _BF16)
    vec = lambda t: t.reshape(1, w)
    return pl.pallas_call(
        functools.partial(_rw_prep_kernel, grid_conv=grid_conv),
        grid=(bsz, nb),
        in_specs=[pl.BlockSpec((None, tq, RKV_W), row),
                  pl.BlockSpec((None, hw, RKV_W), lambda b, i: (b, jnp.maximum(i * per - 1, 0), 0)),
                  pl.BlockSpec((None, hw, RKV_W), lambda b, i: (b, jnp.minimum((i + 1) * per, l // hw - 1), 0)),
                  pl.BlockSpec((None, tq, LORA_W), row),
                  pl.BlockSpec((CONV_K * CONV_K, RKV_W), const2),
                  pl.BlockSpec((2, w), const2),
                  pl.BlockSpec((2, 2, RW_LORA, w), const4),
                  pl.BlockSpec((2, w), const2),
                  pl.BlockSpec((2, 2, RW_LORA, w), const4),
                  pl.BlockSpec((2, RW_GATE_LORA, w), const3),
                  pl.BlockSpec((1, w), const2),
                  pl.BlockSpec((1, w), const2),
                  pl.BlockSpec((1, w), const2),
                  pl.BlockSpec((w, w), const2)],
        out_specs=[tok, tok, tok, tok2, tok2, tok2, tok, tok],
        out_shape=[one_h, one_h, one_h, two, two_h, two_h, one, one],
        compiler_params=_cparams(("parallel", "parallel")),
        name="rwkv_prep_grid" if grid_conv else "rwkv_prep_seq",
    )(rkv, rkv, rkv, lora, kern9, w0, w2, a0, a2, g2, vec(k_k), vec(k_a), vec(r_k), seg)


def _rwkv_scan_kernel(*refs, tt, emit, has_init):
    r_in, kk_in, v_in, w_ref, k_in, b_in = refs[:6]
    rest = list(refs[6:])
    s0_ref = rest.pop(0) if has_init else None
    out_ref = rest.pop(0)
    s_ref, sa_ref, r_ref, kk_ref, v_ref, k_ref, b_ref = rest
    for src, dst in ((r_in, r_ref), (kk_in, kk_ref), (v_in, v_ref), (k_in, k_ref), (b_in, b_ref)):
        dst[...] = src[...].astype(_F32)
    d = pl.program_id(0)
    i = pl.program_id(1)
    n = RW_HEAD_DIM

    @pl.when(i == 0)
    def _():
        s_ref[...] = s0_ref[...] if has_init else jnp.zeros_like(s_ref)

    hv = n // 2
    halves = [pl.ds(h * hv, hv) for h in range(2)]

    def project(t, vs):
        acc = [jnp.zeros((hv, LANES), _F32), jnp.zeros((hv, LANES), _F32)]
        for k in range(n):
            acc[k % 2] = acc[k % 2] + s_ref[k, vs, :] * kk_ref[t, pl.ds(k, 1), :]
        return -(acc[0] + acc[1])

    first = d * (tt - 1)
    for vs in halves:
        sa_ref[vs, :] = project(first, vs)

    def step(j, carry):
        t = j + d * (tt - 1 - 2 * j)
        tn = jnp.clip(t + 1 - 2 * d, 0, tt - 1)
        for vs in halves:
            sa = sa_ref[vs, :]
            vv = v_ref[t, vs, :]
            y = [jnp.zeros((hv, LANES), _F32), jnp.zeros((hv, LANES), _F32)]
            nsa = [jnp.zeros((hv, LANES), _F32), jnp.zeros((hv, LANES), _F32)]
            for k in range(n):
                row = pl.ds(k, 1)
                s_new = s_ref[k, vs, :] * w_ref[t, row, :] + sa * b_ref[t, row, :] + vv * k_ref[t, row, :]
                s_ref[k, vs, :] = s_new
                nsa[k % 2] = nsa[k % 2] + s_new * kk_ref[tn, row, :]
                if emit:
                    y[k % 2] = y[k % 2] + s_new * r_ref[t, row, :]
            sa_ref[vs, :] = -(nsa[0] + nsa[1])
            if emit:
                out_ref[t, vs, :] = y[0] + y[1]
        return carry

    lax.fori_loop(0, tt, step, 0)

    if not emit:
        @pl.when(i == pl.num_programs(1) - 1)
        def _():
            out_ref[...] = s_ref[...]


def _rwkv_scan(shared, per_dir, s0, tt, emit):
    l, n, c = shared[0].shape
    nb = l // tt
    sh_spec = pl.BlockSpec((tt, n, c), lambda d, i: (_dir_block(d, i, nb), 0, 0))
    pd_spec = pl.BlockSpec((None, tt, n, c), lambda d, i: (d, _dir_block(d, i, nb), 0, 0))
    st_spec = pl.BlockSpec((None, n, n, c), lambda d, i: (d, 0, 0, 0))
    in_specs = [sh_spec] * 3 + [pd_spec] * 3
    args = list(shared) + list(per_dir)
    if s0 is not None:
        in_specs.append(st_spec)
        args.append(s0)
    if emit:
        out_spec, out_shape = pd_spec, jax.ShapeDtypeStruct((2, l, n, c), _F32)
    else:
        out_spec, out_shape = st_spec, jax.ShapeDtypeStruct((2, n, n, c), _F32)
    return pl.pallas_call(
        functools.partial(_rwkv_scan_kernel, tt=tt, emit=emit, has_init=s0 is not None),
        grid=(2, nb),
        in_specs=in_specs,
        out_specs=out_spec,
        out_shape=out_shape,
        scratch_shapes=[pltpu.VMEM((n, n, c), _F32), pltpu.VMEM((n, c), _F32)] + [pltpu.VMEM((tt, n, c), _F32)] * 5,
        compiler_params=_cparams(("parallel", "arbitrary")),
        name="rwkv_scan_emit" if emit else "rwkv_scan_state",
    )(*args)


def _hgrn_kernel(*refs, n_chunks, emit, has_init, reverse):
    q_ref, f_ref, i_ref, lb_ref, mats_ref = refs[:5]
    rest = list(refs[5:])
    s0_ref = rest.pop(0) if has_init else None
    out_ref = rest.pop(0)
    st_ref, upd_ref, ent_ref = rest
    c = HG_CHUNK
    tb = n_chunks * c
    blk = pl.program_id(1)

    @pl.when(blk == 0)
    def _():
        st_ref[...] = s0_ref[...] if has_init else jnp.zeros_like(st_ref)

    lb = lb_ref[...]
    omlb = 1.0 - lb
    fr = f_ref[...]
    logf = jnp.log(lb + omlb * jax.nn.sigmoid(fr))
    kf = omlb * jax.nn.sigmoid(-fr)
    hi = logf.astype(_BF16)
    lo = (logf - hi.astype(_F32)).astype(_BF16)
    sums = (jnp.dot(mats_ref[...], hi, preferred_element_type=_F32)
            + jnp.dot(mats_ref[...], lo, preferred_element_type=_F32))
    cum, tot, mid = sums[:tb], sums[tb:2 * tb], sums[2 * tb:]
    dec = jnp.exp(tot)
    kdec = (kf * jnp.exp(tot - cum)).astype(_BF16)
    vb = i_ref[...].astype(_BF16)
    same_head = ((lax.broadcasted_iota(jnp.int32, (LANES, LANES), 0) < HG_DIM)
                 == (lax.broadcasted_iota(jnp.int32, (LANES, LANES), 1) < HG_DIM))
    pairs = [slice(p * LANES, (p + 1) * LANES) for p in range(HG_PAIRS)]
    chunks = [slice(j * c, (j + 1) * c) for j in range(n_chunks)]

    if emit:
        q = q_ref[...]
        qs = q * jax.nn.sigmoid(q) * (HG_DIM ** -0.5)
        qd = qs * jnp.exp(cum - mid)
        kd = (kf * jnp.exp(mid - cum)).astype(_BF16)
        qe = (qs * jnp.exp(cum)).astype(_BF16)
        head0 = lax.broadcasted_iota(jnp.int32, (tb, LANES), 1) < HG_DIM
        causal = mats_ref[:tb, :] > 0
        causal2 = jnp.concatenate([causal, causal], axis=0)
        for ls in pairs:
            qd_p = qd[:, ls]
            q2 = jnp.concatenate([jnp.where(head0, qd_p, 0.0), jnp.where(head0, 0.0, qd_p)], axis=0)
            sc = lax.dot_general(q2.astype(_BF16), kd[:, ls], _NT, preferred_element_type=_F32)
            sc = jnp.where(causal2, sc, 0.0)
            oi = jnp.dot(sc.astype(_BF16), vb[:, ls], preferred_element_type=_F32)
            out_ref[:, ls] = jnp.where(head0, oi[:tb], oi[tb:])

    for j, rows in enumerate(chunks):
        for p, ls in enumerate(pairs):
            upd = lax.dot_general(vb[rows, ls], kdec[rows, ls], _TN, preferred_element_type=_F32)
            upd_ref[j, p] = jnp.where(same_head, upd, 0.0)

    order = list(range(n_chunks))[::-1] if reverse else list(range(n_chunks))
    for p, ls in enumerate(pairs):
        st = st_ref[p]
        for j in order:
            if emit:
                ent_ref[j, p] = st.astype(_BF16)
            st = st * dec[j * c:j * c + 1, ls] + upd_ref[j, p]
        st_ref[p] = st

    if emit:
        for j, rows in enumerate(chunks):
            for p, ls in enumerate(pairs):
                out_ref[rows, ls] += lax.dot_general(qe[rows, ls], ent_ref[j, p], _NT,
                                                     preferred_element_type=_F32)
    else:
        @pl.when(blk == pl.num_programs(1) - 1)
        def _():
            out_ref[...] = st_ref[...]


def _hgrn_mats(tb, reverse):
    t = jnp.arange(tb)
    same = (t[:, None] // HG_CHUNK) == (t[None, :] // HG_CHUNK)
    ref_pos = (t // HG_CHUNK) * HG_CHUNK + HG_CHUNK // 2
    if reverse:
        cum = same & (t[None, :] >= t[:, None])
        mid = same & (t[None, :] >= ref_pos[:, None])
    else:
        cum = same & (t[None, :] <= t[:, None])
        mid = same & (t[None, :] <= ref_pos[:, None])
    return jnp.concatenate([cum, same, mid], axis=0).astype(_BF16)


def _hgrn_scan(hg, lb, s0, tb, emit, reverse):
    bsz, l, _ = hg.shape
    tb = min(tb, l)
    nb = l // tb
    w = HG_WIDTH
    n_chunks = tb // HG_CHUNK
    tblk = (lambda i: nb - 1 - i) if reverse else (lambda i: i)
    col = lambda j: pl.BlockSpec((None, tb, w), lambda b, i: (b, tblk(i), j))
    const = lambda b, i: (0, 0)
    st_spec = pl.BlockSpec((None, HG_PAIRS, LANES, LANES), lambda b, i: (b, 0, 0, 0))
    in_specs = [col(0), col(2 if reverse else 1), col(3), pl.BlockSpec((1, w), const),
                pl.BlockSpec((3 * tb, tb), const)]
    args = [hg, hg, hg, lb, _hgrn_mats(tb, reverse)]
    if s0 is not None:
        in_specs.append(st_spec)
        args.append(s0)
    if emit:
        out_spec = pl.BlockSpec((None, tb, w), lambda b, i: (b, tblk(i), 0))
        out_shape = jax.ShapeDtypeStruct((bsz, l, w), _F32)
    else:
        out_spec = st_spec
        out_shape = jax.ShapeDtypeStruct((bsz, HG_PAIRS, LANES, LANES), _F32)
    return pl.pallas_call(
        functools.partial(_hgrn_kernel, n_chunks=n_chunks, emit=emit, has_init=s0 is not None,
                          reverse=reverse),
        grid=(bsz, nb),
        in_specs=in_specs,
        out_specs=out_spec,
        out_shape=out_shape,
        scratch_shapes=[pltpu.VMEM((HG_PAIRS, LANES, LANES), _F32),
                        pltpu.VMEM((n_chunks, HG_PAIRS, LANES, LANES), _F32),
                        pltpu.VMEM((n_chunks, HG_PAIRS, LANES, LANES), _BF16)],
        compiler_params=_cparams(("parallel", "arbitrary")),
        name=("hgrn_emit" if emit else "hgrn_state") + ("_bwd" if reverse else "_fwd"),
    )(*args)


def _post_kernel(y_ref, bonus_ref, grw_ref, lng_ref, lnb_ref, of_ref, ob_ref, ghg_ref, hgn_ref, seg_ref,
                 x_ref, w_ref, g1_ref, n2_ref, sc_ref, sh_ref, rw_ref, x1_ref, h2_ref, lg_ref):
    inv_n = 1.0 / RW_HEAD_DIM
    ysum = y_ref[0] + y_ref[1]
    mu = _seg_sum(ysum, seg_ref) * inv_n
    dlt = ysum - mu
    var = _seg_sum(dlt * dlt, seg_ref, nonneg=True) * inv_n
    yn = dlt * lax.rsqrt(var + RW_GN_EPS) * lng_ref[...] + lnb_ref[...] + bonus_ref[...]
    y_rw = yn * grw_ref[...]
    o = of_ref[...] + ob_ref[...]
    o = o * lax.rsqrt(_seg_sum(o * o, seg_ref, nonneg=True) * inv_n + EPS) * hgn_ref[...]
    g = ghg_ref[...]
    y_hg = o * (g * jax.nn.sigmoid(g))
    y = (jnp.dot(y_rw.astype(_BF16), w_ref[:RW_WIDTH, :], preferred_element_type=_F32)
         + jnp.dot(y_hg.astype(_BF16), w_ref[RW_WIDTH:, :], preferred_element_type=_F32))
    x1 = x_ref[0] + g1_ref[0] * y
    x1_ref[0] = x1
    h2 = x1 * lax.rsqrt(jnp.mean(x1 * x1, axis=-1, keepdims=True) + EPS) * n2_ref[...]
    h2 = h2 * (1.0 + sc_ref[0]) + sh_ref[0]
    h2_ref[...] = _pack_quarters(h2)
    lg_ref[...] = lax.dot_general(rw_ref[...], h2, _NT, preferred_element_type=_F32, precision=_HI)


def _post(y_dirs, bonus, g_rw, lnx_g, lnx_b, o_dirs, hg, hg_norm_row, seg,
          x, w_out_bf16, g1, norm2_g, sc2, sh2, router_wt, tm):
    bsz, l, d = x.shape
    tm = min(tm, l)
    nt = l // tm
    w = RW_WIDTH
    row = lambda b, i: (b, i, 0)
    per_b = lambda b, i: (b, 0, 0)
    const = lambda b, i: (0, 0)
    tok = pl.BlockSpec((None, tm, w), row)
    tok2 = pl.BlockSpec((2, None, tm, w), lambda b, i: (0, b, i, 0))
    vec = pl.BlockSpec((1, w), const)
    return pl.pallas_call(
        _post_kernel,
        grid=(bsz, nt),
        in_specs=[tok2, tok, tok, vec, vec, tok, tok,
                  pl.BlockSpec((None, tm, w), lambda b, i: (b, i, 4)),
                  vec,
                  pl.BlockSpec((w, w), const),
                  pl.BlockSpec((1, tm, d), row),
                  pl.BlockSpec((d, d), const),
                  pl.BlockSpec((1, 1, d), per_b),
                  pl.BlockSpec((1, d), const),
                  pl.BlockSpec((1, 1, d), per_b),
                  pl.BlockSpec((1, 1, d), per_b),
                  pl.BlockSpec((N_EXPERTS, d), const)],
        out_specs=[pl.BlockSpec((1, tm, d), row),
                   pl.BlockSpec((2, tm, d // 4), lambda b, i: (0, b * nt + i, 0)),
                   pl.BlockSpec((N_EXPERTS, tm), lambda b, i: (0, b * nt + i))],
        out_shape=[jax.ShapeDtypeStruct((bsz, l, d), _F32),
                   jax.ShapeDtypeStruct((2, bsz * l, d // 4), jnp.uint32),
                   jax.ShapeDtypeStruct((N_EXPERTS, bsz * l), _F32)],
        compiler_params=_cparams(("parallel", "parallel")),
        name="outproj_norm2_router",
    )(y_dirs, bonus, g_rw, lnx_g.reshape(1, w), lnx_b.reshape(1, w), o_dirs[0], o_dirs[1], hg, hg_norm_row, seg,
      x, w_out_bf16, g1, norm2_g.reshape(1, d), sc2, sh2, router_wt)


def _sublane_all(x, op):
    for s in (4, 2, 1):
        x = op(x, pltpu.roll(x, s, 0))
    return x


def _router_kernel(lg_ref, bias_ref, tri_ref, e_ref, g_ref, rank_ref, cnt_ref):
    ng, gs = N_GROUPS, GROUP_SIZE
    tn = lg_ref.shape[1]
    neg = -jnp.inf
    gidx = lax.broadcasted_iota(jnp.int32, (ng, tn), 0)
    bias = jnp.concatenate([bias_ref[...]] * (tn // LANES), axis=1)
    scores = [jax.nn.sigmoid(lg_ref[j * ng:(j + 1) * ng, :]) for j in range(gs)]
    sel = [scores[j] + bias[j * ng:(j + 1) * ng, :] for j in range(gs)]

    m1 = functools.reduce(jnp.maximum, sel)
    cnt = functools.reduce(jnp.add, [(s == m1).astype(_F32) for s in sel])
    m2 = functools.reduce(jnp.maximum, [jnp.where(s < m1, s, neg) for s in sel])
    gscore = m1 + jnp.where(cnt >= 2.0, m1, m2)

    rank = jnp.zeros((ng, tn), jnp.int32)
    for s in range(1, ng):
        other = pltpu.roll(gscore, s, 0)
        beats = jnp.where(other > gscore, 1, jnp.where((other == gscore) & (gidx >= s), 1, 0))
        rank = rank + beats
    gsel = rank < TOPK_GROUPS

    key = [jnp.where(gsel, s, neg) for s in sel]
    eidx = [gidx * gs + j for j in range(gs)]
    avail = [jnp.ones((ng, tn), jnp.int32) for _ in range(gs)]
    top_e = jnp.zeros((TOP_K, tn), jnp.int32)
    top_s = jnp.zeros((TOP_K, tn), _F32)
    for r in range(TOP_K):
        cur = functools.reduce(jnp.maximum, [jnp.where(avail[j] > 0, key[j], neg) for j in range(gs)])
        mx = _sublane_all(cur, jnp.maximum)
        cand = [jnp.where((avail[j] > 0) & (key[j] == mx), eidx[j], N_EXPERTS) for j in range(gs)]
        mn = _sublane_all(functools.reduce(jnp.minimum, cand), jnp.minimum)
        picked = functools.reduce(jnp.add, [jnp.where(eidx[j] == mn, scores[j], 0.0) for j in range(gs)])
        avail = [jnp.where(eidx[j] == mn, 0, avail[j]) for j in range(gs)]
        top_e = jnp.where(gidx == r, mn, top_e)
        top_s = jnp.where(gidx == r, _sublane_all(picked, jnp.add), top_s)
    den = _sublane_all(top_s, jnp.add)
    e_ref[...] = top_e
    g_ref[...] = top_s / den * ROUTE_SCALE

    @pl.when(pl.program_id(0) == 0)
    def _():
        cnt_ref[...] = jnp.zeros_like(cnt_ref)

    taken = jnp.concatenate([jnp.where(a == 0, 1.0, 0.0) for a in avail], axis=0)
    before = jnp.dot(taken.astype(_BF16), tri_ref[...], preferred_element_type=_F32) - taken
    before = before + jnp.concatenate([cnt_ref[...]] * (tn // LANES), axis=1)
    rank = jnp.zeros((TOP_K, tn), _F32)
    for r in range(TOP_K):
        e_r = jnp.broadcast_to(top_e[r:r + 1, :], (ng, tn))
        hit = functools.reduce(jnp.add, [jnp.where(eidx[j] == e_r, before[j * ng:(j + 1) * ng, :], 0.0)
                                         for j in range(gs)])
        rank = jnp.where(gidx == r, _sublane_all(hit, jnp.add), rank)
    rank_ref[...] = rank.astype(jnp.int32)
    ones = jnp.ones((tn, LANES), _BF16)
    cnt_ref[...] += jnp.dot(taken.astype(_BF16), ones, preferred_element_type=_F32)


def _router(logits_t, bias_rows, tn):
    assert TOP_K == N_GROUPS == SUBLANES
    t = logits_t.shape[1]
    tn = min(tn, t)
    tri = (jnp.arange(tn)[:, None] <= jnp.arange(tn)[None, :]).astype(_BF16)
    out = pl.BlockSpec((TOP_K, tn), lambda i: (0, i))
    cnt = pl.BlockSpec((N_EXPERTS, LANES), lambda i: (0, 0))
    return pl.pallas_call(
        _router_kernel,
        grid=(t // tn,),
        in_specs=[pl.BlockSpec((N_EXPERTS, tn), lambda i: (0, i)), cnt,
                  pl.BlockSpec((tn, tn), lambda i: (0, 0))],
        out_specs=[out, out, out, cnt],
        out_shape=[jax.ShapeDtypeStruct((TOP_K, t), jnp.int32), jax.ShapeDtypeStruct((TOP_K, t), _F32),
                   jax.ShapeDtypeStruct((TOP_K, t), jnp.int32),
                   jax.ShapeDtypeStruct((N_EXPERTS, LANES), _F32)],
        compiler_params=_cparams(("arbitrary",)),
        name="router_topk",
    )(logits_t, bias_rows, tri)


def _sc_mesh():
    return plsc.VectorSubcoreMesh(core_axis_name="c", subcore_axis_name="s")


def _sc_dispatch(x_half_rows, dest_half_rows, n_out):
    n_rows = x_half_rows.shape[0]

    @pl.kernel(out_type=jax.ShapeDtypeStruct((n_out, SC_ROW), x_half_rows.dtype), mesh=_sc_mesh(),
               scratch_types=[])
    def scatter_rows(x_hbm, d_hbm, o_hbm):
        def body(x_vmem, i_vmem):
            for j in range(TOP_K):
                pltpu.sync_copy(x_vmem, o_hbm.at[i_vmem.at[j]])

        pltpu.emit_pipeline(
            body,
            grid=(n_rows // SC_WINDOW,),
            in_specs=[pl.BlockSpec((SC_WINDOW, SC_ROW), lambda i: (i, 0)),
                      pl.BlockSpec((TOP_K, SC_WINDOW), lambda i: (0, i))],
            out_specs=[],
            core_axis_name=("c", "s"),
            dimension_semantics=(pltpu.PARALLEL,),
        )(x_hbm, d_hbm)

    return scatter_rows(x_half_rows, dest_half_rows)


def _sc_gather(y_half_rows, idx):
    n = idx.shape[1]

    @pl.kernel(out_type=jax.ShapeDtypeStruct((n, SC_ROW), y_half_rows.dtype), mesh=_sc_mesh(),
               scratch_types=[])
    def gather_rows(y_hbm, i_hbm, o_hbm):
        def body(i_vmem, o_vmem):
            pltpu.sync_copy(y_hbm.at[i_vmem.at[0]], o_vmem)

        pltpu.emit_pipeline(
            body,
            grid=(n // SC_WINDOW,),
            in_specs=[pl.BlockSpec((1, SC_WINDOW), lambda i: (0, i))],
            out_specs=[pl.BlockSpec((SC_WINDOW, SC_ROW), lambda i: (i, 0))],
            core_axis_name=("c", "s"),
            dimension_semantics=(pltpu.PARALLEL,),
        )(i_hbm, o_hbm)

    return gather_rows(y_half_rows, idx)


def _dispatch_plan(top_e, rank, counts, n_blocks):
    padded = (counts + MOE_BLOCK - 1) // MOE_BLOCK * MOE_BLOCK
    pad_end = jnp.cumsum(padded)
    pad_start = pad_end - padded
    onehot = top_e[:, :, None] == jnp.arange(N_EXPERTS, dtype=jnp.int32)[None, None, :]
    dest = rank + jnp.sum(jnp.where(onehot, pad_start[None, None, :], 0), axis=-1)
    blk_start = jnp.arange(n_blocks, dtype=jnp.int32) * MOE_BLOCK
    blk_e = jnp.minimum(jnp.sum(pad_end[None, :] <= blk_start[:, None], axis=1), N_EXPERTS - 1)
    n_used = pad_end[-1] // MOE_BLOCK
    return dest, jnp.concatenate([blk_e.astype(jnp.int32), n_used[None].astype(jnp.int32)])


def _swiglu_packed(u0, u1, wg_ref, wu_ref, wd_ref):
    x = jnp.concatenate(_unpack_quarters(u0, u1), axis=1).astype(_BF16)
    a = jnp.dot(x, wg_ref[...].astype(_BF16), preferred_element_type=_F32)
    u = jnp.dot(x, wu_ref[...].astype(_BF16), preferred_element_type=_F32)
    act = (a * jax.nn.sigmoid(a) * u).astype(_BF16)
    return jnp.dot(act, wd_ref[...].astype(_BF16), preferred_element_type=_F32)


def _expert_kernel(plan_ref, x_ref, wg_ref, wu_ref, wd_ref, y_ref):
    @pl.when(pl.program_id(0) < plan_ref[pl.num_programs(0)])
    def _():
        y_ref[...] = _pack_quarters(_swiglu_packed(x_ref[0], x_ref[1], wg_ref, wu_ref, wd_ref))


def _experts(plan, x_sorted, wg, wu, wd):
    _, n_rows, q = x_sorted.shape
    n_blocks = n_rows // MOE_BLOCK
    d = 4 * q
    rows = pl.BlockSpec((2, MOE_BLOCK, q), lambda i, plan: (0, i, 0))
    grid_spec = pltpu.PrefetchScalarGridSpec(
        num_scalar_prefetch=1,
        grid=(n_blocks,),
        in_specs=[rows,
                  pl.BlockSpec((None, d, D_EXPERT), lambda i, plan: (plan[i], 0, 0)),
                  pl.BlockSpec((None, d, D_EXPERT), lambda i, plan: (plan[i], 0, 0)),
                  pl.BlockSpec((None, D_EXPERT, d), lambda i, plan: (plan[i], 0, 0))],
        out_specs=rows,
    )
    return pl.pallas_call(
        _expert_kernel,
        grid_spec=grid_spec,
        out_shape=jax.ShapeDtypeStruct((2, n_rows, q), jnp.uint32),
        compiler_params=_cparams(("arbitrary",)),
        name="moe_routed_experts",
    )(plan, x_sorted, wg, wu, wd)


def _combine_kernel(y_ref, gate_ref, h_ref, x1_ref, g2_ref, fg_ref, sg_ref, su_ref, sd_ref, o_ref):
    gate = gate_ref[...]
    acc = None
    for j in range(TOP_K):
        gj = gate[:, j:j + 1]
        parts = [gj * p for p in _unpack_quarters(y_ref[j, 0], y_ref[j, 1])]
        acc = parts if acc is None else [a + p for a, p in zip(acc, parts)]
    moe = jnp.concatenate(acc, axis=1) + _swiglu_packed(h_ref[0], h_ref[1], sg_ref, su_ref, sd_ref)
    x2 = x1_ref[0] + g2_ref[0] * moe
    o_ref[0] = x2 * lax.rsqrt(jnp.mean(x2 * x2, axis=-1, keepdims=True) + EPS) * fg_ref[...]


def _combine(y_rows, gate_tk, h2_packed, x1, g2, final_g, sg, su, sd, tm):
    bsz, l, d = x1.shape
    tm = min(tm, l)
    nt = l // tm
    q = d // 4
    const = lambda b, i: (0, 0)
    return pl.pallas_call(
        _combine_kernel,
        grid=(bsz, nt),
        in_specs=[pl.BlockSpec((TOP_K, 2, tm, q), lambda b, i: (0, 0, b * nt + i, 0)),
                  pl.BlockSpec((tm, TOP_K), lambda b, i: (b * nt + i, 0)),
                  pl.BlockSpec((2, tm, q), lambda b, i: (0, b * nt + i, 0)),
                  pl.BlockSpec((1, tm, d), lambda b, i: (b, i, 0)),
                  pl.BlockSpec((1, 1, d), lambda b, i: (b, 0, 0)),
                  pl.BlockSpec((1, d), const),
                  pl.BlockSpec((d, D_EXPERT), const),
                  pl.BlockSpec((d, D_EXPERT), const),
                  pl.BlockSpec((D_EXPERT, d), const)],
        out_specs=pl.BlockSpec((1, tm, d), lambda b, i: (b, i, 0)),
        out_shape=jax.ShapeDtypeStruct((bsz, l, d), _F32),
        compiler_params=_cparams(("parallel", "parallel")),
        name="moe_combine_shared_final_norm",
    )(y_rows, gate_tk, h2_packed, x1, g2, final_g.reshape(1, d), sg, su, sd)


def _to_chain_lanes(t):
    lead, (b, l, _) = t.shape[:-3], t.shape[-3:]
    nl = len(lead)
    t = t.reshape(*lead, b, l, RW_HEADS, RW_HEAD_DIM)
    t = t.transpose(*range(nl), nl + 1, nl + 3, nl, nl + 2)
    return t.reshape(*lead, l, RW_HEAD_DIM, b * RW_HEADS)


def _from_chain_lanes(t, b):
    l = t.shape[1]
    return t.reshape(2, l, RW_HEAD_DIM, b, RW_HEADS).transpose(0, 3, 1, 4, 2).reshape(2, b, l, RW_WIDTH)


def _scan_operands(prep):
    r, kk, v, w, kd, bb = prep[:6]
    return [_to_chain_lanes(t) for t in (r, kk, v)], [_to_chain_lanes(t) for t in (w, kd, bb)]


def kernel(x, c, ctx, c_ctx, w_mod, b_mod, norm1_g, norm2_g, w_in, rw_conv, rw_w0, rw_w2, rw_a0, rw_a2,
           rw_g2, rw_k_k, rw_k_a, rw_r_k, rw_lnx_g, rw_lnx_b, hg_lb_logits, hg_norm_g, w_out, router_w,
           router_b, exp_w_gate, exp_w_up, exp_w_down, sh_w_gate, sh_w_up, sh_w_down, final_norm_g):
    bsz, seq, d = x.shape
    assert w_mod.shape[0] == 1 and bsz * RW_HEADS == LANES
    lyr = 0

    c_all = jnp.concatenate([c, c_ctx[None, :], jnp.zeros((SUBLANES - 1, d), _F32)], axis=0)
    mod = _modulation(c_all, w_mod[lyr], b_mod[lyr])
    sh1, sc1, g1, sh2, sc2, g2 = [m[:, None, :] for m in jnp.split(mod[:bsz], N_MOD, axis=-1)]
    mod_ctx = jnp.broadcast_to(mod[bsz][None, None, :], (bsz, 1, N_MOD * d))
    csh1, csc1 = mod_ctx[..., :d], mod_ctx[..., d:2 * d]

    w_in_bf16 = w_in[lyr].astype(_BF16)
    rkv_lat, lora_lat, hg_lat = _inproj(x, norm1_g[lyr], sc1, sh1, w_in_bf16, tm=256)
    rkv_ctx, lora_ctx, hg_ctx = _inproj(ctx, norm1_g[lyr], csc1, csh1, w_in_bf16, tm=256)

    kern9 = rw_conv[lyr].reshape(CONV_K * CONV_K, RKV_W)
    lane_head = jnp.arange(RW_WIDTH) // RW_HEAD_DIM
    seg = (lane_head[:, None] == lane_head[None, :]).astype(_BF16)
    per_dir_split = lambda t: jnp.swapaxes(_split_bf16(t), 0, 1)
    rw_p = (kern9, rw_w0[lyr], per_dir_split(rw_w2[lyr]), rw_a0[lyr], per_dir_split(rw_a2[lyr]),
            _split_bf16(rw_g2[lyr]), rw_k_k[lyr], rw_k_a[lyr], rw_r_k[lyr], seg)
    prep_lat = _rw_prep(rkv_lat, lora_lat, *rw_p, tq=256, grid_conv=True)
    prep_ctx = _rw_prep(rkv_ctx, lora_ctx, *rw_p, tq=256, grid_conv=False)
    s_rw = _rwkv_scan(*_scan_operands(prep_ctx), None, tt=32, emit=False)
    y_dirs = _from_chain_lanes(_rwkv_scan(*_scan_operands(prep_lat), s_rw, tt=32, emit=True), bsz)

    lb = jnp.cumsum(jax.nn.softmax(hg_lb_logits.astype(_F32), axis=1), axis=1)[:, lyr]
    o_dirs = []
    for dd, rev in enumerate((False, True)):
        s_hg = _hgrn_scan(hg_ctx, lb[dd:dd + 1], None, tb=256, emit=False, reverse=rev)
        o_dirs.append(_hgrn_scan(hg_lat, lb[dd:dd + 1], s_hg, tb=256, emit=True, reverse=rev))

    perm = lambda t: t.reshape(N_GROUPS, GROUP_SIZE, -1).transpose(1, 0, 2).reshape(N_EXPERTS, -1)
    router_wt = perm(router_w[lyr].T)
    bias_rows = jnp.broadcast_to(perm(router_b[lyr][:, None]), (N_EXPERTS, LANES))
    hg_norm_row = jnp.tile(hg_norm_g[lyr], HG_HEADS).reshape(1, HG_WIDTH)
    x1, h2, logits_t = _post(y_dirs, prep_lat[6], prep_lat[7], rw_lnx_g[lyr], rw_lnx_b[lyr], o_dirs, hg_lat,
                             hg_norm_row, seg, x, w_out[lyr].astype(_BF16), g1, norm2_g[lyr], sc2, sh2,
                             router_wt, tm=512)
    top_e, gate, rank, counts_rows = _router(logits_t, bias_rows, tn=512)
    counts = counts_rows[:, 0].astype(jnp.int32).reshape(GROUP_SIZE, N_GROUPS).T.reshape(N_EXPERTS)

    n_tok = bsz * seq
    n_blocks = n_tok * TOP_K // MOE_BLOCK + N_EXPERTS
    n_slots = n_blocks * MOE_BLOCK
    dest, plan = _dispatch_plan(top_e, rank, counts, n_blocks)
    dest_half = (dest[:, None, :] + jnp.array([0, n_slots], jnp.int32)[None, :, None]).reshape(TOP_K, 2 * n_tok)
    x_sorted = _sc_dispatch(h2.reshape(2 * n_tok, SC_ROW), dest_half, 2 * n_slots)
    y_sorted = _experts(plan, x_sorted.reshape(2, n_slots, SC_ROW),
                        exp_w_gate[lyr], exp_w_up[lyr], exp_w_down[lyr])
    y_rows = _sc_gather(y_sorted.reshape(2 * n_slots, SC_ROW), dest_half.reshape(1, TOP_K * 2 * n_tok))
    return _combine(y_rows.reshape(TOP_K, 2, n_tok, SC_ROW), gate.T, h2, x1, g2,
                    final_norm_g, sh_w_gate[lyr], sh_w_up[lyr], sh_w_down[lyr], tm=512)
```

```python
import functools

import jax
import jax.numpy as jnp
from jax import lax
from jax.experimental import pallas as pl
from jax.experimental.pallas import tpu as pltpu
from jax.experimental.pallas import tpu_sc as plsc

D_MODEL = 1024
GRID_W = 64
N_MOD = 6
EPS = 1e-6

RW_HEAD_DIM = 64
RW_HEADS = 8
RW_WIDTH = 512
RW_LORA = 64
RW_GATE_LORA = 128
RW_GN_EPS = 1e-5 * RW_HEAD_DIM
CONV_K = 3

HG_HEADS = 8
HG_DIM = 64
HG_WIDTH = 512
HG_CHUNK = 32
HG_PAIRS = HG_HEADS // 2
HG_SCORE_ROWS = 128

RKV_W = 3 * RW_WIDTH
LORA_W = 4 * RW_LORA + RW_GATE_LORA
HG_W = 5 * HG_WIDTH
P_IN = RKV_W + LORA_W + HG_W

N_EXPERTS = 64
N_GROUPS = 8
GROUP_SIZE = N_EXPERTS // N_GROUPS
TOPK_GROUPS = 4
TOP_K = 8
D_EXPERT = 256
ROUTE_SCALE = 2.5
MOE_BLOCK = 1024
SC_ROW = 256
SC_WINDOW = 128

LANES = 128
SUBLANES = 8
VMEM_LIMIT = 48 * 1024 * 1024

_HI = lax.Precision.HIGHEST
_F32 = jnp.float32
_BF16 = jnp.bfloat16
_NT = (((1,), (1,)), ((), ()))
_TN = (((0,), (0,)), ((), ()))


def _cparams(sem):
    return pltpu.CompilerParams(dimension_semantics=sem, vmem_limit_bytes=VMEM_LIMIT)


def _pack_bf16_pair(lo, hi):
    return pltpu.pack_elementwise([lo, hi], packed_dtype=_BF16)


def _unpack_bf16_pair(u):
    return tuple(pltpu.unpack_elementwise(u, index=i, packed_dtype=_BF16, unpacked_dtype=_F32) for i in range(2))


def _pack_quarters(x):
    q = x.shape[1] // 4
    return jnp.stack([_pack_bf16_pair(x[:, 2 * h * q:(2 * h + 1) * q], x[:, (2 * h + 1) * q:(2 * h + 2) * q])
                      for h in range(2)])


def _unpack_quarters(u0, u1):
    return _unpack_bf16_pair(u0) + _unpack_bf16_pair(u1)


def _dir_block(d, i, nb):
    return i + d * (nb - 1 - 2 * i)


def _mod_kernel(c_ref, w_ref, b_ref, o_ref):
    c = c_ref[...]
    s = c * jax.nn.sigmoid(c)
    o_ref[...] = jnp.dot(s, w_ref[...], preferred_element_type=_F32, precision=_HI) + b_ref[...]


def _modulation(c_all, w_mod, b_mod):
    rows, d = c_all.shape
    n = w_mod.shape[1]
    tn = 1024
    return pl.pallas_call(
        _mod_kernel,
        grid=(n // tn,),
        in_specs=[pl.BlockSpec((rows, d), lambda j: (0, 0)),
                  pl.BlockSpec((d, tn), lambda j: (0, j)),
                  pl.BlockSpec((1, tn), lambda j: (0, j))],
        out_specs=pl.BlockSpec((rows, tn), lambda j: (0, j)),
        out_shape=jax.ShapeDtypeStruct((rows, n), _F32),
        compiler_params=_cparams(("parallel",)),
        name="modulation",
    )(c_all, w_mod, b_mod.reshape(1, n))


def _inproj_kernel(x_ref, g_ref, sc_ref, sh_ref, w_ref, rkv_ref, lora_ref, hg_ref):
    x = x_ref[0]
    y = x * lax.rsqrt(jnp.mean(x * x, axis=-1, keepdims=True) + EPS) * g_ref[...]
    h = y * (1.0 + sc_ref[0]) + sh_ref[0]
    u = jnp.dot(h.astype(_BF16), w_ref[...], preferred_element_type=_F32)
    rkv_ref[0] = u[:, :RKV_W]
    lora_ref[0] = u[:, RKV_W:RKV_W + LORA_W]
    hg_ref[0] = u[:, RKV_W + LORA_W:]


def _inproj(x, norm_g, sc, sh, w_in_bf16, tm):
    bsz, l, d = x.shape
    tm = min(tm, l)
    row = lambda b, i: (b, i, 0)
    per_b = lambda b, i: (b, 0, 0)
    return pl.pallas_call(
        _inproj_kernel,
        grid=(bsz, l // tm),
        in_specs=[pl.BlockSpec((1, tm, d), row),
                  pl.BlockSpec((1, d), lambda b, i: (0, 0)),
                  pl.BlockSpec((1, 1, d), per_b),
                  pl.BlockSpec((1, 1, d), per_b),
                  pl.BlockSpec((d, P_IN), lambda b, i: (0, 0))],
        out_specs=[pl.BlockSpec((1, tm, RKV_W), row),
                   pl.BlockSpec((1, tm, LORA_W), row),
                   pl.BlockSpec((1, tm, HG_W), row)],
        out_shape=[jax.ShapeDtypeStruct((bsz, l, RKV_W), _F32),
                   jax.ShapeDtypeStruct((bsz, l, LORA_W), _F32),
                   jax.ShapeDtypeStruct((bsz, l, HG_W), _F32)],
        compiler_params=_cparams(("parallel", "parallel")),
        name="inproj",
    )(x, norm_g.reshape(1, d), sc, sh, w_in_bf16)


def _seg_sum(x, seg_ref, nonneg=False):
    hi = x.astype(_BF16)
    out = jnp.dot(hi, seg_ref[...], preferred_element_type=_F32)
    if nonneg:
        return out
    lo = (x - hi.astype(_F32)).astype(_BF16)
    return out + jnp.dot(lo, seg_ref[...], preferred_element_type=_F32)


def _dot_split(x, w_ref):
    hi = x.astype(_BF16)
    lo = (x - hi.astype(_F32)).astype(_BF16)
    return (jnp.dot(hi, w_ref[0], preferred_element_type=_F32) + jnp.dot(lo, w_ref[0], preferred_element_type=_F32)
            + jnp.dot(hi, w_ref[1], preferred_element_type=_F32))


def _split_bf16(w):
    hi = w.astype(_BF16)
    return jnp.stack([hi, (w - hi.astype(_F32)).astype(_BF16)])


def _rw_prep_kernel(cur_ref, prev_ref, next_ref, lora_ref, kern_ref, w0_ref, w2_ref, a0_ref, a2_ref, g2_ref,
                    kk_ref, ka_ref, rk_ref, seg_ref,
                    r_out, kkn_out, v_out, w_out, kd_out, b_out, bonus_out, g_out, *, grid_conv):
    i = pl.program_id(1)
    tq = cur_ref.shape[0]
    hw = prev_ref.shape[0]
    w = RW_WIDTH
    pmask = jnp.where(i > 0, 1.0, 0.0)
    nmask = jnp.where(i < pl.num_programs(1) - 1, 1.0, 0.0)
    xpos = lax.broadcasted_iota(jnp.int32, (tq + 2 * hw, w), 0) % GRID_W
    conv = []
    for part in range(3):
        cs = slice(part * w, (part + 1) * w)
        xe = jnp.concatenate([prev_ref[:, cs] * pmask, cur_ref[:, cs], next_ref[:, cs] * nmask], axis=0)
        xl = pltpu.roll(xe, 1, 0)
        xr = pltpu.roll(xe, tq + 2 * hw - 1, 0)
        if grid_conv:
            xl = jnp.where(xpos == 0, 0.0, xl)
            xr = jnp.where(xpos == GRID_W - 1, 0.0, xr)
        acc = None
        for dy in range(CONV_K) if grid_conv else (CONV_K // 2,):
            off = hw + (dy - 1) * GRID_W
            for dx, src in enumerate((xl, xe, xr)):
                term = src[off:off + tq] * kern_ref[pl.ds(dy * CONV_K + dx, 1), cs]
                acc = term if acc is None else acc + term
        conv.append(acc)
    r, k, v = conv
    r_out[...] = r.astype(_BF16)
    v_out[...] = v.astype(_BF16)

    kk = k * kk_ref[...]
    kk = kk / jnp.maximum(jnp.sqrt(_seg_sum(kk * kk, seg_ref, nonneg=True)), 1e-12)
    kkn_out[...] = kk.astype(_BF16)

    lora = lora_ref[...]
    ksum = jnp.zeros_like(k)
    for d in range(2):
        wl = lora[:, d * RW_LORA:(d + 1) * RW_LORA]
        al = lora[:, (2 + d) * RW_LORA:(3 + d) * RW_LORA]
        z = w0_ref[pl.ds(d, 1), :] + _dot_split(jnp.tanh(wl), w2_ref.at[d])
        wlog = -jax.nn.softplus(-z) - 0.5
        w_out[d] = jnp.exp(-jnp.exp(wlog))
        a = jax.nn.sigmoid(a0_ref[pl.ds(d, 1), :] + _dot_split(al, a2_ref.at[d]))
        kd = k * (1.0 + (a - 1.0) * ka_ref[...])
        kd_out[d] = kd.astype(_BF16)
        b_out[d] = (kk * a).astype(_BF16)
        ksum = ksum + kd
    bonus_out[...] = _seg_sum(r * ksum * rk_ref[...], seg_ref) * v
    gd = lora[:, 4 * RW_LORA:]
    g_out[...] = _dot_split(jax.nn.sigmoid(gd), g2_ref)


def _rw_prep(rkv, lora, kern9, w0, w2, a0, a2, g2, k_k, k_a, r_k, seg, tq, grid_conv):
    bsz, l, _ = rkv.shape
    tq = min(tq, l)
    nb = l // tq
    hw = GRID_W
    per = tq // hw
    w = RW_WIDTH
    row = lambda b, i: (b, i, 0)
    const2 = lambda b, i: (0, 0)
    const3 = lambda b, i: (0, 0, 0)
    const4 = lambda b, i: (0, 0, 0, 0)
    tok = pl.BlockSpec((None, tq, w), row)
    tok2 = pl.BlockSpec((2, None, tq, w), lambda b, i: (0, b, i, 0))
    one = jax.ShapeDtypeStruct((bsz, l, w), _F32)
    two = jax.ShapeDtypeStruct((2, bsz, l, w), _F32)
    one_h = jax.ShapeDtypeStruct((bsz, l, w), _BF16)
    two_h = jax.ShapeDtypeStruct((2, bsz, l, w), _BF16)
    vec = lambda t: t.reshape(1, w)
    return pl.pallas_call(
        functools.partial(_rw_prep_kernel, grid_conv=grid_conv),
        grid=(bsz, nb),
        in_specs=[pl.BlockSpec((None, tq, RKV_W), row),
                  pl.BlockSpec((None, hw, RKV_W), lambda b, i: (b, jnp.maximum(i * per - 1, 0), 0)),
                  pl.BlockSpec((None, hw, RKV_W), lambda b, i: (b, jnp.minimum((i + 1) * per, l // hw - 1), 0)),
                  pl.BlockSpec((None, tq, LORA_W), row),
                  pl.BlockSpec((CONV_K * CONV_K, RKV_W), const2),
                  pl.BlockSpec((2, w), const2),
                  pl.BlockSpec((2, 2, RW_LORA, w), const4),
                  pl.BlockSpec((2, w), const2),
                  pl.BlockSpec((2, 2, RW_LORA, w), const4),
                  pl.BlockSpec((2, RW_GATE_LORA, w), const3),
                  pl.BlockSpec((1, w), const2),
                  pl.BlockSpec((1, w), const2),
                  pl.BlockSpec((1, w), const2),
                  pl.BlockSpec((w, w), const2)],
        out_specs=[tok, tok, tok, tok2, tok2, tok2, tok, tok],
        out_shape=[one_h, one_h, one_h, two, two_h, two_h, one, one],
        compiler_params=_cparams(("parallel", "parallel")),
        name="rwkv_prep_grid" if grid_conv else "rwkv_prep_seq",
    )(rkv, rkv, rkv, lora, kern9, w0, w2, a0, a2, g2, vec(k_k), vec(k_a), vec(r_k), seg)


def _rwkv_scan_kernel(*refs, tt, emit, has_init):
    r_in, kk_in, v_in, w_ref, k_in, b_in = refs[:6]
    rest = list(refs[6:])
    s0_ref = rest.pop(0) if has_init else None
    out_ref = rest.pop(0)
    s_ref, sa_ref, r_ref, kk_ref, v_ref, k_ref, b_ref = rest
    for src, dst in ((r_in, r_ref), (kk_in, kk_ref), (v_in, v_ref), (k_in, k_ref), (b_in, b_ref)):
        dst[...] = src[...].astype(_F32)
    d = pl.program_id(0)
    i = pl.program_id(1)
    n = RW_HEAD_DIM

    @pl.when(i == 0)
    def _():
        s_ref[...] = s0_ref[...] if has_init else jnp.zeros_like(s_ref)

    hv = n // 2
    halves = [pl.ds(h * hv, hv) for h in range(2)]

    def project(t, vs):
        acc = [jnp.zeros((hv, LANES), _F32), jnp.zeros((hv, LANES), _F32)]
        for k in range(n):
            acc[k % 2] = acc[k % 2] + s_ref[k, vs, :] * kk_ref[t, pl.ds(k, 1), :]
        return -(acc[0] + acc[1])

    first = d * (tt - 1)
    for vs in halves:
        sa_ref[vs, :] = project(first, vs)

    def step(j, carry):
        t = j + d * (tt - 1 - 2 * j)
        tn = jnp.clip(t + 1 - 2 * d, 0, tt - 1)
        for vs in halves:
            sa = sa_ref[vs, :]
            vv = v_ref[t, vs, :]
            y = [jnp.zeros((hv, LANES), _F32), jnp.zeros((hv, LANES), _F32)]
            nsa = [jnp.zeros((hv, LANES), _F32), jnp.zeros((hv, LANES), _F32)]
            for k in range(n):
                row = pl.ds(k, 1)
                s_new = s_ref[k, vs, :] * w_ref[t, row, :] + sa * b_ref[t, row, :] + vv * k_ref[t, row, :]
                s_ref[k, vs, :] = s_new
                nsa[k % 2] = nsa[k % 2] + s_new * kk_ref[tn, row, :]
                if emit:
                    y[k % 2] = y[k % 2] + s_new * r_ref[t, row, :]
            sa_ref[vs, :] = -(nsa[0] + nsa[1])
            if emit:
                out_ref[t, vs, :] = y[0] + y[1]
        return carry

    lax.fori_loop(0, tt, step, 0)

    if not emit:
        @pl.when(i == pl.num_programs(1) - 1)
        def _():
            out_ref[...] = s_ref[...]


def _rwkv_scan(shared, per_dir, s0, tt, emit):
    l, n, c = shared[0].shape
    nb = l // tt
    sh_spec = pl.BlockSpec((tt, n, c), lambda d, i: (_dir_block(d, i, nb), 0, 0))
    pd_spec = pl.BlockSpec((None, tt, n, c), lambda d, i: (d, _dir_block(d, i, nb), 0, 0))
    st_spec = pl.BlockSpec((None, n, n, c), lambda d, i: (d, 0, 0, 0))
    in_specs = [sh_spec] * 3 + [pd_spec] * 3
    args = list(shared) + list(per_dir)
    if s0 is not None:
        in_specs.append(st_spec)
        args.append(s0)
    if emit:
        out_spec, out_shape = pd_spec, jax.ShapeDtypeStruct((2, l, n, c), _F32)
    else:
        out_spec, out_shape = st_spec, jax.ShapeDtypeStruct((2, n, n, c), _F32)
    return pl.pallas_call(
        functools.partial(_rwkv_scan_kernel, tt=tt, emit=emit, has_init=s0 is not None),
        grid=(2, nb),
        in_specs=in_specs,
        out_specs=out_spec,
        out_shape=out_shape,
        scratch_shapes=[pltpu.VMEM((n, n, c), _F32), pltpu.VMEM((n, c), _F32)] + [pltpu.VMEM((tt, n, c), _F32)] * 5,
        compiler_params=_cparams(("parallel", "arbitrary")),
        name="rwkv_scan_emit" if emit else "rwkv_scan_state",
    )(*args)


def _hgrn_kernel(*refs, n_chunks, emit, has_init, reverse):
    q_ref, f_ref, i_ref, lb_ref, mats_ref = refs[:5]
    rest = list(refs[5:])
    s0_ref = rest.pop(0) if has_init else None
    out_ref = rest.pop(0)
    st_ref, upd_ref, ent_ref = rest
    c = HG_CHUNK
    tb = n_chunks * c
    blk = pl.program_id(1)

    @pl.when(blk == 0)
    def _():
        st_ref[...] = s0_ref[...] if has_init else jnp.zeros_like(st_ref)

    lb = lb_ref[...]
    omlb = 1.0 - lb
    fr = f_ref[...]
    logf = jnp.log(lb + omlb * jax.nn.sigmoid(fr))
    kf = omlb * jax.nn.sigmoid(-fr)
    hi = logf.astype(_BF16)
    lo = (logf - hi.astype(_F32)).astype(_BF16)
    sums = (jnp.dot(mats_ref[...], hi, preferred_element_type=_F32)
            + jnp.dot(mats_ref[...], lo, preferred_element_type=_F32))
    cum, tot, mid = sums[:tb], sums[tb:2 * tb], sums[2 * tb:]
    dec = jnp.exp(tot)
    kdec = (kf * jnp.exp(tot - cum)).astype(_BF16)
    vb = i_ref[...].astype(_BF16)
    same_head = ((lax.broadcasted_iota(jnp.int32, (LANES, LANES), 0) < HG_DIM)
                 == (lax.broadcasted_iota(jnp.int32, (LANES, LANES), 1) < HG_DIM))
    pairs = [slice(p * LANES, (p + 1) * LANES) for p in range(HG_PAIRS)]
    chunks = [slice(j * c, (j + 1) * c) for j in range(n_chunks)]

    if emit:
        q = q_ref[...]
        qs = q * jax.nn.sigmoid(q) * (HG_DIM ** -0.5)
        qd = qs * jnp.exp(cum - mid)
        kd = (kf * jnp.exp(mid - cum)).astype(_BF16)
        qe = (qs * jnp.exp(cum)).astype(_BF16)
        sub = min(tb, HG_SCORE_ROWS)
        head0 = lax.broadcasted_iota(jnp.int32, (sub, LANES), 1) < HG_DIM
        for g in range(tb // sub):
            rows = slice(g * sub, (g + 1) * sub)
            causal = mats_ref[rows, rows] > 0
            causal2 = jnp.concatenate([causal, causal], axis=0)
            for ls in pairs:
                qd_p = qd[rows, ls]
                q2 = jnp.concatenate([jnp.where(head0, qd_p, 0.0), jnp.where(head0, 0.0, qd_p)], axis=0)
                sc = lax.dot_general(q2.astype(_BF16), kd[rows, ls], _NT, preferred_element_type=_F32)
                sc = jnp.where(causal2, sc, 0.0)
                oi = jnp.dot(sc.astype(_BF16), vb[rows, ls], preferred_element_type=_F32)
                out_ref[rows, ls] = jnp.where(head0, oi[:sub], oi[sub:])

    for j, rows in enumerate(chunks):
        for p, ls in enumerate(pairs):
            upd = lax.dot_general(vb[rows, ls], kdec[rows, ls], _TN, preferred_element_type=_F32)
            upd_ref[j, p] = jnp.where(same_head, upd, 0.0)

    order = list(range(n_chunks))[::-1] if reverse else list(range(n_chunks))
    for p, ls in enumerate(pairs):
        st = st_ref[p]
        for j in order:
            if emit:
                ent_ref[j, p] = st.astype(_BF16)
            st = st * dec[j * c:j * c + 1, ls] + upd_ref[j, p]
        st_ref[p] = st

    if emit:
        for j, rows in enumerate(chunks):
            for p, ls in enumerate(pairs):
                out_ref[rows, ls] += lax.dot_general(qe[rows, ls], ent_ref[j, p], _NT,
                                                     preferred_element_type=_F32)
    else:
        @pl.when(blk == pl.num_programs(1) - 1)
        def _():
            out_ref[...] = st_ref[...]


def _hgrn_mats(tb, reverse):
    t = jnp.arange(tb)
    same = (t[:, None] // HG_CHUNK) == (t[None, :] // HG_CHUNK)
    ref_pos = (t // HG_CHUNK) * HG_CHUNK + HG_CHUNK // 2
    if reverse:
        cum = same & (t[None, :] >= t[:, None])
        mid = same & (t[None, :] >= ref_pos[:, None])
    else:
        cum = same & (t[None, :] <= t[:, None])
        mid = same & (t[None, :] <= ref_pos[:, None])
    return jnp.concatenate([cum, same, mid], axis=0).astype(_BF16)


def _hgrn_scan(hg, lb, s0, tb, emit, reverse):
    bsz, l, _ = hg.shape
    tb = min(tb, l)
    nb = l // tb
    w = HG_WIDTH
    n_chunks = tb // HG_CHUNK
    tblk = (lambda i: nb - 1 - i) if reverse else (lambda i: i)
    col = lambda j: pl.BlockSpec((None, tb, w), lambda b, i: (b, tblk(i), j))
    const = lambda b, i: (0, 0)
    st_spec = pl.BlockSpec((None, HG_PAIRS, LANES, LANES), lambda b, i: (b, 0, 0, 0))
    in_specs = [col(0), col(2 if reverse else 1), col(3), pl.BlockSpec((1, w), const),
                pl.BlockSpec((3 * tb, tb), const)]
    args = [hg, hg, hg, lb, _hgrn_mats(tb, reverse)]
    if s0 is not None:
        in_specs.append(st_spec)
        args.append(s0)
    if emit:
        out_spec = pl.BlockSpec((None, tb, w), lambda b, i: (b, tblk(i), 0))
        out_shape = jax.ShapeDtypeStruct((bsz, l, w), _F32)
    else:
        out_spec = st_spec
        out_shape = jax.ShapeDtypeStruct((bsz, HG_PAIRS, LANES, LANES), _F32)
    return pl.pallas_call(
        functools.partial(_hgrn_kernel, n_chunks=n_chunks, emit=emit, has_init=s0 is not None,
                          reverse=reverse),
        grid=(bsz, nb),
        in_specs=in_specs,
        out_specs=out_spec,
        out_shape=out_shape,
        scratch_shapes=[pltpu.VMEM((HG_PAIRS, LANES, LANES), _F32),
                        pltpu.VMEM((n_chunks, HG_PAIRS, LANES, LANES), _F32),
                        pltpu.VMEM((n_chunks, HG_PAIRS, LANES, LANES), _BF16)],
        compiler_params=_cparams(("parallel", "arbitrary")),
        name=("hgrn_emit" if emit else "hgrn_state") + ("_bwd" if reverse else "_fwd"),
    )(*args)


def _post_kernel(y_ref, bonus_ref, grw_ref, lng_ref, lnb_ref, of_ref, ob_ref, ghg_ref, hgn_ref, seg_ref,
                 x_ref, w_ref, g1_ref, n2_ref, sc_ref, sh_ref, rw_ref, x1_ref, h2_ref, lg_ref):
    inv_n = 1.0 / RW_HEAD_DIM
    ysum = y_ref[0] + y_ref[1]
    mu = _seg_sum(ysum, seg_ref) * inv_n
    dlt = ysum - mu
    var = _seg_sum(dlt * dlt, seg_ref, nonneg=True) * inv_n
    yn = dlt * lax.rsqrt(var + RW_GN_EPS) * lng_ref[...] + lnb_ref[...] + bonus_ref[...]
    y_rw = yn * grw_ref[...]
    o = of_ref[...] + ob_ref[...]
    o = o * lax.rsqrt(_seg_sum(o * o, seg_ref, nonneg=True) * inv_n + EPS) * hgn_ref[...]
    g = ghg_ref[...]
    y_hg = o * (g * jax.nn.sigmoid(g))
    y = (jnp.dot(y_rw.astype(_BF16), w_ref[:RW_WIDTH, :], preferred_element_type=_F32)
         + jnp.dot(y_hg.astype(_BF16), w_ref[RW_WIDTH:, :], preferred_element_type=_F32))
    x1 = x_ref[0] + g1_ref[0] * y
    x1_ref[0] = x1
    h2 = x1 * lax.rsqrt(jnp.mean(x1 * x1, axis=-1, keepdims=True) + EPS) * n2_ref[...]
    h2 = h2 * (1.0 + sc_ref[0]) + sh_ref[0]
    h2_ref[...] = _pack_quarters(h2)
    lg_ref[...] = lax.dot_general(rw_ref[...], h2, _NT, preferred_element_type=_F32, precision=_HI)


def _post(y_dirs, bonus, g_rw, lnx_g, lnx_b, o_dirs, hg, hg_norm_row, seg,
          x, w_out_bf16, g1, norm2_g, sc2, sh2, router_wt, tm):
    bsz, l, d = x.shape
    tm = min(tm, l)
    nt = l // tm
    w = RW_WIDTH
    row = lambda b, i: (b, i, 0)
    per_b = lambda b, i: (b, 0, 0)
    const = lambda b, i: (0, 0)
    tok = pl.BlockSpec((None, tm, w), row)
    tok2 = pl.BlockSpec((2, None, tm, w), lambda b, i: (0, b, i, 0))
    vec = pl.BlockSpec((1, w), const)
    return pl.pallas_call(
        _post_kernel,
        grid=(bsz, nt),
        in_specs=[tok2, tok, tok, vec, vec, tok, tok,
                  pl.BlockSpec((None, tm, w), lambda b, i: (b, i, 4)),
                  vec,
                  pl.BlockSpec((w, w), const),
                  pl.BlockSpec((1, tm, d), row),
                  pl.BlockSpec((d, d), const),
                  pl.BlockSpec((1, 1, d), per_b),
                  pl.BlockSpec((1, d), const),
                  pl.BlockSpec((1, 1, d), per_b),
                  pl.BlockSpec((1, 1, d), per_b),
                  pl.BlockSpec((N_EXPERTS, d), const)],
        out_specs=[pl.BlockSpec((1, tm, d), row),
                   pl.BlockSpec((2, tm, d // 4), lambda b, i: (0, b * nt + i, 0)),
                   pl.BlockSpec((N_EXPERTS, tm), lambda b, i: (0, b * nt + i))],
        out_shape=[jax.ShapeDtypeStruct((bsz, l, d), _F32),
                   jax.ShapeDtypeStruct((2, bsz * l, d // 4), jnp.uint32),
                   jax.ShapeDtypeStruct((N_EXPERTS, bsz * l), _F32)],
        compiler_params=_cparams(("parallel", "parallel")),
        name="outproj_norm2_router",
    )(y_dirs, bonus, g_rw, lnx_g.reshape(1, w), lnx_b.reshape(1, w), o_dirs[0], o_dirs[1], hg, hg_norm_row, seg,
      x, w_out_bf16, g1, norm2_g.reshape(1, d), sc2, sh2, router_wt)


def _sublane_all(x, op):
    for s in (4, 2, 1):
        x = op(x, pltpu.roll(x, s, 0))
    return x


def _router_kernel(lg_ref, bias_ref, tri_ref, e_ref, g_ref, rank_ref, cnt_ref):
    ng, gs = N_GROUPS, GROUP_SIZE
    tn = lg_ref.shape[1]
    neg = -jnp.inf
    gidx = lax.broadcasted_iota(jnp.int32, (ng, tn), 0)
    bias = jnp.concatenate([bias_ref[...]] * (tn // LANES), axis=1)
    scores = [jax.nn.sigmoid(lg_ref[j * ng:(j + 1) * ng, :]) for j in range(gs)]
    sel = [scores[j] + bias[j * ng:(j + 1) * ng, :] for j in range(gs)]

    m1 = functools.reduce(jnp.maximum, sel)
    cnt = functools.reduce(jnp.add, [(s == m1).astype(_F32) for s in sel])
    m2 = functools.reduce(jnp.maximum, [jnp.where(s < m1, s, neg) for s in sel])
    gscore = m1 + jnp.where(cnt >= 2.0, m1, m2)

    rank = jnp.zeros((ng, tn), jnp.int32)
    for s in range(1, ng):
        other = pltpu.roll(gscore, s, 0)
        beats = jnp.where(other > gscore, 1, jnp.where((other == gscore) & (gidx >= s), 1, 0))
        rank = rank + beats
    gsel = rank < TOPK_GROUPS

    key = [jnp.where(gsel, s, neg) for s in sel]
    eidx = [gidx * gs + j for j in range(gs)]
    avail = [jnp.ones((ng, tn), jnp.int32) for _ in range(gs)]
    top_e = jnp.zeros((TOP_K, tn), jnp.int32)
    top_s = jnp.zeros((TOP_K, tn), _F32)
    for r in range(TOP_K):
        cur = functools.reduce(jnp.maximum, [jnp.where(avail[j] > 0, key[j], neg) for j in range(gs)])
        mx = _sublane_all(cur, jnp.maximum)
        cand = [jnp.where((avail[j] > 0) & (key[j] == mx), eidx[j], N_EXPERTS) for j in range(gs)]
        mn = _sublane_all(functools.reduce(jnp.minimum, cand), jnp.minimum)
        picked = functools.reduce(jnp.add, [jnp.where(eidx[j] == mn, scores[j], 0.0) for j in range(gs)])
        avail = [jnp.where(eidx[j] == mn, 0, avail[j]) for j in range(gs)]
        top_e = jnp.where(gidx == r, mn, top_e)
        top_s = jnp.where(gidx == r, _sublane_all(picked, jnp.add), top_s)
    den = _sublane_all(top_s, jnp.add)
    e_ref[...] = top_e
    g_ref[...] = top_s / den * ROUTE_SCALE

    @pl.when(pl.program_id(0) == 0)
    def _():
        cnt_ref[...] = jnp.zeros_like(cnt_ref)

    taken = jnp.concatenate([jnp.where(a == 0, 1.0, 0.0) for a in avail], axis=0)
    before = jnp.dot(taken.astype(_BF16), tri_ref[...], preferred_element_type=_F32) - taken
    before = before + jnp.concatenate([cnt_ref[...]] * (tn // LANES), axis=1)
    rank = jnp.zeros((TOP_K, tn), _F32)
    for r in range(TOP_K):
        e_r = jnp.broadcast_to(top_e[r:r + 1, :], (ng, tn))
        hit = functools.reduce(jnp.add, [jnp.where(eidx[j] == e_r, before[j * ng:(j + 1) * ng, :], 0.0)
                                         for j in range(gs)])
        rank = jnp.where(gidx == r, _sublane_all(hit, jnp.add), rank)
    rank_ref[...] = rank.astype(jnp.int32)
    ones = jnp.ones((tn, LANES), _BF16)
    cnt_ref[...] += jnp.dot(taken.astype(_BF16), ones, preferred_element_type=_F32)


def _router(logits_t, bias_rows, tn):
    assert TOP_K == N_GROUPS == SUBLANES
    t = logits_t.shape[1]
    tn = min(tn, t)
    tri = (jnp.arange(tn)[:, None] <= jnp.arange(tn)[None, :]).astype(_BF16)
    out = pl.BlockSpec((TOP_K, tn), lambda i: (0, i))
    cnt = pl.BlockSpec((N_EXPERTS, LANES), lambda i: (0, 0))
    return pl.pallas_call(
        _router_kernel,
        grid=(t // tn,),
        in_specs=[pl.BlockSpec((N_EXPERTS, tn), lambda i: (0, i)), cnt,
                  pl.BlockSpec((tn, tn), lambda i: (0, 0))],
        out_specs=[out, out, out, cnt],
        out_shape=[jax.ShapeDtypeStruct((TOP_K, t), jnp.int32), jax.ShapeDtypeStruct((TOP_K, t), _F32),
                   jax.ShapeDtypeStruct((TOP_K, t), jnp.int32),
                   jax.ShapeDtypeStruct((N_EXPERTS, LANES), _F32)],
        compiler_params=_cparams(("arbitrary",)),
        name="router_topk",
    )(logits_t, bias_rows, tri)


def _sc_mesh():
    return plsc.VectorSubcoreMesh(core_axis_name="c", subcore_axis_name="s")


def _sc_dispatch(x_half_rows, dest_half_rows, n_out):
    n_rows = x_half_rows.shape[0]

    @pl.kernel(out_type=jax.ShapeDtypeStruct((n_out, SC_ROW), x_half_rows.dtype), mesh=_sc_mesh(),
               scratch_types=[])
    def scatter_rows(x_hbm, d_hbm, o_hbm):
        def body(x_vmem, i_vmem):
            for j in range(TOP_K):
                pltpu.sync_copy(x_vmem, o_hbm.at[i_vmem.at[j]])

        pltpu.emit_pipeline(
            body,
            grid=(n_rows // SC_WINDOW,),
            in_specs=[pl.BlockSpec((SC_WINDOW, SC_ROW), lambda i: (i, 0)),
                      pl.BlockSpec((TOP_K, SC_WINDOW), lambda i: (0, i))],
            out_specs=[],
            core_axis_name=("c", "s"),
            dimension_semantics=(pltpu.PARALLEL,),
        )(x_hbm, d_hbm)

    return scatter_rows(x_half_rows, dest_half_rows)


def _sc_gather(y_half_rows, idx):
    n = idx.shape[1]

    @pl.kernel(out_type=jax.ShapeDtypeStruct((n, SC_ROW), y_half_rows.dtype), mesh=_sc_mesh(),
               scratch_types=[])
    def gather_rows(y_hbm, i_hbm, o_hbm):
        def body(i_vmem, o_vmem):
            pltpu.sync_copy(y_hbm.at[i_vmem.at[0]], o_vmem)

        pltpu.emit_pipeline(
            body,
            grid=(n // SC_WINDOW,),
            in_specs=[pl.BlockSpec((1, SC_WINDOW), lambda i: (0, i))],
            out_specs=[pl.BlockSpec((SC_WINDOW, SC_ROW), lambda i: (i, 0))],
            core_axis_name=("c", "s"),
            dimension_semantics=(pltpu.PARALLEL,),
        )(i_hbm, o_hbm)

    return gather_rows(y_half_rows, idx)


def _dispatch_plan(top_e, rank, counts, n_blocks):
    padded = (counts + MOE_BLOCK - 1) // MOE_BLOCK * MOE_BLOCK
    pad_end = jnp.cumsum(padded)
    pad_start = pad_end - padded
    onehot = top_e[:, :, None] == jnp.arange(N_EXPERTS, dtype=jnp.int32)[None, None, :]
    dest = rank + jnp.sum(jnp.where(onehot, pad_start[None, None, :], 0), axis=-1)
    blk_start = jnp.arange(n_blocks, dtype=jnp.int32) * MOE_BLOCK
    blk_e = jnp.minimum(jnp.sum(pad_end[None, :] <= blk_start[:, None], axis=1), N_EXPERTS - 1)
    n_used = pad_end[-1] // MOE_BLOCK
    return dest, jnp.concatenate([blk_e.astype(jnp.int32), n_used[None].astype(jnp.int32)])


def _swiglu_packed(u0, u1, wg_ref, wu_ref, wd_ref):
    x = jnp.concatenate(_unpack_quarters(u0, u1), axis=1).astype(_BF16)
    a = jnp.dot(x, wg_ref[...].astype(_BF16), preferred_element_type=_F32)
    u = jnp.dot(x, wu_ref[...].astype(_BF16), preferred_element_type=_F32)
    act = (a * jax.nn.sigmoid(a) * u).astype(_BF16)
    return jnp.dot(act, wd_ref[...].astype(_BF16), preferred_element_type=_F32)


def _expert_kernel(plan_ref, x_ref, wg_ref, wu_ref, wd_ref, y_ref):
    @pl.when(pl.program_id(0) < plan_ref[pl.num_programs(0)])
    def _():
        y_ref[...] = _pack_quarters(_swiglu_packed(x_ref[0], x_ref[1], wg_ref, wu_ref, wd_ref))


def _experts(plan, x_sorted, wg, wu, wd):
    _, n_rows, q = x_sorted.shape
    n_blocks = n_rows // MOE_BLOCK
    d = 4 * q
    rows = pl.BlockSpec((2, MOE_BLOCK, q), lambda i, plan: (0, i, 0))
    grid_spec = pltpu.PrefetchScalarGridSpec(
        num_scalar_prefetch=1,
        grid=(n_blocks,),
        in_specs=[rows,
                  pl.BlockSpec((None, d, D_EXPERT), lambda i, plan: (plan[i], 0, 0)),
                  pl.BlockSpec((None, d, D_EXPERT), lambda i, plan: (plan[i], 0, 0)),
                  pl.BlockSpec((None, D_EXPERT, d), lambda i, plan: (plan[i], 0, 0))],
        out_specs=rows,
    )
    return pl.pallas_call(
        _expert_kernel,
        grid_spec=grid_spec,
        out_shape=jax.ShapeDtypeStruct((2, n_rows, q), jnp.uint32),
        compiler_params=_cparams(("arbitrary",)),
        name="moe_routed_experts",
    )(plan, x_sorted, wg, wu, wd)


def _combine_kernel(y_ref, gate_ref, h_ref, x1_ref, g2_ref, fg_ref, sg_ref, su_ref, sd_ref, o_ref):
    gate = gate_ref[...]
    acc = None
    for j in range(TOP_K):
        gj = gate[:, j:j + 1]
        parts = [gj * p for p in _unpack_quarters(y_ref[j, 0], y_ref[j, 1])]
        acc = parts if acc is None else [a + p for a, p in zip(acc, parts)]
    moe = jnp.concatenate(acc, axis=1) + _swiglu_packed(h_ref[0], h_ref[1], sg_ref, su_ref, sd_ref)
    x2 = x1_ref[0] + g2_ref[0] * moe
    o_ref[0] = x2 * lax.rsqrt(jnp.mean(x2 * x2, axis=-1, keepdims=True) + EPS) * fg_ref[...]


def _combine(y_rows, gate_tk, h2_packed, x1, g2, final_g, sg, su, sd, tm):
    bsz, l, d = x1.shape
    tm = min(tm, l)
    nt = l // tm
    q = d // 4
    const = lambda b, i: (0, 0)
    return pl.pallas_call(
        _combine_kernel,
        grid=(bsz, nt),
        in_specs=[pl.BlockSpec((TOP_K, 2, tm, q), lambda b, i: (0, 0, b * nt + i, 0)),
                  pl.BlockSpec((tm, TOP_K), lambda b, i: (b * nt + i, 0)),
                  pl.BlockSpec((2, tm, q), lambda b, i: (0, b * nt + i, 0)),
                  pl.BlockSpec((1, tm, d), lambda b, i: (b, i, 0)),
                  pl.BlockSpec((1, 1, d), lambda b, i: (b, 0, 0)),
                  pl.BlockSpec((1, d), const),
                  pl.BlockSpec((d, D_EXPERT), const),
                  pl.BlockSpec((d, D_EXPERT), const),
                  pl.BlockSpec((D_EXPERT, d), const)],
        out_specs=pl.BlockSpec((1, tm, d), lambda b, i: (b, i, 0)),
        out_shape=jax.ShapeDtypeStruct((bsz, l, d), _F32),
        compiler_params=_cparams(("parallel", "parallel")),
        name="moe_combine_shared_final_norm",
    )(y_rows, gate_tk, h2_packed, x1, g2, final_g.reshape(1, d), sg, su, sd)


def _to_chain_lanes(t):
    lead, (b, l, _) = t.shape[:-3], t.shape[-3:]
    nl = len(lead)
    t = t.reshape(*lead, b, l, RW_HEADS, RW_HEAD_DIM)
    t = t.transpose(*range(nl), nl + 1, nl + 3, nl, nl + 2)
    return t.reshape(*lead, l, RW_HEAD_DIM, b * RW_HEADS)


def _from_chain_lanes(t, b):
    l = t.shape[1]
    return t.reshape(2, l, RW_HEAD_DIM, b, RW_HEADS).transpose(0, 3, 1, 4, 2).reshape(2, b, l, RW_WIDTH)


def _scan_operands(prep):
    r, kk, v, w, kd, bb = prep[:6]
    return [_to_chain_lanes(t) for t in (r, kk, v)], [_to_chain_lanes(t) for t in (w, kd, bb)]


def kernel(x, c, ctx, c_ctx, w_mod, b_mod, norm1_g, norm2_g, w_in, rw_conv, rw_w0, rw_w2, rw_a0, rw_a2,
           rw_g2, rw_k_k, rw_k_a, rw_r_k, rw_lnx_g, rw_lnx_b, hg_lb_logits, hg_norm_g, w_out, router_w,
           router_b, exp_w_gate, exp_w_up, exp_w_down, sh_w_gate, sh_w_up, sh_w_down, final_norm_g):
    bsz, seq, d = x.shape
    assert w_mod.shape[0] == 1 and bsz * RW_HEADS == LANES
    lyr = 0

    c_all = jnp.concatenate([c, c_ctx[None, :], jnp.zeros((SUBLANES - 1, d), _F32)], axis=0)
    mod = _modulation(c_all, w_mod[lyr], b_mod[lyr])
    sh1, sc1, g1, sh2, sc2, g2 = [m[:, None, :] for m in jnp.split(mod[:bsz], N_MOD, axis=-1)]
    mod_ctx = jnp.broadcast_to(mod[bsz][None, None, :], (bsz, 1, N_MOD * d))
    csh1, csc1 = mod_ctx[..., :d], mod_ctx[..., d:2 * d]

    w_in_bf16 = w_in[lyr].astype(_BF16)
    rkv_lat, lora_lat, hg_lat = _inproj(x, norm1_g[lyr], sc1, sh1, w_in_bf16, tm=256)
    rkv_ctx, lora_ctx, hg_ctx = _inproj(ctx, norm1_g[lyr], csc1, csh1, w_in_bf16, tm=256)

    kern9 = rw_conv[lyr].reshape(CONV_K * CONV_K, RKV_W)
    lane_head = jnp.arange(RW_WIDTH) // RW_HEAD_DIM
    seg = (lane_head[:, None] == lane_head[None, :]).astype(_BF16)
    per_dir_split = lambda t: jnp.swapaxes(_split_bf16(t), 0, 1)
    rw_p = (kern9, rw_w0[lyr], per_dir_split(rw_w2[lyr]), rw_a0[lyr], per_dir_split(rw_a2[lyr]),
            _split_bf16(rw_g2[lyr]), rw_k_k[lyr], rw_k_a[lyr], rw_r_k[lyr], seg)
    prep_lat = _rw_prep(rkv_lat, lora_lat, *rw_p, tq=512, grid_conv=True)
    prep_ctx = _rw_prep(rkv_ctx, lora_ctx, *rw_p, tq=256, grid_conv=False)
    s_rw = _rwkv_scan(*_scan_operands(prep_ctx), None, tt=32, emit=False)
    y_dirs = _from_chain_lanes(_rwkv_scan(*_scan_operands(prep_lat), s_rw, tt=32, emit=True), bsz)

    lb = jnp.cumsum(jax.nn.softmax(hg_lb_logits.astype(_F32), axis=1), axis=1)[:, lyr]
    o_dirs = []
    for dd, rev in enumerate((False, True)):
        s_hg = _hgrn_scan(hg_ctx, lb[dd:dd + 1], None, tb=256, emit=False, reverse=rev)
        o_dirs.append(_hgrn_scan(hg_lat, lb[dd:dd + 1], s_hg, tb=256, emit=True, reverse=rev))

    perm = lambda t: t.reshape(N_GROUPS, GROUP_SIZE, -1).transpose(1, 0, 2).reshape(N_EXPERTS, -1)
    router_wt = perm(router_w[lyr].T)
    bias_rows = jnp.broadcast_to(perm(router_b[lyr][:, None]), (N_EXPERTS, LANES))
    hg_norm_row = jnp.tile(hg_norm_g[lyr], HG_HEADS).reshape(1, HG_WIDTH)
    x1, h2, logits_t = _post(y_dirs, prep_lat[6], prep_lat[7], rw_lnx_g[lyr], rw_lnx_b[lyr], o_dirs, hg_lat,
                             hg_norm_row, seg, x, w_out[lyr].astype(_BF16), g1, norm2_g[lyr], sc2, sh2,
                             router_wt, tm=512)
    top_e, gate, rank, counts_rows = _router(logits_t, bias_rows, tn=512)
    counts = counts_rows[:, 0].astype(jnp.int32).reshape(GROUP_SIZE, N_GROUPS).T.reshape(N_EXPERTS)

    n_tok = bsz * seq
    n_blocks = n_tok * TOP_K // MOE_BLOCK + N_EXPERTS
    n_slots = n_blocks * MOE_BLOCK
    dest, plan = _dispatch_plan(top_e, rank, counts, n_blocks)
    dest_half = (dest[:, None, :] + jnp.array([0, n_slots], jnp.int32)[None, :, None]).reshape(TOP_K, 2 * n_tok)
    x_sorted = _sc_dispatch(h2.reshape(2 * n_tok, SC_ROW), dest_half, 2 * n_slots)
    y_sorted = _experts(plan, x_sorted.reshape(2, n_slots, SC_ROW),
                        exp_w_gate[lyr], exp_w_up[lyr], exp_w_down[lyr])
    y_rows = _sc_gather(y_sorted.reshape(2 * n_slots, SC_ROW), dest_half.reshape(1, TOP_K * 2 * n_tok))
    return _combine(y_rows.reshape(TOP_K, 2, n_tok, SC_ROW), gate.T, h2, x1, g2,
                    final_norm_g, sh_w_gate[lyr], sh_w_up[lyr], sh_w_down[lyr], tm=512)
```

```python
import functools

import jax
import jax.numpy as jnp
from jax import lax
from jax.experimental import pallas as pl
from jax.experimental.pallas import tpu as pltpu
from jax.experimental.pallas import tpu_sc as plsc

D_MODEL = 1024
GRID_W = 64
N_MOD = 6
EPS = 1e-6

RW_HEAD_DIM = 64
RW_HEADS = 8
RW_WIDTH = 512
RW_LORA = 64
RW_GATE_LORA = 128
RW_GN_EPS = 1e-5 * RW_HEAD_DIM
CONV_K = 3

HG_HEADS = 8
HG_DIM = 64
HG_WIDTH = 512
HG_CHUNK = 32
HG_PAIRS = HG_HEADS // 2
HG_SCORE_ROWS = 128

RKV_W = 3 * RW_WIDTH
LORA_W = 4 * RW_LORA + RW_GATE_LORA
HG_W = 5 * HG_WIDTH
P_IN = RKV_W + LORA_W + HG_W

N_EXPERTS = 64
N_GROUPS = 8
GROUP_SIZE = N_EXPERTS // N_GROUPS
TOPK_GROUPS = 4
TOP_K = 8
D_EXPERT = 256
ROUTE_SCALE = 2.5
MOE_BLOCK = 1024
SC_ROW = 256
SC_WINDOW = 128
MOE_SPLITS = 2

LANES = 128
SUBLANES = 8
VMEM_LIMIT = 48 * 1024 * 1024

_HI = lax.Precision.HIGHEST
_F32 = jnp.float32
_BF16 = jnp.bfloat16
_NT = (((1,), (1,)), ((), ()))
_TN = (((0,), (0,)), ((), ()))


def _cparams(sem):
    return pltpu.CompilerParams(dimension_semantics=sem, vmem_limit_bytes=VMEM_LIMIT)


def _pack_bf16_pair(lo, hi):
    return pltpu.pack_elementwise([lo, hi], packed_dtype=_BF16)


def _unpack_bf16_pair(u):
    return tuple(pltpu.unpack_elementwise(u, index=i, packed_dtype=_BF16, unpacked_dtype=_F32) for i in range(2))


def _pack_quarters(x):
    q = x.shape[1] // 4
    return jnp.stack([_pack_bf16_pair(x[:, 2 * h * q:(2 * h + 1) * q], x[:, (2 * h + 1) * q:(2 * h + 2) * q])
                      for h in range(2)])


def _unpack_quarters(u0, u1):
    return _unpack_bf16_pair(u0) + _unpack_bf16_pair(u1)


def _dir_block(d, i, nb):
    return i + d * (nb - 1 - 2 * i)


def _mod_kernel(c_ref, w_ref, b_ref, o_ref):
    c = c_ref[...]
    s = c * jax.nn.sigmoid(c)
    o_ref[...] = jnp.dot(s, w_ref[...], preferred_element_type=_F32, precision=_HI) + b_ref[...]


def _modulation(c_all, w_mod, b_mod):
    rows, d = c_all.shape
    n = w_mod.shape[1]
    tn = 1024
    return pl.pallas_call(
        _mod_kernel,
        grid=(n // tn,),
        in_specs=[pl.BlockSpec((rows, d), lambda j: (0, 0)),
                  pl.BlockSpec((d, tn), lambda j: (0, j)),
                  pl.BlockSpec((1, tn), lambda j: (0, j))],
        out_specs=pl.BlockSpec((rows, tn), lambda j: (0, j)),
        out_shape=jax.ShapeDtypeStruct((rows, n), _F32),
        compiler_params=_cparams(("parallel",)),
        name="modulation",
    )(c_all, w_mod, b_mod.reshape(1, n))


def _inproj_kernel(x_ref, g_ref, sc_ref, sh_ref, w_ref, rkv_ref, lora_ref, hg_ref):
    x = x_ref[0]
    y = x * lax.rsqrt(jnp.mean(x * x, axis=-1, keepdims=True) + EPS) * g_ref[...]
    h = y * (1.0 + sc_ref[0]) + sh_ref[0]
    u = jnp.dot(h.astype(_BF16), w_ref[...], preferred_element_type=_F32)
    rkv_ref[0] = u[:, :RKV_W]
    lora_ref[0] = u[:, RKV_W:RKV_W + LORA_W]
    hg_ref[0] = u[:, RKV_W + LORA_W:]


def _inproj(x, norm_g, sc, sh, w_in_bf16, tm):
    bsz, l, d = x.shape
    tm = min(tm, l)
    row = lambda b, i: (b, i, 0)
    per_b = lambda b, i: (b, 0, 0)
    return pl.pallas_call(
        _inproj_kernel,
        grid=(bsz, l // tm),
        in_specs=[pl.BlockSpec((1, tm, d), row),
                  pl.BlockSpec((1, d), lambda b, i: (0, 0)),
                  pl.BlockSpec((1, 1, d), per_b),
                  pl.BlockSpec((1, 1, d), per_b),
                  pl.BlockSpec((d, P_IN), lambda b, i: (0, 0))],
        out_specs=[pl.BlockSpec((1, tm, RKV_W), row),
                   pl.BlockSpec((1, tm, LORA_W), row),
                   pl.BlockSpec((1, tm, HG_W), row)],
        out_shape=[jax.ShapeDtypeStruct((bsz, l, RKV_W), _F32),
                   jax.ShapeDtypeStruct((bsz, l, LORA_W), _F32),
                   jax.ShapeDtypeStruct((bsz, l, HG_W), _F32)],
        compiler_params=_cparams(("parallel", "parallel")),
        name="inproj",
    )(x, norm_g.reshape(1, d), sc, sh, w_in_bf16)


def _seg_sum(x, seg_ref, nonneg=False):
    hi = x.astype(_BF16)
    out = jnp.dot(hi, seg_ref[...], preferred_element_type=_F32)
    if nonneg:
        return out
    lo = (x - hi.astype(_F32)).astype(_BF16)
    return out + jnp.dot(lo, seg_ref[...], preferred_element_type=_F32)


def _dot_split(x, w_ref):
    hi = x.astype(_BF16)
    lo = (x - hi.astype(_F32)).astype(_BF16)
    return (jnp.dot(hi, w_ref[0], preferred_element_type=_F32) + jnp.dot(lo, w_ref[0], preferred_element_type=_F32)
            + jnp.dot(hi, w_ref[1], preferred_element_type=_F32))


def _split_bf16(w):
    hi = w.astype(_BF16)
    return jnp.stack([hi, (w - hi.astype(_F32)).astype(_BF16)])


def _rw_prep_kernel(cur_ref, prev_ref, next_ref, lora_ref, kern_ref, w0_ref, w2_ref, a0_ref, a2_ref, g2_ref,
                    kk_ref, ka_ref, rk_ref, seg_ref,
                    r_out, kkn_out, v_out, w_out, kd_out, b_out, bonus_out, g_out, *, grid_conv):
    i = pl.program_id(1)
    tq = cur_ref.shape[0]
    hw = prev_ref.shape[0]
    w = RW_WIDTH
    pmask = jnp.where(i > 0, 1.0, 0.0)
    nmask = jnp.where(i < pl.num_programs(1) - 1, 1.0, 0.0)
    xpos = lax.broadcasted_iota(jnp.int32, (tq + 2 * hw, w), 0) % GRID_W
    conv = []
    for part in range(3):
        cs = slice(part * w, (part + 1) * w)
        xe = jnp.concatenate([prev_ref[:, cs] * pmask, cur_ref[:, cs], next_ref[:, cs] * nmask], axis=0)
        xl = pltpu.roll(xe, 1, 0)
        xr = pltpu.roll(xe, tq + 2 * hw - 1, 0)
        if grid_conv:
            xl = jnp.where(xpos == 0, 0.0, xl)
            xr = jnp.where(xpos == GRID_W - 1, 0.0, xr)
        acc = None
        for dy in range(CONV_K) if grid_conv else (CONV_K // 2,):
            off = hw + (dy - 1) * GRID_W
            for dx, src in enumerate((xl, xe, xr)):
                term = src[off:off + tq] * kern_ref[pl.ds(dy * CONV_K + dx, 1), cs]
                acc = term if acc is None else acc + term
        conv.append(acc)
    r, k, v = conv
    r_out[...] = r.astype(_BF16)
    v_out[...] = v.astype(_BF16)

    kk = k * kk_ref[...]
    kk = kk / jnp.maximum(jnp.sqrt(_seg_sum(kk * kk, seg_ref, nonneg=True)), 1e-12)
    kkn_out[...] = kk.astype(_BF16)

    lora = lora_ref[...]
    ksum = jnp.zeros_like(k)
    for d in range(2):
        wl = lora[:, d * RW_LORA:(d + 1) * RW_LORA]
        al = lora[:, (2 + d) * RW_LORA:(3 + d) * RW_LORA]
        z = w0_ref[pl.ds(d, 1), :] + _dot_split(jnp.tanh(wl), w2_ref.at[d])
        wlog = -jax.nn.softplus(-z) - 0.5
        w_out[d] = jnp.exp(-jnp.exp(wlog))
        a = jax.nn.sigmoid(a0_ref[pl.ds(d, 1), :] + _dot_split(al, a2_ref.at[d]))
        kd = k * (1.0 + (a - 1.0) * ka_ref[...])
        kd_out[d] = kd.astype(_BF16)
        b_out[d] = (kk * a).astype(_BF16)
        ksum = ksum + kd
    bonus_out[...] = _seg_sum(r * ksum * rk_ref[...], seg_ref) * v
    gd = lora[:, 4 * RW_LORA:]
    g_out[...] = _dot_split(jax.nn.sigmoid(gd), g2_ref)


def _rw_prep(rkv, lora, kern9, w0, w2, a0, a2, g2, k_k, k_a, r_k, seg, tq, grid_conv):
    bsz, l, _ = rkv.shape
    tq = min(tq, l)
    nb = l // tq
    hw = GRID_W
    per = tq // hw
    w = RW_WIDTH
    row = lambda b, i: (b, i, 0)
    const2 = lambda b, i: (0, 0)
    const3 = lambda b, i: (0, 0, 0)
    const4 = lambda b, i: (0, 0, 0, 0)
    tok = pl.BlockSpec((None, tq, w), row)
    tok2 = pl.BlockSpec((2, None, tq, w), lambda b, i: (0, b, i, 0))
    one = jax.ShapeDtypeStruct((bsz, l, w), _F32)
    two = jax.ShapeDtypeStruct((2, bsz, l, w), _F32)
    one_h = jax.ShapeDtypeStruct((bsz, l, w), _BF16)
    two_h = jax.ShapeDtypeStruct((2, bsz, l, w), _BF16)
    vec = lambda t: t.reshape(1, w)
    return pl.pallas_call(
        functools.partial(_rw_prep_kernel, grid_conv=grid_conv),
        grid=(bsz, nb),
        in_specs=[pl.BlockSpec((None, tq, RKV_W), row),
                  pl.BlockSpec((None, hw, RKV_W), lambda b, i: (b, jnp.maximum(i * per - 1, 0), 0)),
                  pl.BlockSpec((None, hw, RKV_W), lambda b, i: (b, jnp.minimum((i + 1) * per, l // hw - 1), 0)),
                  pl.BlockSpec((None, tq, LORA_W), row),
                  pl.BlockSpec((CONV_K * CONV_K, RKV_W), const2),
                  pl.BlockSpec((2, w), const2),
                  pl.BlockSpec((2, 2, RW_LORA, w), const4),
                  pl.BlockSpec((2, w), const2),
                  pl.BlockSpec((2, 2, RW_LORA, w), const4),
                  pl.BlockSpec((2, RW_GATE_LORA, w), const3),
                  pl.BlockSpec((1, w), const2),
                  pl.BlockSpec((1, w), const2),
                  pl.BlockSpec((1, w), const2),
                  pl.BlockSpec((w, w), const2)],
        out_specs=[tok, tok, tok, tok2, tok2, tok2, tok, tok],
        out_shape=[one_h, one_h, one_h, two, two_h, two_h, one, one],
        compiler_params=_cparams(("parallel", "parallel")),
        name="rwkv_prep_grid" if grid_conv else "rwkv_prep_seq",
    )(rkv, rkv, rkv, lora, kern9, w0, w2, a0, a2, g2, vec(k_k), vec(k_a), vec(r_k), seg)


def _rwkv_scan_kernel(*refs, tt, emit, has_init):
    r_in, kk_in, v_in, w_ref, k_in, b_in = refs[:6]
    rest = list(refs[6:])
    s0_ref = rest.pop(0) if has_init else None
    out_ref = rest.pop(0)
    s_ref, sa_ref, r_ref, kk_ref, v_ref, k_ref, b_ref = rest
    for src, dst in ((r_in, r_ref), (kk_in, kk_ref), (v_in, v_ref), (k_in, k_ref), (b_in, b_ref)):
        dst[...] = src[...].astype(_F32)
    d = pl.program_id(0)
    i = pl.program_id(1)
    n = RW_HEAD_DIM

    @pl.when(i == 0)
    def _():
        s_ref[...] = s0_ref[...] if has_init else jnp.zeros_like(s_ref)

    hv = n // 2
    halves = [pl.ds(h * hv, hv) for h in range(2)]

    def project(t, vs):
        acc = [jnp.zeros((hv, LANES), _F32), jnp.zeros((hv, LANES), _F32)]
        for k in range(n):
            acc[k % 2] = acc[k % 2] + s_ref[k, vs, :] * kk_ref[t, pl.ds(k, 1), :]
        return -(acc[0] + acc[1])

    first = d * (tt - 1)
    for vs in halves:
        sa_ref[vs, :] = project(first, vs)

    def step(j, carry):
        t = j + d * (tt - 1 - 2 * j)
        tn = jnp.clip(t + 1 - 2 * d, 0, tt - 1)
        for vs in halves:
            sa = sa_ref[vs, :]
            vv = v_ref[t, vs, :]
            y = [jnp.zeros((hv, LANES), _F32), jnp.zeros((hv, LANES), _F32)]
            nsa = [jnp.zeros((hv, LANES), _F32), jnp.zeros((hv, LANES), _F32)]
            for k in range(n):
                row = pl.ds(k, 1)
                s_new = s_ref[k, vs, :] * w_ref[t, row, :] + sa * b_ref[t, row, :] + vv * k_ref[t, row, :]
                s_ref[k, vs, :] = s_new
                nsa[k % 2] = nsa[k % 2] + s_new * kk_ref[tn, row, :]
                if emit:
                    y[k % 2] = y[k % 2] + s_new * r_ref[t, row, :]
            sa_ref[vs, :] = -(nsa[0] + nsa[1])
            if emit:
                out_ref[t, vs, :] = y[0] + y[1]
        return carry

    lax.fori_loop(0, tt, step, 0)

    if not emit:
        @pl.when(i == pl.num_programs(1) - 1)
        def _():
            out_ref[...] = s_ref[...]


def _rwkv_scan(shared, per_dir, s0, tt, emit):
    l, n, c = shared[0].shape
    nb = l // tt
    sh_spec = pl.BlockSpec((tt, n, c), lambda d, i: (_dir_block(d, i, nb), 0, 0))
    pd_spec = pl.BlockSpec((None, tt, n, c), lambda d, i: (d, _dir_block(d, i, nb), 0, 0))
    st_spec = pl.BlockSpec((None, n, n, c), lambda d, i: (d, 0, 0, 0))
    in_specs = [sh_spec] * 3 + [pd_spec] * 3
    args = list(shared) + list(per_dir)
    if s0 is not None:
        in_specs.append(st_spec)
        args.append(s0)
    if emit:
        out_spec, out_shape = pd_spec, jax.ShapeDtypeStruct((2, l, n, c), _F32)
    else:
        out_spec, out_shape = st_spec, jax.ShapeDtypeStruct((2, n, n, c), _F32)
    return pl.pallas_call(
        functools.partial(_rwkv_scan_kernel, tt=tt, emit=emit, has_init=s0 is not None),
        grid=(2, nb),
        in_specs=in_specs,
        out_specs=out_spec,
        out_shape=out_shape,
        scratch_shapes=[pltpu.VMEM((n, n, c), _F32), pltpu.VMEM((n, c), _F32)] + [pltpu.VMEM((tt, n, c), _F32)] * 5,
        compiler_params=_cparams(("parallel", "arbitrary")),
        name="rwkv_scan_emit" if emit else "rwkv_scan_state",
    )(*args)


def _hgrn_kernel(*refs, n_chunks, emit, has_init, reverse):
    q_ref, f_ref, i_ref, lb_ref, mats_ref = refs[:5]
    rest = list(refs[5:])
    s0_ref = rest.pop(0) if has_init else None
    out_ref = rest.pop(0)
    st_ref, upd_ref, ent_ref = rest
    c = HG_CHUNK
    tb = n_chunks * c
    blk = pl.program_id(1)

    @pl.when(blk == 0)
    def _():
        st_ref[...] = s0_ref[...] if has_init else jnp.zeros_like(st_ref)

    lb = lb_ref[...]
    omlb = 1.0 - lb
    fr = f_ref[...]
    logf = jnp.log(lb + omlb * jax.nn.sigmoid(fr))
    kf = omlb * jax.nn.sigmoid(-fr)
    hi = logf.astype(_BF16)
    lo = (logf - hi.astype(_F32)).astype(_BF16)
    sums = (jnp.dot(mats_ref[...], hi, preferred_element_type=_F32)
            + jnp.dot(mats_ref[...], lo, preferred_element_type=_F32))
    cum, tot, mid = sums[:tb], sums[tb:2 * tb], sums[2 * tb:]
    dec = jnp.exp(tot)
    kdec = (kf * jnp.exp(tot - cum)).astype(_BF16)
    vb = i_ref[...].astype(_BF16)
    same_head = ((lax.broadcasted_iota(jnp.int32, (LANES, LANES), 0) < HG_DIM)
                 == (lax.broadcasted_iota(jnp.int32, (LANES, LANES), 1) < HG_DIM))
    pairs = [slice(p * LANES, (p + 1) * LANES) for p in range(HG_PAIRS)]
    chunks = [slice(j * c, (j + 1) * c) for j in range(n_chunks)]

    if emit:
        q = q_ref[...]
        qs = q * jax.nn.sigmoid(q) * (HG_DIM ** -0.5)
        qd = qs * jnp.exp(cum - mid)
        kd = (kf * jnp.exp(mid - cum)).astype(_BF16)
        qe = (qs * jnp.exp(cum)).astype(_BF16)
        sub = min(tb, HG_SCORE_ROWS)
        head0 = lax.broadcasted_iota(jnp.int32, (sub, LANES), 1) < HG_DIM
        for g in range(tb // sub):
            rows = slice(g * sub, (g + 1) * sub)
            causal = mats_ref[rows, rows] > 0
            causal2 = jnp.concatenate([causal, causal], axis=0)
            for ls in pairs:
                qd_p = qd[rows, ls]
                q2 = jnp.concatenate([jnp.where(head0, qd_p, 0.0), jnp.where(head0, 0.0, qd_p)], axis=0)
                sc = lax.dot_general(q2.astype(_BF16), kd[rows, ls], _NT, preferred_element_type=_F32)
                sc = jnp.where(causal2, sc, 0.0)
                oi = jnp.dot(sc.astype(_BF16), vb[rows, ls], preferred_element_type=_F32)
                out_ref[rows, ls] = jnp.where(head0, oi[:sub], oi[sub:])

    for j, rows in enumerate(chunks):
        for p, ls in enumerate(pairs):
            upd = lax.dot_general(vb[rows, ls], kdec[rows, ls], _TN, preferred_element_type=_F32)
            upd_ref[j, p] = jnp.where(same_head, upd, 0.0)

    order = list(range(n_chunks))[::-1] if reverse else list(range(n_chunks))
    for p, ls in enumerate(pairs):
        st = st_ref[p]
        for j in order:
            if emit:
                ent_ref[j, p] = st.astype(_BF16)
            st = st * dec[j * c:j * c + 1, ls] + upd_ref[j, p]
        st_ref[p] = st

    if emit:
        for j, rows in enumerate(chunks):
            for p, ls in enumerate(pairs):
                out_ref[rows, ls] += lax.dot_general(qe[rows, ls], ent_ref[j, p], _NT,
                                                     preferred_element_type=_F32)
    else:
        @pl.when(blk == pl.num_programs(1) - 1)
        def _():
            out_ref[...] = st_ref[...]


def _hgrn_mats(tb, reverse):
    t = jnp.arange(tb)
    same = (t[:, None] // HG_CHUNK) == (t[None, :] // HG_CHUNK)
    ref_pos = (t // HG_CHUNK) * HG_CHUNK + HG_CHUNK // 2
    if reverse:
        cum = same & (t[None, :] >= t[:, None])
        mid = same & (t[None, :] >= ref_pos[:, None])
    else:
        cum = same & (t[None, :] <= t[:, None])
        mid = same & (t[None, :] <= ref_pos[:, None])
    return jnp.concatenate([cum, same, mid], axis=0).astype(_BF16)


def _hgrn_scan(hg, lb, s0, tb, emit, reverse):
    bsz, l, _ = hg.shape
    tb = min(tb, l)
    nb = l // tb
    w = HG_WIDTH
    n_chunks = tb // HG_CHUNK
    tblk = (lambda i: nb - 1 - i) if reverse else (lambda i: i)
    col = lambda j: pl.BlockSpec((None, tb, w), lambda b, i: (b, tblk(i), j))
    const = lambda b, i: (0, 0)
    st_spec = pl.BlockSpec((None, HG_PAIRS, LANES, LANES), lambda b, i: (b, 0, 0, 0))
    in_specs = [col(0), col(2 if reverse else 1), col(3), pl.BlockSpec((1, w), const),
                pl.BlockSpec((3 * tb, tb), const)]
    args = [hg, hg, hg, lb, _hgrn_mats(tb, reverse)]
    if s0 is not None:
        in_specs.append(st_spec)
        args.append(s0)
    if emit:
        out_spec = pl.BlockSpec((None, tb, w), lambda b, i: (b, tblk(i), 0))
        out_shape = jax.ShapeDtypeStruct((bsz, l, w), _F32)
    else:
        out_spec = st_spec
        out_shape = jax.ShapeDtypeStruct((bsz, HG_PAIRS, LANES, LANES), _F32)
    return pl.pallas_call(
        functools.partial(_hgrn_kernel, n_chunks=n_chunks, emit=emit, has_init=s0 is not None,
                          reverse=reverse),
        grid=(bsz, nb),
        in_specs=in_specs,
        out_specs=out_spec,
        out_shape=out_shape,
        scratch_shapes=[pltpu.VMEM((HG_PAIRS, LANES, LANES), _F32),
                        pltpu.VMEM((n_chunks, HG_PAIRS, LANES, LANES), _F32),
                        pltpu.VMEM((n_chunks, HG_PAIRS, LANES, LANES), _BF16)],
        compiler_params=_cparams(("parallel", "arbitrary")),
        name=("hgrn_emit" if emit else "hgrn_state") + ("_bwd" if reverse else "_fwd"),
    )(*args)


def _post_kernel(y_ref, bonus_ref, grw_ref, lng_ref, lnb_ref, of_ref, ob_ref, ghg_ref, hgn_ref, seg_ref,
                 x_ref, w_ref, g1_ref, n2_ref, sc_ref, sh_ref, rw_ref, x1_ref, h2_ref, lg_ref):
    inv_n = 1.0 / RW_HEAD_DIM
    ysum = y_ref[0] + y_ref[1]
    mu = _seg_sum(ysum, seg_ref) * inv_n
    dlt = ysum - mu
    var = _seg_sum(dlt * dlt, seg_ref, nonneg=True) * inv_n
    yn = dlt * lax.rsqrt(var + RW_GN_EPS) * lng_ref[...] + lnb_ref[...] + bonus_ref[...]
    y_rw = yn * grw_ref[...]
    o = of_ref[...] + ob_ref[...]
    o = o * lax.rsqrt(_seg_sum(o * o, seg_ref, nonneg=True) * inv_n + EPS) * hgn_ref[...]
    g = ghg_ref[...]
    y_hg = o * (g * jax.nn.sigmoid(g))
    y = (jnp.dot(y_rw.astype(_BF16), w_ref[:RW_WIDTH, :], preferred_element_type=_F32)
         + jnp.dot(y_hg.astype(_BF16), w_ref[RW_WIDTH:, :], preferred_element_type=_F32))
    x1 = x_ref[0] + g1_ref[0] * y
    x1_ref[0] = x1
    h2 = x1 * lax.rsqrt(jnp.mean(x1 * x1, axis=-1, keepdims=True) + EPS) * n2_ref[...]
    h2 = h2 * (1.0 + sc_ref[0]) + sh_ref[0]
    h2_ref[...] = _pack_quarters(h2)
    lg_ref[...] = lax.dot_general(rw_ref[...], h2, _NT, preferred_element_type=_F32, precision=_HI)


def _post(y_dirs, bonus, g_rw, lnx_g, lnx_b, o_dirs, hg, hg_norm_row, seg,
          x, w_out_bf16, g1, norm2_g, sc2, sh2, router_wt, tm):
    bsz, l, d = x.shape
    tm = min(tm, l)
    nt = l // tm
    w = RW_WIDTH
    row = lambda b, i: (b, i, 0)
    per_b = lambda b, i: (b, 0, 0)
    const = lambda b, i: (0, 0)
    tok = pl.BlockSpec((None, tm, w), row)
    tok2 = pl.BlockSpec((2, None, tm, w), lambda b, i: (0, b, i, 0))
    vec = pl.BlockSpec((1, w), const)
    return pl.pallas_call(
        _post_kernel,
        grid=(bsz, nt),
        in_specs=[tok2, tok, tok, vec, vec, tok, tok,
                  pl.BlockSpec((None, tm, w), lambda b, i: (b, i, 4)),
                  vec,
                  pl.BlockSpec((w, w), const),
                  pl.BlockSpec((1, tm, d), row),
                  pl.BlockSpec((d, d), const),
                  pl.BlockSpec((1, 1, d), per_b),
                  pl.BlockSpec((1, d), const),
                  pl.BlockSpec((1, 1, d), per_b),
                  pl.BlockSpec((1, 1, d), per_b),
                  pl.BlockSpec((N_EXPERTS, d), const)],
        out_specs=[pl.BlockSpec((1, tm, d), row),
                   pl.BlockSpec((2, tm, d // 4), lambda b, i: (0, b * nt + i, 0)),
                   pl.BlockSpec((N_EXPERTS, tm), lambda b, i: (0, b * nt + i))],
        out_shape=[jax.ShapeDtypeStruct((bsz, l, d), _F32),
                   jax.ShapeDtypeStruct((2, bsz * l, d // 4), jnp.uint32),
                   jax.ShapeDtypeStruct((N_EXPERTS, bsz * l), _F32)],
        compiler_params=_cparams(("parallel", "parallel")),
        name="outproj_norm2_router",
    )(y_dirs, bonus, g_rw, lnx_g.reshape(1, w), lnx_b.reshape(1, w), o_dirs[0], o_dirs[1], hg, hg_norm_row, seg,
      x, w_out_bf16, g1, norm2_g.reshape(1, d), sc2, sh2, router_wt)


def _sublane_all(x, op):
    for s in (4, 2, 1):
        x = op(x, pltpu.roll(x, s, 0))
    return x


def _router_kernel(lg_ref, bias_ref, tri_ref, e_ref, g_ref, rank_ref, cnt_ref):
    ng, gs = N_GROUPS, GROUP_SIZE
    tn = lg_ref.shape[1]
    neg = -jnp.inf
    gidx = lax.broadcasted_iota(jnp.int32, (ng, tn), 0)
    bias = jnp.concatenate([bias_ref[...]] * (tn // LANES), axis=1)
    scores = [jax.nn.sigmoid(lg_ref[j * ng:(j + 1) * ng, :]) for j in range(gs)]
    sel = [scores[j] + bias[j * ng:(j + 1) * ng, :] for j in range(gs)]

    m1 = functools.reduce(jnp.maximum, sel)
    cnt = functools.reduce(jnp.add, [(s == m1).astype(_F32) for s in sel])
    m2 = functools.reduce(jnp.maximum, [jnp.where(s < m1, s, neg) for s in sel])
    gscore = m1 + jnp.where(cnt >= 2.0, m1, m2)

    rank = jnp.zeros((ng, tn), jnp.int32)
    for s in range(1, ng):
        other = pltpu.roll(gscore, s, 0)
        beats = jnp.where(other > gscore, 1, jnp.where((other == gscore) & (gidx >= s), 1, 0))
        rank = rank + beats
    gsel = rank < TOPK_GROUPS

    key = [jnp.where(gsel, s, neg) for s in sel]
    eidx = [gidx * gs + j for j in range(gs)]
    avail = [jnp.ones((ng, tn), jnp.int32) for _ in range(gs)]
    top_e = jnp.zeros((TOP_K, tn), jnp.int32)
    top_s = jnp.zeros((TOP_K, tn), _F32)
    for r in range(TOP_K):
        cur = functools.reduce(jnp.maximum, [jnp.where(avail[j] > 0, key[j], neg) for j in range(gs)])
        mx = _sublane_all(cur, jnp.maximum)
        cand = [jnp.where((avail[j] > 0) & (key[j] == mx), eidx[j], N_EXPERTS) for j in range(gs)]
        mn = _sublane_all(functools.reduce(jnp.minimum, cand), jnp.minimum)
        picked = functools.reduce(jnp.add, [jnp.where(eidx[j] == mn, scores[j], 0.0) for j in range(gs)])
        avail = [jnp.where(eidx[j] == mn, 0, avail[j]) for j in range(gs)]
        top_e = jnp.where(gidx == r, mn, top_e)
        top_s = jnp.where(gidx == r, _sublane_all(picked, jnp.add), top_s)
    den = _sublane_all(top_s, jnp.add)
    e_ref[...] = top_e
    g_ref[...] = top_s / den * ROUTE_SCALE

    @pl.when(pl.program_id(0) == 0)
    def _():
        cnt_ref[...] = jnp.zeros_like(cnt_ref)

    taken = jnp.concatenate([jnp.where(a == 0, 1.0, 0.0) for a in avail], axis=0)
    before = jnp.dot(taken.astype(_BF16), tri_ref[...], preferred_element_type=_F32) - taken
    before = before + jnp.concatenate([cnt_ref[...]] * (tn // LANES), axis=1)
    rank = jnp.zeros((TOP_K, tn), _F32)
    for r in range(TOP_K):
        e_r = jnp.broadcast_to(top_e[r:r + 1, :], (ng, tn))
        hit = functools.reduce(jnp.add, [jnp.where(eidx[j] == e_r, before[j * ng:(j + 1) * ng, :], 0.0)
                                         for j in range(gs)])
        rank = jnp.where(gidx == r, _sublane_all(hit, jnp.add), rank)
    rank_ref[...] = rank.astype(jnp.int32)
    ones = jnp.ones((tn, LANES), _BF16)
    cnt_ref[...] += jnp.dot(taken.astype(_BF16), ones, preferred_element_type=_F32)


def _router(logits_t, bias_rows, tn, t0, t):
    assert TOP_K == N_GROUPS == SUBLANES
    tn = min(tn, t)
    blk0 = t0 // tn
    tri = (jnp.arange(tn)[:, None] <= jnp.arange(tn)[None, :]).astype(_BF16)
    out = pl.BlockSpec((TOP_K, tn), lambda i: (0, i))
    cnt = pl.BlockSpec((N_EXPERTS, LANES), lambda i: (0, 0))
    return pl.pallas_call(
        _router_kernel,
        grid=(t // tn,),
        in_specs=[pl.BlockSpec((N_EXPERTS, tn), lambda i: (0, blk0 + i)), cnt,
                  pl.BlockSpec((tn, tn), lambda i: (0, 0))],
        out_specs=[out, out, out, cnt],
        out_shape=[jax.ShapeDtypeStruct((TOP_K, t), jnp.int32), jax.ShapeDtypeStruct((TOP_K, t), _F32),
                   jax.ShapeDtypeStruct((TOP_K, t), jnp.int32),
                   jax.ShapeDtypeStruct((N_EXPERTS, LANES), _F32)],
        compiler_params=_cparams(("arbitrary",)),
        name="router_topk",
    )(logits_t, bias_rows, tri)


def _sc_mesh():
    return plsc.VectorSubcoreMesh(core_axis_name="c", subcore_axis_name="s")


def _sc_dispatch(x_half_rows, dest_half_rows, n_out):
    n_rows = x_half_rows.shape[0]

    @pl.kernel(out_type=jax.ShapeDtypeStruct((n_out, SC_ROW), x_half_rows.dtype), mesh=_sc_mesh(),
               scratch_types=[])
    def scatter_rows(x_hbm, d_hbm, o_hbm):
        def body(x_vmem, i_vmem):
            for j in range(TOP_K):
                pltpu.sync_copy(x_vmem, o_hbm.at[i_vmem.at[j]])

        pltpu.emit_pipeline(
            body,
            grid=(n_rows // SC_WINDOW,),
            in_specs=[pl.BlockSpec((SC_WINDOW, SC_ROW), lambda i: (i, 0)),
                      pl.BlockSpec((TOP_K, SC_WINDOW), lambda i: (0, i))],
            out_specs=[],
            core_axis_name=("c", "s"),
            dimension_semantics=(pltpu.PARALLEL,),
        )(x_hbm, d_hbm)

    return scatter_rows(x_half_rows, dest_half_rows)


def _sc_gather(y_half_rows, idx):
    n = idx.shape[1]

    @pl.kernel(out_type=jax.ShapeDtypeStruct((n, SC_ROW), y_half_rows.dtype), mesh=_sc_mesh(),
               scratch_types=[])
    def gather_rows(y_hbm, i_hbm, o_hbm):
        def body(i_vmem, o_vmem):
            pltpu.sync_copy(y_hbm.at[i_vmem.at[0]], o_vmem)

        pltpu.emit_pipeline(
            body,
            grid=(n // SC_WINDOW,),
            in_specs=[pl.BlockSpec((1, SC_WINDOW), lambda i: (0, i))],
            out_specs=[pl.BlockSpec((SC_WINDOW, SC_ROW), lambda i: (i, 0))],
            core_axis_name=("c", "s"),
            dimension_semantics=(pltpu.PARALLEL,),
        )(i_hbm, o_hbm)

    return gather_rows(y_half_rows, idx)


def _dispatch_plan(top_e, rank, counts, n_blocks):
    padded = (counts + MOE_BLOCK - 1) // MOE_BLOCK * MOE_BLOCK
    pad_end = jnp.cumsum(padded)
    pad_start = pad_end - padded
    onehot = top_e[:, :, None] == jnp.arange(N_EXPERTS, dtype=jnp.int32)[None, None, :]
    dest = rank + jnp.sum(jnp.where(onehot, pad_start[None, None, :], 0), axis=-1)
    blk_start = jnp.arange(n_blocks, dtype=jnp.int32) * MOE_BLOCK
    blk_e = jnp.minimum(jnp.sum(pad_end[None, :] <= blk_start[:, None], axis=1), N_EXPERTS - 1)
    n_used = pad_end[-1] // MOE_BLOCK
    return dest, jnp.concatenate([blk_e.astype(jnp.int32), n_used[None].astype(jnp.int32)])


def _swiglu_packed(u0, u1, wg_ref, wu_ref, wd_ref):
    x = jnp.concatenate(_unpack_quarters(u0, u1), axis=1).astype(_BF16)
    a = jnp.dot(x, wg_ref[...].astype(_BF16), preferred_element_type=_F32)
    u = jnp.dot(x, wu_ref[...].astype(_BF16), preferred_element_type=_F32)
    act = (a * jax.nn.sigmoid(a) * u).astype(_BF16)
    return jnp.dot(act, wd_ref[...].astype(_BF16), preferred_element_type=_F32)


def _expert_kernel(plan_ref, x_ref, wg_ref, wu_ref, wd_ref, y_ref):
    @pl.when(pl.program_id(0) < plan_ref[pl.num_programs(0)])
    def _():
        y_ref[...] = _pack_quarters(_swiglu_packed(x_ref[0], x_ref[1], wg_ref, wu_ref, wd_ref))


def _experts(plan, x_sorted, wg, wu, wd):
    _, n_rows, q = x_sorted.shape
    n_blocks = n_rows // MOE_BLOCK
    d = 4 * q
    rows = pl.BlockSpec((2, MOE_BLOCK, q), lambda i, plan: (0, i, 0))
    grid_spec = pltpu.PrefetchScalarGridSpec(
        num_scalar_prefetch=1,
        grid=(n_blocks,),
        in_specs=[rows,
                  pl.BlockSpec((None, d, D_EXPERT), lambda i, plan: (plan[i], 0, 0)),
                  pl.BlockSpec((None, d, D_EXPERT), lambda i, plan: (plan[i], 0, 0)),
                  pl.BlockSpec((None, D_EXPERT, d), lambda i, plan: (plan[i], 0, 0))],
        out_specs=rows,
    )
    return pl.pallas_call(
        _expert_kernel,
        grid_spec=grid_spec,
        out_shape=jax.ShapeDtypeStruct((2, n_rows, q), jnp.uint32),
        compiler_params=_cparams(("arbitrary",)),
        name="moe_routed_experts",
    )(plan, x_sorted, wg, wu, wd)


def _combine_kernel(y_ref, gate_ref, h_ref, x1_ref, g2_ref, fg_ref, sg_ref, su_ref, sd_ref, o_ref):
    gate = gate_ref[...]
    acc = None
    for j in range(TOP_K):
        gj = gate[:, j:j + 1]
        parts = [gj * p for p in _unpack_quarters(y_ref[j, 0], y_ref[j, 1])]
        acc = parts if acc is None else [a + p for a, p in zip(acc, parts)]
    moe = jnp.concatenate(acc, axis=1) + _swiglu_packed(h_ref[0], h_ref[1], sg_ref, su_ref, sd_ref)
    x2 = x1_ref[0] + g2_ref[0] * moe
    o_ref[0] = x2 * lax.rsqrt(jnp.mean(x2 * x2, axis=-1, keepdims=True) + EPS) * fg_ref[...]


def _combine(y_rows, gate_tk, h2_packed, x1, g2, final_g, sg, su, sd, tm, b0, nb, prev):
    bsz, l, d = x1.shape
    tm = min(tm, l)
    nt = l // tm
    q = d // 4
    const = lambda b, i: (0, 0)
    in_specs = [pl.BlockSpec((TOP_K, 2, tm, q), lambda b, i: (0, 0, b * nt + i, 0)),
                pl.BlockSpec((tm, TOP_K), lambda b, i: (b * nt + i, 0)),
                pl.BlockSpec((2, tm, q), lambda b, i: (0, (b0 + b) * nt + i, 0)),
                pl.BlockSpec((1, tm, d), lambda b, i: (b0 + b, i, 0)),
                pl.BlockSpec((1, 1, d), lambda b, i: (b0 + b, 0, 0)),
                pl.BlockSpec((1, d), const),
                pl.BlockSpec((d, D_EXPERT), const),
                pl.BlockSpec((d, D_EXPERT), const),
                pl.BlockSpec((D_EXPERT, d), const)]
    args = [y_rows, gate_tk, h2_packed, x1, g2, final_g.reshape(1, d), sg, su, sd]
    kernel_fn, aliases = _combine_kernel, {}
    if prev is not None:
        in_specs.append(pl.BlockSpec(memory_space=pl.ANY))
        args.append(prev)
        aliases = {len(args) - 1: 0}
        kernel_fn = lambda *refs: _combine_kernel(*refs[:9], refs[10])
    return pl.pallas_call(
        kernel_fn,
        grid=(nb, nt),
        in_specs=in_specs,
        out_specs=pl.BlockSpec((1, tm, d), lambda b, i: (b0 + b, i, 0)),
        out_shape=jax.ShapeDtypeStruct((bsz, l, d), _F32),
        input_output_aliases=aliases,
        compiler_params=_cparams(("parallel", "parallel")),
        name="moe_combine_shared_final_norm",
    )(*args)


def _to_chain_lanes(t):
    lead, (b, l, _) = t.shape[:-3], t.shape[-3:]
    nl = len(lead)
    t = t.reshape(*lead, b, l, RW_HEADS, RW_HEAD_DIM)
    t = t.transpose(*range(nl), nl + 1, nl + 3, nl, nl + 2)
    return t.reshape(*lead, l, RW_HEAD_DIM, b * RW_HEADS)


def _from_chain_lanes(t, b):
    l = t.shape[1]
    return t.reshape(2, l, RW_HEAD_DIM, b, RW_HEADS).transpose(0, 3, 1, 4, 2).reshape(2, b, l, RW_WIDTH)


def _scan_operands(prep):
    r, kk, v, w, kd, bb = prep[:6]
    return [_to_chain_lanes(t) for t in (r, kk, v)], [_to_chain_lanes(t) for t in (w, kd, bb)]


def kernel(x, c, ctx, c_ctx, w_mod, b_mod, norm1_g, norm2_g, w_in, rw_conv, rw_w0, rw_w2, rw_a0, rw_a2,
           rw_g2, rw_k_k, rw_k_a, rw_r_k, rw_lnx_g, rw_lnx_b, hg_lb_logits, hg_norm_g, w_out, router_w,
           router_b, exp_w_gate, exp_w_up, exp_w_down, sh_w_gate, sh_w_up, sh_w_down, final_norm_g):
    bsz, seq, d = x.shape
    assert w_mod.shape[0] == 1 and bsz * RW_HEADS == LANES
    lyr = 0

    c_all = jnp.concatenate([c, c_ctx[None, :], jnp.zeros((SUBLANES - 1, d), _F32)], axis=0)
    mod = _modulation(c_all, w_mod[lyr], b_mod[lyr])
    sh1, sc1, g1, sh2, sc2, g2 = [m[:, None, :] for m in jnp.split(mod[:bsz], N_MOD, axis=-1)]
    mod_ctx = jnp.broadcast_to(mod[bsz][None, None, :], (bsz, 1, N_MOD * d))
    csh1, csc1 = mod_ctx[..., :d], mod_ctx[..., d:2 * d]

    w_in_bf16 = w_in[lyr].astype(_BF16)
    rkv_lat, lora_lat, hg_lat = _inproj(x, norm1_g[lyr], sc1, sh1, w_in_bf16, tm=256)
    rkv_ctx, lora_ctx, hg_ctx = _inproj(ctx, norm1_g[lyr], csc1, csh1, w_in_bf16, tm=256)

    kern9 = rw_conv[lyr].reshape(CONV_K * CONV_K, RKV_W)
    lane_head = jnp.arange(RW_WIDTH) // RW_HEAD_DIM
    seg = (lane_head[:, None] == lane_head[None, :]).astype(_BF16)
    per_dir_split = lambda t: jnp.swapaxes(_split_bf16(t), 0, 1)
    rw_p = (kern9, rw_w0[lyr], per_dir_split(rw_w2[lyr]), rw_a0[lyr], per_dir_split(rw_a2[lyr]),
            _split_bf16(rw_g2[lyr]), rw_k_k[lyr], rw_k_a[lyr], rw_r_k[lyr], seg)
    prep_lat = _rw_prep(rkv_lat, lora_lat, *rw_p, tq=512, grid_conv=True)
    prep_ctx = _rw_prep(rkv_ctx, lora_ctx, *rw_p, tq=256, grid_conv=False)
    s_rw = _rwkv_scan(*_scan_operands(prep_ctx), None, tt=32, emit=False)
    y_dirs = _from_chain_lanes(_rwkv_scan(*_scan_operands(prep_lat), s_rw, tt=32, emit=True), bsz)

    lb = jnp.cumsum(jax.nn.softmax(hg_lb_logits.astype(_F32), axis=1), axis=1)[:, lyr]
    o_dirs = []
    for dd, rev in enumerate((False, True)):
        s_hg = _hgrn_scan(hg_ctx, lb[dd:dd + 1], None, tb=256, emit=False, reverse=rev)
        o_dirs.append(_hgrn_scan(hg_lat, lb[dd:dd + 1], s_hg, tb=256, emit=True, reverse=rev))

    perm = lambda t: t.reshape(N_GROUPS, GROUP_SIZE, -1).transpose(1, 0, 2).reshape(N_EXPERTS, -1)
    router_wt = perm(router_w[lyr].T)
    bias_rows = jnp.broadcast_to(perm(router_b[lyr][:, None]), (N_EXPERTS, LANES))
    hg_norm_row = jnp.tile(hg_norm_g[lyr], HG_HEADS).reshape(1, HG_WIDTH)
    x1, h2, logits_t = _post(y_dirs, prep_lat[6], prep_lat[7], rw_lnx_g[lyr], rw_lnx_b[lyr], o_dirs, hg_lat,
                             hg_norm_row, seg, x, w_out[lyr].astype(_BF16), g1, norm2_g[lyr], sc2, sh2,
                             router_wt, tm=512)
    nb = bsz // MOE_SPLITS
    n_tok = nb * seq
    n_blocks = n_tok * TOP_K // MOE_BLOCK + N_EXPERTS
    n_slots = n_blocks * MOE_BLOCK
    out = None
    for part in range(MOE_SPLITS):
        t0 = part * n_tok
        top_e, gate, rank, counts_rows = _router(logits_t, bias_rows, 512, t0, n_tok)
        counts = counts_rows[:, 0].astype(jnp.int32).reshape(GROUP_SIZE, N_GROUPS).T.reshape(N_EXPERTS)
        dest, plan = _dispatch_plan(top_e, rank, counts, n_blocks)
        dest_half = (dest[:, None, :] + jnp.array([0, n_slots], jnp.int32)[None, :, None]).reshape(TOP_K, 2 * n_tok)
        x_sorted = _sc_dispatch(h2[:, t0:t0 + n_tok].reshape(2 * n_tok, SC_ROW), dest_half, 2 * n_slots)
        y_sorted = _experts(plan, x_sorted.reshape(2, n_slots, SC_ROW),
                            exp_w_gate[lyr], exp_w_up[lyr], exp_w_down[lyr])
        y_rows = _sc_gather(y_sorted.reshape(2 * n_slots, SC_ROW), dest_half.reshape(1, TOP_K * 2 * n_tok))
        out = _combine(y_rows.reshape(TOP_K, 2, n_tok, SC_ROW), gate.T, h2, x1, g2, final_norm_g,
                       sh_w_gate[lyr], sh_w_up[lyr], sh_w_down[lyr], 512, part * nb, nb, out)
    return out
```

```python
import functools

import jax
import jax.numpy as jnp
from jax import lax
from jax.experimental import pallas as pl
from jax.experimental.pallas import tpu as pltpu
from jax.experimental.pallas import tpu_sc as plsc

D_MODEL = 1024
GRID_W = 64
N_MOD = 6
EPS = 1e-6

RW_HEAD_DIM = 64
RW_HEADS = 8
RW_WIDTH = 512
RW_LORA = 64
RW_GATE_LORA = 128
RW_GN_EPS = 1e-5 * RW_HEAD_DIM
N_SCAN_OPS = 7
CONV_K = 3

HG_HEADS = 8
HG_DIM = 64
HG_WIDTH = 512
HG_CHUNK = 32
HG_PAIRS = HG_HEADS // 2
HG_SCORE_ROWS = 128

RKV_W = 3 * RW_WIDTH
LORA_W = 4 * RW_LORA + RW_GATE_LORA
HG_W = 5 * HG_WIDTH
P_IN = RKV_W + LORA_W + HG_W

N_EXPERTS = 64
N_GROUPS = 8
GROUP_SIZE = N_EXPERTS // N_GROUPS
TOPK_GROUPS = 4
TOP_K = 8
D_EXPERT = 256
ROUTE_SCALE = 2.5
MOE_BLOCK = 1024
SC_ROW = 256
SC_WINDOW = 128

LANES = 128
SUBLANES = 8
VMEM_LIMIT = 48 * 1024 * 1024

_HI = lax.Precision.HIGHEST
_F32 = jnp.float32
_BF16 = jnp.bfloat16
_NT = (((1,), (1,)), ((), ()))
_TN = (((0,), (0,)), ((), ()))


def _cparams(sem):
    return pltpu.CompilerParams(dimension_semantics=sem, vmem_limit_bytes=VMEM_LIMIT)


def _pack_bf16_pair(lo, hi):
    return pltpu.pack_elementwise([lo, hi], packed_dtype=_BF16)


def _unpack_bf16_pair(u):
    return tuple(pltpu.unpack_elementwise(u, index=i, packed_dtype=_BF16, unpacked_dtype=_F32) for i in range(2))


def _pack_quarters(x):
    q = x.shape[1] // 4
    return jnp.stack([_pack_bf16_pair(x[:, 2 * h * q:(2 * h + 1) * q], x[:, (2 * h + 1) * q:(2 * h + 2) * q])
                      for h in range(2)])


def _unpack_quarters(u0, u1):
    return _unpack_bf16_pair(u0) + _unpack_bf16_pair(u1)


def _dir_block(d, i, nb):
    return i + d * (nb - 1 - 2 * i)


def _mod_kernel(c_ref, w_ref, b_ref, o_ref):
    c = c_ref[...]
    s = c * jax.nn.sigmoid(c)
    o_ref[...] = jnp.dot(s, w_ref[...], preferred_element_type=_F32, precision=_HI) + b_ref[...]


def _modulation(c_all, w_mod, b_mod):
    rows, d = c_all.shape
    n = w_mod.shape[1]
    tn = 1024
    return pl.pallas_call(
        _mod_kernel,
        grid=(n // tn,),
        in_specs=[pl.BlockSpec((rows, d), lambda j: (0, 0)),
                  pl.BlockSpec((d, tn), lambda j: (0, j)),
                  pl.BlockSpec((1, tn), lambda j: (0, j))],
        out_specs=pl.BlockSpec((rows, tn), lambda j: (0, j)),
        out_shape=jax.ShapeDtypeStruct((rows, n), _F32),
        compiler_params=_cparams(("parallel",)),
        name="modulation",
    )(c_all, w_mod, b_mod.reshape(1, n))


def _inproj_kernel(x_ref, g_ref, sc_ref, sh_ref, w_ref, rkv_ref, lora_ref, hg_ref):
    x = x_ref[0]
    y = x * lax.rsqrt(jnp.mean(x * x, axis=-1, keepdims=True) + EPS) * g_ref[...]
    h = y * (1.0 + sc_ref[0]) + sh_ref[0]
    u = jnp.dot(h.astype(_BF16), w_ref[...], preferred_element_type=_F32)
    rkv_ref[0] = u[:, :RKV_W]
    lora_ref[0] = u[:, RKV_W:RKV_W + LORA_W]
    hg_ref[0] = u[:, RKV_W + LORA_W:]


def _inproj(x, norm_g, sc, sh, w_in_bf16, tm):
    bsz, l, d = x.shape
    tm = min(tm, l)
    row = lambda b, i: (b, i, 0)
    per_b = lambda b, i: (b, 0, 0)
    return pl.pallas_call(
        _inproj_kernel,
        grid=(bsz, l // tm),
        in_specs=[pl.BlockSpec((1, tm, d), row),
                  pl.BlockSpec((1, d), lambda b, i: (0, 0)),
                  pl.BlockSpec((1, 1, d), per_b),
                  pl.BlockSpec((1, 1, d), per_b),
                  pl.BlockSpec((d, P_IN), lambda b, i: (0, 0))],
        out_specs=[pl.BlockSpec((1, tm, RKV_W), row),
                   pl.BlockSpec((1, tm, LORA_W), row),
                   pl.BlockSpec((1, tm, HG_W), row)],
        out_shape=[jax.ShapeDtypeStruct((bsz, l, RKV_W), _F32),
                   jax.ShapeDtypeStruct((bsz, l, LORA_W), _F32),
                   jax.ShapeDtypeStruct((bsz, l, HG_W), _F32)],
        compiler_params=_cparams(("parallel", "parallel")),
        name="inproj",
    )(x, norm_g.reshape(1, d), sc, sh, w_in_bf16)


def _seg_sum(x, seg_ref, nonneg=False):
    hi = x.astype(_BF16)
    out = jnp.dot(hi, seg_ref[...], preferred_element_type=_F32)
    if nonneg:
        return out
    lo = (x - hi.astype(_F32)).astype(_BF16)
    return out + jnp.dot(lo, seg_ref[...], preferred_element_type=_F32)


def _dot_split(x, w_ref):
    hi = x.astype(_BF16)
    lo = (x - hi.astype(_F32)).astype(_BF16)
    return (jnp.dot(hi, w_ref[0], preferred_element_type=_F32) + jnp.dot(lo, w_ref[0], preferred_element_type=_F32)
            + jnp.dot(hi, w_ref[1], preferred_element_type=_F32))


def _split_bf16(w):
    hi = w.astype(_BF16)
    return jnp.stack([hi, (w - hi.astype(_F32)).astype(_BF16)])


def _rw_prep_kernel(cur_ref, prev_ref, next_ref, lora_ref, kern_ref, w0_ref, w2_ref, a0_ref, a2_ref, g2_ref,
                    kk_ref, ka_ref, rk_ref, seg_ref,
                    ops_out, w_out, bonus_out, g_out, *, grid_conv):
    i = pl.program_id(1)
    tq = cur_ref.shape[0]
    hw = prev_ref.shape[0]
    w = RW_WIDTH
    pmask = jnp.where(i > 0, 1.0, 0.0)
    nmask = jnp.where(i < pl.num_programs(1) - 1, 1.0, 0.0)
    xpos = lax.broadcasted_iota(jnp.int32, (tq + 2 * hw, w), 0) % GRID_W
    conv = []
    for part in range(3):
        cs = slice(part * w, (part + 1) * w)
        xe = jnp.concatenate([prev_ref[:, cs] * pmask, cur_ref[:, cs], next_ref[:, cs] * nmask], axis=0)
        xl = pltpu.roll(xe, 1, 0)
        xr = pltpu.roll(xe, tq + 2 * hw - 1, 0)
        if grid_conv:
            xl = jnp.where(xpos == 0, 0.0, xl)
            xr = jnp.where(xpos == GRID_W - 1, 0.0, xr)
        acc = None
        for dy in range(CONV_K) if grid_conv else (CONV_K // 2,):
            off = hw + (dy - 1) * GRID_W
            for dx, src in enumerate((xl, xe, xr)):
                term = src[off:off + tq] * kern_ref[pl.ds(dy * CONV_K + dx, 1), cs]
                acc = term if acc is None else acc + term
        conv.append(acc)
    r, k, v = conv
    ops_out[0] = r.astype(_BF16)
    ops_out[2] = v.astype(_BF16)

    kk = k * kk_ref[...]
    kk = kk / jnp.maximum(jnp.sqrt(_seg_sum(kk * kk, seg_ref, nonneg=True)), 1e-12)
    ops_out[1] = kk.astype(_BF16)

    lora = lora_ref[...]
    ksum = jnp.zeros_like(k)
    for d in range(2):
        wl = lora[:, d * RW_LORA:(d + 1) * RW_LORA]
        al = lora[:, (2 + d) * RW_LORA:(3 + d) * RW_LORA]
        z = w0_ref[pl.ds(d, 1), :] + _dot_split(jnp.tanh(wl), w2_ref.at[d])
        wlog = -jax.nn.softplus(-z) - 0.5
        w_out[d] = jnp.exp(-jnp.exp(wlog))
        a = jax.nn.sigmoid(a0_ref[pl.ds(d, 1), :] + _dot_split(al, a2_ref.at[d]))
        kd = k * (1.0 + (a - 1.0) * ka_ref[...])
        ops_out[3 + d] = kd.astype(_BF16)
        ops_out[5 + d] = (kk * a).astype(_BF16)
        ksum = ksum + kd
    bonus_out[...] = _seg_sum(r * ksum * rk_ref[...], seg_ref) * v
    gd = lora[:, 4 * RW_LORA:]
    g_out[...] = _dot_split(jax.nn.sigmoid(gd), g2_ref)


def _rw_prep(rkv, lora, kern9, w0, w2, a0, a2, g2, k_k, k_a, r_k, seg, tq, grid_conv):
    bsz, l, _ = rkv.shape
    tq = min(tq, l)
    nb = l // tq
    hw = GRID_W
    per = tq // hw
    w = RW_WIDTH
    row = lambda b, i: (b, i, 0)
    const2 = lambda b, i: (0, 0)
    const3 = lambda b, i: (0, 0, 0)
    const4 = lambda b, i: (0, 0, 0, 0)
    tok = pl.BlockSpec((None, tq, w), row)
    tok2 = pl.BlockSpec((2, None, tq, w), lambda b, i: (0, b, i, 0))
    one = jax.ShapeDtypeStruct((bsz, l, w), _F32)
    two = jax.ShapeDtypeStruct((2, bsz, l, w), _F32)
    ops = jax.ShapeDtypeStruct((N_SCAN_OPS, bsz, l, w), _BF16)
    ops_spec = pl.BlockSpec((N_SCAN_OPS, None, tq, w), lambda b, i: (0, b, i, 0))
    vec = lambda t: t.reshape(1, w)
    return pl.pallas_call(
        functools.partial(_rw_prep_kernel, grid_conv=grid_conv),
        grid=(bsz, nb),
        in_specs=[pl.BlockSpec((None, tq, RKV_W), row),
                  pl.BlockSpec((None, hw, RKV_W), lambda b, i: (b, jnp.maximum(i * per - 1, 0), 0)),
                  pl.BlockSpec((None, hw, RKV_W), lambda b, i: (b, jnp.minimum((i + 1) * per, l // hw - 1), 0)),
                  pl.BlockSpec((None, tq, LORA_W), row),
                  pl.BlockSpec((CONV_K * CONV_K, RKV_W), const2),
                  pl.BlockSpec((2, w), const2),
                  pl.BlockSpec((2, 2, RW_LORA, w), const4),
                  pl.BlockSpec((2, w), const2),
                  pl.BlockSpec((2, 2, RW_LORA, w), const4),
                  pl.BlockSpec((2, RW_GATE_LORA, w), const3),
                  pl.BlockSpec((1, w), const2),
                  pl.BlockSpec((1, w), const2),
                  pl.BlockSpec((1, w), const2),
                  pl.BlockSpec((w, w), const2)],
        out_specs=[ops_spec, tok2, tok, tok],
        out_shape=[ops, two, one, one],
        compiler_params=_cparams(("parallel", "parallel")),
        name="rwkv_prep_grid" if grid_conv else "rwkv_prep_seq",
    )(rkv, rkv, rkv, lora, kern9, w0, w2, a0, a2, g2, vec(k_k), vec(k_a), vec(r_k), seg)


def _rwkv_scan_kernel(*refs, tt, emit, has_init):
    r_in, kk_in, v_in, w_ref, k_in, b_in = refs[:6]
    rest = list(refs[6:])
    s0_ref = rest.pop(0) if has_init else None
    out_ref = rest.pop(0)
    s_ref, sa_ref, r_ref, kk_ref, v_ref, k_ref, b_ref = rest
    for src, dst in ((r_in, r_ref), (kk_in, kk_ref), (v_in, v_ref), (k_in, k_ref), (b_in, b_ref)):
        dst[...] = src[...].astype(_F32)
    d = pl.program_id(0)
    i = pl.program_id(1)
    n = RW_HEAD_DIM

    @pl.when(i == 0)
    def _():
        s_ref[...] = s0_ref[...] if has_init else jnp.zeros_like(s_ref)

    hv = n // 2
    halves = [pl.ds(h * hv, hv) for h in range(2)]

    def project(t, vs):
        acc = [jnp.zeros((hv, LANES), _F32), jnp.zeros((hv, LANES), _F32)]
        for k in range(n):
            acc[k % 2] = acc[k % 2] + s_ref[k, vs, :] * kk_ref[t, pl.ds(k, 1), :]
        return -(acc[0] + acc[1])

    first = d * (tt - 1)
    for vs in halves:
        sa_ref[vs, :] = project(first, vs)

    def step(j, carry):
        t = j + d * (tt - 1 - 2 * j)
        tn = jnp.clip(t + 1 - 2 * d, 0, tt - 1)
        for vs in halves:
            sa = sa_ref[vs, :]
            vv = v_ref[t, vs, :]
            y = [jnp.zeros((hv, LANES), _F32), jnp.zeros((hv, LANES), _F32)]
            nsa = [jnp.zeros((hv, LANES), _F32), jnp.zeros((hv, LANES), _F32)]
            for k in range(n):
                row = pl.ds(k, 1)
                s_new = s_ref[k, vs, :] * w_ref[t, row, :] + sa * b_ref[t, row, :] + vv * k_ref[t, row, :]
                s_ref[k, vs, :] = s_new
                nsa[k % 2] = nsa[k % 2] + s_new * kk_ref[tn, row, :]
                if emit:
                    y[k % 2] = y[k % 2] + s_new * r_ref[t, row, :]
            sa_ref[vs, :] = -(nsa[0] + nsa[1])
            if emit:
                out_ref[t, vs, :] = y[0] + y[1]
        return carry

    lax.fori_loop(0, tt, step, 0)

    if not emit:
        @pl.when(i == pl.num_programs(1) - 1)
        def _():
            out_ref[...] = s_ref[...]


def _rwkv_scan(ops, w, s0, tt, emit):
    _, l, n, c = ops.shape
    nb = l // tt
    op_spec = lambda j, per_dir: pl.BlockSpec(
        (None, tt, n, c), lambda d, i: (j + d * per_dir, _dir_block(d, i, nb), 0, 0))
    pd_spec = pl.BlockSpec((None, tt, n, c), lambda d, i: (d, _dir_block(d, i, nb), 0, 0))
    st_spec = pl.BlockSpec((None, n, n, c), lambda d, i: (d, 0, 0, 0))
    in_specs = [op_spec(0, 0), op_spec(1, 0), op_spec(2, 0), pd_spec, op_spec(3, 1), op_spec(5, 1)]
    args = [ops, ops, ops, w, ops, ops]
    if s0 is not None:
        in_specs.append(st_spec)
        args.append(s0)
    if emit:
        out_spec, out_shape = pd_spec, jax.ShapeDtypeStruct((2, l, n, c), _F32)
    else:
        out_spec, out_shape = st_spec, jax.ShapeDtypeStruct((2, n, n, c), _F32)
    return pl.pallas_call(
        functools.partial(_rwkv_scan_kernel, tt=tt, emit=emit, has_init=s0 is not None),
        grid=(2, nb),
        in_specs=in_specs,
        out_specs=out_spec,
        out_shape=out_shape,
        scratch_shapes=[pltpu.VMEM((n, n, c), _F32), pltpu.VMEM((n, c), _F32)] + [pltpu.VMEM((tt, n, c), _F32)] * 5,
        compiler_params=_cparams(("parallel", "arbitrary")),
        name="rwkv_scan_emit" if emit else "rwkv_scan_state",
    )(*args)


def _hgrn_kernel(*refs, n_chunks, emit, has_init, reverse):
    q_ref, f_ref, i_ref, lb_ref, mats_ref = refs[:5]
    rest = list(refs[5:])
    s0_ref = rest.pop(0) if has_init else None
    out_ref = rest.pop(0)
    st_ref, upd_ref, ent_ref = rest
    c = HG_CHUNK
    tb = n_chunks * c
    blk = pl.program_id(1)

    @pl.when(blk == 0)
    def _():
        st_ref[...] = s0_ref[...] if has_init else jnp.zeros_like(st_ref)

    lb = lb_ref[...]
    omlb = 1.0 - lb
    fr = f_ref[...]
    logf = jnp.log(lb + omlb * jax.nn.sigmoid(fr))
    kf = omlb * jax.nn.sigmoid(-fr)
    hi = logf.astype(_BF16)
    lo = (logf - hi.astype(_F32)).astype(_BF16)
    sums = (jnp.dot(mats_ref[...], hi, preferred_element_type=_F32)
            + jnp.dot(mats_ref[...], lo, preferred_element_type=_F32))
    cum, tot, mid = sums[:tb], sums[tb:2 * tb], sums[2 * tb:]
    dec = jnp.exp(tot)
    kdec = (kf * jnp.exp(tot - cum)).astype(_BF16)
    vb = i_ref[...].astype(_BF16)
    same_head = ((lax.broadcasted_iota(jnp.int32, (LANES, LANES), 0) < HG_DIM)
                 == (lax.broadcasted_iota(jnp.int32, (LANES, LANES), 1) < HG_DIM))
    pairs = [slice(p * LANES, (p + 1) * LANES) for p in range(HG_PAIRS)]
    chunks = [slice(j * c, (j + 1) * c) for j in range(n_chunks)]

    if emit:
        q = q_ref[...]
        qs = q * jax.nn.sigmoid(q) * (HG_DIM ** -0.5)
        qd = qs * jnp.exp(cum - mid)
        kd = (kf * jnp.exp(mid - cum)).astype(_BF16)
        qe = (qs * jnp.exp(cum)).astype(_BF16)
        sub = min(tb, HG_SCORE_ROWS)
        head0 = lax.broadcasted_iota(jnp.int32, (sub, LANES), 1) < HG_DIM
        for g in range(tb // sub):
            rows = slice(g * sub, (g + 1) * sub)
            causal = mats_ref[rows, rows] > 0
            causal2 = jnp.concatenate([causal, causal], axis=0)
            for ls in pairs:
                qd_p = qd[rows, ls]
                q2 = jnp.concatenate([jnp.where(head0, qd_p, 0.0), jnp.where(head0, 0.0, qd_p)], axis=0)
                sc = lax.dot_general(q2.astype(_BF16), kd[rows, ls], _NT, preferred_element_type=_F32)
                sc = jnp.where(causal2, sc, 0.0)
                oi = jnp.dot(sc.astype(_BF16), vb[rows, ls], preferred_element_type=_F32)
                out_ref[rows, ls] = jnp.where(head0, oi[:sub], oi[sub:])

    for j, rows in enumerate(chunks):
        for p, ls in enumerate(pairs):
            upd = lax.dot_general(vb[rows, ls], kdec[rows, ls], _TN, preferred_element_type=_F32)
            upd_ref[j, p] = jnp.where(same_head, upd, 0.0)

    order = list(range(n_chunks))[::-1] if reverse else list(range(n_chunks))
    for p, ls in enumerate(pairs):
        st = st_ref[p]
        for j in order:
            if emit:
                ent_ref[j, p] = st.astype(_BF16)
            st = st * dec[j * c:j * c + 1, ls] + upd_ref[j, p]
        st_ref[p] = st

    if emit:
        for j, rows in enumerate(chunks):
            for p, ls in enumerate(pairs):
                out_ref[rows, ls] += lax.dot_general(qe[rows, ls], ent_ref[j, p], _NT,
                                                     preferred_element_type=_F32)
    else:
        @pl.when(blk == pl.num_programs(1) - 1)
        def _():
            out_ref[...] = st_ref[...]


def _hgrn_mats(tb, reverse):
    t = jnp.arange(tb)
    same = (t[:, None] // HG_CHUNK) == (t[None, :] // HG_CHUNK)
    ref_pos = (t // HG_CHUNK) * HG_CHUNK + HG_CHUNK // 2
    if reverse:
        cum = same & (t[None, :] >= t[:, None])
        mid = same & (t[None, :] >= ref_pos[:, None])
    else:
        cum = same & (t[None, :] <= t[:, None])
        mid = same & (t[None, :] <= ref_pos[:, None])
    return jnp.concatenate([cum, same, mid], axis=0).astype(_BF16)


def _hgrn_scan(hg, lb, s0, tb, emit, reverse):
    bsz, l, _ = hg.shape
    tb = min(tb, l)
    nb = l // tb
    w = HG_WIDTH
    n_chunks = tb // HG_CHUNK
    tblk = (lambda i: nb - 1 - i) if reverse else (lambda i: i)
    col = lambda j: pl.BlockSpec((None, tb, w), lambda b, i: (b, tblk(i), j))
    const = lambda b, i: (0, 0)
    st_spec = pl.BlockSpec((None, HG_PAIRS, LANES, LANES), lambda b, i: (b, 0, 0, 0))
    in_specs = [col(0), col(2 if reverse else 1), col(3), pl.BlockSpec((1, w), const),
                pl.BlockSpec((3 * tb, tb), const)]
    args = [hg, hg, hg, lb, _hgrn_mats(tb, reverse)]
    if s0 is not None:
        in_specs.append(st_spec)
        args.append(s0)
    if emit:
        out_spec = pl.BlockSpec((None, tb, w), lambda b, i: (b, tblk(i), 0))
        out_shape = jax.ShapeDtypeStruct((bsz, l, w), _F32)
    else:
        out_spec = st_spec
        out_shape = jax.ShapeDtypeStruct((bsz, HG_PAIRS, LANES, LANES), _F32)
    return pl.pallas_call(
        functools.partial(_hgrn_kernel, n_chunks=n_chunks, emit=emit, has_init=s0 is not None,
                          reverse=reverse),
        grid=(bsz, nb),
        in_specs=in_specs,
        out_specs=out_spec,
        out_shape=out_shape,
        scratch_shapes=[pltpu.VMEM((HG_PAIRS, LANES, LANES), _F32),
                        pltpu.VMEM((n_chunks, HG_PAIRS, LANES, LANES), _F32),
                        pltpu.VMEM((n_chunks, HG_PAIRS, LANES, LANES), _BF16)],
        compiler_params=_cparams(("parallel", "arbitrary")),
        name=("hgrn_emit" if emit else "hgrn_state") + ("_bwd" if reverse else "_fwd"),
    )(*args)


def _post_kernel(y_ref, bonus_ref, grw_ref, lng_ref, lnb_ref, of_ref, ob_ref, ghg_ref, hgn_ref, seg_ref,
                 x_ref, w_ref, g1_ref, n2_ref, sc_ref, sh_ref, rw_ref, x1_ref, h2_ref, lg_ref):
    inv_n = 1.0 / RW_HEAD_DIM
    ysum = y_ref[0] + y_ref[1]
    mu = _seg_sum(ysum, seg_ref) * inv_n
    dlt = ysum - mu
    var = _seg_sum(dlt * dlt, seg_ref, nonneg=True) * inv_n
    yn = dlt * lax.rsqrt(var + RW_GN_EPS) * lng_ref[...] + lnb_ref[...] + bonus_ref[...]
    y_rw = yn * grw_ref[...]
    o = of_ref[...] + ob_ref[...]
    o = o * lax.rsqrt(_seg_sum(o * o, seg_ref, nonneg=True) * inv_n + EPS) * hgn_ref[...]
    g = ghg_ref[...]
    y_hg = o * (g * jax.nn.sigmoid(g))
    y = (jnp.dot(y_rw.astype(_BF16), w_ref[:RW_WIDTH, :], preferred_element_type=_F32)
         + jnp.dot(y_hg.astype(_BF16), w_ref[RW_WIDTH:, :], preferred_element_type=_F32))
    x1 = x_ref[0] + g1_ref[0] * y
    x1_ref[0] = x1
    h2 = x1 * lax.rsqrt(jnp.mean(x1 * x1, axis=-1, keepdims=True) + EPS) * n2_ref[...]
    h2 = h2 * (1.0 + sc_ref[0]) + sh_ref[0]
    h2_ref[...] = _pack_quarters(h2)
    lg_ref[...] = lax.dot_general(rw_ref[...], h2, _NT, preferred_element_type=_F32, precision=_HI)


def _post(y_dirs, bonus, g_rw, lnx_g, lnx_b, o_dirs, hg, hg_norm_row, seg,
          x, w_out_bf16, g1, norm2_g, sc2, sh2, router_wt, tm):
    bsz, l, d = x.shape
    tm = min(tm, l)
    nt = l // tm
    w = RW_WIDTH
    row = lambda b, i: (b, i, 0)
    per_b = lambda b, i: (b, 0, 0)
    const = lambda b, i: (0, 0)
    tok = pl.BlockSpec((None, tm, w), row)
    tok2 = pl.BlockSpec((2, None, tm, w), lambda b, i: (0, b, i, 0))
    vec = pl.BlockSpec((1, w), const)
    return pl.pallas_call(
        _post_kernel,
        grid=(bsz, nt),
        in_specs=[tok2, tok, tok, vec, vec, tok, tok,
                  pl.BlockSpec((None, tm, w), lambda b, i: (b, i, 4)),
                  vec,
                  pl.BlockSpec((w, w), const),
                  pl.BlockSpec((1, tm, d), row),
                  pl.BlockSpec((d, d), const),
                  pl.BlockSpec((1, 1, d), per_b),
                  pl.BlockSpec((1, d), const),
                  pl.BlockSpec((1, 1, d), per_b),
                  pl.BlockSpec((1, 1, d), per_b),
                  pl.BlockSpec((N_EXPERTS, d), const)],
        out_specs=[pl.BlockSpec((1, tm, d), row),
                   pl.BlockSpec((2, tm, d // 4), lambda b, i: (0, b * nt + i, 0)),
                   pl.BlockSpec((N_EXPERTS, tm), lambda b, i: (0, b * nt + i))],
        out_shape=[jax.ShapeDtypeStruct((bsz, l, d), _F32),
                   jax.ShapeDtypeStruct((2, bsz * l, d // 4), jnp.uint32),
                   jax.ShapeDtypeStruct((N_EXPERTS, bsz * l), _F32)],
        compiler_params=_cparams(("parallel", "parallel")),
        name="outproj_norm2_router",
    )(y_dirs, bonus, g_rw, lnx_g.reshape(1, w), lnx_b.reshape(1, w), o_dirs[0], o_dirs[1], hg, hg_norm_row, seg,
      x, w_out_bf16, g1, norm2_g.reshape(1, d), sc2, sh2, router_wt)


def _sublane_all(x, op):
    for s in (4, 2, 1):
        x = op(x, pltpu.roll(x, s, 0))
    return x


def _router_kernel(lg_ref, bias_ref, tri_ref, e_ref, g_ref, rank_ref, cnt_ref):
    ng, gs = N_GROUPS, GROUP_SIZE
    tn = lg_ref.shape[1]
    neg = -jnp.inf
    gidx = lax.broadcasted_iota(jnp.int32, (ng, tn), 0)
    bias = jnp.concatenate([bias_ref[...]] * (tn // LANES), axis=1)
    scores = [jax.nn.sigmoid(lg_ref[j * ng:(j + 1) * ng, :]) for j in range(gs)]
    sel = [scores[j] + bias[j * ng:(j + 1) * ng, :] for j in range(gs)]

    m1 = functools.reduce(jnp.maximum, sel)
    cnt = functools.reduce(jnp.add, [(s == m1).astype(_F32) for s in sel])
    m2 = functools.reduce(jnp.maximum, [jnp.where(s < m1, s, neg) for s in sel])
    gscore = m1 + jnp.where(cnt >= 2.0, m1, m2)

    rank = jnp.zeros((ng, tn), jnp.int32)
    for s in range(1, ng):
        other = pltpu.roll(gscore, s, 0)
        beats = jnp.where(other > gscore, 1, jnp.where((other == gscore) & (gidx >= s), 1, 0))
        rank = rank + beats
    gsel = rank < TOPK_GROUPS

    key = [jnp.where(gsel, s, neg) for s in sel]
    eidx = [gidx * gs + j for j in range(gs)]
    avail = [jnp.ones((ng, tn), jnp.int32) for _ in range(gs)]
    top_e = jnp.zeros((TOP_K, tn), jnp.int32)
    top_s = jnp.zeros((TOP_K, tn), _F32)
    for r in range(TOP_K):
        cur = functools.reduce(jnp.maximum, [jnp.where(avail[j] > 0, key[j], neg) for j in range(gs)])
        mx = _sublane_all(cur, jnp.maximum)
        cand = [jnp.where((avail[j] > 0) & (key[j] == mx), eidx[j], N_EXPERTS) for j in range(gs)]
        mn = _sublane_all(functools.reduce(jnp.minimum, cand), jnp.minimum)
        picked = functools.reduce(jnp.add, [jnp.where(eidx[j] == mn, scores[j], 0.0) for j in range(gs)])
        avail = [jnp.where(eidx[j] == mn, 0, avail[j]) for j in range(gs)]
        top_e = jnp.where(gidx == r, mn, top_e)
        top_s = jnp.where(gidx == r, _sublane_all(picked, jnp.add), top_s)
    den = _sublane_all(top_s, jnp.add)
    e_ref[...] = top_e
    g_ref[...] = top_s / den * ROUTE_SCALE

    @pl.when(pl.program_id(0) == 0)
    def _():
        cnt_ref[...] = jnp.zeros_like(cnt_ref)

    taken = jnp.concatenate([jnp.where(a == 0, 1.0, 0.0) for a in avail], axis=0)
    before = jnp.dot(taken.astype(_BF16), tri_ref[...], preferred_element_type=_F32) - taken
    before = before + jnp.concatenate([cnt_ref[...]] * (tn // LANES), axis=1)
    rank = jnp.zeros((TOP_K, tn), _F32)
    for r in range(TOP_K):
        e_r = jnp.broadcast_to(top_e[r:r + 1, :], (ng, tn))
        hit = functools.reduce(jnp.add, [jnp.where(eidx[j] == e_r, before[j * ng:(j + 1) * ng, :], 0.0)
                                         for j in range(gs)])
        rank = jnp.where(gidx == r, _sublane_all(hit, jnp.add), rank)
    rank_ref[...] = rank.astype(jnp.int32)
    ones = jnp.ones((tn, LANES), _BF16)
    cnt_ref[...] += jnp.dot(taken.astype(_BF16), ones, preferred_element_type=_F32)


def _router(logits_t, bias_rows, tn):
    assert TOP_K == N_GROUPS == SUBLANES
    t = logits_t.shape[1]
    tn = min(tn, t)
    tri = (jnp.arange(tn)[:, None] <= jnp.arange(tn)[None, :]).astype(_BF16)
    out = pl.BlockSpec((TOP_K, tn), lambda i: (0, i))
    cnt = pl.BlockSpec((N_EXPERTS, LANES), lambda i: (0, 0))
    return pl.pallas_call(
        _router_kernel,
        grid=(t // tn,),
        in_specs=[pl.BlockSpec((N_EXPERTS, tn), lambda i: (0, i)), cnt,
                  pl.BlockSpec((tn, tn), lambda i: (0, 0))],
        out_specs=[out, out, out, cnt],
        out_shape=[jax.ShapeDtypeStruct((TOP_K, t), jnp.int32), jax.ShapeDtypeStruct((TOP_K, t), _F32),
                   jax.ShapeDtypeStruct((TOP_K, t), jnp.int32),
                   jax.ShapeDtypeStruct((N_EXPERTS, LANES), _F32)],
        compiler_params=_cparams(("arbitrary",)),
        name="router_topk",
    )(logits_t, bias_rows, tri)


def _sc_mesh():
    return plsc.VectorSubcoreMesh(core_axis_name="c", subcore_axis_name="s")


def _sc_dispatch(x_half_rows, dest_half_rows, n_out):
    n_rows = x_half_rows.shape[0]

    @pl.kernel(out_type=jax.ShapeDtypeStruct((n_out, SC_ROW), x_half_rows.dtype), mesh=_sc_mesh(),
               scratch_types=[])
    def scatter_rows(x_hbm, d_hbm, o_hbm):
        def body(x_vmem, i_vmem):
            for j in range(TOP_K):
                pltpu.sync_copy(x_vmem, o_hbm.at[i_vmem.at[j]])

        pltpu.emit_pipeline(
            body,
            grid=(n_rows // SC_WINDOW,),
            in_specs=[pl.BlockSpec((SC_WINDOW, SC_ROW), lambda i: (i, 0)),
                      pl.BlockSpec((TOP_K, SC_WINDOW), lambda i: (0, i))],
            out_specs=[],
            core_axis_name=("c", "s"),
            dimension_semantics=(pltpu.PARALLEL,),
        )(x_hbm, d_hbm)

    return scatter_rows(x_half_rows, dest_half_rows)


def _sc_gather(y_half_rows, idx):
    n = idx.shape[1]

    @pl.kernel(out_type=jax.ShapeDtypeStruct((n, SC_ROW), y_half_rows.dtype), mesh=_sc_mesh(),
               scratch_types=[])
    def gather_rows(y_hbm, i_hbm, o_hbm):
        def body(i_vmem, o_vmem):
            pltpu.sync_copy(y_hbm.at[i_vmem.at[0]], o_vmem)

        pltpu.emit_pipeline(
            body,
            grid=(n // SC_WINDOW,),
            in_specs=[pl.BlockSpec((1, SC_WINDOW), lambda i: (0, i))],
            out_specs=[pl.BlockSpec((SC_WINDOW, SC_ROW), lambda i: (i, 0))],
            core_axis_name=("c", "s"),
            dimension_semantics=(pltpu.PARALLEL,),
        )(i_hbm, o_hbm)

    return gather_rows(y_half_rows, idx)


def _dispatch_plan(top_e, rank, counts, n_blocks):
    padded = (counts + MOE_BLOCK - 1) // MOE_BLOCK * MOE_BLOCK
    pad_end = jnp.cumsum(padded)
    pad_start = pad_end - padded
    onehot = top_e[:, :, None] == jnp.arange(N_EXPERTS, dtype=jnp.int32)[None, None, :]
    dest = rank + jnp.sum(jnp.where(onehot, pad_start[None, None, :], 0), axis=-1)
    blk_start = jnp.arange(n_blocks, dtype=jnp.int32) * MOE_BLOCK
    blk_e = jnp.minimum(jnp.sum(pad_end[None, :] <= blk_start[:, None], axis=1), N_EXPERTS - 1)
    n_used = pad_end[-1] // MOE_BLOCK
    return dest, jnp.concatenate([blk_e.astype(jnp.int32), n_used[None].astype(jnp.int32)])


def _swiglu_packed(u0, u1, wg_ref, wu_ref, wd_ref):
    x = jnp.concatenate(_unpack_quarters(u0, u1), axis=1).astype(_BF16)
    a = jnp.dot(x, wg_ref[...].astype(_BF16), preferred_element_type=_F32)
    u = jnp.dot(x, wu_ref[...].astype(_BF16), preferred_element_type=_F32)
    act = (a * jax.nn.sigmoid(a) * u).astype(_BF16)
    return jnp.dot(act, wd_ref[...].astype(_BF16), preferred_element_type=_F32)


def _expert_kernel(plan_ref, x_ref, wg_ref, wu_ref, wd_ref, y_ref):
    @pl.when(pl.program_id(0) < plan_ref[pl.num_programs(0)])
    def _():
        y_ref[...] = _pack_quarters(_swiglu_packed(x_ref[0], x_ref[1], wg_ref, wu_ref, wd_ref))


def _experts(plan, x_sorted, wg, wu, wd):
    _, n_rows, q = x_sorted.shape
    n_blocks = n_rows // MOE_BLOCK
    d = 4 * q
    rows = pl.BlockSpec((2, MOE_BLOCK, q), lambda i, plan: (0, i, 0))
    grid_spec = pltpu.PrefetchScalarGridSpec(
        num_scalar_prefetch=1,
        grid=(n_blocks,),
        in_specs=[rows,
                  pl.BlockSpec((None, d, D_EXPERT), lambda i, plan: (plan[i], 0, 0)),
                  pl.BlockSpec((None, d, D_EXPERT), lambda i, plan: (plan[i], 0, 0)),
                  pl.BlockSpec((None, D_EXPERT, d), lambda i, plan: (plan[i], 0, 0))],
        out_specs=rows,
    )
    return pl.pallas_call(
        _expert_kernel,
        grid_spec=grid_spec,
        out_shape=jax.ShapeDtypeStruct((2, n_rows, q), jnp.uint32),
        compiler_params=_cparams(("arbitrary",)),
        name="moe_routed_experts",
    )(plan, x_sorted, wg, wu, wd)


def _combine_kernel(y_ref, gate_ref, h_ref, x1_ref, g2_ref, fg_ref, sg_ref, su_ref, sd_ref, o_ref):
    gate = gate_ref[...]
    acc = None
    for j in range(TOP_K):
        gj = gate[:, j:j + 1]
        parts = [gj * p for p in _unpack_quarters(y_ref[j, 0], y_ref[j, 1])]
        acc = parts if acc is None else [a + p for a, p in zip(acc, parts)]
    moe = jnp.concatenate(acc, axis=1) + _swiglu_packed(h_ref[0], h_ref[1], sg_ref, su_ref, sd_ref)
    x2 = x1_ref[0] + g2_ref[0] * moe
    o_ref[0] = x2 * lax.rsqrt(jnp.mean(x2 * x2, axis=-1, keepdims=True) + EPS) * fg_ref[...]


def _combine(y_rows, gate_tk, h2_packed, x1, g2, final_g, sg, su, sd, tm):
    bsz, l, d = x1.shape
    tm = min(tm, l)
    nt = l // tm
    q = d // 4
    const = lambda b, i: (0, 0)
    return pl.pallas_call(
        _combine_kernel,
        grid=(bsz, nt),
        in_specs=[pl.BlockSpec((TOP_K, 2, tm, q), lambda b, i: (0, 0, b * nt + i, 0)),
                  pl.BlockSpec((tm, TOP_K), lambda b, i: (b * nt + i, 0)),
                  pl.BlockSpec((2, tm, q), lambda b, i: (0, b * nt + i, 0)),
                  pl.BlockSpec((1, tm, d), lambda b, i: (b, i, 0)),
                  pl.BlockSpec((1, 1, d), lambda b, i: (b, 0, 0)),
                  pl.BlockSpec((1, d), const),
                  pl.BlockSpec((d, D_EXPERT), const),
                  pl.BlockSpec((d, D_EXPERT), const),
                  pl.BlockSpec((D_EXPERT, d), const)],
        out_specs=pl.BlockSpec((1, tm, d), lambda b, i: (b, i, 0)),
        out_shape=jax.ShapeDtypeStruct((bsz, l, d), _F32),
        compiler_params=_cparams(("parallel", "parallel")),
        name="moe_combine_shared_final_norm",
    )(y_rows, gate_tk, h2_packed, x1, g2, final_g.reshape(1, d), sg, su, sd)


def _to_chain_lanes(t):
    lead, (b, l, _) = t.shape[:-3], t.shape[-3:]
    nl = len(lead)
    t = t.reshape(*lead, b, l, RW_HEADS, RW_HEAD_DIM)
    t = t.transpose(*range(nl), nl + 1, nl + 3, nl, nl + 2)
    return t.reshape(*lead, l, RW_HEAD_DIM, b * RW_HEADS)


def _from_chain_lanes(t, b):
    l = t.shape[1]
    return t.reshape(2, l, RW_HEAD_DIM, b, RW_HEADS).transpose(0, 3, 1, 4, 2).reshape(2, b, l, RW_WIDTH)


def _scan_operands(prep):
    return _to_chain_lanes(prep[0]), _to_chain_lanes(prep[1])


def kernel(x, c, ctx, c_ctx, w_mod, b_mod, norm1_g, norm2_g, w_in, rw_conv, rw_w0, rw_w2, rw_a0, rw_a2,
           rw_g2, rw_k_k, rw_k_a, rw_r_k, rw_lnx_g, rw_lnx_b, hg_lb_logits, hg_norm_g, w_out, router_w,
           router_b, exp_w_gate, exp_w_up, exp_w_down, sh_w_gate, sh_w_up, sh_w_down, final_norm_g):
    bsz, seq, d = x.shape
    assert w_mod.shape[0] == 1 and bsz * RW_HEADS == LANES
    lyr = 0

    c_all = jnp.concatenate([c, c_ctx[None, :], jnp.zeros((SUBLANES - 1, d), _F32)], axis=0)
    mod = _modulation(c_all, w_mod[lyr], b_mod[lyr])
    sh1, sc1, g1, sh2, sc2, g2 = [m[:, None, :] for m in jnp.split(mod[:bsz], N_MOD, axis=-1)]
    mod_ctx = jnp.broadcast_to(mod[bsz][None, None, :], (bsz, 1, N_MOD * d))
    csh1, csc1 = mod_ctx[..., :d], mod_ctx[..., d:2 * d]

    w_in_bf16 = w_in[lyr].astype(_BF16)
    rkv_lat, lora_lat, hg_lat = _inproj(x, norm1_g[lyr], sc1, sh1, w_in_bf16, tm=256)
    rkv_ctx, lora_ctx, hg_ctx = _inproj(ctx, norm1_g[lyr], csc1, csh1, w_in_bf16, tm=256)

    kern9 = rw_conv[lyr].reshape(CONV_K * CONV_K, RKV_W)
    lane_head = jnp.arange(RW_WIDTH) // RW_HEAD_DIM
    seg = (lane_head[:, None] == lane_head[None, :]).astype(_BF16)
    per_dir_split = lambda t: jnp.swapaxes(_split_bf16(t), 0, 1)
    rw_p = (kern9, rw_w0[lyr], per_dir_split(rw_w2[lyr]), rw_a0[lyr], per_dir_split(rw_a2[lyr]),
            _split_bf16(rw_g2[lyr]), rw_k_k[lyr], rw_k_a[lyr], rw_r_k[lyr], seg)
    prep_lat = _rw_prep(rkv_lat, lora_lat, *rw_p, tq=512, grid_conv=True)
    prep_ctx = _rw_prep(rkv_ctx, lora_ctx, *rw_p, tq=256, grid_conv=False)
    s_rw = _rwkv_scan(*_scan_operands(prep_ctx), None, tt=32, emit=False)
    y_dirs = _from_chain_lanes(_rwkv_scan(*_scan_operands(prep_lat), s_rw, tt=32, emit=True), bsz)

    lb = jnp.cumsum(jax.nn.softmax(hg_lb_logits.astype(_F32), axis=1), axis=1)[:, lyr]
    o_dirs = []
    for dd, rev in enumerate((False, True)):
        s_hg = _hgrn_scan(hg_ctx, lb[dd:dd + 1], None, tb=256, emit=False, reverse=rev)
        o_dirs.append(_hgrn_scan(hg_lat, lb[dd:dd + 1], s_hg, tb=256, emit=True, reverse=rev))

    perm = lambda t: t.reshape(N_GROUPS, GROUP_SIZE, -1).transpose(1, 0, 2).reshape(N_EXPERTS, -1)
    router_wt = perm(router_w[lyr].T)
    bias_rows = jnp.broadcast_to(perm(router_b[lyr][:, None]), (N_EXPERTS, LANES))
    hg_norm_row = jnp.tile(hg_norm_g[lyr], HG_HEADS).reshape(1, HG_WIDTH)
    x1, h2, logits_t = _post(y_dirs, prep_lat[2], prep_lat[3], rw_lnx_g[lyr], rw_lnx_b[lyr], o_dirs, hg_lat,
                             hg_norm_row, seg, x, w_out[lyr].astype(_BF16), g1, norm2_g[lyr], sc2, sh2,
                             router_wt, tm=512)
    top_e, gate, rank, counts_rows = _router(logits_t, bias_rows, tn=512)
    counts = counts_rows[:, 0].astype(jnp.int32).reshape(GROUP_SIZE, N_GROUPS).T.reshape(N_EXPERTS)

    n_tok = bsz * seq
    n_blocks = n_tok * TOP_K // MOE_BLOCK + N_EXPERTS
    n_slots = n_blocks * MOE_BLOCK
    dest, plan = _dispatch_plan(top_e, rank, counts, n_blocks)
    dest_half = (dest[:, None, :] + jnp.array([0, n_slots], jnp.int32)[None, :, None]).reshape(TOP_K, 2 * n_tok)
    x_sorted = _sc_dispatch(h2.reshape(2 * n_tok, SC_ROW), dest_half, 2 * n_slots)
    y_sorted = _experts(plan, x_sorted.reshape(2, n_slots, SC_ROW),
                        exp_w_gate[lyr], exp_w_up[lyr], exp_w_down[lyr])
    y_rows = _sc_gather(y_sorted.reshape(2 * n_slots, SC_ROW), dest_half.reshape(1, TOP_K * 2 * n_tok))
    return _combine(y_rows.reshape(TOP_K, 2, n_tok, SC_ROW), gate.T, h2, x1, g2,
                    final_norm_g, sh_w_gate[lyr], sh_w_up[lyr], sh_w_down[lyr], tm=512)
```

```python
import functools

import jax
import jax.numpy as jnp
from jax import lax
from jax.experimental import pallas as pl
from jax.experimental.pallas import tpu as pltpu
from jax.experimental.pallas import tpu_sc as plsc

D_MODEL = 1024
GRID_W = 64
N_MOD = 6
EPS = 1e-6

RW_HEAD_DIM = 64
RW_HEADS = 8
RW_WIDTH = 512
RW_LORA = 64
RW_GATE_LORA = 128
RW_GN_EPS = 1e-5 * RW_HEAD_DIM
N_SCAN_OPS = 7
CONV_K = 3

HG_HEADS = 8
HG_DIM = 64
HG_WIDTH = 512
HG_CHUNK = 32
HG_PAIRS = HG_HEADS // 2
HG_SCORE_ROWS = 128

RKV_W = 3 * RW_WIDTH
LORA_W = 4 * RW_LORA + RW_GATE_LORA
HG_W = 5 * HG_WIDTH
P_IN = RKV_W + LORA_W + HG_W

N_EXPERTS = 64
N_GROUPS = 8
GROUP_SIZE = N_EXPERTS // N_GROUPS
TOPK_GROUPS = 4
TOP_K = 8
D_EXPERT = 256
ROUTE_SCALE = 2.5
MOE_BLOCK = 1024
SC_ROW = 256
SC_WINDOW = 128

LANES = 128
SUBLANES = 8
VMEM_LIMIT = 48 * 1024 * 1024

_HI = lax.Precision.HIGHEST
_F32 = jnp.float32
_BF16 = jnp.bfloat16
_NT = (((1,), (1,)), ((), ()))
_TN = (((0,), (0,)), ((), ()))


def _cparams(sem):
    return pltpu.CompilerParams(dimension_semantics=sem, vmem_limit_bytes=VMEM_LIMIT)


def _pack_bf16_pair(lo, hi):
    return pltpu.pack_elementwise([lo, hi], packed_dtype=_BF16)


def _unpack_bf16_pair(u):
    return tuple(pltpu.unpack_elementwise(u, index=i, packed_dtype=_BF16, unpacked_dtype=_F32) for i in range(2))


def _pack_quarters(x):
    q = x.shape[1] // 4
    return jnp.stack([_pack_bf16_pair(x[:, 2 * h * q:(2 * h + 1) * q], x[:, (2 * h + 1) * q:(2 * h + 2) * q])
                      for h in range(2)])


def _unpack_quarters(u0, u1):
    return _unpack_bf16_pair(u0) + _unpack_bf16_pair(u1)


def _dir_block(d, i, nb):
    return i + d * (nb - 1 - 2 * i)


def _mod_kernel(c_ref, w_ref, b_ref, o_ref):
    c = c_ref[...]
    s = c * jax.nn.sigmoid(c)
    o_ref[...] = jnp.dot(s, w_ref[...], preferred_element_type=_F32, precision=_HI) + b_ref[...]


def _modulation(c_all, w_mod, b_mod):
    rows, d = c_all.shape
    n = w_mod.shape[1]
    tn = 1024
    return pl.pallas_call(
        _mod_kernel,
        grid=(n // tn,),
        in_specs=[pl.BlockSpec((rows, d), lambda j: (0, 0)),
                  pl.BlockSpec((d, tn), lambda j: (0, j)),
                  pl.BlockSpec((1, tn), lambda j: (0, j))],
        out_specs=pl.BlockSpec((rows, tn), lambda j: (0, j)),
        out_shape=jax.ShapeDtypeStruct((rows, n), _F32),
        compiler_params=_cparams(("parallel",)),
        name="modulation",
    )(c_all, w_mod, b_mod.reshape(1, n))


def _inproj_kernel(x_ref, g_ref, sc_ref, sh_ref, w_ref, rkv_ref, lora_ref, hg_ref):
    x = x_ref[0]
    y = x * lax.rsqrt(jnp.mean(x * x, axis=-1, keepdims=True) + EPS) * g_ref[...]
    h = y * (1.0 + sc_ref[0]) + sh_ref[0]
    u = jnp.dot(h.astype(_BF16), w_ref[...], preferred_element_type=_F32)
    rkv_ref[0] = u[:, :RKV_W]
    lora_ref[0] = u[:, RKV_W:RKV_W + LORA_W]
    hg_ref[0] = u[:, RKV_W + LORA_W:]


def _inproj(x, norm_g, sc, sh, w_in_bf16, tm):
    bsz, l, d = x.shape
    tm = min(tm, l)
    row = lambda b, i: (b, i, 0)
    per_b = lambda b, i: (b, 0, 0)
    return pl.pallas_call(
        _inproj_kernel,
        grid=(bsz, l // tm),
        in_specs=[pl.BlockSpec((1, tm, d), row),
                  pl.BlockSpec((1, d), lambda b, i: (0, 0)),
                  pl.BlockSpec((1, 1, d), per_b),
                  pl.BlockSpec((1, 1, d), per_b),
                  pl.BlockSpec((d, P_IN), lambda b, i: (0, 0))],
        out_specs=[pl.BlockSpec((1, tm, RKV_W), row),
                   pl.BlockSpec((1, tm, LORA_W), row),
                   pl.BlockSpec((1, tm, HG_W), row)],
        out_shape=[jax.ShapeDtypeStruct((bsz, l, RKV_W), _F32),
                   jax.ShapeDtypeStruct((bsz, l, LORA_W), _F32),
                   jax.ShapeDtypeStruct((bsz, l, HG_W), _F32)],
        compiler_params=_cparams(("parallel", "parallel")),
        name="inproj",
    )(x, norm_g.reshape(1, d), sc, sh, w_in_bf16)


def _seg_sum(x, seg_ref, nonneg=False):
    hi = x.astype(_BF16)
    out = jnp.dot(hi, seg_ref[...], preferred_element_type=_F32)
    if nonneg:
        return out
    lo = (x - hi.astype(_F32)).astype(_BF16)
    return out + jnp.dot(lo, seg_ref[...], preferred_element_type=_F32)


def _dot_split(x, w_ref):
    hi = x.astype(_BF16)
    lo = (x - hi.astype(_F32)).astype(_BF16)
    return (jnp.dot(hi, w_ref[0], preferred_element_type=_F32) + jnp.dot(lo, w_ref[0], preferred_element_type=_F32)
            + jnp.dot(hi, w_ref[1], preferred_element_type=_F32))


def _split_bf16(w):
    hi = w.astype(_BF16)
    return jnp.stack([hi, (w - hi.astype(_F32)).astype(_BF16)])


def _rw_prep_kernel(cur_ref, prev_ref, next_ref, lora_ref, kern_ref, w0_ref, w2_ref, a0_ref, a2_ref, g2_ref,
                    kk_ref, ka_ref, rk_ref, seg_ref,
                    ops_out, w_out, bonus_out, g_out, *, grid_conv):
    i = pl.program_id(1)
    tq = cur_ref.shape[0]
    hw = prev_ref.shape[0]
    w = RW_WIDTH
    pmask = jnp.where(i > 0, 1.0, 0.0)
    nmask = jnp.where(i < pl.num_programs(1) - 1, 1.0, 0.0)
    xpos = lax.broadcasted_iota(jnp.int32, (tq + 2 * hw, w), 0) % GRID_W
    conv = []
    for part in range(3):
        cs = slice(part * w, (part + 1) * w)
        xe = jnp.concatenate([prev_ref[:, cs] * pmask, cur_ref[:, cs], next_ref[:, cs] * nmask], axis=0)
        xl = pltpu.roll(xe, 1, 0)
        xr = pltpu.roll(xe, tq + 2 * hw - 1, 0)
        if grid_conv:
            xl = jnp.where(xpos == 0, 0.0, xl)
            xr = jnp.where(xpos == GRID_W - 1, 0.0, xr)
        acc = None
        for dy in range(CONV_K) if grid_conv else (CONV_K // 2,):
            off = hw + (dy - 1) * GRID_W
            for dx, src in enumerate((xl, xe, xr)):
                term = src[off:off + tq] * kern_ref[pl.ds(dy * CONV_K + dx, 1), cs]
                acc = term if acc is None else acc + term
        conv.append(acc)
    r, k, v = conv
    ops_out[0] = r.astype(_BF16)
    ops_out[2] = v.astype(_BF16)

    kk = k * kk_ref[...]
    kk = kk / jnp.maximum(jnp.sqrt(_seg_sum(kk * kk, seg_ref, nonneg=True)), 1e-12)
    ops_out[1] = kk.astype(_BF16)

    lora = lora_ref[...]
    ksum = jnp.zeros_like(k)
    for d in range(2):
        wl = lora[:, d * RW_LORA:(d + 1) * RW_LORA]
        al = lora[:, (2 + d) * RW_LORA:(3 + d) * RW_LORA]
        z = w0_ref[pl.ds(d, 1), :] + _dot_split(jnp.tanh(wl), w2_ref.at[d])
        wlog = -jax.nn.softplus(-z) - 0.5
        w_out[d] = jnp.exp(-jnp.exp(wlog))
        a = jax.nn.sigmoid(a0_ref[pl.ds(d, 1), :] + _dot_split(al, a2_ref.at[d]))
        kd = k * (1.0 + (a - 1.0) * ka_ref[...])
        ops_out[3 + d] = kd.astype(_BF16)
        ops_out[5 + d] = (kk * a).astype(_BF16)
        ksum = ksum + kd
    bonus_out[...] = _seg_sum(r * ksum * rk_ref[...], seg_ref) * v
    gd = lora[:, 4 * RW_LORA:]
    g_out[...] = _dot_split(jax.nn.sigmoid(gd), g2_ref)


def _rw_prep(rkv, lora, kern9, w0, w2, a0, a2, g2, k_k, k_a, r_k, seg, tq, grid_conv):
    bsz, l, _ = rkv.shape
    tq = min(tq, l)
    nb = l // tq
    hw = GRID_W
    per = tq // hw
    w = RW_WIDTH
    row = lambda b, i: (b, i, 0)
    const2 = lambda b, i: (0, 0)
    const3 = lambda b, i: (0, 0, 0)
    const4 = lambda b, i: (0, 0, 0, 0)
    tok = pl.BlockSpec((None, tq, w), row)
    tok2 = pl.BlockSpec((2, None, tq, w), lambda b, i: (0, b, i, 0))
    one = jax.ShapeDtypeStruct((bsz, l, w), _F32)
    two = jax.ShapeDtypeStruct((2, bsz, l, w), _F32)
    ops = jax.ShapeDtypeStruct((N_SCAN_OPS, bsz, l, w), _BF16)
    ops_spec = pl.BlockSpec((N_SCAN_OPS, None, tq, w), lambda b, i: (0, b, i, 0))
    vec = lambda t: t.reshape(1, w)
    return pl.pallas_call(
        functools.partial(_rw_prep_kernel, grid_conv=grid_conv),
        grid=(bsz, nb),
        in_specs=[pl.BlockSpec((None, tq, RKV_W), row),
                  pl.BlockSpec((None, hw, RKV_W), lambda b, i: (b, jnp.maximum(i * per - 1, 0), 0)),
                  pl.BlockSpec((None, hw, RKV_W), lambda b, i: (b, jnp.minimum((i + 1) * per, l // hw - 1), 0)),
                  pl.BlockSpec((None, tq, LORA_W), row),
                  pl.BlockSpec((CONV_K * CONV_K, RKV_W), const2),
                  pl.BlockSpec((2, w), const2),
                  pl.BlockSpec((2, 2, RW_LORA, w), const4),
                  pl.BlockSpec((2, w), const2),
                  pl.BlockSpec((2, 2, RW_LORA, w), const4),
                  pl.BlockSpec((2, RW_GATE_LORA, w), const3),
                  pl.BlockSpec((1, w), const2),
                  pl.BlockSpec((1, w), const2),
                  pl.BlockSpec((1, w), const2),
                  pl.BlockSpec((w, w), const2)],
        out_specs=[ops_spec, tok2, tok, tok],
        out_shape=[ops, two, one, one],
        compiler_params=_cparams(("parallel", "parallel")),
        name="rwkv_prep_grid" if grid_conv else "rwkv_prep_seq",
    )(rkv, rkv, rkv, lora, kern9, w0, w2, a0, a2, g2, vec(k_k), vec(k_a), vec(r_k), seg)


def _rwkv_scan_kernel(*refs, tt, emit, has_init):
    r_in, kk_in, v_in, w_ref, k_in, b_in = refs[:6]
    rest = list(refs[6:])
    s0_ref = rest.pop(0) if has_init else None
    out_ref = rest.pop(0)
    s_ref, sa_ref, r_ref, kk_ref, v_ref, k_ref, b_ref = rest
    for src, dst in ((r_in, r_ref), (kk_in, kk_ref), (v_in, v_ref), (k_in, k_ref), (b_in, b_ref)):
        dst[...] = src[...].astype(_F32)
    d = pl.program_id(0)
    i = pl.program_id(1)
    n = RW_HEAD_DIM

    @pl.when(i == 0)
    def _():
        s_ref[...] = s0_ref[...] if has_init else jnp.zeros_like(s_ref)

    hv = n // 2
    halves = [pl.ds(h * hv, hv) for h in range(2)]

    def project(t, vs):
        acc = [jnp.zeros((hv, LANES), _F32), jnp.zeros((hv, LANES), _F32)]
        for k in range(n):
            acc[k % 2] = acc[k % 2] + s_ref[k, vs, :] * kk_ref[t, pl.ds(k, 1), :]
        return -(acc[0] + acc[1])

    first = d * (tt - 1)
    for vs in halves:
        sa_ref[vs, :] = project(first, vs)

    def step(j, carry):
        t = j + d * (tt - 1 - 2 * j)
        tn = jnp.clip(t + 1 - 2 * d, 0, tt - 1)
        for vs in halves:
            sa = sa_ref[vs, :]
            vv = v_ref[t, vs, :]
            y = [jnp.zeros((hv, LANES), _F32), jnp.zeros((hv, LANES), _F32)]
            nsa = [jnp.zeros((hv, LANES), _F32), jnp.zeros((hv, LANES), _F32)]
            for k in range(n):
                row = pl.ds(k, 1)
                s_new = s_ref[k, vs, :] * w_ref[t, row, :] + sa * b_ref[t, row, :] + vv * k_ref[t, row, :]
                s_ref[k, vs, :] = s_new
                nsa[k % 2] = nsa[k % 2] + s_new * kk_ref[tn, row, :]
                if emit:
                    y[k % 2] = y[k % 2] + s_new * r_ref[t, row, :]
            sa_ref[vs, :] = -(nsa[0] + nsa[1])
            if emit:
                out_ref[t, vs, :] = (y[0] + y[1]).astype(out_ref.dtype)
        return carry

    lax.fori_loop(0, tt, step, 0)

    if not emit:
        @pl.when(i == pl.num_programs(1) - 1)
        def _():
            out_ref[...] = s_ref[...]


def _rwkv_scan(ops, w, s0, tt, emit):
    _, l, n, c = ops.shape
    nb = l // tt
    op_spec = lambda j, per_dir: pl.BlockSpec(
        (None, tt, n, c), lambda d, i: (j + d * per_dir, _dir_block(d, i, nb), 0, 0))
    pd_spec = pl.BlockSpec((None, tt, n, c), lambda d, i: (d, _dir_block(d, i, nb), 0, 0))
    st_spec = pl.BlockSpec((None, n, n, c), lambda d, i: (d, 0, 0, 0))
    in_specs = [op_spec(0, 0), op_spec(1, 0), op_spec(2, 0), pd_spec, op_spec(3, 1), op_spec(5, 1)]
    args = [ops, ops, ops, w, ops, ops]
    if s0 is not None:
        in_specs.append(st_spec)
        args.append(s0)
    if emit:
        out_spec, out_shape = pd_spec, jax.ShapeDtypeStruct((2, l, n, c), _BF16)
    else:
        out_spec, out_shape = st_spec, jax.ShapeDtypeStruct((2, n, n, c), _F32)
    return pl.pallas_call(
        functools.partial(_rwkv_scan_kernel, tt=tt, emit=emit, has_init=s0 is not None),
        grid=(2, nb),
        in_specs=in_specs,
        out_specs=out_spec,
        out_shape=out_shape,
        scratch_shapes=[pltpu.VMEM((n, n, c), _F32), pltpu.VMEM((n, c), _F32)] + [pltpu.VMEM((tt, n, c), _F32)] * 5,
        compiler_params=_cparams(("parallel", "arbitrary")),
        name="rwkv_scan_emit" if emit else "rwkv_scan_state",
    )(*args)


def _hgrn_kernel(*refs, n_chunks, emit, has_init, reverse):
    q_ref, f_ref, i_ref, lb_ref, mats_ref = refs[:5]
    rest = list(refs[5:])
    s0_ref = rest.pop(0) if has_init else None
    out_ref = rest.pop(0)
    st_ref, upd_ref, ent_ref = rest
    c = HG_CHUNK
    tb = n_chunks * c
    blk = pl.program_id(1)

    @pl.when(blk == 0)
    def _():
        st_ref[...] = s0_ref[...] if has_init else jnp.zeros_like(st_ref)

    lb = lb_ref[...]
    omlb = 1.0 - lb
    fr = f_ref[...]
    logf = jnp.log(lb + omlb * jax.nn.sigmoid(fr))
    kf = omlb * jax.nn.sigmoid(-fr)
    hi = logf.astype(_BF16)
    lo = (logf - hi.astype(_F32)).astype(_BF16)
    sums = (jnp.dot(mats_ref[...], hi, preferred_element_type=_F32)
            + jnp.dot(mats_ref[...], lo, preferred_element_type=_F32))
    cum, tot, mid = sums[:tb], sums[tb:2 * tb], sums[2 * tb:]
    dec = jnp.exp(tot)
    kdec = (kf * jnp.exp(tot - cum)).astype(_BF16)
    vb = i_ref[...].astype(_BF16)
    same_head = ((lax.broadcasted_iota(jnp.int32, (LANES, LANES), 0) < HG_DIM)
                 == (lax.broadcasted_iota(jnp.int32, (LANES, LANES), 1) < HG_DIM))
    pairs = [slice(p * LANES, (p + 1) * LANES) for p in range(HG_PAIRS)]
    chunks = [slice(j * c, (j + 1) * c) for j in range(n_chunks)]

    if emit:
        q = q_ref[...]
        qs = q * jax.nn.sigmoid(q) * (HG_DIM ** -0.5)
        qd = qs * jnp.exp(cum - mid)
        kd = (kf * jnp.exp(mid - cum)).astype(_BF16)
        qe = (qs * jnp.exp(cum)).astype(_BF16)
        sub = min(tb, HG_SCORE_ROWS)
        head0 = lax.broadcasted_iota(jnp.int32, (sub, LANES), 1) < HG_DIM
        for g in range(tb // sub):
            rows = slice(g * sub, (g + 1) * sub)
            causal = mats_ref[rows, rows] > 0
            causal2 = jnp.concatenate([causal, causal], axis=0)
            for ls in pairs:
                qd_p = qd[rows, ls]
                q2 = jnp.concatenate([jnp.where(head0, qd_p, 0.0), jnp.where(head0, 0.0, qd_p)], axis=0)
                sc = lax.dot_general(q2.astype(_BF16), kd[rows, ls], _NT, preferred_element_type=_F32)
                sc = jnp.where(causal2, sc, 0.0)
                oi = jnp.dot(sc.astype(_BF16), vb[rows, ls], preferred_element_type=_F32)
                out_ref[rows, ls] = jnp.where(head0, oi[:sub], oi[sub:])

    for j, rows in enumerate(chunks):
        for p, ls in enumerate(pairs):
            upd = lax.dot_general(vb[rows, ls], kdec[rows, ls], _TN, preferred_element_type=_F32)
            upd_ref[j, p] = jnp.where(same_head, upd, 0.0)

    order = list(range(n_chunks))[::-1] if reverse else list(range(n_chunks))
    for p, ls in enumerate(pairs):
        st = st_ref[p]
        for j in order:
            if emit:
                ent_ref[j, p] = st.astype(_BF16)
            st = st * dec[j * c:j * c + 1, ls] + upd_ref[j, p]
        st_ref[p] = st

    if emit:
        for j, rows in enumerate(chunks):
            for p, ls in enumerate(pairs):
                out_ref[rows, ls] += lax.dot_general(qe[rows, ls], ent_ref[j, p], _NT,
                                                     preferred_element_type=_F32)
    else:
        @pl.when(blk == pl.num_programs(1) - 1)
        def _():
            out_ref[...] = st_ref[...]


def _hgrn_mats(tb, reverse):
    t = jnp.arange(tb)
    same = (t[:, None] // HG_CHUNK) == (t[None, :] // HG_CHUNK)
    ref_pos = (t // HG_CHUNK) * HG_CHUNK + HG_CHUNK // 2
    if reverse:
        cum = same & (t[None, :] >= t[:, None])
        mid = same & (t[None, :] >= ref_pos[:, None])
    else:
        cum = same & (t[None, :] <= t[:, None])
        mid = same & (t[None, :] <= ref_pos[:, None])
    return jnp.concatenate([cum, same, mid], axis=0).astype(_BF16)


def _hgrn_scan(hg, lb, s0, tb, emit, reverse):
    bsz, l, _ = hg.shape
    tb = min(tb, l)
    nb = l // tb
    w = HG_WIDTH
    n_chunks = tb // HG_CHUNK
    tblk = (lambda i: nb - 1 - i) if reverse else (lambda i: i)
    col = lambda j: pl.BlockSpec((None, tb, w), lambda b, i: (b, tblk(i), j))
    const = lambda b, i: (0, 0)
    st_spec = pl.BlockSpec((None, HG_PAIRS, LANES, LANES), lambda b, i: (b, 0, 0, 0))
    in_specs = [col(0), col(2 if reverse else 1), col(3), pl.BlockSpec((1, w), const),
                pl.BlockSpec((3 * tb, tb), const)]
    args = [hg, hg, hg, lb, _hgrn_mats(tb, reverse)]
    if s0 is not None:
        in_specs.append(st_spec)
        args.append(s0)
    if emit:
        out_spec = pl.BlockSpec((None, tb, w), lambda b, i: (b, tblk(i), 0))
        out_shape = jax.ShapeDtypeStruct((bsz, l, w), _F32)
    else:
        out_spec = st_spec
        out_shape = jax.ShapeDtypeStruct((bsz, HG_PAIRS, LANES, LANES), _F32)
    return pl.pallas_call(
        functools.partial(_hgrn_kernel, n_chunks=n_chunks, emit=emit, has_init=s0 is not None,
                          reverse=reverse),
        grid=(bsz, nb),
        in_specs=in_specs,
        out_specs=out_spec,
        out_shape=out_shape,
        scratch_shapes=[pltpu.VMEM((HG_PAIRS, LANES, LANES), _F32),
                        pltpu.VMEM((n_chunks, HG_PAIRS, LANES, LANES), _F32),
                        pltpu.VMEM((n_chunks, HG_PAIRS, LANES, LANES), _BF16)],
        compiler_params=_cparams(("parallel", "arbitrary")),
        name=("hgrn_emit" if emit else "hgrn_state") + ("_bwd" if reverse else "_fwd"),
    )(*args)


def _post_kernel(y_ref, bonus_ref, grw_ref, lng_ref, lnb_ref, of_ref, ob_ref, ghg_ref, hgn_ref, seg_ref,
                 x_ref, w_ref, g1_ref, n2_ref, sc_ref, sh_ref, rw_ref, x1_ref, h2_ref, lg_ref):
    inv_n = 1.0 / RW_HEAD_DIM
    ysum = y_ref[0].astype(_F32) + y_ref[1].astype(_F32)
    mu = _seg_sum(ysum, seg_ref) * inv_n
    dlt = ysum - mu
    var = _seg_sum(dlt * dlt, seg_ref, nonneg=True) * inv_n
    yn = dlt * lax.rsqrt(var + RW_GN_EPS) * lng_ref[...] + lnb_ref[...] + bonus_ref[...]
    y_rw = yn * grw_ref[...]
    o = of_ref[...] + ob_ref[...]
    o = o * lax.rsqrt(_seg_sum(o * o, seg_ref, nonneg=True) * inv_n + EPS) * hgn_ref[...]
    g = ghg_ref[...]
    y_hg = o * (g * jax.nn.sigmoid(g))
    y = (jnp.dot(y_rw.astype(_BF16), w_ref[:RW_WIDTH, :], preferred_element_type=_F32)
         + jnp.dot(y_hg.astype(_BF16), w_ref[RW_WIDTH:, :], preferred_element_type=_F32))
    x1 = x_ref[0] + g1_ref[0] * y
    x1_ref[0] = x1
    h2 = x1 * lax.rsqrt(jnp.mean(x1 * x1, axis=-1, keepdims=True) + EPS) * n2_ref[...]
    h2 = h2 * (1.0 + sc_ref[0]) + sh_ref[0]
    h2_ref[...] = _pack_quarters(h2)
    lg_ref[...] = lax.dot_general(rw_ref[...], h2, _NT, preferred_element_type=_F32, precision=_HI)


def _post(y_dirs, bonus, g_rw, lnx_g, lnx_b, o_dirs, hg, hg_norm_row, seg,
          x, w_out_bf16, g1, norm2_g, sc2, sh2, router_wt, tm):
    bsz, l, d = x.shape
    tm = min(tm, l)
    nt = l // tm
    w = RW_WIDTH
    row = lambda b, i: (b, i, 0)
    per_b = lambda b, i: (b, 0, 0)
    const = lambda b, i: (0, 0)
    tok = pl.BlockSpec((None, tm, w), row)
    tok2 = pl.BlockSpec((2, None, tm, w), lambda b, i: (0, b, i, 0))
    vec = pl.BlockSpec((1, w), const)
    return pl.pallas_call(
        _post_kernel,
        grid=(bsz, nt),
        in_specs=[tok2, tok, tok, vec, vec, tok, tok,
                  pl.BlockSpec((None, tm, w), lambda b, i: (b, i, 4)),
                  vec,
                  pl.BlockSpec((w, w), const),
                  pl.BlockSpec((1, tm, d), row),
                  pl.BlockSpec((d, d), const),
                  pl.BlockSpec((1, 1, d), per_b),
                  pl.BlockSpec((1, d), const),
                  pl.BlockSpec((1, 1, d), per_b),
                  pl.BlockSpec((1, 1, d), per_b),
                  pl.BlockSpec((N_EXPERTS, d), const)],
        out_specs=[pl.BlockSpec((1, tm, d), row),
                   pl.BlockSpec((2, tm, d // 4), lambda b, i: (0, b * nt + i, 0)),
                   pl.BlockSpec((N_EXPERTS, tm), lambda b, i: (0, b * nt + i))],
        out_shape=[jax.ShapeDtypeStruct((bsz, l, d), _F32),
                   jax.ShapeDtypeStruct((2, bsz * l, d // 4), jnp.uint32),
                   jax.ShapeDtypeStruct((N_EXPERTS, bsz * l), _F32)],
        compiler_params=_cparams(("parallel", "parallel")),
        name="outproj_norm2_router",
    )(y_dirs, bonus, g_rw, lnx_g.reshape(1, w), lnx_b.reshape(1, w), o_dirs[0], o_dirs[1], hg, hg_norm_row, seg,
      x, w_out_bf16, g1, norm2_g.reshape(1, d), sc2, sh2, router_wt)


def _sublane_all(x, op):
    for s in (4, 2, 1):
        x = op(x, pltpu.roll(x, s, 0))
    return x


def _router_kernel(lg_ref, bias_ref, tri_ref, e_ref, g_ref, rank_ref, cnt_ref):
    ng, gs = N_GROUPS, GROUP_SIZE
    tn = lg_ref.shape[1]
    neg = -jnp.inf
    gidx = lax.broadcasted_iota(jnp.int32, (ng, tn), 0)
    bias = jnp.concatenate([bias_ref[...]] * (tn // LANES), axis=1)
    scores = [jax.nn.sigmoid(lg_ref[j * ng:(j + 1) * ng, :]) for j in range(gs)]
    sel = [scores[j] + bias[j * ng:(j + 1) * ng, :] for j in range(gs)]

    m1 = functools.reduce(jnp.maximum, sel)
    cnt = functools.reduce(jnp.add, [(s == m1).astype(_F32) for s in sel])
    m2 = functools.reduce(jnp.maximum, [jnp.where(s < m1, s, neg) for s in sel])
    gscore = m1 + jnp.where(cnt >= 2.0, m1, m2)

    rank = jnp.zeros((ng, tn), jnp.int32)
    for s in range(1, ng):
        other = pltpu.roll(gscore, s, 0)
        beats = jnp.where(other > gscore, 1, jnp.where((other == gscore) & (gidx >= s), 1, 0))
        rank = rank + beats
    gsel = rank < TOPK_GROUPS

    key = [jnp.where(gsel, s, neg) for s in sel]
    eidx = [gidx * gs + j for j in range(gs)]
    avail = [jnp.ones((ng, tn), jnp.int32) for _ in range(gs)]
    top_e = jnp.zeros((TOP_K, tn), jnp.int32)
    top_s = jnp.zeros((TOP_K, tn), _F32)
    for r in range(TOP_K):
        cur = functools.reduce(jnp.maximum, [jnp.where(avail[j] > 0, key[j], neg) for j in range(gs)])
        mx = _sublane_all(cur, jnp.maximum)
        cand = [jnp.where((avail[j] > 0) & (key[j] == mx), eidx[j], N_EXPERTS) for j in range(gs)]
        mn = _sublane_all(functools.reduce(jnp.minimum, cand), jnp.minimum)
        picked = functools.reduce(jnp.add, [jnp.where(eidx[j] == mn, scores[j], 0.0) for j in range(gs)])
        avail = [jnp.where(eidx[j] == mn, 0, avail[j]) for j in range(gs)]
        top_e = jnp.where(gidx == r, mn, top_e)
        top_s = jnp.where(gidx == r, _sublane_all(picked, jnp.add), top_s)
    den = _sublane_all(top_s, jnp.add)
    e_ref[...] = top_e
    g_ref[...] = top_s / den * ROUTE_SCALE

    @pl.when(pl.program_id(0) == 0)
    def _():
        cnt_ref[...] = jnp.zeros_like(cnt_ref)

    taken = jnp.concatenate([jnp.where(a == 0, 1.0, 0.0) for a in avail], axis=0)
    before = jnp.dot(taken.astype(_BF16), tri_ref[...], preferred_element_type=_F32) - taken
    before = before + jnp.concatenate([cnt_ref[...]] * (tn // LANES), axis=1)
    rank = jnp.zeros((TOP_K, tn), _F32)
    for r in range(TOP_K):
        e_r = jnp.broadcast_to(top_e[r:r + 1, :], (ng, tn))
        hit = functools.reduce(jnp.add, [jnp.where(eidx[j] == e_r, before[j * ng:(j + 1) * ng, :], 0.0)
                                         for j in range(gs)])
        rank = jnp.where(gidx == r, _sublane_all(hit, jnp.add), rank)
    rank_ref[...] = rank.astype(jnp.int32)
    ones = jnp.ones((tn, LANES), _BF16)
    cnt_ref[...] += jnp.dot(taken.astype(_BF16), ones, preferred_element_type=_F32)


def _router(logits_t, bias_rows, tn):
    assert TOP_K == N_GROUPS == SUBLANES
    t = logits_t.shape[1]
    tn = min(tn, t)
    tri = (jnp.arange(tn)[:, None] <= jnp.arange(tn)[None, :]).astype(_BF16)
    out = pl.BlockSpec((TOP_K, tn), lambda i: (0, i))
    cnt = pl.BlockSpec((N_EXPERTS, LANES), lambda i: (0, 0))
    return pl.pallas_call(
        _router_kernel,
        grid=(t // tn,),
        in_specs=[pl.BlockSpec((N_EXPERTS, tn), lambda i: (0, i)), cnt,
                  pl.BlockSpec((tn, tn), lambda i: (0, 0))],
        out_specs=[out, out, out, cnt],
        out_shape=[jax.ShapeDtypeStruct((TOP_K, t), jnp.int32), jax.ShapeDtypeStruct((TOP_K, t), _F32),
                   jax.ShapeDtypeStruct((TOP_K, t), jnp.int32),
                   jax.ShapeDtypeStruct((N_EXPERTS, LANES), _F32)],
        compiler_params=_cparams(("arbitrary",)),
        name="router_topk",
    )(logits_t, bias_rows, tri)


def _sc_mesh():
    return plsc.VectorSubcoreMesh(core_axis_name="c", subcore_axis_name="s")


def _sc_dispatch(x_half_rows, dest_half_rows, n_out):
    n_rows = x_half_rows.shape[0]

    @pl.kernel(out_type=jax.ShapeDtypeStruct((n_out, SC_ROW), x_half_rows.dtype), mesh=_sc_mesh(),
               scratch_types=[])
    def scatter_rows(x_hbm, d_hbm, o_hbm):
        def body(x_vmem, i_vmem):
            for j in range(TOP_K):
                pltpu.sync_copy(x_vmem, o_hbm.at[i_vmem.at[j]])

        pltpu.emit_pipeline(
            body,
            grid=(n_rows // SC_WINDOW,),
            in_specs=[pl.BlockSpec((SC_WINDOW, SC_ROW), lambda i: (i, 0)),
                      pl.BlockSpec((TOP_K, SC_WINDOW), lambda i: (0, i))],
            out_specs=[],
            core_axis_name=("c", "s"),
            dimension_semantics=(pltpu.PARALLEL,),
        )(x_hbm, d_hbm)

    return scatter_rows(x_half_rows, dest_half_rows)


def _sc_gather(y_half_rows, idx):
    n = idx.shape[1]

    @pl.kernel(out_type=jax.ShapeDtypeStruct((n, SC_ROW), y_half_rows.dtype), mesh=_sc_mesh(),
               scratch_types=[])
    def gather_rows(y_hbm, i_hbm, o_hbm):
        def body(i_vmem, o_vmem):
            pltpu.sync_copy(y_hbm.at[i_vmem.at[0]], o_vmem)

        pltpu.emit_pipeline(
            body,
            grid=(n // SC_WINDOW,),
            in_specs=[pl.BlockSpec((1, SC_WINDOW), lambda i: (0, i))],
            out_specs=[pl.BlockSpec((SC_WINDOW, SC_ROW), lambda i: (i, 0))],
            core_axis_name=("c", "s"),
            dimension_semantics=(pltpu.PARALLEL,),
        )(i_hbm, o_hbm)

    return gather_rows(y_half_rows, idx)


def _dispatch_plan(top_e, rank, counts, n_blocks):
    padded = (counts + MOE_BLOCK - 1) // MOE_BLOCK * MOE_BLOCK
    pad_end = jnp.cumsum(padded)
    pad_start = pad_end - padded
    onehot = top_e[:, :, None] == jnp.arange(N_EXPERTS, dtype=jnp.int32)[None, None, :]
    dest = rank + jnp.sum(jnp.where(onehot, pad_start[None, None, :], 0), axis=-1)
    blk_start = jnp.arange(n_blocks, dtype=jnp.int32) * MOE_BLOCK
    blk_e = jnp.minimum(jnp.sum(pad_end[None, :] <= blk_start[:, None], axis=1), N_EXPERTS - 1)
    n_used = pad_end[-1] // MOE_BLOCK
    return dest, jnp.concatenate([blk_e.astype(jnp.int32), n_used[None].astype(jnp.int32)])


def _swiglu_packed(u0, u1, wg_ref, wu_ref, wd_ref):
    x = jnp.concatenate(_unpack_quarters(u0, u1), axis=1).astype(_BF16)
    a = jnp.dot(x, wg_ref[...].astype(_BF16), preferred_element_type=_F32)
    u = jnp.dot(x, wu_ref[...].astype(_BF16), preferred_element_type=_F32)
    act = (a * jax.nn.sigmoid(a) * u).astype(_BF16)
    return jnp.dot(act, wd_ref[...].astype(_BF16), preferred_element_type=_F32)


def _expert_kernel(plan_ref, x_ref, wg_ref, wu_ref, wd_ref, y_ref):
    @pl.when(pl.program_id(0) < plan_ref[pl.num_programs(0)])
    def _():
        y_ref[...] = _pack_quarters(_swiglu_packed(x_ref[0], x_ref[1], wg_ref, wu_ref, wd_ref))


def _experts(plan, x_sorted, wg, wu, wd):
    _, n_rows, q = x_sorted.shape
    n_blocks = n_rows // MOE_BLOCK
    d = 4 * q
    rows = pl.BlockSpec((2, MOE_BLOCK, q), lambda i, plan: (0, i, 0))
    grid_spec = pltpu.PrefetchScalarGridSpec(
        num_scalar_prefetch=1,
        grid=(n_blocks,),
        in_specs=[rows,
                  pl.BlockSpec((None, d, D_EXPERT), lambda i, plan: (plan[i], 0, 0)),
                  pl.BlockSpec((None, d, D_EXPERT), lambda i, plan: (plan[i], 0, 0)),
                  pl.BlockSpec((None, D_EXPERT, d), lambda i, plan: (plan[i], 0, 0))],
        out_specs=rows,
    )
    return pl.pallas_call(
        _expert_kernel,
        grid_spec=grid_spec,
        out_shape=jax.ShapeDtypeStruct((2, n_rows, q), jnp.uint32),
        compiler_params=_cparams(("arbitrary",)),
        name="moe_routed_experts",
    )(plan, x_sorted, wg, wu, wd)


def _combine_kernel(y_ref, gate_ref, h_ref, x1_ref, g2_ref, fg_ref, sg_ref, su_ref, sd_ref, o_ref):
    gate = gate_ref[...]
    acc = None
    for j in range(TOP_K):
        gj = gate[:, j:j + 1]
        parts = [gj * p for p in _unpack_quarters(y_ref[j, 0], y_ref[j, 1])]
        acc = parts if acc is None else [a + p for a, p in zip(acc, parts)]
    moe = jnp.concatenate(acc, axis=1) + _swiglu_packed(h_ref[0], h_ref[1], sg_ref, su_ref, sd_ref)
    x2 = x1_ref[0] + g2_ref[0] * moe
    o_ref[0] = x2 * lax.rsqrt(jnp.mean(x2 * x2, axis=-1, keepdims=True) + EPS) * fg_ref[...]


def _combine(y_rows, gate_tk, h2_packed, x1, g2, final_g, sg, su, sd, tm):
    bsz, l, d = x1.shape
    tm = min(tm, l)
    nt = l // tm
    q = d // 4
    const = lambda b, i: (0, 0)
    return pl.pallas_call(
        _combine_kernel,
        grid=(bsz, nt),
        in_specs=[pl.BlockSpec((TOP_K, 2, tm, q), lambda b, i: (0, 0, b * nt + i, 0)),
                  pl.BlockSpec((tm, TOP_K), lambda b, i: (b * nt + i, 0)),
                  pl.BlockSpec((2, tm, q), lambda b, i: (0, b * nt + i, 0)),
                  pl.BlockSpec((1, tm, d), lambda b, i: (b, i, 0)),
                  pl.BlockSpec((1, 1, d), lambda b, i: (b, 0, 0)),
                  pl.BlockSpec((1, d), const),
                  pl.BlockSpec((d, D_EXPERT), const),
                  pl.BlockSpec((d, D_EXPERT), const),
                  pl.BlockSpec((D_EXPERT, d), const)],
        out_specs=pl.BlockSpec((1, tm, d), lambda b, i: (b, i, 0)),
        out_shape=jax.ShapeDtypeStruct((bsz, l, d), _F32),
        compiler_params=_cparams(("parallel", "parallel")),
        name="moe_combine_shared_final_norm",
    )(y_rows, gate_tk, h2_packed, x1, g2, final_g.reshape(1, d), sg, su, sd)


def _to_chain_lanes(t):
    lead, (b, l, _) = t.shape[:-3], t.shape[-3:]
    nl = len(lead)
    t = t.reshape(*lead, b, l, RW_HEADS, RW_HEAD_DIM)
    t = t.transpose(*range(nl), nl + 1, nl + 3, nl, nl + 2)
    return t.reshape(*lead, l, RW_HEAD_DIM, b * RW_HEADS)


def _from_chain_lanes(t, b):
    l = t.shape[1]
    return t.reshape(2, l, RW_HEAD_DIM, b, RW_HEADS).transpose(0, 3, 1, 4, 2).reshape(2, b, l, RW_WIDTH)


def _scan_operands(prep):
    return _to_chain_lanes(prep[0]), _to_chain_lanes(prep[1])


def kernel(x, c, ctx, c_ctx, w_mod, b_mod, norm1_g, norm2_g, w_in, rw_conv, rw_w0, rw_w2, rw_a0, rw_a2,
           rw_g2, rw_k_k, rw_k_a, rw_r_k, rw_lnx_g, rw_lnx_b, hg_lb_logits, hg_norm_g, w_out, router_w,
           router_b, exp_w_gate, exp_w_up, exp_w_down, sh_w_gate, sh_w_up, sh_w_down, final_norm_g):
    bsz, seq, d = x.shape
    assert w_mod.shape[0] == 1 and bsz * RW_HEADS == LANES
    lyr = 0

    c_all = jnp.concatenate([c, c_ctx[None, :], jnp.zeros((SUBLANES - 1, d), _F32)], axis=0)
    mod = _modulation(c_all, w_mod[lyr], b_mod[lyr])
    sh1, sc1, g1, sh2, sc2, g2 = [m[:, None, :] for m in jnp.split(mod[:bsz], N_MOD, axis=-1)]
    mod_ctx = jnp.broadcast_to(mod[bsz][None, None, :], (bsz, 1, N_MOD * d))
    csh1, csc1 = mod_ctx[..., :d], mod_ctx[..., d:2 * d]

    w_in_bf16 = w_in[lyr].astype(_BF16)
    rkv_lat, lora_lat, hg_lat = _inproj(x, norm1_g[lyr], sc1, sh1, w_in_bf16, tm=256)
    rkv_ctx, lora_ctx, hg_ctx = _inproj(ctx, norm1_g[lyr], csc1, csh1, w_in_bf16, tm=256)

    kern9 = rw_conv[lyr].reshape(CONV_K * CONV_K, RKV_W)
    lane_head = jnp.arange(RW_WIDTH) // RW_HEAD_DIM
    seg = (lane_head[:, None] == lane_head[None, :]).astype(_BF16)
    per_dir_split = lambda t: jnp.swapaxes(_split_bf16(t), 0, 1)
    rw_p = (kern9, rw_w0[lyr], per_dir_split(rw_w2[lyr]), rw_a0[lyr], per_dir_split(rw_a2[lyr]),
            _split_bf16(rw_g2[lyr]), rw_k_k[lyr], rw_k_a[lyr], rw_r_k[lyr], seg)
    prep_lat = _rw_prep(rkv_lat, lora_lat, *rw_p, tq=512, grid_conv=True)
    prep_ctx = _rw_prep(rkv_ctx, lora_ctx, *rw_p, tq=256, grid_conv=False)
    s_rw = _rwkv_scan(*_scan_operands(prep_ctx), None, tt=32, emit=False)
    y_dirs = _from_chain_lanes(_rwkv_scan(*_scan_operands(prep_lat), s_rw, tt=32, emit=True), bsz)

    lb = jnp.cumsum(jax.nn.softmax(hg_lb_logits.astype(_F32), axis=1), axis=1)[:, lyr]
    o_dirs = []
    for dd, rev in enumerate((False, True)):
        s_hg = _hgrn_scan(hg_ctx, lb[dd:dd + 1], None, tb=256, emit=False, reverse=rev)
        o_dirs.append(_hgrn_scan(hg_lat, lb[dd:dd + 1], s_hg, tb=256, emit=True, reverse=rev))

    perm = lambda t: t.reshape(N_GROUPS, GROUP_SIZE, -1).transpose(1, 0, 2).reshape(N_EXPERTS, -1)
    router_wt = perm(router_w[lyr].T)
    bias_rows = jnp.broadcast_to(perm(router_b[lyr][:, None]), (N_EXPERTS, LANES))
    hg_norm_row = jnp.tile(hg_norm_g[lyr], HG_HEADS).reshape(1, HG_WIDTH)
    x1, h2, logits_t = _post(y_dirs, prep_lat[2], prep_lat[3], rw_lnx_g[lyr], rw_lnx_b[lyr], o_dirs, hg_lat,
                             hg_norm_row, seg, x, w_out[lyr].astype(_BF16), g1, norm2_g[lyr], sc2, sh2,
                             router_wt, tm=512)
    top_e, gate, rank, counts_rows = _router(logits_t, bias_rows, tn=512)
    counts = counts_rows[:, 0].astype(jnp.int32).reshape(GROUP_SIZE, N_GROUPS).T.reshape(N_EXPERTS)

    n_tok = bsz * seq
    n_blocks = n_tok * TOP_K // MOE_BLOCK + N_EXPERTS
    n_slots = n_blocks * MOE_BLOCK
    dest, plan = _dispatch_plan(top_e, rank, counts, n_blocks)
    dest_half = (dest[:, None, :] + jnp.array([0, n_slots], jnp.int32)[None, :, None]).reshape(TOP_K, 2 * n_tok)
    x_sorted = _sc_dispatch(h2.reshape(2 * n_tok, SC_ROW), dest_half, 2 * n_slots)
    y_sorted = _experts(plan, x_sorted.reshape(2, n_slots, SC_ROW),
                        exp_w_gate[lyr], exp_w_up[lyr], exp_w_down[lyr])
    y_rows = _sc_gather(y_sorted.reshape(2 * n_slots, SC_ROW), dest_half.reshape(1, TOP_K * 2 * n_tok))
    return _combine(y_rows.reshape(TOP_K, 2, n_tok, SC_ROW), gate.T, h2, x1, g2,
                    final_norm_g, sh_w_gate[lyr], sh_w_up[lyr], sh_w_down[lyr], tm=512)
```

```python
import functools

import jax
import jax.numpy as jnp
from jax import lax
from jax.experimental import pallas as pl
from jax.experimental.pallas import tpu as pltpu
from jax.experimental.pallas import tpu_sc as plsc

D_MODEL = 1024
GRID_W = 64
N_MOD = 6
EPS = 1e-6

RW_HEAD_DIM = 64
RW_HEADS = 8
RW_WIDTH = 512
RW_LORA = 64
RW_GATE_LORA = 128
RW_GN_EPS = 1e-5 * RW_HEAD_DIM
N_SCAN_OPS = 7
CONV_K = 3

HG_HEADS = 8
HG_DIM = 64
HG_WIDTH = 512
HG_CHUNK = 32
HG_PAIRS = HG_HEADS // 2
HG_SCORE_ROWS = 128

RKV_W = 3 * RW_WIDTH
LORA_W = 4 * RW_LORA + RW_GATE_LORA
HG_W = 5 * HG_WIDTH
P_IN = RKV_W + LORA_W + HG_W

N_EXPERTS = 64
N_GROUPS = 8
GROUP_SIZE = N_EXPERTS // N_GROUPS
TOPK_GROUPS = 4
TOP_K = 8
D_EXPERT = 256
ROUTE_SCALE = 2.5
MOE_BLOCK = 1024
SC_ROW = 256
SC_WINDOW = 128

LANES = 128
SUBLANES = 8
VMEM_LIMIT = 48 * 1024 * 1024

_HI = lax.Precision.HIGHEST
_F32 = jnp.float32
_BF16 = jnp.bfloat16
_NT = (((1,), (1,)), ((), ()))
_TN = (((0,), (0,)), ((), ()))


def _cparams(sem):
    return pltpu.CompilerParams(dimension_semantics=sem, vmem_limit_bytes=VMEM_LIMIT)


def _pack_bf16_pair(lo, hi):
    return pltpu.pack_elementwise([lo, hi], packed_dtype=_BF16)


def _unpack_bf16_pair(u):
    return tuple(pltpu.unpack_elementwise(u, index=i, packed_dtype=_BF16, unpacked_dtype=_F32) for i in range(2))


def _pack_quarters(x):
    q = x.shape[1] // 4
    return jnp.stack([_pack_bf16_pair(x[:, 2 * h * q:(2 * h + 1) * q], x[:, (2 * h + 1) * q:(2 * h + 2) * q])
                      for h in range(2)])


def _unpack_quarters(u0, u1):
    return _unpack_bf16_pair(u0) + _unpack_bf16_pair(u1)


def _dir_block(d, i, nb):
    return i + d * (nb - 1 - 2 * i)


def _mod_kernel(c_ref, w_ref, b_ref, o_ref):
    c = c_ref[...]
    s = c * jax.nn.sigmoid(c)
    o_ref[...] = jnp.dot(s, w_ref[...], preferred_element_type=_F32, precision=_HI) + b_ref[...]


def _modulation(c_all, w_mod, b_mod):
    rows, d = c_all.shape
    n = w_mod.shape[1]
    tn = 1024
    return pl.pallas_call(
        _mod_kernel,
        grid=(n // tn,),
        in_specs=[pl.BlockSpec((rows, d), lambda j: (0, 0)),
                  pl.BlockSpec((d, tn), lambda j: (0, j)),
                  pl.BlockSpec((1, tn), lambda j: (0, j))],
        out_specs=pl.BlockSpec((rows, tn), lambda j: (0, j)),
        out_shape=jax.ShapeDtypeStruct((rows, n), _F32),
        compiler_params=_cparams(("parallel",)),
        name="modulation",
    )(c_all, w_mod, b_mod.reshape(1, n))


def _inproj_kernel(x_ref, g_ref, sc_ref, sh_ref, w_ref, rkv_ref, lora_ref, hg_ref):
    x = x_ref[0]
    y = x * lax.rsqrt(jnp.mean(x * x, axis=-1, keepdims=True) + EPS) * g_ref[...]
    h = y * (1.0 + sc_ref[0]) + sh_ref[0]
    u = jnp.dot(h.astype(_BF16), w_ref[...], preferred_element_type=_F32)
    rkv_ref[0] = u[:, :RKV_W]
    lora_ref[0] = u[:, RKV_W:RKV_W + LORA_W]
    hg_ref[0] = u[:, RKV_W + LORA_W:]


def _inproj(x, norm_g, sc, sh, w_in_bf16, tm):
    bsz, l, d = x.shape
    tm = min(tm, l)
    row = lambda b, i: (b, i, 0)
    per_b = lambda b, i: (b, 0, 0)
    return pl.pallas_call(
        _inproj_kernel,
        grid=(bsz, l // tm),
        in_specs=[pl.BlockSpec((1, tm, d), row),
                  pl.BlockSpec((1, d), lambda b, i: (0, 0)),
                  pl.BlockSpec((1, 1, d), per_b),
                  pl.BlockSpec((1, 1, d), per_b),
                  pl.BlockSpec((d, P_IN), lambda b, i: (0, 0))],
        out_specs=[pl.BlockSpec((1, tm, RKV_W), row),
                   pl.BlockSpec((1, tm, LORA_W), row),
                   pl.BlockSpec((1, tm, HG_W), row)],
        out_shape=[jax.ShapeDtypeStruct((bsz, l, RKV_W), _F32),
                   jax.ShapeDtypeStruct((bsz, l, LORA_W), _F32),
                   jax.ShapeDtypeStruct((bsz, l, HG_W), _F32)],
        compiler_params=_cparams(("parallel", "parallel")),
        name="inproj",
    )(x, norm_g.reshape(1, d), sc, sh, w_in_bf16)


def _seg_sum(x, seg_ref, nonneg=False):
    hi = x.astype(_BF16)
    out = jnp.dot(hi, seg_ref[...], preferred_element_type=_F32)
    if nonneg:
        return out
    lo = (x - hi.astype(_F32)).astype(_BF16)
    return out + jnp.dot(lo, seg_ref[...], preferred_element_type=_F32)


def _dot_split(x, w_ref):
    hi = x.astype(_BF16)
    lo = (x - hi.astype(_F32)).astype(_BF16)
    return (jnp.dot(hi, w_ref[0], preferred_element_type=_F32) + jnp.dot(lo, w_ref[0], preferred_element_type=_F32)
            + jnp.dot(hi, w_ref[1], preferred_element_type=_F32))


def _split_bf16(w):
    hi = w.astype(_BF16)
    return jnp.stack([hi, (w - hi.astype(_F32)).astype(_BF16)])


def _rw_prep_kernel(cur_ref, prev_ref, next_ref, lora_ref, kern_ref, w0_ref, w2_ref, a0_ref, a2_ref, g2_ref,
                    kk_ref, ka_ref, rk_ref, seg_ref,
                    ops_out, w_out, bonus_out, g_out, *, grid_conv):
    i = pl.program_id(1)
    tq = cur_ref.shape[0]
    hw = prev_ref.shape[0]
    w = RW_WIDTH
    pmask = jnp.where(i > 0, 1.0, 0.0)
    nmask = jnp.where(i < pl.num_programs(1) - 1, 1.0, 0.0)
    xpos = lax.broadcasted_iota(jnp.int32, (tq + 2 * hw, w), 0) % GRID_W
    conv = []
    for part in range(3):
        cs = slice(part * w, (part + 1) * w)
        xe = jnp.concatenate([prev_ref[:, cs] * pmask, cur_ref[:, cs], next_ref[:, cs] * nmask], axis=0)
        xl = pltpu.roll(xe, 1, 0)
        xr = pltpu.roll(xe, tq + 2 * hw - 1, 0)
        if grid_conv:
            xl = jnp.where(xpos == 0, 0.0, xl)
            xr = jnp.where(xpos == GRID_W - 1, 0.0, xr)
        acc = None
        for dy in range(CONV_K) if grid_conv else (CONV_K // 2,):
            off = hw + (dy - 1) * GRID_W
            for dx, src in enumerate((xl, xe, xr)):
                term = src[off:off + tq] * kern_ref[pl.ds(dy * CONV_K + dx, 1), cs]
                acc = term if acc is None else acc + term
        conv.append(acc)
    r, k, v = conv
    ops_out[0] = r.astype(_BF16)
    ops_out[2] = v.astype(_BF16)

    kk = k * kk_ref[...]
    kk = kk / jnp.maximum(jnp.sqrt(_seg_sum(kk * kk, seg_ref, nonneg=True)), 1e-12)
    ops_out[1] = kk.astype(_BF16)

    lora = lora_ref[...]
    ksum = jnp.zeros_like(k)
    log_decay = []
    for d in range(2):
        wl = lora[:, d * RW_LORA:(d + 1) * RW_LORA]
        al = lora[:, (2 + d) * RW_LORA:(3 + d) * RW_LORA]
        z = w0_ref[pl.ds(d, 1), :] + _dot_split(jnp.tanh(wl), w2_ref.at[d])
        wlog = -jax.nn.softplus(-z) - 0.5
        log_decay.append(-jnp.exp(wlog))
        a = jax.nn.sigmoid(a0_ref[pl.ds(d, 1), :] + _dot_split(al, a2_ref.at[d]))
        kd = k * (1.0 + (a - 1.0) * ka_ref[...])
        ops_out[3 + d] = kd.astype(_BF16)
        ops_out[5 + d] = (kk * a).astype(_BF16)
        ksum = ksum + kd
    w_out[...] = _pack_bf16_pair(log_decay[0], log_decay[1])
    bonus_out[...] = _seg_sum(r * ksum * rk_ref[...], seg_ref) * v
    gd = lora[:, 4 * RW_LORA:]
    g_out[...] = _dot_split(jax.nn.sigmoid(gd), g2_ref)


def _rw_prep(rkv, lora, kern9, w0, w2, a0, a2, g2, k_k, k_a, r_k, seg, tq, grid_conv):
    bsz, l, _ = rkv.shape
    tq = min(tq, l)
    nb = l // tq
    hw = GRID_W
    per = tq // hw
    w = RW_WIDTH
    row = lambda b, i: (b, i, 0)
    const2 = lambda b, i: (0, 0)
    const3 = lambda b, i: (0, 0, 0)
    const4 = lambda b, i: (0, 0, 0, 0)
    tok = pl.BlockSpec((None, tq, w), row)
    tok2 = pl.BlockSpec((2, None, tq, w), lambda b, i: (0, b, i, 0))
    one = jax.ShapeDtypeStruct((bsz, l, w), _F32)
    two = jax.ShapeDtypeStruct((2, bsz, l, w), _F32)
    ops = jax.ShapeDtypeStruct((N_SCAN_OPS, bsz, l, w), _BF16)
    ops_spec = pl.BlockSpec((N_SCAN_OPS, None, tq, w), lambda b, i: (0, b, i, 0))
    vec = lambda t: t.reshape(1, w)
    return pl.pallas_call(
        functools.partial(_rw_prep_kernel, grid_conv=grid_conv),
        grid=(bsz, nb),
        in_specs=[pl.BlockSpec((None, tq, RKV_W), row),
                  pl.BlockSpec((None, hw, RKV_W), lambda b, i: (b, jnp.maximum(i * per - 1, 0), 0)),
                  pl.BlockSpec((None, hw, RKV_W), lambda b, i: (b, jnp.minimum((i + 1) * per, l // hw - 1), 0)),
                  pl.BlockSpec((None, tq, LORA_W), row),
                  pl.BlockSpec((CONV_K * CONV_K, RKV_W), const2),
                  pl.BlockSpec((2, w), const2),
                  pl.BlockSpec((2, 2, RW_LORA, w), const4),
                  pl.BlockSpec((2, w), const2),
                  pl.BlockSpec((2, 2, RW_LORA, w), const4),
                  pl.BlockSpec((2, RW_GATE_LORA, w), const3),
                  pl.BlockSpec((1, w), const2),
                  pl.BlockSpec((1, w), const2),
                  pl.BlockSpec((1, w), const2),
                  pl.BlockSpec((w, w), const2)],
        out_specs=[ops_spec, tok, tok, tok],
        out_shape=[ops, jax.ShapeDtypeStruct((bsz, l, w), jnp.uint32), one, one],
        compiler_params=_cparams(("parallel", "parallel")),
        name="rwkv_prep_grid" if grid_conv else "rwkv_prep_seq",
    )(rkv, rkv, rkv, lora, kern9, w0, w2, a0, a2, g2, vec(k_k), vec(k_a), vec(r_k), seg)


def _rwkv_scan_kernel(*refs, tt, emit, has_init):
    r_in, kk_in, v_in, w_in, k_in, b_in = refs[:6]
    rest = list(refs[6:])
    s0_ref = rest.pop(0) if has_init else None
    out_ref = rest.pop(0)
    s_ref, sa_ref, r_ref, kk_ref, v_ref, k_ref, b_ref, w_ref = rest
    for src, dst in ((r_in, r_ref), (kk_in, kk_ref), (v_in, v_ref), (k_in, k_ref), (b_in, b_ref)):
        dst[...] = src[...].astype(_F32)
    d = pl.program_id(0)
    log_fwd, log_bwd = _unpack_bf16_pair(w_in[...])
    w_ref[...] = jnp.exp(jnp.where(d == 0, log_fwd, log_bwd))
    i = pl.program_id(1)
    n = RW_HEAD_DIM

    @pl.when(i == 0)
    def _():
        s_ref[...] = s0_ref[...] if has_init else jnp.zeros_like(s_ref)

    hv = n // 2
    halves = [pl.ds(h * hv, hv) for h in range(2)]

    def project(t, vs):
        acc = [jnp.zeros((hv, LANES), _F32), jnp.zeros((hv, LANES), _F32)]
        for k in range(n):
            acc[k % 2] = acc[k % 2] + s_ref[k, vs, :] * kk_ref[t, pl.ds(k, 1), :]
        return -(acc[0] + acc[1])

    first = d * (tt - 1)
    for vs in halves:
        sa_ref[vs, :] = project(first, vs)

    def step(j, carry):
        t = j + d * (tt - 1 - 2 * j)
        tn = jnp.clip(t + 1 - 2 * d, 0, tt - 1)
        for vs in halves:
            sa = sa_ref[vs, :]
            vv = v_ref[t, vs, :]
            y = [jnp.zeros((hv, LANES), _F32), jnp.zeros((hv, LANES), _F32)]
            nsa = [jnp.zeros((hv, LANES), _F32), jnp.zeros((hv, LANES), _F32)]
            for k in range(n):
                row = pl.ds(k, 1)
                s_new = s_ref[k, vs, :] * w_ref[t, row, :] + sa * b_ref[t, row, :] + vv * k_ref[t, row, :]
                s_ref[k, vs, :] = s_new
                nsa[k % 2] = nsa[k % 2] + s_new * kk_ref[tn, row, :]
                if emit:
                    y[k % 2] = y[k % 2] + s_new * r_ref[t, row, :]
            sa_ref[vs, :] = -(nsa[0] + nsa[1])
            if emit:
                out_ref[t, vs, :] = (y[0] + y[1]).astype(out_ref.dtype)
        return carry

    lax.fori_loop(0, tt, step, 0)

    if not emit:
        @pl.when(i == pl.num_programs(1) - 1)
        def _():
            out_ref[...] = s_ref[...]


def _rwkv_scan(ops, w, s0, tt, emit):
    _, l, n, c = ops.shape
    nb = l // tt
    op_spec = lambda j, per_dir: pl.BlockSpec(
        (None, tt, n, c), lambda d, i: (j + d * per_dir, _dir_block(d, i, nb), 0, 0))
    pd_spec = pl.BlockSpec((None, tt, n, c), lambda d, i: (d, _dir_block(d, i, nb), 0, 0))
    st_spec = pl.BlockSpec((None, n, n, c), lambda d, i: (d, 0, 0, 0))
    w_spec = pl.BlockSpec((tt, n, c), lambda d, i: (_dir_block(d, i, nb), 0, 0))
    in_specs = [op_spec(0, 0), op_spec(1, 0), op_spec(2, 0), w_spec, op_spec(3, 1), op_spec(5, 1)]
    args = [ops, ops, ops, w, ops, ops]
    if s0 is not None:
        in_specs.append(st_spec)
        args.append(s0)
    if emit:
        out_spec, out_shape = pd_spec, jax.ShapeDtypeStruct((2, l, n, c), _BF16)
    else:
        out_spec, out_shape = st_spec, jax.ShapeDtypeStruct((2, n, n, c), _F32)
    return pl.pallas_call(
        functools.partial(_rwkv_scan_kernel, tt=tt, emit=emit, has_init=s0 is not None),
        grid=(2, nb),
        in_specs=in_specs,
        out_specs=out_spec,
        out_shape=out_shape,
        scratch_shapes=[pltpu.VMEM((n, n, c), _F32), pltpu.VMEM((n, c), _F32)] + [pltpu.VMEM((tt, n, c), _F32)] * 6,
        compiler_params=_cparams(("parallel", "arbitrary")),
        name="rwkv_scan_emit" if emit else "rwkv_scan_state",
    )(*args)


def _hgrn_kernel(*refs, n_chunks, emit, has_init, reverse):
    q_ref, f_ref, i_ref, lb_ref, mats_ref = refs[:5]
    rest = list(refs[5:])
    s0_ref = rest.pop(0) if has_init else None
    out_ref = rest.pop(0)
    st_ref, upd_ref, ent_ref = rest
    c = HG_CHUNK
    tb = n_chunks * c
    blk = pl.program_id(1)

    @pl.when(blk == 0)
    def _():
        st_ref[...] = s0_ref[...] if has_init else jnp.zeros_like(st_ref)

    lb = lb_ref[...]
    omlb = 1.0 - lb
    fr = f_ref[...]
    logf = jnp.log(lb + omlb * jax.nn.sigmoid(fr))
    kf = omlb * jax.nn.sigmoid(-fr)
    hi = logf.astype(_BF16)
    lo = (logf - hi.astype(_F32)).astype(_BF16)
    sums = (jnp.dot(mats_ref[...], hi, preferred_element_type=_F32)
            + jnp.dot(mats_ref[...], lo, preferred_element_type=_F32))
    cum, tot, mid = sums[:tb], sums[tb:2 * tb], sums[2 * tb:]
    dec = jnp.exp(tot)
    kdec = (kf * jnp.exp(tot - cum)).astype(_BF16)
    vb = i_ref[...].astype(_BF16)
    same_head = ((lax.broadcasted_iota(jnp.int32, (LANES, LANES), 0) < HG_DIM)
                 == (lax.broadcasted_iota(jnp.int32, (LANES, LANES), 1) < HG_DIM))
    pairs = [slice(p * LANES, (p + 1) * LANES) for p in range(HG_PAIRS)]
    chunks = [slice(j * c, (j + 1) * c) for j in range(n_chunks)]

    if emit:
        q = q_ref[...]
        qs = q * jax.nn.sigmoid(q) * (HG_DIM ** -0.5)
        qd = qs * jnp.exp(cum - mid)
        kd = (kf * jnp.exp(mid - cum)).astype(_BF16)
        qe = (qs * jnp.exp(cum)).astype(_BF16)
        sub = min(tb, HG_SCORE_ROWS)
        head0 = lax.broadcasted_iota(jnp.int32, (sub, LANES), 1) < HG_DIM
        for g in range(tb // sub):
            rows = slice(g * sub, (g + 1) * sub)
            causal = mats_ref[rows, rows] > 0
            causal2 = jnp.concatenate([causal, causal], axis=0)
            for ls in pairs:
                qd_p = qd[rows, ls]
                q2 = jnp.concatenate([jnp.where(head0, qd_p, 0.0), jnp.where(head0, 0.0, qd_p)], axis=0)
                sc = lax.dot_general(q2.astype(_BF16), kd[rows, ls], _NT, preferred_element_type=_F32)
                sc = jnp.where(causal2, sc, 0.0)
                oi = jnp.dot(sc.astype(_BF16), vb[rows, ls], preferred_element_type=_F32)
                out_ref[rows, ls] = jnp.where(head0, oi[:sub], oi[sub:])

    for j, rows in enumerate(chunks):
        for p, ls in enumerate(pairs):
            upd = lax.dot_general(vb[rows, ls], kdec[rows, ls], _TN, preferred_element_type=_F32)
            upd_ref[j, p] = jnp.where(same_head, upd, 0.0)

    order = list(range(n_chunks))[::-1] if reverse else list(range(n_chunks))
    for p, ls in enumerate(pairs):
        st = st_ref[p]
        for j in order:
            if emit:
                ent_ref[j, p] = st.astype(_BF16)
            st = st * dec[j * c:j * c + 1, ls] + upd_ref[j, p]
        st_ref[p] = st

    if emit:
        for j, rows in enumerate(chunks):
            for p, ls in enumerate(pairs):
                out_ref[rows, ls] += lax.dot_general(qe[rows, ls], ent_ref[j, p], _NT,
                                                     preferred_element_type=_F32)
    else:
        @pl.when(blk == pl.num_programs(1) - 1)
        def _():
            out_ref[...] = st_ref[...]


def _hgrn_mats(tb, reverse):
    t = jnp.arange(tb)
    same = (t[:, None] // HG_CHUNK) == (t[None, :] // HG_CHUNK)
    ref_pos = (t // HG_CHUNK) * HG_CHUNK + HG_CHUNK // 2
    if reverse:
        cum = same & (t[None, :] >= t[:, None])
        mid = same & (t[None, :] >= ref_pos[:, None])
    else:
        cum = same & (t[None, :] <= t[:, None])
        mid = same & (t[None, :] <= ref_pos[:, None])
    return jnp.concatenate([cum, same, mid], axis=0).astype(_BF16)


def _hgrn_scan(hg, lb, s0, tb, emit, reverse):
    bsz, l, _ = hg.shape
    tb = min(tb, l)
    nb = l // tb
    w = HG_WIDTH
    n_chunks = tb // HG_CHUNK
    tblk = (lambda i: nb - 1 - i) if reverse else (lambda i: i)
    col = lambda j: pl.BlockSpec((None, tb, w), lambda b, i: (b, tblk(i), j))
    const = lambda b, i: (0, 0)
    st_spec = pl.BlockSpec((None, HG_PAIRS, LANES, LANES), lambda b, i: (b, 0, 0, 0))
    in_specs = [col(0), col(2 if reverse else 1), col(3), pl.BlockSpec((1, w), const),
                pl.BlockSpec((3 * tb, tb), const)]
    args = [hg, hg, hg, lb, _hgrn_mats(tb, reverse)]
    if s0 is not None:
        in_specs.append(st_spec)
        args.append(s0)
    if emit:
        out_spec = pl.BlockSpec((None, tb, w), lambda b, i: (b, tblk(i), 0))
        out_shape = jax.ShapeDtypeStruct((bsz, l, w), _F32)
    else:
        out_spec = st_spec
        out_shape = jax.ShapeDtypeStruct((bsz, HG_PAIRS, LANES, LANES), _F32)
    return pl.pallas_call(
        functools.partial(_hgrn_kernel, n_chunks=n_chunks, emit=emit, has_init=s0 is not None,
                          reverse=reverse),
        grid=(bsz, nb),
        in_specs=in_specs,
        out_specs=out_spec,
        out_shape=out_shape,
        scratch_shapes=[pltpu.VMEM((HG_PAIRS, LANES, LANES), _F32),
                        pltpu.VMEM((n_chunks, HG_PAIRS, LANES, LANES), _F32),
                        pltpu.VMEM((n_chunks, HG_PAIRS, LANES, LANES), _BF16)],
        compiler_params=_cparams(("parallel", "arbitrary")),
        name=("hgrn_emit" if emit else "hgrn_state") + ("_bwd" if reverse else "_fwd"),
    )(*args)


def _post_kernel(y_ref, bonus_ref, grw_ref, lng_ref, lnb_ref, of_ref, ob_ref, ghg_ref, hgn_ref, seg_ref,
                 x_ref, w_ref, g1_ref, n2_ref, sc_ref, sh_ref, rw_ref, x1_ref, h2_ref, lg_ref):
    inv_n = 1.0 / RW_HEAD_DIM
    ysum = y_ref[0].astype(_F32) + y_ref[1].astype(_F32)
    mu = _seg_sum(ysum, seg_ref) * inv_n
    dlt = ysum - mu
    var = _seg_sum(dlt * dlt, seg_ref, nonneg=True) * inv_n
    yn = dlt * lax.rsqrt(var + RW_GN_EPS) * lng_ref[...] + lnb_ref[...] + bonus_ref[...]
    y_rw = yn * grw_ref[...]
    o = of_ref[...] + ob_ref[...]
    o = o * lax.rsqrt(_seg_sum(o * o, seg_ref, nonneg=True) * inv_n + EPS) * hgn_ref[...]
    g = ghg_ref[...]
    y_hg = o * (g * jax.nn.sigmoid(g))
    y = (jnp.dot(y_rw.astype(_BF16), w_ref[:RW_WIDTH, :], preferred_element_type=_F32)
         + jnp.dot(y_hg.astype(_BF16), w_ref[RW_WIDTH:, :], preferred_element_type=_F32))
    x1 = x_ref[0] + g1_ref[0] * y
    x1_ref[0] = x1
    h2 = x1 * lax.rsqrt(jnp.mean(x1 * x1, axis=-1, keepdims=True) + EPS) * n2_ref[...]
    h2 = h2 * (1.0 + sc_ref[0]) + sh_ref[0]
    h2_ref[...] = _pack_quarters(h2)
    lg_ref[...] = lax.dot_general(rw_ref[...], h2, _NT, preferred_element_type=_F32, precision=_HI)


def _post(y_dirs, bonus, g_rw, lnx_g, lnx_b, o_dirs, hg, hg_norm_row, seg,
          x, w_out_bf16, g1, norm2_g, sc2, sh2, router_wt, tm):
    bsz, l, d = x.shape
    tm = min(tm, l)
    nt = l // tm
    w = RW_WIDTH
    row = lambda b, i: (b, i, 0)
    per_b = lambda b, i: (b, 0, 0)
    const = lambda b, i: (0, 0)
    tok = pl.BlockSpec((None, tm, w), row)
    tok2 = pl.BlockSpec((2, None, tm, w), lambda b, i: (0, b, i, 0))
    vec = pl.BlockSpec((1, w), const)
    return pl.pallas_call(
        _post_kernel,
        grid=(bsz, nt),
        in_specs=[tok2, tok, tok, vec, vec, tok, tok,
                  pl.BlockSpec((None, tm, w), lambda b, i: (b, i, 4)),
                  vec,
                  pl.BlockSpec((w, w), const),
                  pl.BlockSpec((1, tm, d), row),
                  pl.BlockSpec((d, d), const),
                  pl.BlockSpec((1, 1, d), per_b),
                  pl.BlockSpec((1, d), const),
                  pl.BlockSpec((1, 1, d), per_b),
                  pl.BlockSpec((1, 1, d), per_b),
                  pl.BlockSpec((N_EXPERTS, d), const)],
        out_specs=[pl.BlockSpec((1, tm, d), row),
                   pl.BlockSpec((2, tm, d // 4), lambda b, i: (0, b * nt + i, 0)),
                   pl.BlockSpec((N_EXPERTS, tm), lambda b, i: (0, b * nt + i))],
        out_shape=[jax.ShapeDtypeStruct((bsz, l, d), _F32),
                   jax.ShapeDtypeStruct((2, bsz * l, d // 4), jnp.uint32),
                   jax.ShapeDtypeStruct((N_EXPERTS, bsz * l), _F32)],
        compiler_params=_cparams(("parallel", "parallel")),
        name="outproj_norm2_router",
    )(y_dirs, bonus, g_rw, lnx_g.reshape(1, w), lnx_b.reshape(1, w), o_dirs[0], o_dirs[1], hg, hg_norm_row, seg,
      x, w_out_bf16, g1, norm2_g.reshape(1, d), sc2, sh2, router_wt)


def _sublane_all(x, op):
    for s in (4, 2, 1):
        x = op(x, pltpu.roll(x, s, 0))
    return x


def _router_kernel(lg_ref, bias_ref, tri_ref, e_ref, g_ref, rank_ref, cnt_ref):
    ng, gs = N_GROUPS, GROUP_SIZE
    tn = lg_ref.shape[1]
    neg = -jnp.inf
    gidx = lax.broadcasted_iota(jnp.int32, (ng, tn), 0)
    bias = jnp.concatenate([bias_ref[...]] * (tn // LANES), axis=1)
    scores = [jax.nn.sigmoid(lg_ref[j * ng:(j + 1) * ng, :]) for j in range(gs)]
    sel = [scores[j] + bias[j * ng:(j + 1) * ng, :] for j in range(gs)]

    m1 = functools.reduce(jnp.maximum, sel)
    cnt = functools.reduce(jnp.add, [(s == m1).astype(_F32) for s in sel])
    m2 = functools.reduce(jnp.maximum, [jnp.where(s < m1, s, neg) for s in sel])
    gscore = m1 + jnp.where(cnt >= 2.0, m1, m2)

    rank = jnp.zeros((ng, tn), jnp.int32)
    for s in range(1, ng):
        other = pltpu.roll(gscore, s, 0)
        beats = jnp.where(other > gscore, 1, jnp.where((other == gscore) & (gidx >= s), 1, 0))
        rank = rank + beats
    gsel = rank < TOPK_GROUPS

    key = [jnp.where(gsel, s, neg) for s in sel]
    eidx = [gidx * gs + j for j in range(gs)]
    avail = [jnp.ones((ng, tn), jnp.int32) for _ in range(gs)]
    top_e = jnp.zeros((TOP_K, tn), jnp.int32)
    top_s = jnp.zeros((TOP_K, tn), _F32)
    for r in range(TOP_K):
        cur = functools.reduce(jnp.maximum, [jnp.where(avail[j] > 0, key[j], neg) for j in range(gs)])
        mx = _sublane_all(cur, jnp.maximum)
        cand = [jnp.where((avail[j] > 0) & (key[j] == mx), eidx[j], N_EXPERTS) for j in range(gs)]
        mn = _sublane_all(functools.reduce(jnp.minimum, cand), jnp.minimum)
        picked = functools.reduce(jnp.add, [jnp.where(eidx[j] == mn, scores[j], 0.0) for j in range(gs)])
        avail = [jnp.where(eidx[j] == mn, 0, avail[j]) for j in range(gs)]
        top_e = jnp.where(gidx == r, mn, top_e)
        top_s = jnp.where(gidx == r, _sublane_all(picked, jnp.add), top_s)
    den = _sublane_all(top_s, jnp.add)
    e_ref[...] = top_e
    g_ref[...] = top_s / den * ROUTE_SCALE

    @pl.when(pl.program_id(0) == 0)
    def _():
        cnt_ref[...] = jnp.zeros_like(cnt_ref)

    taken = jnp.concatenate([jnp.where(a == 0, 1.0, 0.0) for a in avail], axis=0)
    before = jnp.dot(taken.astype(_BF16), tri_ref[...], preferred_element_type=_F32) - taken
    before = before + jnp.concatenate([cnt_ref[...]] * (tn // LANES), axis=1)
    rank = jnp.zeros((TOP_K, tn), _F32)
    for r in range(TOP_K):
        e_r = jnp.broadcast_to(top_e[r:r + 1, :], (ng, tn))
        hit = functools.reduce(jnp.add, [jnp.where(eidx[j] == e_r, before[j * ng:(j + 1) * ng, :], 0.0)
                                         for j in range(gs)])
        rank = jnp.where(gidx == r, _sublane_all(hit, jnp.add), rank)
    rank_ref[...] = rank.astype(jnp.int32)
    ones = jnp.ones((tn, LANES), _BF16)
    cnt_ref[...] += jnp.dot(taken.astype(_BF16), ones, preferred_element_type=_F32)


def _router(logits_t, bias_rows, tn):
    assert TOP_K == N_GROUPS == SUBLANES
    t = logits_t.shape[1]
    tn = min(tn, t)
    tri = (jnp.arange(tn)[:, None] <= jnp.arange(tn)[None, :]).astype(_BF16)
    out = pl.BlockSpec((TOP_K, tn), lambda i: (0, i))
    cnt = pl.BlockSpec((N_EXPERTS, LANES), lambda i: (0, 0))
    return pl.pallas_call(
        _router_kernel,
        grid=(t // tn,),
        in_specs=[pl.BlockSpec((N_EXPERTS, tn), lambda i: (0, i)), cnt,
                  pl.BlockSpec((tn, tn), lambda i: (0, 0))],
        out_specs=[out, out, out, cnt],
        out_shape=[jax.ShapeDtypeStruct((TOP_K, t), jnp.int32), jax.ShapeDtypeStruct((TOP_K, t), _F32),
                   jax.ShapeDtypeStruct((TOP_K, t), jnp.int32),
                   jax.ShapeDtypeStruct((N_EXPERTS, LANES), _F32)],
        compiler_params=_cparams(("arbitrary",)),
        name="router_topk",
    )(logits_t, bias_rows, tri)


def _sc_mesh():
    return plsc.VectorSubcoreMesh(core_axis_name="c", subcore_axis_name="s")


def _sc_dispatch(x_half_rows, dest_half_rows, n_out):
    n_rows = x_half_rows.shape[0]

    @pl.kernel(out_type=jax.ShapeDtypeStruct((n_out, SC_ROW), x_half_rows.dtype), mesh=_sc_mesh(),
               scratch_types=[])
    def scatter_rows(x_hbm, d_hbm, o_hbm):
        def body(x_vmem, i_vmem):
            for j in range(TOP_K):
                pltpu.sync_copy(x_vmem, o_hbm.at[i_vmem.at[j]])

        pltpu.emit_pipeline(
            body,
            grid=(n_rows // SC_WINDOW,),
            in_specs=[pl.BlockSpec((SC_WINDOW, SC_ROW), lambda i: (i, 0)),
                      pl.BlockSpec((TOP_K, SC_WINDOW), lambda i: (0, i))],
            out_specs=[],
            core_axis_name=("c", "s"),
            dimension_semantics=(pltpu.PARALLEL,),
        )(x_hbm, d_hbm)

    return scatter_rows(x_half_rows, dest_half_rows)


def _sc_gather(y_half_rows, idx):
    n = idx.shape[1]

    @pl.kernel(out_type=jax.ShapeDtypeStruct((n, SC_ROW), y_half_rows.dtype), mesh=_sc_mesh(),
               scratch_types=[])
    def gather_rows(y_hbm, i_hbm, o_hbm):
        def body(i_vmem, o_vmem):
            pltpu.sync_copy(y_hbm.at[i_vmem.at[0]], o_vmem)

        pltpu.emit_pipeline(
            body,
            grid=(n // SC_WINDOW,),
            in_specs=[pl.BlockSpec((1, SC_WINDOW), lambda i: (0, i))],
            out_specs=[pl.BlockSpec((SC_WINDOW, SC_ROW), lambda i: (i, 0))],
            core_axis_name=("c", "s"),
            dimension_semantics=(pltpu.PARALLEL,),
        )(i_hbm, o_hbm)

    return gather_rows(y_half_rows, idx)


def _dispatch_plan(top_e, rank, counts, n_blocks):
    padded = (counts + MOE_BLOCK - 1) // MOE_BLOCK * MOE_BLOCK
    pad_end = jnp.cumsum(padded)
    pad_start = pad_end - padded
    onehot = top_e[:, :, None] == jnp.arange(N_EXPERTS, dtype=jnp.int32)[None, None, :]
    dest = rank + jnp.sum(jnp.where(onehot, pad_start[None, None, :], 0), axis=-1)
    blk_start = jnp.arange(n_blocks, dtype=jnp.int32) * MOE_BLOCK
    blk_e = jnp.minimum(jnp.sum(pad_end[None, :] <= blk_start[:, None], axis=1), N_EXPERTS - 1)
    n_used = pad_end[-1] // MOE_BLOCK
    return dest, jnp.concatenate([blk_e.astype(jnp.int32), n_used[None].astype(jnp.int32)])


def _swiglu_packed(u0, u1, wg_ref, wu_ref, wd_ref):
    x = jnp.concatenate(_unpack_quarters(u0, u1), axis=1).astype(_BF16)
    a = jnp.dot(x, wg_ref[...].astype(_BF16), preferred_element_type=_F32)
    u = jnp.dot(x, wu_ref[...].astype(_BF16), preferred_element_type=_F32)
    act = (a * jax.nn.sigmoid(a) * u).astype(_BF16)
    return jnp.dot(act, wd_ref[...].astype(_BF16), preferred_element_type=_F32)


def _expert_kernel(plan_ref, x_ref, wg_ref, wu_ref, wd_ref, y_ref):
    @pl.when(pl.program_id(0) < plan_ref[pl.num_programs(0)])
    def _():
        y_ref[...] = _pack_quarters(_swiglu_packed(x_ref[0], x_ref[1], wg_ref, wu_ref, wd_ref))


def _experts(plan, x_sorted, wg, wu, wd):
    _, n_rows, q = x_sorted.shape
    n_blocks = n_rows // MOE_BLOCK
    d = 4 * q
    rows = pl.BlockSpec((2, MOE_BLOCK, q), lambda i, plan: (0, i, 0))
    grid_spec = pltpu.PrefetchScalarGridSpec(
        num_scalar_prefetch=1,
        grid=(n_blocks,),
        in_specs=[rows,
                  pl.BlockSpec((None, d, D_EXPERT), lambda i, plan: (plan[i], 0, 0)),
                  pl.BlockSpec((None, d, D_EXPERT), lambda i, plan: (plan[i], 0, 0)),
                  pl.BlockSpec((None, D_EXPERT, d), lambda i, plan: (plan[i], 0, 0))],
        out_specs=rows,
    )
    return pl.pallas_call(
        _expert_kernel,
        grid_spec=grid_spec,
        out_shape=jax.ShapeDtypeStruct((2, n_rows, q), jnp.uint32),
        compiler_params=_cparams(("arbitrary",)),
        name="moe_routed_experts",
    )(plan, x_sorted, wg, wu, wd)


def _combine_kernel(y_ref, gate_ref, h_ref, x1_ref, g2_ref, fg_ref, sg_ref, su_ref, sd_ref, o_ref):
    gate = gate_ref[...]
    acc = None
    for j in range(TOP_K):
        gj = gate[:, j:j + 1]
        parts = [gj * p for p in _unpack_quarters(y_ref[j, 0], y_ref[j, 1])]
        acc = parts if acc is None else [a + p for a, p in zip(acc, parts)]
    moe = jnp.concatenate(acc, axis=1) + _swiglu_packed(h_ref[0], h_ref[1], sg_ref, su_ref, sd_ref)
    x2 = x1_ref[0] + g2_ref[0] * moe
    o_ref[0] = x2 * lax.rsqrt(jnp.mean(x2 * x2, axis=-1, keepdims=True) + EPS) * fg_ref[...]


def _combine(y_rows, gate_tk, h2_packed, x1, g2, final_g, sg, su, sd, tm):
    bsz, l, d = x1.shape
    tm = min(tm, l)
    nt = l // tm
    q = d // 4
    const = lambda b, i: (0, 0)
    return pl.pallas_call(
        _combine_kernel,
        grid=(bsz, nt),
        in_specs=[pl.BlockSpec((TOP_K, 2, tm, q), lambda b, i: (0, 0, b * nt + i, 0)),
                  pl.BlockSpec((tm, TOP_K), lambda b, i: (b * nt + i, 0)),
                  pl.BlockSpec((2, tm, q), lambda b, i: (0, b * nt + i, 0)),
                  pl.BlockSpec((1, tm, d), lambda b, i: (b, i, 0)),
                  pl.BlockSpec((1, 1, d), lambda b, i: (b, 0, 0)),
                  pl.BlockSpec((1, d), const),
                  pl.BlockSpec((d, D_EXPERT), const),
                  pl.BlockSpec((d, D_EXPERT), const),
                  pl.BlockSpec((D_EXPERT, d), const)],
        out_specs=pl.BlockSpec((1, tm, d), lambda b, i: (b, i, 0)),
        out_shape=jax.ShapeDtypeStruct((bsz, l, d), _F32),
        compiler_params=_cparams(("parallel", "parallel")),
        name="moe_combine_shared_final_norm",
    )(y_rows, gate_tk, h2_packed, x1, g2, final_g.reshape(1, d), sg, su, sd)


def _to_chain_lanes(t):
    lead, (b, l, _) = t.shape[:-3], t.shape[-3:]
    nl = len(lead)
    t = t.reshape(*lead, b, l, RW_HEADS, RW_HEAD_DIM)
    t = t.transpose(*range(nl), nl + 1, nl + 3, nl, nl + 2)
    return t.reshape(*lead, l, RW_HEAD_DIM, b * RW_HEADS)


def _from_chain_lanes(t, b):
    l = t.shape[1]
    return t.reshape(2, l, RW_HEAD_DIM, b, RW_HEADS).transpose(0, 3, 1, 4, 2).reshape(2, b, l, RW_WIDTH)


def _scan_operands(prep):
    return _to_chain_lanes(prep[0]), _to_chain_lanes(prep[1])


def kernel(x, c, ctx, c_ctx, w_mod, b_mod, norm1_g, norm2_g, w_in, rw_conv, rw_w0, rw_w2, rw_a0, rw_a2,
           rw_g2, rw_k_k, rw_k_a, rw_r_k, rw_lnx_g, rw_lnx_b, hg_lb_logits, hg_norm_g, w_out, router_w,
           router_b, exp_w_gate, exp_w_up, exp_w_down, sh_w_gate, sh_w_up, sh_w_down, final_norm_g):
    bsz, seq, d = x.shape
    assert w_mod.shape[0] == 1 and bsz * RW_HEADS == LANES
    lyr = 0

    c_all = jnp.concatenate([c, c_ctx[None, :], jnp.zeros((SUBLANES - 1, d), _F32)], axis=0)
    mod = _modulation(c_all, w_mod[lyr], b_mod[lyr])
    sh1, sc1, g1, sh2, sc2, g2 = [m[:, None, :] for m in jnp.split(mod[:bsz], N_MOD, axis=-1)]
    mod_ctx = jnp.broadcast_to(mod[bsz][None, None, :], (bsz, 1, N_MOD * d))
    csh1, csc1 = mod_ctx[..., :d], mod_ctx[..., d:2 * d]

    w_in_bf16 = w_in[lyr].astype(_BF16)
    rkv_lat, lora_lat, hg_lat = _inproj(x, norm1_g[lyr], sc1, sh1, w_in_bf16, tm=256)
    rkv_ctx, lora_ctx, hg_ctx = _inproj(ctx, norm1_g[lyr], csc1, csh1, w_in_bf16, tm=256)

    kern9 = rw_conv[lyr].reshape(CONV_K * CONV_K, RKV_W)
    lane_head = jnp.arange(RW_WIDTH) // RW_HEAD_DIM
    seg = (lane_head[:, None] == lane_head[None, :]).astype(_BF16)
    per_dir_split = lambda t: jnp.swapaxes(_split_bf16(t), 0, 1)
    rw_p = (kern9, rw_w0[lyr], per_dir_split(rw_w2[lyr]), rw_a0[lyr], per_dir_split(rw_a2[lyr]),
            _split_bf16(rw_g2[lyr]), rw_k_k[lyr], rw_k_a[lyr], rw_r_k[lyr], seg)
    prep_lat = _rw_prep(rkv_lat, lora_lat, *rw_p, tq=512, grid_conv=True)
    prep_ctx = _rw_prep(rkv_ctx, lora_ctx, *rw_p, tq=256, grid_conv=False)
    s_rw = _rwkv_scan(*_scan_operands(prep_ctx), None, tt=32, emit=False)
    y_dirs = _from_chain_lanes(_rwkv_scan(*_scan_operands(prep_lat), s_rw, tt=32, emit=True), bsz)

    lb = jnp.cumsum(jax.nn.softmax(hg_lb_logits.astype(_F32), axis=1), axis=1)[:, lyr]
    o_dirs = []
    for dd, rev in enumerate((False, True)):
        s_hg = _hgrn_scan(hg_ctx, lb[dd:dd + 1], None, tb=256, emit=False, reverse=rev)
        o_dirs.append(_hgrn_scan(hg_lat, lb[dd:dd + 1], s_hg, tb=256, emit=True, reverse=rev))

    perm = lambda t: t.reshape(N_GROUPS, GROUP_SIZE, -1).transpose(1, 0, 2).reshape(N_EXPERTS, -1)
    router_wt = perm(router_w[lyr].T)
    bias_rows = jnp.broadcast_to(perm(router_b[lyr][:, None]), (N_EXPERTS, LANES))
    hg_norm_row = jnp.tile(hg_norm_g[lyr], HG_HEADS).reshape(1, HG_WIDTH)
    x1, h2, logits_t = _post(y_dirs, prep_lat[2], prep_lat[3], rw_lnx_g[lyr], rw_lnx_b[lyr], o_dirs, hg_lat,
                             hg_norm_row, seg, x, w_out[lyr].astype(_BF16), g1, norm2_g[lyr], sc2, sh2,
                             router_wt, tm=512)
    top_e, gate, rank, counts_rows = _router(logits_t, bias_rows, tn=512)
    counts = counts_rows[:, 0].astype(jnp.int32).reshape(GROUP_SIZE, N_GROUPS).T.reshape(N_EXPERTS)

    n_tok = bsz * seq
    n_blocks = n_tok * TOP_K // MOE_BLOCK + N_EXPERTS
    n_slots = n_blocks * MOE_BLOCK
    dest, plan = _dispatch_plan(top_e, rank, counts, n_blocks)
    dest_half = (dest[:, None, :] + jnp.array([0, n_slots], jnp.int32)[None, :, None]).reshape(TOP_K, 2 * n_tok)
    x_sorted = _sc_dispatch(h2.reshape(2 * n_tok, SC_ROW), dest_half, 2 * n_slots)
    y_sorted = _experts(plan, x_sorted.reshape(2, n_slots, SC_ROW),
                        exp_w_gate[lyr], exp_w_up[lyr], exp_w_down[lyr])
    y_rows = _sc_gather(y_sorted.reshape(2 * n_slots, SC_ROW), dest_half.reshape(1, TOP_K * 2 * n_tok))
    return _combine(y_rows.reshape(TOP_K, 2, n_tok, SC_ROW), gate.T, h2, x1, g2,
                    final_norm_g, sh_w_gate[lyr], sh_w_up[lyr], sh_w_down[lyr], tm=512)
```

```python
import functools

import jax
import jax.numpy as jnp
from jax import lax
from jax.experimental import pallas as pl
from jax.experimental.pallas import tpu as pltpu
from jax.experimental.pallas import tpu_sc as plsc

D_MODEL = 1024
GRID_W = 64
N_MOD = 6
EPS = 1e-6

RW_HEAD_DIM = 64
RW_HEADS = 8
RW_WIDTH = 512
RW_LORA = 64
RW_GATE_LORA = 128
RW_GN_EPS = 1e-5 * RW_HEAD_DIM
N_SCAN_OPS = 4
CONV_K = 3

HG_HEADS = 8
HG_DIM = 64
HG_WIDTH = 512
HG_CHUNK = 32
HG_PAIRS = HG_HEADS // 2
HG_SCORE_ROWS = 128

RKV_W = 3 * RW_WIDTH
LORA_W = 4 * RW_LORA + RW_GATE_LORA
HG_W = 5 * HG_WIDTH
P_IN = RKV_W + LORA_W + HG_W

N_EXPERTS = 64
N_GROUPS = 8
GROUP_SIZE = N_EXPERTS // N_GROUPS
TOPK_GROUPS = 4
TOP_K = 8
D_EXPERT = 256
ROUTE_SCALE = 2.5
MOE_BLOCK = 1024
SC_ROW = 256
SC_WINDOW = 128

LANES = 128
SUBLANES = 8
VMEM_LIMIT = 48 * 1024 * 1024

_HI = lax.Precision.HIGHEST
_F32 = jnp.float32
_BF16 = jnp.bfloat16
_NT = (((1,), (1,)), ((), ()))
_TN = (((0,), (0,)), ((), ()))


def _cparams(sem):
    return pltpu.CompilerParams(dimension_semantics=sem, vmem_limit_bytes=VMEM_LIMIT)


def _pack_bf16_pair(lo, hi):
    return pltpu.pack_elementwise([lo, hi], packed_dtype=_BF16)


def _unpack_bf16_pair(u):
    return tuple(pltpu.unpack_elementwise(u, index=i, packed_dtype=_BF16, unpacked_dtype=_F32) for i in range(2))


def _pack_quarters(x):
    q = x.shape[1] // 4
    return jnp.stack([_pack_bf16_pair(x[:, 2 * h * q:(2 * h + 1) * q], x[:, (2 * h + 1) * q:(2 * h + 2) * q])
                      for h in range(2)])


def _unpack_quarters(u0, u1):
    return _unpack_bf16_pair(u0) + _unpack_bf16_pair(u1)


def _dir_block(d, i, nb):
    return i + d * (nb - 1 - 2 * i)


def _mod_kernel(c_ref, w_ref, b_ref, o_ref):
    c = c_ref[...]
    s = c * jax.nn.sigmoid(c)
    o_ref[...] = jnp.dot(s, w_ref[...], preferred_element_type=_F32, precision=_HI) + b_ref[...]


def _modulation(c_all, w_mod, b_mod):
    rows, d = c_all.shape
    n = w_mod.shape[1]
    tn = 1024
    return pl.pallas_call(
        _mod_kernel,
        grid=(n // tn,),
        in_specs=[pl.BlockSpec((rows, d), lambda j: (0, 0)),
                  pl.BlockSpec((d, tn), lambda j: (0, j)),
                  pl.BlockSpec((1, tn), lambda j: (0, j))],
        out_specs=pl.BlockSpec((rows, tn), lambda j: (0, j)),
        out_shape=jax.ShapeDtypeStruct((rows, n), _F32),
        compiler_params=_cparams(("parallel",)),
        name="modulation",
    )(c_all, w_mod, b_mod.reshape(1, n))


def _inproj_kernel(x_ref, g_ref, sc_ref, sh_ref, w_ref, rkv_ref, lora_ref, hg_ref):
    x = x_ref[0]
    y = x * lax.rsqrt(jnp.mean(x * x, axis=-1, keepdims=True) + EPS) * g_ref[...]
    h = y * (1.0 + sc_ref[0]) + sh_ref[0]
    u = jnp.dot(h.astype(_BF16), w_ref[...], preferred_element_type=_F32)
    rkv_ref[0] = u[:, :RKV_W]
    lora_ref[0] = u[:, RKV_W:RKV_W + LORA_W]
    hg_ref[0] = u[:, RKV_W + LORA_W:]


def _inproj(x, norm_g, sc, sh, w_in_bf16, tm):
    bsz, l, d = x.shape
    tm = min(tm, l)
    row = lambda b, i: (b, i, 0)
    per_b = lambda b, i: (b, 0, 0)
    return pl.pallas_call(
        _inproj_kernel,
        grid=(bsz, l // tm),
        in_specs=[pl.BlockSpec((1, tm, d), row),
                  pl.BlockSpec((1, d), lambda b, i: (0, 0)),
                  pl.BlockSpec((1, 1, d), per_b),
                  pl.BlockSpec((1, 1, d), per_b),
                  pl.BlockSpec((d, P_IN), lambda b, i: (0, 0))],
        out_specs=[pl.BlockSpec((1, tm, RKV_W), row),
                   pl.BlockSpec((1, tm, LORA_W), row),
                   pl.BlockSpec((1, tm, HG_W), row)],
        out_shape=[jax.ShapeDtypeStruct((bsz, l, RKV_W), _F32),
                   jax.ShapeDtypeStruct((bsz, l, LORA_W), _F32),
                   jax.ShapeDtypeStruct((bsz, l, HG_W), _F32)],
        compiler_params=_cparams(("parallel", "parallel")),
        name="inproj",
    )(x, norm_g.reshape(1, d), sc, sh, w_in_bf16)


def _seg_sum(x, seg_ref, nonneg=False):
    hi = x.astype(_BF16)
    out = jnp.dot(hi, seg_ref[...], preferred_element_type=_F32)
    if nonneg:
        return out
    lo = (x - hi.astype(_F32)).astype(_BF16)
    return out + jnp.dot(lo, seg_ref[...], preferred_element_type=_F32)


def _dot_split(x, w_ref):
    hi = x.astype(_BF16)
    lo = (x - hi.astype(_F32)).astype(_BF16)
    return (jnp.dot(hi, w_ref[0], preferred_element_type=_F32) + jnp.dot(lo, w_ref[0], preferred_element_type=_F32)
            + jnp.dot(hi, w_ref[1], preferred_element_type=_F32))


def _split_bf16(w):
    hi = w.astype(_BF16)
    return jnp.stack([hi, (w - hi.astype(_F32)).astype(_BF16)])


def _rw_prep_kernel(cur_ref, prev_ref, next_ref, lora_ref, kern_ref, w0_ref, w2_ref, a0_ref, a2_ref, g2_ref,
                    kk_ref, ka_ref, rk_ref, seg_ref,
                    ops_out, w_out, bonus_out, g_out, *, grid_conv):
    i = pl.program_id(1)
    tq = cur_ref.shape[0]
    hw = prev_ref.shape[0]
    w = RW_WIDTH
    pmask = jnp.where(i > 0, 1.0, 0.0)
    nmask = jnp.where(i < pl.num_programs(1) - 1, 1.0, 0.0)
    xpos = lax.broadcasted_iota(jnp.int32, (tq + 2 * hw, w), 0) % GRID_W
    conv = []
    for part in range(3):
        cs = slice(part * w, (part + 1) * w)
        xe = jnp.concatenate([prev_ref[:, cs] * pmask, cur_ref[:, cs], next_ref[:, cs] * nmask], axis=0)
        xl = pltpu.roll(xe, 1, 0)
        xr = pltpu.roll(xe, tq + 2 * hw - 1, 0)
        if grid_conv:
            xl = jnp.where(xpos == 0, 0.0, xl)
            xr = jnp.where(xpos == GRID_W - 1, 0.0, xr)
        acc = None
        for dy in range(CONV_K) if grid_conv else (CONV_K // 2,):
            off = hw + (dy - 1) * GRID_W
            for dx, src in enumerate((xl, xe, xr)):
                term = src[off:off + tq] * kern_ref[pl.ds(dy * CONV_K + dx, 1), cs]
                acc = term if acc is None else acc + term
        conv.append(acc)
    r, k, v = conv
    ops_out[1] = _pack_bf16_pair(v, v)

    kk = k * kk_ref[...]
    kk = kk / jnp.maximum(jnp.sqrt(_seg_sum(kk * kk, seg_ref, nonneg=True)), 1e-12)
    ops_out[0] = _pack_bf16_pair(r, kk)

    lora = lora_ref[...]
    ksum = jnp.zeros_like(k)
    log_decay = []
    for d in range(2):
        wl = lora[:, d * RW_LORA:(d + 1) * RW_LORA]
        al = lora[:, (2 + d) * RW_LORA:(3 + d) * RW_LORA]
        z = w0_ref[pl.ds(d, 1), :] + _dot_split(jnp.tanh(wl), w2_ref.at[d])
        wlog = -jax.nn.softplus(-z) - 0.5
        log_decay.append(-jnp.exp(wlog))
        a = jax.nn.sigmoid(a0_ref[pl.ds(d, 1), :] + _dot_split(al, a2_ref.at[d]))
        kd = k * (1.0 + (a - 1.0) * ka_ref[...])
        ops_out[2 + d] = _pack_bf16_pair(kd, kk * a)
        ksum = ksum + kd
    w_out[...] = _pack_bf16_pair(log_decay[0], log_decay[1])
    bonus_out[...] = _seg_sum(r * ksum * rk_ref[...], seg_ref) * v
    gd = lora[:, 4 * RW_LORA:]
    g_out[...] = _dot_split(jax.nn.sigmoid(gd), g2_ref)


def _rw_prep(rkv, lora, kern9, w0, w2, a0, a2, g2, k_k, k_a, r_k, seg, tq, grid_conv):
    bsz, l, _ = rkv.shape
    tq = min(tq, l)
    nb = l // tq
    hw = GRID_W
    per = tq // hw
    w = RW_WIDTH
    row = lambda b, i: (b, i, 0)
    const2 = lambda b, i: (0, 0)
    const3 = lambda b, i: (0, 0, 0)
    const4 = lambda b, i: (0, 0, 0, 0)
    tok = pl.BlockSpec((None, tq, w), row)
    tok2 = pl.BlockSpec((2, None, tq, w), lambda b, i: (0, b, i, 0))
    one = jax.ShapeDtypeStruct((bsz, l, w), _F32)
    two = jax.ShapeDtypeStruct((2, bsz, l, w), _F32)
    ops = jax.ShapeDtypeStruct((N_SCAN_OPS, bsz, l, w), jnp.uint32)
    ops_spec = pl.BlockSpec((N_SCAN_OPS, None, tq, w), lambda b, i: (0, b, i, 0))
    vec = lambda t: t.reshape(1, w)
    return pl.pallas_call(
        functools.partial(_rw_prep_kernel, grid_conv=grid_conv),
        grid=(bsz, nb),
        in_specs=[pl.BlockSpec((None, tq, RKV_W), row),
                  pl.BlockSpec((None, hw, RKV_W), lambda b, i: (b, jnp.maximum(i * per - 1, 0), 0)),
                  pl.BlockSpec((None, hw, RKV_W), lambda b, i: (b, jnp.minimum((i + 1) * per, l // hw - 1), 0)),
                  pl.BlockSpec((None, tq, LORA_W), row),
                  pl.BlockSpec((CONV_K * CONV_K, RKV_W), const2),
                  pl.BlockSpec((2, w), const2),
                  pl.BlockSpec((2, 2, RW_LORA, w), const4),
                  pl.BlockSpec((2, w), const2),
                  pl.BlockSpec((2, 2, RW_LORA, w), const4),
                  pl.BlockSpec((2, RW_GATE_LORA, w), const3),
                  pl.BlockSpec((1, w), const2),
                  pl.BlockSpec((1, w), const2),
                  pl.BlockSpec((1, w), const2),
                  pl.BlockSpec((w, w), const2)],
        out_specs=[ops_spec, tok, tok, tok],
        out_shape=[ops, jax.ShapeDtypeStruct((bsz, l, w), jnp.uint32), one, one],
        compiler_params=_cparams(("parallel", "parallel")),
        name="rwkv_prep_grid" if grid_conv else "rwkv_prep_seq",
    )(rkv, rkv, rkv, lora, kern9, w0, w2, a0, a2, g2, vec(k_k), vec(k_a), vec(r_k), seg)


def _rwkv_scan_kernel(*refs, tt, emit, has_init):
    rkk_in, v_in, w_in, kb_in = refs[:4]
    rest = list(refs[4:])
    s0_ref = rest.pop(0) if has_init else None
    out_ref = rest.pop(0)
    s_ref, sa_ref, r_ref, kk_ref, v_ref, k_ref, b_ref, w_ref = rest
    r_ref[...], kk_ref[...] = _unpack_bf16_pair(rkk_in[...])
    v_ref[...] = _unpack_bf16_pair(v_in[...])[0]
    k_ref[...], b_ref[...] = _unpack_bf16_pair(kb_in[...])
    d = pl.program_id(0)
    log_fwd, log_bwd = _unpack_bf16_pair(w_in[...])
    w_ref[...] = jnp.exp(jnp.where(d == 0, log_fwd, log_bwd))
    i = pl.program_id(1)
    n = RW_HEAD_DIM

    @pl.when(i == 0)
    def _():
        s_ref[...] = s0_ref[...] if has_init else jnp.zeros_like(s_ref)

    hv = n // 2
    halves = [pl.ds(h * hv, hv) for h in range(2)]

    def project(t, vs):
        acc = [jnp.zeros((hv, LANES), _F32), jnp.zeros((hv, LANES), _F32)]
        for k in range(n):
            acc[k % 2] = acc[k % 2] + s_ref[k, vs, :] * kk_ref[t, pl.ds(k, 1), :]
        return -(acc[0] + acc[1])

    first = d * (tt - 1)
    for vs in halves:
        sa_ref[vs, :] = project(first, vs)

    def step(j, carry):
        t = j + d * (tt - 1 - 2 * j)
        tn = jnp.clip(t + 1 - 2 * d, 0, tt - 1)
        for vs in halves:
            sa = sa_ref[vs, :]
            vv = v_ref[t, vs, :]
            y = [jnp.zeros((hv, LANES), _F32), jnp.zeros((hv, LANES), _F32)]
            nsa = [jnp.zeros((hv, LANES), _F32), jnp.zeros((hv, LANES), _F32)]
            for k in range(n):
                row = pl.ds(k, 1)
                s_new = s_ref[k, vs, :] * w_ref[t, row, :] + sa * b_ref[t, row, :] + vv * k_ref[t, row, :]
                s_ref[k, vs, :] = s_new
                nsa[k % 2] = nsa[k % 2] + s_new * kk_ref[tn, row, :]
                if emit:
                    y[k % 2] = y[k % 2] + s_new * r_ref[t, row, :]
            sa_ref[vs, :] = -(nsa[0] + nsa[1])
            if emit:
                out_ref[t, vs, :] = (y[0] + y[1]).astype(out_ref.dtype)
        return carry

    lax.fori_loop(0, tt, step, 0)

    if not emit:
        @pl.when(i == pl.num_programs(1) - 1)
        def _():
            out_ref[...] = s_ref[...]


def _rwkv_scan(ops, w, s0, tt, emit):
    _, l, n, c = ops.shape
    nb = l // tt
    op_spec = lambda j, per_dir: pl.BlockSpec(
        (None, tt, n, c), lambda d, i: (j + d * per_dir, _dir_block(d, i, nb), 0, 0))
    pd_spec = pl.BlockSpec((None, tt, n, c), lambda d, i: (d, _dir_block(d, i, nb), 0, 0))
    st_spec = pl.BlockSpec((None, n, n, c), lambda d, i: (d, 0, 0, 0))
    w_spec = pl.BlockSpec((tt, n, c), lambda d, i: (_dir_block(d, i, nb), 0, 0))
    in_specs = [op_spec(0, 0), op_spec(1, 0), w_spec, op_spec(2, 1)]
    args = [ops, ops, w, ops]
    if s0 is not None:
        in_specs.append(st_spec)
        args.append(s0)
    if emit:
        out_spec, out_shape = pd_spec, jax.ShapeDtypeStruct((2, l, n, c), _BF16)
    else:
        out_spec, out_shape = st_spec, jax.ShapeDtypeStruct((2, n, n, c), _F32)
    return pl.pallas_call(
        functools.partial(_rwkv_scan_kernel, tt=tt, emit=emit, has_init=s0 is not None),
        grid=(2, nb),
        in_specs=in_specs,
        out_specs=out_spec,
        out_shape=out_shape,
        scratch_shapes=[pltpu.VMEM((n, n, c), _F32), pltpu.VMEM((n, c), _F32)] + [pltpu.VMEM((tt, n, c), _F32)] * 6,
        compiler_params=_cparams(("parallel", "arbitrary")),
        name="rwkv_scan_emit" if emit else "rwkv_scan_state",
    )(*args)


def _hgrn_kernel(*refs, n_chunks, emit, has_init, reverse):
    q_ref, f_ref, i_ref, lb_ref, mats_ref = refs[:5]
    rest = list(refs[5:])
    s0_ref = rest.pop(0) if has_init else None
    out_ref = rest.pop(0)
    st_ref, upd_ref, ent_ref = rest
    c = HG_CHUNK
    tb = n_chunks * c
    blk = pl.program_id(1)

    @pl.when(blk == 0)
    def _():
        st_ref[...] = s0_ref[...] if has_init else jnp.zeros_like(st_ref)

    lb = lb_ref[...]
    omlb = 1.0 - lb
    fr = f_ref[...]
    logf = jnp.log(lb + omlb * jax.nn.sigmoid(fr))
    kf = omlb * jax.nn.sigmoid(-fr)
    hi = logf.astype(_BF16)
    lo = (logf - hi.astype(_F32)).astype(_BF16)
    sums = (jnp.dot(mats_ref[...], hi, preferred_element_type=_F32)
            + jnp.dot(mats_ref[...], lo, preferred_element_type=_F32))
    cum, tot, mid = sums[:tb], sums[tb:2 * tb], sums[2 * tb:]
    dec = jnp.exp(tot)
    kdec = (kf * jnp.exp(tot - cum)).astype(_BF16)
    vb = i_ref[...].astype(_BF16)
    same_head = ((lax.broadcasted_iota(jnp.int32, (LANES, LANES), 0) < HG_DIM)
                 == (lax.broadcasted_iota(jnp.int32, (LANES, LANES), 1) < HG_DIM))
    pairs = [slice(p * LANES, (p + 1) * LANES) for p in range(HG_PAIRS)]
    chunks = [slice(j * c, (j + 1) * c) for j in range(n_chunks)]

    if emit:
        q = q_ref[...]
        qs = q * jax.nn.sigmoid(q) * (HG_DIM ** -0.5)
        qd = qs * jnp.exp(cum - mid)
        kd = (kf * jnp.exp(mid - cum)).astype(_BF16)
        qe = (qs * jnp.exp(cum)).astype(_BF16)
        sub = min(tb, HG_SCORE_ROWS)
        head0 = lax.broadcasted_iota(jnp.int32, (sub, LANES), 1) < HG_DIM
        for g in range(tb // sub):
            rows = slice(g * sub, (g + 1) * sub)
            causal = mats_ref[rows, rows] > 0
            causal2 = jnp.concatenate([causal, causal], axis=0)
            for ls in pairs:
                qd_p = qd[rows, ls]
                q2 = jnp.concatenate([jnp.where(head0, qd_p, 0.0), jnp.where(head0, 0.0, qd_p)], axis=0)
                sc = lax.dot_general(q2.astype(_BF16), kd[rows, ls], _NT, preferred_element_type=_F32)
                sc = jnp.where(causal2, sc, 0.0)
                oi = jnp.dot(sc.astype(_BF16), vb[rows, ls], preferred_element_type=_F32)
                out_ref[rows, ls] = jnp.where(head0, oi[:sub], oi[sub:])

    for j, rows in enumerate(chunks):
        for p, ls in enumerate(pairs):
            upd = lax.dot_general(vb[rows, ls], kdec[rows, ls], _TN, preferred_element_type=_F32)
            upd_ref[j, p] = jnp.where(same_head, upd, 0.0)

    order = list(range(n_chunks))[::-1] if reverse else list(range(n_chunks))
    for p, ls in enumerate(pairs):
        st = st_ref[p]
        for j in order:
            if emit:
                ent_ref[j, p] = st.astype(_BF16)
            st = st * dec[j * c:j * c + 1, ls] + upd_ref[j, p]
        st_ref[p] = st

    if emit:
        for j, rows in enumerate(chunks):
            for p, ls in enumerate(pairs):
                out_ref[rows, ls] += lax.dot_general(qe[rows, ls], ent_ref[j, p], _NT,
                                                     preferred_element_type=_F32)
    else:
        @pl.when(blk == pl.num_programs(1) - 1)
        def _():
            out_ref[...] = st_ref[...]


def _hgrn_mats(tb, reverse):
    t = jnp.arange(tb)
    same = (t[:, None] // HG_CHUNK) == (t[None, :] // HG_CHUNK)
    ref_pos = (t // HG_CHUNK) * HG_CHUNK + HG_CHUNK // 2
    if reverse:
        cum = same & (t[None, :] >= t[:, None])
        mid = same & (t[None, :] >= ref_pos[:, None])
    else:
        cum = same & (t[None, :] <= t[:, None])
        mid = same & (t[None, :] <= ref_pos[:, None])
    return jnp.concatenate([cum, same, mid], axis=0).astype(_BF16)


def _hgrn_scan(hg, lb, s0, tb, emit, reverse):
    bsz, l, _ = hg.shape
    tb = min(tb, l)
    nb = l // tb
    w = HG_WIDTH
    n_chunks = tb // HG_CHUNK
    tblk = (lambda i: nb - 1 - i) if reverse else (lambda i: i)
    col = lambda j: pl.BlockSpec((None, tb, w), lambda b, i: (b, tblk(i), j))
    const = lambda b, i: (0, 0)
    st_spec = pl.BlockSpec((None, HG_PAIRS, LANES, LANES), lambda b, i: (b, 0, 0, 0))
    in_specs = [col(0), col(2 if reverse else 1), col(3), pl.BlockSpec((1, w), const),
                pl.BlockSpec((3 * tb, tb), const)]
    args = [hg, hg, hg, lb, _hgrn_mats(tb, reverse)]
    if s0 is not None:
        in_specs.append(st_spec)
        args.append(s0)
    if emit:
        out_spec = pl.BlockSpec((None, tb, w), lambda b, i: (b, tblk(i), 0))
        out_shape = jax.ShapeDtypeStruct((bsz, l, w), _F32)
    else:
        out_spec = st_spec
        out_shape = jax.ShapeDtypeStruct((bsz, HG_PAIRS, LANES, LANES), _F32)
    return pl.pallas_call(
        functools.partial(_hgrn_kernel, n_chunks=n_chunks, emit=emit, has_init=s0 is not None,
                          reverse=reverse),
        grid=(bsz, nb),
        in_specs=in_specs,
        out_specs=out_spec,
        out_shape=out_shape,
        scratch_shapes=[pltpu.VMEM((HG_PAIRS, LANES, LANES), _F32),
                        pltpu.VMEM((n_chunks, HG_PAIRS, LANES, LANES), _F32),
                        pltpu.VMEM((n_chunks, HG_PAIRS, LANES, LANES), _BF16)],
        compiler_params=_cparams(("parallel", "arbitrary")),
        name=("hgrn_emit" if emit else "hgrn_state") + ("_bwd" if reverse else "_fwd"),
    )(*args)


def _post_kernel(y_ref, bonus_ref, grw_ref, lng_ref, lnb_ref, of_ref, ob_ref, ghg_ref, hgn_ref, seg_ref,
                 x_ref, w_ref, g1_ref, n2_ref, sc_ref, sh_ref, rw_ref, x1_ref, h2_ref, lg_ref):
    inv_n = 1.0 / RW_HEAD_DIM
    ysum = y_ref[0].astype(_F32) + y_ref[1].astype(_F32)
    mu = _seg_sum(ysum, seg_ref) * inv_n
    dlt = ysum - mu
    var = _seg_sum(dlt * dlt, seg_ref, nonneg=True) * inv_n
    yn = dlt * lax.rsqrt(var + RW_GN_EPS) * lng_ref[...] + lnb_ref[...] + bonus_ref[...]
    y_rw = yn * grw_ref[...]
    o = of_ref[...] + ob_ref[...]
    o = o * lax.rsqrt(_seg_sum(o * o, seg_ref, nonneg=True) * inv_n + EPS) * hgn_ref[...]
    g = ghg_ref[...]
    y_hg = o * (g * jax.nn.sigmoid(g))
    y = (jnp.dot(y_rw.astype(_BF16), w_ref[:RW_WIDTH, :], preferred_element_type=_F32)
         + jnp.dot(y_hg.astype(_BF16), w_ref[RW_WIDTH:, :], preferred_element_type=_F32))
    x1 = x_ref[0] + g1_ref[0] * y
    x1_ref[0] = x1
    h2 = x1 * lax.rsqrt(jnp.mean(x1 * x1, axis=-1, keepdims=True) + EPS) * n2_ref[...]
    h2 = h2 * (1.0 + sc_ref[0]) + sh_ref[0]
    h2_ref[...] = _pack_quarters(h2)
    lg_ref[...] = lax.dot_general(rw_ref[...], h2, _NT, preferred_element_type=_F32, precision=_HI)


def _post(y_dirs, bonus, g_rw, lnx_g, lnx_b, o_dirs, hg, hg_norm_row, seg,
          x, w_out_bf16, g1, norm2_g, sc2, sh2, router_wt, tm):
    bsz, l, d = x.shape
    tm = min(tm, l)
    nt = l // tm
    w = RW_WIDTH
    row = lambda b, i: (b, i, 0)
    per_b = lambda b, i: (b, 0, 0)
    const = lambda b, i: (0, 0)
    tok = pl.BlockSpec((None, tm, w), row)
    tok2 = pl.BlockSpec((2, None, tm, w), lambda b, i: (0, b, i, 0))
    vec = pl.BlockSpec((1, w), const)
    return pl.pallas_call(
        _post_kernel,
        grid=(bsz, nt),
        in_specs=[tok2, tok, tok, vec, vec, tok, tok,
                  pl.BlockSpec((None, tm, w), lambda b, i: (b, i, 4)),
                  vec,
                  pl.BlockSpec((w, w), const),
                  pl.BlockSpec((1, tm, d), row),
                  pl.BlockSpec((d, d), const),
                  pl.BlockSpec((1, 1, d), per_b),
                  pl.BlockSpec((1, d), const),
                  pl.BlockSpec((1, 1, d), per_b),
                  pl.BlockSpec((1, 1, d), per_b),
                  pl.BlockSpec((N_EXPERTS, d), const)],
        out_specs=[pl.BlockSpec((1, tm, d), row),
                   pl.BlockSpec((2, tm, d // 4), lambda b, i: (0, b * nt + i, 0)),
                   pl.BlockSpec((N_EXPERTS, tm), lambda b, i: (0, b * nt + i))],
        out_shape=[jax.ShapeDtypeStruct((bsz, l, d), _F32),
                   jax.ShapeDtypeStruct((2, bsz * l, d // 4), jnp.uint32),
                   jax.ShapeDtypeStruct((N_EXPERTS, bsz * l), _F32)],
        compiler_params=_cparams(("parallel", "parallel")),
        name="outproj_norm2_router",
    )(y_dirs, bonus, g_rw, lnx_g.reshape(1, w), lnx_b.reshape(1, w), o_dirs[0], o_dirs[1], hg, hg_norm_row, seg,
      x, w_out_bf16, g1, norm2_g.reshape(1, d), sc2, sh2, router_wt)


def _sublane_all(x, op):
    for s in (4, 2, 1):
        x = op(x, pltpu.roll(x, s, 0))
    return x


def _router_kernel(lg_ref, bias_ref, tri_ref, e_ref, g_ref, rank_ref, cnt_ref):
    ng, gs = N_GROUPS, GROUP_SIZE
    tn = lg_ref.shape[1]
    neg = -jnp.inf
    gidx = lax.broadcasted_iota(jnp.int32, (ng, tn), 0)
    bias = jnp.concatenate([bias_ref[...]] * (tn // LANES), axis=1)
    scores = [jax.nn.sigmoid(lg_ref[j * ng:(j + 1) * ng, :]) for j in range(gs)]
    sel = [scores[j] + bias[j * ng:(j + 1) * ng, :] for j in range(gs)]

    m1 = functools.reduce(jnp.maximum, sel)
    cnt = functools.reduce(jnp.add, [(s == m1).astype(_F32) for s in sel])
    m2 = functools.reduce(jnp.maximum, [jnp.where(s < m1, s, neg) for s in sel])
    gscore = m1 + jnp.where(cnt >= 2.0, m1, m2)

    rank = jnp.zeros((ng, tn), jnp.int32)
    for s in range(1, ng):
        other = pltpu.roll(gscore, s, 0)
        beats = jnp.where(other > gscore, 1, jnp.where((other == gscore) & (gidx >= s), 1, 0))
        rank = rank + beats
    gsel = rank < TOPK_GROUPS

    key = [jnp.where(gsel, s, neg) for s in sel]
    eidx = [gidx * gs + j for j in range(gs)]
    avail = [jnp.ones((ng, tn), jnp.int32) for _ in range(gs)]
    top_e = jnp.zeros((TOP_K, tn), jnp.int32)
    top_s = jnp.zeros((TOP_K, tn), _F32)
    for r in range(TOP_K):
        cur = functools.reduce(jnp.maximum, [jnp.where(avail[j] > 0, key[j], neg) for j in range(gs)])
        mx = _sublane_all(cur, jnp.maximum)
        cand = [jnp.where((avail[j] > 0) & (key[j] == mx), eidx[j], N_EXPERTS) for j in range(gs)]
        mn = _sublane_all(functools.reduce(jnp.minimum, cand), jnp.minimum)
        picked = functools.reduce(jnp.add, [jnp.where(eidx[j] == mn, scores[j], 0.0) for j in range(gs)])
        avail = [jnp.where(eidx[j] == mn, 0, avail[j]) for j in range(gs)]
        top_e = jnp.where(gidx == r, mn, top_e)
        top_s = jnp.where(gidx == r, _sublane_all(picked, jnp.add), top_s)
    den = _sublane_all(top_s, jnp.add)
    e_ref[...] = top_e
    g_ref[...] = top_s / den * ROUTE_SCALE

    @pl.when(pl.program_id(0) == 0)
    def _():
        cnt_ref[...] = jnp.zeros_like(cnt_ref)

    taken = jnp.concatenate([jnp.where(a == 0, 1.0, 0.0) for a in avail], axis=0)
    before = jnp.dot(taken.astype(_BF16), tri_ref[...], preferred_element_type=_F32) - taken
    before = before + jnp.concatenate([cnt_ref[...]] * (tn // LANES), axis=1)
    rank = jnp.zeros((TOP_K, tn), _F32)
    for r in range(TOP_K):
        e_r = jnp.broadcast_to(top_e[r:r + 1, :], (ng, tn))
        hit = functools.reduce(jnp.add, [jnp.where(eidx[j] == e_r, before[j * ng:(j + 1) * ng, :], 0.0)
                                         for j in range(gs)])
        rank = jnp.where(gidx == r, _sublane_all(hit, jnp.add), rank)
    rank_ref[...] = rank.astype(jnp.int32)
    ones = jnp.ones((tn, LANES), _BF16)
    cnt_ref[...] += jnp.dot(taken.astype(_BF16), ones, preferred_element_type=_F32)


def _router(logits_t, bias_rows, tn):
    assert TOP_K == N_GROUPS == SUBLANES
    t = logits_t.shape[1]
    tn = min(tn, t)
    tri = (jnp.arange(tn)[:, None] <= jnp.arange(tn)[None, :]).astype(_BF16)
    out = pl.BlockSpec((TOP_K, tn), lambda i: (0, i))
    cnt = pl.BlockSpec((N_EXPERTS, LANES), lambda i: (0, 0))
    return pl.pallas_call(
        _router_kernel,
        grid=(t // tn,),
        in_specs=[pl.BlockSpec((N_EXPERTS, tn), lambda i: (0, i)), cnt,
                  pl.BlockSpec((tn, tn), lambda i: (0, 0))],
        out_specs=[out, out, out, cnt],
        out_shape=[jax.ShapeDtypeStruct((TOP_K, t), jnp.int32), jax.ShapeDtypeStruct((TOP_K, t), _F32),
                   jax.ShapeDtypeStruct((TOP_K, t), jnp.int32),
                   jax.ShapeDtypeStruct((N_EXPERTS, LANES), _F32)],
        compiler_params=_cparams(("arbitrary",)),
        name="router_topk",
    )(logits_t, bias_rows, tri)


def _sc_mesh():
    return plsc.VectorSubcoreMesh(core_axis_name="c", subcore_axis_name="s")


def _sc_dispatch(x_half_rows, dest_half_rows, n_out):
    n_rows = x_half_rows.shape[0]

    @pl.kernel(out_type=jax.ShapeDtypeStruct((n_out, SC_ROW), x_half_rows.dtype), mesh=_sc_mesh(),
               scratch_types=[])
    def scatter_rows(x_hbm, d_hbm, o_hbm):
        def body(x_vmem, i_vmem):
            for j in range(TOP_K):
                pltpu.sync_copy(x_vmem, o_hbm.at[i_vmem.at[j]])

        pltpu.emit_pipeline(
            body,
            grid=(n_rows // SC_WINDOW,),
            in_specs=[pl.BlockSpec((SC_WINDOW, SC_ROW), lambda i: (i, 0)),
                      pl.BlockSpec((TOP_K, SC_WINDOW), lambda i: (0, i))],
            out_specs=[],
            core_axis_name=("c", "s"),
            dimension_semantics=(pltpu.PARALLEL,),
        )(x_hbm, d_hbm)

    return scatter_rows(x_half_rows, dest_half_rows)


def _sc_gather(y_half_rows, idx):
    n = idx.shape[1]

    @pl.kernel(out_type=jax.ShapeDtypeStruct((n, SC_ROW), y_half_rows.dtype), mesh=_sc_mesh(),
               scratch_types=[])
    def gather_rows(y_hbm, i_hbm, o_hbm):
        def body(i_vmem, o_vmem):
            pltpu.sync_copy(y_hbm.at[i_vmem.at[0]], o_vmem)

        pltpu.emit_pipeline(
            body,
            grid=(n // SC_WINDOW,),
            in_specs=[pl.BlockSpec((1, SC_WINDOW), lambda i: (0, i))],
            out_specs=[pl.BlockSpec((SC_WINDOW, SC_ROW), lambda i: (i, 0))],
            core_axis_name=("c", "s"),
            dimension_semantics=(pltpu.PARALLEL,),
        )(i_hbm, o_hbm)

    return gather_rows(y_half_rows, idx)


def _dispatch_plan(top_e, rank, counts, n_blocks):
    padded = (counts + MOE_BLOCK - 1) // MOE_BLOCK * MOE_BLOCK
    pad_end = jnp.cumsum(padded)
    pad_start = pad_end - padded
    onehot = top_e[:, :, None] == jnp.arange(N_EXPERTS, dtype=jnp.int32)[None, None, :]
    dest = rank + jnp.sum(jnp.where(onehot, pad_start[None, None, :], 0), axis=-1)
    blk_start = jnp.arange(n_blocks, dtype=jnp.int32) * MOE_BLOCK
    blk_e = jnp.minimum(jnp.sum(pad_end[None, :] <= blk_start[:, None], axis=1), N_EXPERTS - 1)
    n_used = pad_end[-1] // MOE_BLOCK
    return dest, jnp.concatenate([blk_e.astype(jnp.int32), n_used[None].astype(jnp.int32)])


def _swiglu_packed(u0, u1, wg_ref, wu_ref, wd_ref):
    x = jnp.concatenate(_unpack_quarters(u0, u1), axis=1).astype(_BF16)
    a = jnp.dot(x, wg_ref[...].astype(_BF16), preferred_element_type=_F32)
    u = jnp.dot(x, wu_ref[...].astype(_BF16), preferred_element_type=_F32)
    act = (a * jax.nn.sigmoid(a) * u).astype(_BF16)
    return jnp.dot(act, wd_ref[...].astype(_BF16), preferred_element_type=_F32)


def _expert_kernel(plan_ref, x_ref, wg_ref, wu_ref, wd_ref, y_ref):
    @pl.when(pl.program_id(0) < plan_ref[pl.num_programs(0)])
    def _():
        y_ref[...] = _pack_quarters(_swiglu_packed(x_ref[0], x_ref[1], wg_ref, wu_ref, wd_ref))


def _experts(plan, x_sorted, wg, wu, wd):
    _, n_rows, q = x_sorted.shape
    n_blocks = n_rows // MOE_BLOCK
    d = 4 * q
    rows = pl.BlockSpec((2, MOE_BLOCK, q), lambda i, plan: (0, i, 0))
    grid_spec = pltpu.PrefetchScalarGridSpec(
        num_scalar_prefetch=1,
        grid=(n_blocks,),
        in_specs=[rows,
                  pl.BlockSpec((None, d, D_EXPERT), lambda i, plan: (plan[i], 0, 0)),
                  pl.BlockSpec((None, d, D_EXPERT), lambda i, plan: (plan[i], 0, 0)),
                  pl.BlockSpec((None, D_EXPERT, d), lambda i, plan: (plan[i], 0, 0))],
        out_specs=rows,
    )
    return pl.pallas_call(
        _expert_kernel,
        grid_spec=grid_spec,
        out_shape=jax.ShapeDtypeStruct((2, n_rows, q), jnp.uint32),
        compiler_params=_cparams(("arbitrary",)),
        name="moe_routed_experts",
    )(plan, x_sorted, wg, wu, wd)


def _combine_kernel(y_ref, gate_ref, h_ref, x1_ref, g2_ref, fg_ref, sg_ref, su_ref, sd_ref, o_ref):
    gate = gate_ref[...]
    acc = None
    for j in range(TOP_K):
        gj = gate[:, j:j + 1]
        parts = [gj * p for p in _unpack_quarters(y_ref[j, 0], y_ref[j, 1])]
        acc = parts if acc is None else [a + p for a, p in zip(acc, parts)]
    moe = jnp.concatenate(acc, axis=1) + _swiglu_packed(h_ref[0], h_ref[1], sg_ref, su_ref, sd_ref)
    x2 = x1_ref[0] + g2_ref[0] * moe
    o_ref[0] = x2 * lax.rsqrt(jnp.mean(x2 * x2, axis=-1, keepdims=True) + EPS) * fg_ref[...]


def _combine(y_rows, gate_tk, h2_packed, x1, g2, final_g, sg, su, sd, tm):
    bsz, l, d = x1.shape
    tm = min(tm, l)
    nt = l // tm
    q = d // 4
    const = lambda b, i: (0, 0)
    return pl.pallas_call(
        _combine_kernel,
        grid=(bsz, nt),
        in_specs=[pl.BlockSpec((TOP_K, 2, tm, q), lambda b, i: (0, 0, b * nt + i, 0)),
                  pl.BlockSpec((tm, TOP_K), lambda b, i: (b * nt + i, 0)),
                  pl.BlockSpec((2, tm, q), lambda b, i: (0, b * nt + i, 0)),
                  pl.BlockSpec((1, tm, d), lambda b, i: (b, i, 0)),
                  pl.BlockSpec((1, 1, d), lambda b, i: (b, 0, 0)),
                  pl.BlockSpec((1, d), const),
                  pl.BlockSpec((d, D_EXPERT), const),
                  pl.BlockSpec((d, D_EXPERT), const),
                  pl.BlockSpec((D_EXPERT, d), const)],
        out_specs=pl.BlockSpec((1, tm, d), lambda b, i: (b, i, 0)),
        out_shape=jax.ShapeDtypeStruct((bsz, l, d), _F32),
        compiler_params=_cparams(("parallel", "parallel")),
        name="moe_combine_shared_final_norm",
    )(y_rows, gate_tk, h2_packed, x1, g2, final_g.reshape(1, d), sg, su, sd)


def _to_chain_lanes(t):
    lead, (b, l, _) = t.shape[:-3], t.shape[-3:]
    nl = len(lead)
    t = t.reshape(*lead, b, l, RW_HEADS, RW_HEAD_DIM)
    t = t.transpose(*range(nl), nl + 1, nl + 3, nl, nl + 2)
    return t.reshape(*lead, l, RW_HEAD_DIM, b * RW_HEADS)


def _from_chain_lanes(t, b):
    l = t.shape[1]
    return t.reshape(2, l, RW_HEAD_DIM, b, RW_HEADS).transpose(0, 3, 1, 4, 2).reshape(2, b, l, RW_WIDTH)


def _scan_operands(prep):
    return _to_chain_lanes(prep[0]), _to_chain_lanes(prep[1])


def kernel(x, c, ctx, c_ctx, w_mod, b_mod, norm1_g, norm2_g, w_in, rw_conv, rw_w0, rw_w2, rw_a0, rw_a2,
           rw_g2, rw_k_k, rw_k_a, rw_r_k, rw_lnx_g, rw_lnx_b, hg_lb_logits, hg_norm_g, w_out, router_w,
           router_b, exp_w_gate, exp_w_up, exp_w_down, sh_w_gate, sh_w_up, sh_w_down, final_norm_g):
    bsz, seq, d = x.shape
    assert w_mod.shape[0] == 1 and bsz * RW_HEADS == LANES
    lyr = 0

    c_all = jnp.concatenate([c, c_ctx[None, :], jnp.zeros((SUBLANES - 1, d), _F32)], axis=0)
    mod = _modulation(c_all, w_mod[lyr], b_mod[lyr])
    sh1, sc1, g1, sh2, sc2, g2 = [m[:, None, :] for m in jnp.split(mod[:bsz], N_MOD, axis=-1)]
    mod_ctx = jnp.broadcast_to(mod[bsz][None, None, :], (bsz, 1, N_MOD * d))
    csh1, csc1 = mod_ctx[..., :d], mod_ctx[..., d:2 * d]

    w_in_bf16 = w_in[lyr].astype(_BF16)
    rkv_lat, lora_lat, hg_lat = _inproj(x, norm1_g[lyr], sc1, sh1, w_in_bf16, tm=256)
    rkv_ctx, lora_ctx, hg_ctx = _inproj(ctx, norm1_g[lyr], csc1, csh1, w_in_bf16, tm=256)

    kern9 = rw_conv[lyr].reshape(CONV_K * CONV_K, RKV_W)
    lane_head = jnp.arange(RW_WIDTH) // RW_HEAD_DIM
    seg = (lane_head[:, None] == lane_head[None, :]).astype(_BF16)
    per_dir_split = lambda t: jnp.swapaxes(_split_bf16(t), 0, 1)
    rw_p = (kern9, rw_w0[lyr], per_dir_split(rw_w2[lyr]), rw_a0[lyr], per_dir_split(rw_a2[lyr]),
            _split_bf16(rw_g2[lyr]), rw_k_k[lyr], rw_k_a[lyr], rw_r_k[lyr], seg)
    prep_lat = _rw_prep(rkv_lat, lora_lat, *rw_p, tq=512, grid_conv=True)
    prep_ctx = _rw_prep(rkv_ctx, lora_ctx, *rw_p, tq=256, grid_conv=False)
    s_rw = _rwkv_scan(*_scan_operands(prep_ctx), None, tt=32, emit=False)
    y_dirs = _from_chain_lanes(_rwkv_scan(*_scan_operands(prep_lat), s_rw, tt=32, emit=True), bsz)

    lb = jnp.cumsum(jax.nn.softmax(hg_lb_logits.astype(_F32), axis=1), axis=1)[:, lyr]
    o_dirs = []
    for dd, rev in enumerate((False, True)):
        s_hg = _hgrn_scan(hg_ctx, lb[dd:dd + 1], None, tb=256, emit=False, reverse=rev)
        o_dirs.append(_hgrn_scan(hg_lat, lb[dd:dd + 1], s_hg, tb=256, emit=True, reverse=rev))

    perm = lambda t: t.reshape(N_GROUPS, GROUP_SIZE, -1).transpose(1, 0, 2).reshape(N_EXPERTS, -1)
    router_wt = perm(router_w[lyr].T)
    bias_rows = jnp.broadcast_to(perm(router_b[lyr][:, None]), (N_EXPERTS, LANES))
    hg_norm_row = jnp.tile(hg_norm_g[lyr], HG_HEADS).reshape(1, HG_WIDTH)
    x1, h2, logits_t = _post(y_dirs, prep_lat[2], prep_lat[3], rw_lnx_g[lyr], rw_lnx_b[lyr], o_dirs, hg_lat,
                             hg_norm_row, seg, x, w_out[lyr].astype(_BF16), g1, norm2_g[lyr], sc2, sh2,
                             router_wt, tm=512)
    top_e, gate, rank, counts_rows = _router(logits_t, bias_rows, tn=512)
    counts = counts_rows[:, 0].astype(jnp.int32).reshape(GROUP_SIZE, N_GROUPS).T.reshape(N_EXPERTS)

    n_tok = bsz * seq
    n_blocks = n_tok * TOP_K // MOE_BLOCK + N_EXPERTS
    n_slots = n_blocks * MOE_BLOCK
    dest, plan = _dispatch_plan(top_e, rank, counts, n_blocks)
    dest_half = (dest[:, None, :] + jnp.array([0, n_slots], jnp.int32)[None, :, None]).reshape(TOP_K, 2 * n_tok)
    x_sorted = _sc_dispatch(h2.reshape(2 * n_tok, SC_ROW), dest_half, 2 * n_slots)
    y_sorted = _experts(plan, x_sorted.reshape(2, n_slots, SC_ROW),
                        exp_w_gate[lyr], exp_w_up[lyr], exp_w_down[lyr])
    y_rows = _sc_gather(y_sorted.reshape(2 * n_slots, SC_ROW), dest_half.reshape(1, TOP_K * 2 * n_tok))
    return _combine(y_rows.reshape(TOP_K, 2, n_tok, SC_ROW), gate.T, h2, x1, g2,
                    final_norm_g, sh_w_gate[lyr], sh_w_up[lyr], sh_w_down[lyr], tm=512)
```

```python
import functools

import jax
import jax.numpy as jnp
from jax import lax
from jax.experimental import pallas as pl
from jax.experimental.pallas import tpu as pltpu
from jax.experimental.pallas import tpu_sc as plsc

D_MODEL = 1024
GRID_W = 64
N_MOD = 6
EPS = 1e-6

RW_HEAD_DIM = 64
RW_HEADS = 8
RW_WIDTH = 512
RW_LORA = 64
RW_GATE_LORA = 128
RW_GN_EPS = 1e-5 * RW_HEAD_DIM
N_SCAN_OPS = 4
CONV_K = 3

HG_HEADS = 8
HG_DIM = 64
HG_WIDTH = 512
HG_CHUNK = 32
HG_PAIRS = HG_HEADS // 2
HG_SCORE_ROWS = 128

RKV_W = 3 * RW_WIDTH
LORA_W = 4 * RW_LORA + RW_GATE_LORA
HG_W = 5 * HG_WIDTH
P_IN = RKV_W + LORA_W + HG_W

N_EXPERTS = 64
N_GROUPS = 8
GROUP_SIZE = N_EXPERTS // N_GROUPS
TOPK_GROUPS = 4
TOP_K = 8
D_EXPERT = 256
ROUTE_SCALE = 2.5
MOE_BLOCK = 1024
SC_ROW = 256
SC_WINDOW = 128

LANES = 128
SUBLANES = 8
VMEM_LIMIT = 48 * 1024 * 1024

_HI = lax.Precision.HIGHEST
_F32 = jnp.float32
_BF16 = jnp.bfloat16
_NT = (((1,), (1,)), ((), ()))
_TN = (((0,), (0,)), ((), ()))


def _cparams(sem):
    return pltpu.CompilerParams(dimension_semantics=sem, vmem_limit_bytes=VMEM_LIMIT)


def _pack_bf16_pair(lo, hi):
    return pltpu.pack_elementwise([lo, hi], packed_dtype=_BF16)


def _unpack_bf16_pair(u):
    return tuple(pltpu.unpack_elementwise(u, index=i, packed_dtype=_BF16, unpacked_dtype=_F32) for i in range(2))


def _pack_quarters(x):
    q = x.shape[1] // 4
    return jnp.stack([_pack_bf16_pair(x[:, 2 * h * q:(2 * h + 1) * q], x[:, (2 * h + 1) * q:(2 * h + 2) * q])
                      for h in range(2)])


def _unpack_quarters(u0, u1):
    return _unpack_bf16_pair(u0) + _unpack_bf16_pair(u1)


def _dir_block(d, i, nb):
    return i + d * (nb - 1 - 2 * i)


def _mod_kernel(c_ref, w_ref, b_ref, o_ref):
    c = c_ref[...]
    s = c * jax.nn.sigmoid(c)
    o_ref[...] = jnp.dot(s, w_ref[...], preferred_element_type=_F32, precision=_HI) + b_ref[...]


def _modulation(c_all, w_mod, b_mod):
    rows, d = c_all.shape
    n = w_mod.shape[1]
    tn = 1024
    return pl.pallas_call(
        _mod_kernel,
        grid=(n // tn,),
        in_specs=[pl.BlockSpec((rows, d), lambda j: (0, 0)),
                  pl.BlockSpec((d, tn), lambda j: (0, j)),
                  pl.BlockSpec((1, tn), lambda j: (0, j))],
        out_specs=pl.BlockSpec((rows, tn), lambda j: (0, j)),
        out_shape=jax.ShapeDtypeStruct((rows, n), _F32),
        compiler_params=_cparams(("parallel",)),
        name="modulation",
    )(c_all, w_mod, b_mod.reshape(1, n))


def _inproj_kernel(x_ref, g_ref, sc_ref, sh_ref, w_ref, rkv_ref, lora_ref, hg_ref):
    x = x_ref[0]
    y = x * lax.rsqrt(jnp.mean(x * x, axis=-1, keepdims=True) + EPS) * g_ref[...]
    h = y * (1.0 + sc_ref[0]) + sh_ref[0]
    u = jnp.dot(h.astype(_BF16), w_ref[...], preferred_element_type=_F32)
    rkv_ref[0] = u[:, :RKV_W]
    lora_ref[0] = u[:, RKV_W:RKV_W + LORA_W]
    hg_ref[0] = u[:, RKV_W + LORA_W:]


def _inproj(x, norm_g, sc, sh, w_in_bf16, tm):
    bsz, l, d = x.shape
    tm = min(tm, l)
    row = lambda b, i: (b, i, 0)
    per_b = lambda b, i: (b, 0, 0)
    return pl.pallas_call(
        _inproj_kernel,
        grid=(bsz, l // tm),
        in_specs=[pl.BlockSpec((1, tm, d), row),
                  pl.BlockSpec((1, d), lambda b, i: (0, 0)),
                  pl.BlockSpec((1, 1, d), per_b),
                  pl.BlockSpec((1, 1, d), per_b),
                  pl.BlockSpec((d, P_IN), lambda b, i: (0, 0))],
        out_specs=[pl.BlockSpec((1, tm, RKV_W), row),
                   pl.BlockSpec((1, tm, LORA_W), row),
                   pl.BlockSpec((1, tm, HG_W), row)],
        out_shape=[jax.ShapeDtypeStruct((bsz, l, RKV_W), _F32),
                   jax.ShapeDtypeStruct((bsz, l, LORA_W), _F32),
                   jax.ShapeDtypeStruct((bsz, l, HG_W), _F32)],
        compiler_params=_cparams(("parallel", "parallel")),
        name="inproj",
    )(x, norm_g.reshape(1, d), sc, sh, w_in_bf16)


def _seg_sum(x, seg_ref, nonneg=False):
    hi = x.astype(_BF16)
    out = jnp.dot(hi, seg_ref[...], preferred_element_type=_F32)
    if nonneg:
        return out
    lo = (x - hi.astype(_F32)).astype(_BF16)
    return out + jnp.dot(lo, seg_ref[...], preferred_element_type=_F32)


def _dot_split(x, w_ref):
    hi = x.astype(_BF16)
    lo = (x - hi.astype(_F32)).astype(_BF16)
    return (jnp.dot(hi, w_ref[0], preferred_element_type=_F32) + jnp.dot(lo, w_ref[0], preferred_element_type=_F32)
            + jnp.dot(hi, w_ref[1], preferred_element_type=_F32))


def _split_bf16(w):
    hi = w.astype(_BF16)
    return jnp.stack([hi, (w - hi.astype(_F32)).astype(_BF16)])


def _rw_prep_kernel(cur_ref, prev_ref, next_ref, lora_ref, kern_ref, w0_ref, w2_ref, a0_ref, a2_ref, g2_ref,
                    kk_ref, ka_ref, rk_ref, seg_ref,
                    ops_out, w_out, bonus_out, g_out, *, grid_conv):
    i = pl.program_id(1)
    tq = cur_ref.shape[0]
    hw = prev_ref.shape[0]
    w = RW_WIDTH
    pmask = jnp.where(i > 0, 1.0, 0.0)
    nmask = jnp.where(i < pl.num_programs(1) - 1, 1.0, 0.0)
    xpos = lax.broadcasted_iota(jnp.int32, (tq + 2 * hw, w), 0) % GRID_W
    conv = []
    for part in range(3):
        cs = slice(part * w, (part + 1) * w)
        xe = jnp.concatenate([prev_ref[:, cs] * pmask, cur_ref[:, cs], next_ref[:, cs] * nmask], axis=0)
        xl = pltpu.roll(xe, 1, 0)
        xr = pltpu.roll(xe, tq + 2 * hw - 1, 0)
        if grid_conv:
            xl = jnp.where(xpos == 0, 0.0, xl)
            xr = jnp.where(xpos == GRID_W - 1, 0.0, xr)
        acc = None
        for dy in range(CONV_K) if grid_conv else (CONV_K // 2,):
            off = hw + (dy - 1) * GRID_W
            for dx, src in enumerate((xl, xe, xr)):
                term = src[off:off + tq] * kern_ref[pl.ds(dy * CONV_K + dx, 1), cs]
                acc = term if acc is None else acc + term
        conv.append(acc)
    r, k, v = conv
    ops_out[1] = _pack_bf16_pair(v, v)

    kk = k * kk_ref[...]
    kk = kk / jnp.maximum(jnp.sqrt(_seg_sum(kk * kk, seg_ref, nonneg=True)), 1e-12)
    ops_out[0] = _pack_bf16_pair(r, kk)

    lora = lora_ref[...]
    ksum = jnp.zeros_like(k)
    log_decay = []
    for d in range(2):
        wl = lora[:, d * RW_LORA:(d + 1) * RW_LORA]
        al = lora[:, (2 + d) * RW_LORA:(3 + d) * RW_LORA]
        z = w0_ref[pl.ds(d, 1), :] + _dot_split(jnp.tanh(wl), w2_ref.at[d])
        wlog = -jax.nn.softplus(-z) - 0.5
        log_decay.append(-jnp.exp(wlog))
        a = jax.nn.sigmoid(a0_ref[pl.ds(d, 1), :] + _dot_split(al, a2_ref.at[d]))
        kd = k * (1.0 + (a - 1.0) * ka_ref[...])
        ops_out[2 + d] = _pack_bf16_pair(kd, kk * a)
        ksum = ksum + kd
    w_out[...] = _pack_bf16_pair(log_decay[0], log_decay[1])
    bonus_out[...] = _seg_sum(r * ksum * rk_ref[...], seg_ref) * v
    gd = lora[:, 4 * RW_LORA:]
    g_out[...] = _dot_split(jax.nn.sigmoid(gd), g2_ref)


def _rw_prep(rkv, lora, kern9, w0, w2, a0, a2, g2, k_k, k_a, r_k, seg, tq, grid_conv):
    bsz, l, _ = rkv.shape
    tq = min(tq, l)
    nb = l // tq
    hw = GRID_W
    per = tq // hw
    w = RW_WIDTH
    row = lambda b, i: (b, i, 0)
    const2 = lambda b, i: (0, 0)
    const3 = lambda b, i: (0, 0, 0)
    const4 = lambda b, i: (0, 0, 0, 0)
    tok = pl.BlockSpec((None, tq, w), row)
    tok2 = pl.BlockSpec((2, None, tq, w), lambda b, i: (0, b, i, 0))
    one = jax.ShapeDtypeStruct((bsz, l, w), _F32)
    two = jax.ShapeDtypeStruct((2, bsz, l, w), _F32)
    ops = jax.ShapeDtypeStruct((N_SCAN_OPS, bsz, l, w), jnp.uint32)
    ops_spec = pl.BlockSpec((N_SCAN_OPS, None, tq, w), lambda b, i: (0, b, i, 0))
    vec = lambda t: t.reshape(1, w)
    return pl.pallas_call(
        functools.partial(_rw_prep_kernel, grid_conv=grid_conv),
        grid=(bsz, nb),
        in_specs=[pl.BlockSpec((None, tq, RKV_W), row),
                  pl.BlockSpec((None, hw, RKV_W), lambda b, i: (b, jnp.maximum(i * per - 1, 0), 0)),
                  pl.BlockSpec((None, hw, RKV_W), lambda b, i: (b, jnp.minimum((i + 1) * per, l // hw - 1), 0)),
                  pl.BlockSpec((None, tq, LORA_W), row),
                  pl.BlockSpec((CONV_K * CONV_K, RKV_W), const2),
                  pl.BlockSpec((2, w), const2),
                  pl.BlockSpec((2, 2, RW_LORA, w), const4),
                  pl.BlockSpec((2, w), const2),
                  pl.BlockSpec((2, 2, RW_LORA, w), const4),
                  pl.BlockSpec((2, RW_GATE_LORA, w), const3),
                  pl.BlockSpec((1, w), const2),
                  pl.BlockSpec((1, w), const2),
                  pl.BlockSpec((1, w), const2),
                  pl.BlockSpec((w, w), const2)],
        out_specs=[ops_spec, tok, tok, tok],
        out_shape=[ops, jax.ShapeDtypeStruct((bsz, l, w), jnp.uint32), one, one],
        compiler_params=_cparams(("parallel", "parallel")),
        name="rwkv_prep_grid" if grid_conv else "rwkv_prep_seq",
    )(rkv, rkv, rkv, lora, kern9, w0, w2, a0, a2, g2, vec(k_k), vec(k_a), vec(r_k), seg)


def _rwkv_scan_kernel(*refs, tt, emit, has_init):
    rkk_in, v_in, w_in, kb_in = refs[:4]
    rest = list(refs[4:])
    s0_ref = rest.pop(0) if has_init else None
    out_ref = rest.pop(0)
    s_ref, sa_ref, r_ref, kk_ref, v_ref, k_ref, b_ref, w_ref = rest
    r_ref[...], kk_ref[...] = _unpack_bf16_pair(rkk_in[...])
    v_ref[...] = _unpack_bf16_pair(v_in[...])[0]
    k_ref[...], b_ref[...] = _unpack_bf16_pair(kb_in[...])
    d = pl.program_id(0)
    log_fwd, log_bwd = _unpack_bf16_pair(w_in[...])
    w_ref[...] = jnp.exp(jnp.where(d == 0, log_fwd, log_bwd))
    i = pl.program_id(1)
    n = RW_HEAD_DIM

    @pl.when(i == 0)
    def _():
        s_ref[...] = s0_ref[...] if has_init else jnp.zeros_like(s_ref)

    hv = n // 2
    halves = [pl.ds(h * hv, hv) for h in range(2)]

    def project(t, vs):
        acc = [jnp.zeros((hv, LANES), _F32), jnp.zeros((hv, LANES), _F32)]
        for k in range(n):
            acc[k % 2] = acc[k % 2] + s_ref[k, vs, :] * kk_ref[t, pl.ds(k, 1), :]
        return -(acc[0] + acc[1])

    first = d * (tt - 1)
    for vs in halves:
        sa_ref[vs, :] = project(first, vs)

    def step(j, carry):
        t = j + d * (tt - 1 - 2 * j)
        tn = jnp.clip(t + 1 - 2 * d, 0, tt - 1)
        for vs in halves:
            sa = sa_ref[vs, :]
            vv = v_ref[t, vs, :]
            y = [jnp.zeros((hv, LANES), _F32), jnp.zeros((hv, LANES), _F32)]
            nsa = [jnp.zeros((hv, LANES), _F32), jnp.zeros((hv, LANES), _F32)]
            for k in range(n):
                row = pl.ds(k, 1)
                s_new = s_ref[k, vs, :] * w_ref[t, row, :] + sa * b_ref[t, row, :] + vv * k_ref[t, row, :]
                s_ref[k, vs, :] = s_new
                nsa[k % 2] = nsa[k % 2] + s_new * kk_ref[tn, row, :]
                if emit:
                    y[k % 2] = y[k % 2] + s_new * r_ref[t, row, :]
            sa_ref[vs, :] = -(nsa[0] + nsa[1])
            if emit:
                out_ref[t, vs, :] = (y[0] + y[1]).astype(out_ref.dtype)
        return carry

    lax.fori_loop(0, tt, step, 0)

    if not emit:
        @pl.when(i == pl.num_programs(1) - 1)
        def _():
            out_ref[...] = s_ref[...]


def _rwkv_scan(ops, w, s0, tt, emit):
    _, l, n, c = ops.shape
    nb = l // tt
    op_spec = lambda j, per_dir: pl.BlockSpec(
        (None, tt, n, c), lambda d, i: (j + d * per_dir, _dir_block(d, i, nb), 0, 0))
    pd_spec = pl.BlockSpec((None, tt, n, c), lambda d, i: (d, _dir_block(d, i, nb), 0, 0))
    st_spec = pl.BlockSpec((None, n, n, c), lambda d, i: (d, 0, 0, 0))
    w_spec = pl.BlockSpec((tt, n, c), lambda d, i: (_dir_block(d, i, nb), 0, 0))
    in_specs = [op_spec(0, 0), op_spec(1, 0), w_spec, op_spec(2, 1)]
    args = [ops, ops, w, ops]
    if s0 is not None:
        in_specs.append(st_spec)
        args.append(s0)
    if emit:
        out_spec, out_shape = pd_spec, jax.ShapeDtypeStruct((2, l, n, c), _BF16)
    else:
        out_spec, out_shape = st_spec, jax.ShapeDtypeStruct((2, n, n, c), _F32)
    return pl.pallas_call(
        functools.partial(_rwkv_scan_kernel, tt=tt, emit=emit, has_init=s0 is not None),
        grid=(2, nb),
        in_specs=in_specs,
        out_specs=out_spec,
        out_shape=out_shape,
        scratch_shapes=[pltpu.VMEM((n, n, c), _F32), pltpu.VMEM((n, c), _F32)] + [pltpu.VMEM((tt, n, c), _F32)] * 6,
        compiler_params=_cparams(("parallel", "arbitrary")),
        name="rwkv_scan_emit" if emit else "rwkv_scan_state",
    )(*args)


def _hgrn_kernel(*refs, n_chunks, emit, has_init, reverse):
    q_ref, f_ref, i_ref, lb_ref, mats_ref = refs[:5]
    rest = list(refs[5:])
    s0_ref = rest.pop(0) if has_init else None
    out_ref = rest.pop(0)
    st_ref, upd_ref, ent_ref = rest
    c = HG_CHUNK
    tb = n_chunks * c
    blk = pl.program_id(1)

    @pl.when(blk == 0)
    def _():
        st_ref[...] = s0_ref[...] if has_init else jnp.zeros_like(st_ref)

    lb = lb_ref[...]
    omlb = 1.0 - lb
    fr = f_ref[...]
    logf = jnp.log(lb + omlb * jax.nn.sigmoid(fr))
    kf = omlb * jax.nn.sigmoid(-fr)
    hi = logf.astype(_BF16)
    lo = (logf - hi.astype(_F32)).astype(_BF16)
    sums = (jnp.dot(mats_ref[...], hi, preferred_element_type=_F32)
            + jnp.dot(mats_ref[...], lo, preferred_element_type=_F32))
    cum, tot, mid = sums[:tb], sums[tb:2 * tb], sums[2 * tb:]
    dec = jnp.exp(tot)
    kdec = (kf * jnp.exp(tot - cum)).astype(_BF16)
    vb = i_ref[...].astype(_BF16)
    same_head = ((lax.broadcasted_iota(jnp.int32, (LANES, LANES), 0) < HG_DIM)
                 == (lax.broadcasted_iota(jnp.int32, (LANES, LANES), 1) < HG_DIM))
    pairs = [slice(p * LANES, (p + 1) * LANES) for p in range(HG_PAIRS)]
    chunks = [slice(j * c, (j + 1) * c) for j in range(n_chunks)]

    if emit:
        q = q_ref[...]
        qs = q * jax.nn.sigmoid(q) * (HG_DIM ** -0.5)
        qd = qs * jnp.exp(cum - mid)
        kd = (kf * jnp.exp(mid - cum)).astype(_BF16)
        qe = (qs * jnp.exp(cum)).astype(_BF16)
        sub = min(tb, HG_SCORE_ROWS)
        head0 = lax.broadcasted_iota(jnp.int32, (sub, LANES), 1) < HG_DIM
        for g in range(tb // sub):
            rows = slice(g * sub, (g + 1) * sub)
            causal = mats_ref[rows, rows] > 0
            causal2 = jnp.concatenate([causal, causal], axis=0)
            for ls in pairs:
                qd_p = qd[rows, ls]
                q2 = jnp.concatenate([jnp.where(head0, qd_p, 0.0), jnp.where(head0, 0.0, qd_p)], axis=0)
                sc = lax.dot_general(q2.astype(_BF16), kd[rows, ls], _NT, preferred_element_type=_F32)
                sc = jnp.where(causal2, sc, 0.0)
                oi = jnp.dot(sc.astype(_BF16), vb[rows, ls], preferred_element_type=_F32)
                out_ref[rows, ls] = jnp.where(head0, oi[:sub], oi[sub:])

    for j, rows in enumerate(chunks):
        for p, ls in enumerate(pairs):
            upd = lax.dot_general(vb[rows, ls], kdec[rows, ls], _TN, preferred_element_type=_F32)
            upd_ref[j, p] = jnp.where(same_head, upd, 0.0)

    order = list(range(n_chunks))[::-1] if reverse else list(range(n_chunks))
    for p, ls in enumerate(pairs):
        st = st_ref[p]
        for j in order:
            if emit:
                ent_ref[j, p] = st.astype(_BF16)
            st = st * dec[j * c:j * c + 1, ls] + upd_ref[j, p]
        st_ref[p] = st

    if emit:
        for j, rows in enumerate(chunks):
            for p, ls in enumerate(pairs):
                out_ref[rows, ls] += lax.dot_general(qe[rows, ls], ent_ref[j, p], _NT,
                                                     preferred_element_type=_F32)
    else:
        @pl.when(blk == pl.num_programs(1) - 1)
        def _():
            out_ref[...] = st_ref[...]


def _hgrn_mats(tb, reverse):
    t = jnp.arange(tb)
    same = (t[:, None] // HG_CHUNK) == (t[None, :] // HG_CHUNK)
    ref_pos = (t // HG_CHUNK) * HG_CHUNK + HG_CHUNK // 2
    if reverse:
        cum = same & (t[None, :] >= t[:, None])
        mid = same & (t[None, :] >= ref_pos[:, None])
    else:
        cum = same & (t[None, :] <= t[:, None])
        mid = same & (t[None, :] <= ref_pos[:, None])
    return jnp.concatenate([cum, same, mid], axis=0).astype(_BF16)


def _hgrn_scan(hg, lb, s0, tb, emit, reverse):
    bsz, l, _ = hg.shape
    tb = min(tb, l)
    nb = l // tb
    w = HG_WIDTH
    n_chunks = tb // HG_CHUNK
    tblk = (lambda i: nb - 1 - i) if reverse else (lambda i: i)
    col = lambda j: pl.BlockSpec((None, tb, w), lambda b, i: (b, tblk(i), j))
    const = lambda b, i: (0, 0)
    st_spec = pl.BlockSpec((None, HG_PAIRS, LANES, LANES), lambda b, i: (b, 0, 0, 0))
    in_specs = [col(0), col(2 if reverse else 1), col(3), pl.BlockSpec((1, w), const),
                pl.BlockSpec((3 * tb, tb), const)]
    args = [hg, hg, hg, lb, _hgrn_mats(tb, reverse)]
    if s0 is not None:
        in_specs.append(st_spec)
        args.append(s0)
    if emit:
        out_spec = pl.BlockSpec((None, tb, w), lambda b, i: (b, tblk(i), 0))
        out_shape = jax.ShapeDtypeStruct((bsz, l, w), _F32)
    else:
        out_spec = st_spec
        out_shape = jax.ShapeDtypeStruct((bsz, HG_PAIRS, LANES, LANES), _F32)
    return pl.pallas_call(
        functools.partial(_hgrn_kernel, n_chunks=n_chunks, emit=emit, has_init=s0 is not None,
                          reverse=reverse),
        grid=(bsz, nb),
        in_specs=in_specs,
        out_specs=out_spec,
        out_shape=out_shape,
        scratch_shapes=[pltpu.VMEM((HG_PAIRS, LANES, LANES), _F32),
                        pltpu.VMEM((n_chunks, HG_PAIRS, LANES, LANES), _F32),
                        pltpu.VMEM((n_chunks, HG_PAIRS, LANES, LANES), _BF16)],
        compiler_params=_cparams(("parallel", "arbitrary")),
        name=("hgrn_emit" if emit else "hgrn_state") + ("_bwd" if reverse else "_fwd"),
    )(*args)


def _post_kernel(y_ref, bonus_ref, grw_ref, lng_ref, lnb_ref, of_ref, ob_ref, ghg_ref, hgn_ref, seg_ref,
                 x_ref, w_ref, g1_ref, n2_ref, sc_ref, sh_ref, rw_ref, x1_ref, h2_ref, lg_ref):
    inv_n = 1.0 / RW_HEAD_DIM
    ysum = y_ref[0].astype(_F32) + y_ref[1].astype(_F32)
    mu = _seg_sum(ysum, seg_ref) * inv_n
    dlt = ysum - mu
    var = _seg_sum(dlt * dlt, seg_ref, nonneg=True) * inv_n
    yn = dlt * lax.rsqrt(var + RW_GN_EPS) * lng_ref[...] + lnb_ref[...] + bonus_ref[...]
    y_rw = yn * grw_ref[...]
    o = of_ref[...] + ob_ref[...]
    o = o * lax.rsqrt(_seg_sum(o * o, seg_ref, nonneg=True) * inv_n + EPS) * hgn_ref[...]
    g = ghg_ref[...]
    y_hg = o * (g * jax.nn.sigmoid(g))
    y = (jnp.dot(y_rw.astype(_BF16), w_ref[:RW_WIDTH, :], preferred_element_type=_F32)
         + jnp.dot(y_hg.astype(_BF16), w_ref[RW_WIDTH:, :], preferred_element_type=_F32))
    x1 = x_ref[0] + g1_ref[0] * y
    x1_ref[0] = x1
    h2 = x1 * lax.rsqrt(jnp.mean(x1 * x1, axis=-1, keepdims=True) + EPS) * n2_ref[...]
    h2 = h2 * (1.0 + sc_ref[0]) + sh_ref[0]
    h2_ref[...] = _pack_quarters(h2)
    lg_ref[...] = lax.dot_general(rw_ref[...], h2, _NT, preferred_element_type=_F32, precision=_HI)


def _post(y_dirs, bonus, g_rw, lnx_g, lnx_b, o_dirs, hg, hg_norm_row, seg,
          x, w_out_bf16, g1, norm2_g, sc2, sh2, router_wt, tm):
    bsz, l, d = x.shape
    tm = min(tm, l)
    nt = l // tm
    w = RW_WIDTH
    row = lambda b, i: (b, i, 0)
    per_b = lambda b, i: (b, 0, 0)
    const = lambda b, i: (0, 0)
    tok = pl.BlockSpec((None, tm, w), row)
    tok2 = pl.BlockSpec((2, None, tm, w), lambda b, i: (0, b, i, 0))
    vec = pl.BlockSpec((1, w), const)
    return pl.pallas_call(
        _post_kernel,
        grid=(bsz, nt),
        in_specs=[tok2, tok, tok, vec, vec, tok, tok,
                  pl.BlockSpec((None, tm, w), lambda b, i: (b, i, 4)),
                  vec,
                  pl.BlockSpec((w, w), const),
                  pl.BlockSpec((1, tm, d), row),
                  pl.BlockSpec((d, d), const),
                  pl.BlockSpec((1, 1, d), per_b),
                  pl.BlockSpec((1, d), const),
                  pl.BlockSpec((1, 1, d), per_b),
                  pl.BlockSpec((1, 1, d), per_b),
                  pl.BlockSpec((N_EXPERTS, d), const)],
        out_specs=[pl.BlockSpec((1, tm, d), row),
                   pl.BlockSpec((2, tm, d // 4), lambda b, i: (0, b * nt + i, 0)),
                   pl.BlockSpec((N_EXPERTS, tm), lambda b, i: (0, b * nt + i))],
        out_shape=[jax.ShapeDtypeStruct((bsz, l, d), _F32),
                   jax.ShapeDtypeStruct((2, bsz * l, d // 4), jnp.uint32),
                   jax.ShapeDtypeStruct((N_EXPERTS, bsz * l), _F32)],
        compiler_params=_cparams(("parallel", "parallel")),
        name="outproj_norm2_router",
    )(y_dirs, bonus, g_rw, lnx_g.reshape(1, w), lnx_b.reshape(1, w), o_dirs[0], o_dirs[1], hg, hg_norm_row, seg,
      x, w_out_bf16, g1, norm2_g.reshape(1, d), sc2, sh2, router_wt)


def _sublane_all(x, op):
    for s in (4, 2, 1):
        x = op(x, pltpu.roll(x, s, 0))
    return x


def _router_kernel(lg_ref, bias_ref, tri_ref, e_ref, g_ref, rank_ref, cnt_ref):
    ng, gs = N_GROUPS, GROUP_SIZE
    tn = lg_ref.shape[1]
    neg = -jnp.inf
    gidx = lax.broadcasted_iota(jnp.int32, (ng, tn), 0)
    bias = jnp.concatenate([bias_ref[...]] * (tn // LANES), axis=1)
    scores = [jax.nn.sigmoid(lg_ref[j * ng:(j + 1) * ng, :]) for j in range(gs)]
    sel = [scores[j] + bias[j * ng:(j + 1) * ng, :] for j in range(gs)]

    m1 = functools.reduce(jnp.maximum, sel)
    cnt = functools.reduce(jnp.add, [(s == m1).astype(_F32) for s in sel])
    m2 = functools.reduce(jnp.maximum, [jnp.where(s < m1, s, neg) for s in sel])
    gscore = m1 + jnp.where(cnt >= 2.0, m1, m2)

    rank = jnp.zeros((ng, tn), jnp.int32)
    for s in range(1, ng):
        other = pltpu.roll(gscore, s, 0)
        beats = jnp.where(other > gscore, 1, jnp.where((other == gscore) & (gidx >= s), 1, 0))
        rank = rank + beats
    gsel = rank < TOPK_GROUPS

    key = [jnp.where(gsel, s, neg) for s in sel]
    eidx = [gidx * gs + j for j in range(gs)]
    avail = [jnp.ones((ng, tn), jnp.int32) for _ in range(gs)]
    top_e = jnp.zeros((TOP_K, tn), jnp.int32)
    top_s = jnp.zeros((TOP_K, tn), _F32)
    for r in range(TOP_K):
        cur = functools.reduce(jnp.maximum, [jnp.where(avail[j] > 0, key[j], neg) for j in range(gs)])
        mx = _sublane_all(cur, jnp.maximum)
        cand = [jnp.where((avail[j] > 0) & (key[j] == mx), eidx[j], N_EXPERTS) for j in range(gs)]
        mn = _sublane_all(functools.reduce(jnp.minimum, cand), jnp.minimum)
        picked = functools.reduce(jnp.add, [jnp.where(eidx[j] == mn, scores[j], 0.0) for j in range(gs)])
        avail = [jnp.where(eidx[j] == mn, 0, avail[j]) for j in range(gs)]
        top_e = jnp.where(gidx == r, mn, top_e)
        top_s = jnp.where(gidx == r, _sublane_all(picked, jnp.add), top_s)
    den = _sublane_all(top_s, jnp.add)
    e_ref[...] = top_e
    g_ref[...] = top_s / den * ROUTE_SCALE

    @pl.when(pl.program_id(0) == 0)
    def _():
        cnt_ref[...] = jnp.zeros_like(cnt_ref)

    taken = jnp.concatenate([jnp.where(a == 0, 1.0, 0.0) for a in avail], axis=0)
    before = jnp.dot(taken.astype(_BF16), tri_ref[...], preferred_element_type=_F32) - taken
    before = before + jnp.concatenate([cnt_ref[...]] * (tn // LANES), axis=1)
    rank = jnp.zeros((TOP_K, tn), _F32)
    for r in range(TOP_K):
        e_r = jnp.broadcast_to(top_e[r:r + 1, :], (ng, tn))
        hit = functools.reduce(jnp.add, [jnp.where(eidx[j] == e_r, before[j * ng:(j + 1) * ng, :], 0.0)
                                         for j in range(gs)])
        rank = jnp.where(gidx == r, _sublane_all(hit, jnp.add), rank)
    rank_ref[...] = rank.astype(jnp.int32)
    ones = jnp.ones((tn, LANES), _BF16)
    cnt_ref[...] += jnp.dot(taken.astype(_BF16), ones, preferred_element_type=_F32)


def _router(logits_t, bias_rows, tn):
    assert TOP_K == N_GROUPS == SUBLANES
    t = logits_t.shape[1]
    tn = min(tn, t)
    tri = (jnp.arange(tn)[:, None] <= jnp.arange(tn)[None, :]).astype(_BF16)
    out = pl.BlockSpec((TOP_K, tn), lambda i: (0, i))
    cnt = pl.BlockSpec((N_EXPERTS, LANES), lambda i: (0, 0))
    return pl.pallas_call(
        _router_kernel,
        grid=(t // tn,),
        in_specs=[pl.BlockSpec((N_EXPERTS, tn), lambda i: (0, i)), cnt,
                  pl.BlockSpec((tn, tn), lambda i: (0, 0))],
        out_specs=[out, out, out, cnt],
        out_shape=[jax.ShapeDtypeStruct((TOP_K, t), jnp.int32), jax.ShapeDtypeStruct((TOP_K, t), _F32),
                   jax.ShapeDtypeStruct((TOP_K, t), jnp.int32),
                   jax.ShapeDtypeStruct((N_EXPERTS, LANES), _F32)],
        compiler_params=_cparams(("arbitrary",)),
        name="router_topk",
    )(logits_t, bias_rows, tri)


def _sc_mesh():
    return plsc.VectorSubcoreMesh(core_axis_name="c", subcore_axis_name="s")


def _sc_dispatch(x_half_rows, dest_half_rows, n_out):
    n_rows = x_half_rows.shape[0]

    @pl.kernel(out_type=jax.ShapeDtypeStruct((n_out, SC_ROW), x_half_rows.dtype), mesh=_sc_mesh(),
               scratch_types=[])
    def scatter_rows(x_hbm, d_hbm, o_hbm):
        def body(x_vmem, i_vmem):
            for j in range(TOP_K):
                pltpu.sync_copy(x_vmem, o_hbm.at[i_vmem.at[j]])

        pltpu.emit_pipeline(
            body,
            grid=(n_rows // SC_WINDOW,),
            in_specs=[pl.BlockSpec((SC_WINDOW, SC_ROW), lambda i: (i, 0)),
                      pl.BlockSpec((TOP_K, SC_WINDOW), lambda i: (0, i))],
            out_specs=[],
            core_axis_name=("c", "s"),
            dimension_semantics=(pltpu.PARALLEL,),
        )(x_hbm, d_hbm)

    return scatter_rows(x_half_rows, dest_half_rows)


def _sc_gather(y_half_rows, idx):
    n = idx.shape[1]

    @pl.kernel(out_type=jax.ShapeDtypeStruct((n, SC_ROW), y_half_rows.dtype), mesh=_sc_mesh(),
               scratch_types=[])
    def gather_rows(y_hbm, i_hbm, o_hbm):
        def body(i_vmem, o_vmem):
            pltpu.sync_copy(y_hbm.at[i_vmem.at[0]], o_vmem)

        pltpu.emit_pipeline(
            body,
            grid=(n // SC_WINDOW,),
            in_specs=[pl.BlockSpec((1, SC_WINDOW), lambda i: (0, i))],
            out_specs=[pl.BlockSpec((SC_WINDOW, SC_ROW), lambda i: (i, 0))],
            core_axis_name=("c", "s"),
            dimension_semantics=(pltpu.PARALLEL,),
        )(i_hbm, o_hbm)

    return gather_rows(y_half_rows, idx)


def _dispatch_plan(top_e, rank, counts, n_blocks):
    padded = (counts + MOE_BLOCK - 1) // MOE_BLOCK * MOE_BLOCK
    pad_end = jnp.cumsum(padded)
    pad_start = pad_end - padded
    onehot = top_e[:, :, None] == jnp.arange(N_EXPERTS, dtype=jnp.int32)[None, None, :]
    dest = rank + jnp.sum(jnp.where(onehot, pad_start[None, None, :], 0), axis=-1)
    blk_start = jnp.arange(n_blocks, dtype=jnp.int32) * MOE_BLOCK
    blk_e = jnp.minimum(jnp.sum(pad_end[None, :] <= blk_start[:, None], axis=1), N_EXPERTS - 1)
    n_used = pad_end[-1] // MOE_BLOCK
    return dest, jnp.concatenate([blk_e.astype(jnp.int32), n_used[None].astype(jnp.int32)])


def _swiglu_packed(u0, u1, wg_ref, wu_ref, wd_ref):
    x = jnp.concatenate(_unpack_quarters(u0, u1), axis=1).astype(_BF16)
    a = jnp.dot(x, wg_ref[...].astype(_BF16), preferred_element_type=_F32)
    u = jnp.dot(x, wu_ref[...].astype(_BF16), preferred_element_type=_F32)
    act = (a * jax.nn.sigmoid(a) * u).astype(_BF16)
    return jnp.dot(act, wd_ref[...].astype(_BF16), preferred_element_type=_F32)


def _expert_kernel(plan_ref, x_ref, wg_ref, wu_ref, wd_ref, y_ref):
    @pl.when(pl.program_id(0) < plan_ref[pl.num_programs(0)])
    def _():
        y_ref[...] = _pack_quarters(_swiglu_packed(x_ref[0], x_ref[1], wg_ref, wu_ref, wd_ref))


def _experts(plan, x_sorted, wg, wu, wd):
    _, n_rows, q = x_sorted.shape
    n_blocks = n_rows // MOE_BLOCK
    d = 4 * q
    rows = pl.BlockSpec((2, MOE_BLOCK, q), lambda i, plan: (0, i, 0))
    grid_spec = pltpu.PrefetchScalarGridSpec(
        num_scalar_prefetch=1,
        grid=(n_blocks,),
        in_specs=[rows,
                  pl.BlockSpec((None, d, D_EXPERT), lambda i, plan: (plan[i], 0, 0)),
                  pl.BlockSpec((None, d, D_EXPERT), lambda i, plan: (plan[i], 0, 0)),
                  pl.BlockSpec((None, D_EXPERT, d), lambda i, plan: (plan[i], 0, 0))],
        out_specs=rows,
    )
    return pl.pallas_call(
        _expert_kernel,
        grid_spec=grid_spec,
        out_shape=jax.ShapeDtypeStruct((2, n_rows, q), jnp.uint32),
        compiler_params=_cparams(("arbitrary",)),
        name="moe_routed_experts",
    )(plan, x_sorted, wg, wu, wd)


def _combine_kernel(y_ref, gate_ref, h_ref, x1_ref, g2_ref, fg_ref, sg_ref, su_ref, sd_ref, o_ref):
    gate = gate_ref[...]
    acc = None
    for j in range(TOP_K):
        gj = gate[:, j:j + 1]
        parts = [gj * p for p in _unpack_quarters(y_ref[j, 0], y_ref[j, 1])]
        acc = parts if acc is None else [a + p for a, p in zip(acc, parts)]
    moe = jnp.concatenate(acc, axis=1) + _swiglu_packed(h_ref[0], h_ref[1], sg_ref, su_ref, sd_ref)
    x2 = x1_ref[0] + g2_ref[0] * moe
    o_ref[0] = x2 * lax.rsqrt(jnp.mean(x2 * x2, axis=-1, keepdims=True) + EPS) * fg_ref[...]


def _combine(y_rows, gate_tk, h2_packed, x1, g2, final_g, sg, su, sd, tm):
    bsz, l, d = x1.shape
    tm = min(tm, l)
    nt = l // tm
    q = d // 4
    const = lambda b, i: (0, 0)
    return pl.pallas_call(
        _combine_kernel,
        grid=(bsz, nt),
        in_specs=[pl.BlockSpec((TOP_K, 2, tm, q), lambda b, i: (0, 0, b * nt + i, 0)),
                  pl.BlockSpec((tm, TOP_K), lambda b, i: (b * nt + i, 0)),
                  pl.BlockSpec((2, tm, q), lambda b, i: (0, b * nt + i, 0)),
                  pl.BlockSpec((1, tm, d), lambda b, i: (b, i, 0)),
                  pl.BlockSpec((1, 1, d), lambda b, i: (b, 0, 0)),
                  pl.BlockSpec((1, d), const),
                  pl.BlockSpec((d, D_EXPERT), const),
                  pl.BlockSpec((d, D_EXPERT), const),
                  pl.BlockSpec((D_EXPERT, d), const)],
        out_specs=pl.BlockSpec((1, tm, d), lambda b, i: (b, i, 0)),
        out_shape=jax.ShapeDtypeStruct((bsz, l, d), _F32),
        compiler_params=_cparams(("parallel", "parallel")),
        name="moe_combine_shared_final_norm",
    )(y_rows, gate_tk, h2_packed, x1, g2, final_g.reshape(1, d), sg, su, sd)


def _to_chain_lanes(t):
    lead, (b, l, _) = t.shape[:-3], t.shape[-3:]
    nl = len(lead)
    t = t.reshape(*lead, b, l, RW_HEADS, RW_HEAD_DIM)
    t = t.transpose(*range(nl), nl + 1, nl + 3, nl, nl + 2)
    return t.reshape(*lead, l, RW_HEAD_DIM, b * RW_HEADS)


def _from_chain_lanes(t, b):
    l = t.shape[1]
    return t.reshape(2, l, RW_HEAD_DIM, b, RW_HEADS).transpose(0, 3, 1, 4, 2).reshape(2, b, l, RW_WIDTH)


def _scan_operands(prep):
    return _to_chain_lanes(prep[0]), _to_chain_lanes(prep[1])


def kernel(x, c, ctx, c_ctx, w_mod, b_mod, norm1_g, norm2_g, w_in, rw_conv, rw_w0, rw_w2, rw_a0, rw_a2,
           rw_g2, rw_k_k, rw_k_a, rw_r_k, rw_lnx_g, rw_lnx_b, hg_lb_logits, hg_norm_g, w_out, router_w,
           router_b, exp_w_gate, exp_w_up, exp_w_down, sh_w_gate, sh_w_up, sh_w_down, final_norm_g):
    bsz, seq, d = x.shape
    assert w_mod.shape[0] == 1 and bsz * RW_HEADS == LANES
    lyr = 0

    c_all = jnp.concatenate([c, c_ctx[None, :], jnp.zeros((SUBLANES - 1, d), _F32)], axis=0)
    mod = _modulation(c_all, w_mod[lyr], b_mod[lyr])
    sh1, sc1, g1, sh2, sc2, g2 = [m[:, None, :] for m in jnp.split(mod[:bsz], N_MOD, axis=-1)]
    mod_ctx = jnp.broadcast_to(mod[bsz][None, None, :], (bsz, 1, N_MOD * d))
    csh1, csc1 = mod_ctx[..., :d], mod_ctx[..., d:2 * d]

    w_in_bf16 = w_in[lyr].astype(_BF16)
    rkv_lat, lora_lat, hg_lat = _inproj(x, norm1_g[lyr], sc1, sh1, w_in_bf16, tm=256)
    rkv_ctx, lora_ctx, hg_ctx = _inproj(ctx, norm1_g[lyr], csc1, csh1, w_in_bf16, tm=256)

    kern9 = rw_conv[lyr].reshape(CONV_K * CONV_K, RKV_W)
    lane_head = jnp.arange(RW_WIDTH) // RW_HEAD_DIM
    seg = (lane_head[:, None] == lane_head[None, :]).astype(_BF16)
    per_dir_split = lambda t: jnp.swapaxes(_split_bf16(t), 0, 1)
    rw_p = (kern9, rw_w0[lyr], per_dir_split(rw_w2[lyr]), rw_a0[lyr], per_dir_split(rw_a2[lyr]),
            _split_bf16(rw_g2[lyr]), rw_k_k[lyr], rw_k_a[lyr], rw_r_k[lyr], seg)
    prep_lat = _rw_prep(rkv_lat, lora_lat, *rw_p, tq=512, grid_conv=True)
    prep_ctx = _rw_prep(rkv_ctx, lora_ctx, *rw_p, tq=256, grid_conv=False)
    s_rw = _rwkv_scan(*_scan_operands(prep_ctx), None, tt=64, emit=False)
    y_dirs = _from_chain_lanes(_rwkv_scan(*_scan_operands(prep_lat), s_rw, tt=64, emit=True), bsz)

    lb = jnp.cumsum(jax.nn.softmax(hg_lb_logits.astype(_F32), axis=1), axis=1)[:, lyr]
    o_dirs = []
    for dd, rev in enumerate((False, True)):
        s_hg = _hgrn_scan(hg_ctx, lb[dd:dd + 1], None, tb=256, emit=False, reverse=rev)
        o_dirs.append(_hgrn_scan(hg_lat, lb[dd:dd + 1], s_hg, tb=256, emit=True, reverse=rev))

    perm = lambda t: t.reshape(N_GROUPS, GROUP_SIZE, -1).transpose(1, 0, 2).reshape(N_EXPERTS, -1)
    router_wt = perm(router_w[lyr].T)
    bias_rows = jnp.broadcast_to(perm(router_b[lyr][:, None]), (N_EXPERTS, LANES))
    hg_norm_row = jnp.tile(hg_norm_g[lyr], HG_HEADS).reshape(1, HG_WIDTH)
    x1, h2, logits_t = _post(y_dirs, prep_lat[2], prep_lat[3], rw_lnx_g[lyr], rw_lnx_b[lyr], o_dirs, hg_lat,
                             hg_norm_row, seg, x, w_out[lyr].astype(_BF16), g1, norm2_g[lyr], sc2, sh2,
                             router_wt, tm=512)
    top_e, gate, rank, counts_rows = _router(logits_t, bias_rows, tn=512)
    counts = counts_rows[:, 0].astype(jnp.int32).reshape(GROUP_SIZE, N_GROUPS).T.reshape(N_EXPERTS)

    n_tok = bsz * seq
    n_blocks = n_tok * TOP_K // MOE_BLOCK + N_EXPERTS
    n_slots = n_blocks * MOE_BLOCK
    dest, plan = _dispatch_plan(top_e, rank, counts, n_blocks)
    dest_half = (dest[:, None, :] + jnp.array([0, n_slots], jnp.int32)[None, :, None]).reshape(TOP_K, 2 * n_tok)
    x_sorted = _sc_dispatch(h2.reshape(2 * n_tok, SC_ROW), dest_half, 2 * n_slots)
    y_sorted = _experts(plan, x_sorted.reshape(2, n_slots, SC_ROW),
                        exp_w_gate[lyr], exp_w_up[lyr], exp_w_down[lyr])
    y_rows = _sc_gather(y_sorted.reshape(2 * n_slots, SC_ROW), dest_half.reshape(1, TOP_K * 2 * n_tok))
    return _combine(y_rows.reshape(TOP_K, 2, n_tok, SC_ROW), gate.T, h2, x1, g2,
                    final_norm_g, sh_w_gate[lyr], sh_w_up[lyr], sh_w_down[lyr], tm=512)
```
